```python
import math
import jax, jax.numpy as jnp
from jax import lax
import numpy as np

D_MODEL = 2048
BATCH = 4
SEQ = 2048
DEPTH = 4
DEC_BATCH = 32
DEC_SEQ = 1
PAST_LEN = 16384
PAGE_SIZE = 128

N_META = 16
S5_GROUP = 16
S5_WIDTH = D_MODEL // 2
S5_GROUPS = S5_WIDTH // S5_GROUP
S5_STATE = 64
GLA_HEADS = 4
GLA_DK = D_MODEL // 16
GLA_DV = D_MODEL // 8
GLA_RANK = 16
GLA_TAU = 16.0
GLA_CHUNK = 16
SWA_QH = 16
SWA_KVH = 4
SWA_GRP = SWA_QH // SWA_KVH
SWA_HD = 64
WINDOW = 128
N_BUCKETS = 32
MAX_DISTANCE = WINDOW
D_FF = 4 * D_MODEL
N_BRANCH = 3
EPS = 1e-6
IN_SIZES = (S5_WIDTH,
            GLA_HEADS * GLA_DK,
            GLA_HEADS * GLA_DK,
            GLA_HEADS * GLA_DV,
            GLA_RANK,
            GLA_HEADS * GLA_DV,
            SWA_QH * SWA_HD,
            SWA_KVH * SWA_HD,
            SWA_KVH * SWA_HD,
            N_BRANCH * D_MODEL)
IN_WIDTH = sum(IN_SIZES)

kernel_name = 'hybrid_s5_gla_swa_decoder_step'

F32 = jnp.float32


def _rmsnorm(x, g):
    xf = x.astype(F32)
    y = xf * lax.rsqrt(jnp.mean(xf * xf, axis=-1, keepdims=True) + EPS)
    return (y * g.astype(F32)).astype(x.dtype)


def _split_cols(z):
    offs = [0]
    for s in IN_SIZES:
        offs.append(offs[-1] + s)
    return [z[..., a:b] for a, b in zip(offs[:-1], offs[1:])]


def _cplx_combine(e1, e2):
    a1r, a1i, b1r, b1i = e1
    a2r, a2i, b2r, b2i = e2
    return (a1r * a2r - a1i * a2i,
            a1r * a2i + a1i * a2r,
            a2r * b1r - a2i * b1i + b2r,
            a2r * b1i + a2i * b1r + b2i)


def _s5(u, x0_re, x0_im, lam_re, lam_im, log_step, b_re, b_im, c_re, c_im, d_skip, w_glu, b_glu):
    n, t, _ = u.shape
    dt = jnp.exp(log_step.astype(F32))[:, None]
    lr, li = lam_re.astype(F32), lam_im.astype(F32)
    mag = jnp.exp(lr * dt)
    ar, ai = mag * jnp.cos(li * dt), mag * jnp.sin(li * dt)
    den = lr * lr + li * li
    fr = ((ar - 1.0) * lr + ai * li) / den
    fi = (ai * lr - (ar - 1.0) * li) / den
    br, bi = b_re.astype(F32), b_im.astype(F32)
    bbr = fr[..., None] * br - fi[..., None] * bi
    bbi = fr[..., None] * bi + fi[..., None] * br
    uf = u.astype(F32)
    ug = uf.reshape(n, t, S5_GROUPS, S5_GROUP)
    e_re = jnp.einsum('gph,ntgh->tngp', bbr, ug)
    e_im = jnp.einsum('gph,ntgh->tngp', bbi, ug)
    x0r, x0i = x0_re.astype(F32), x0_im.astype(F32)
    e_re = e_re.at[0].add(ar * x0r - ai * x0i)
    e_im = e_im.at[0].add(ar * x0i + ai * x0r)
    a_re = jnp.broadcast_to(ar, (t, 1, S5_GROUPS, S5_STATE))
    a_im = jnp.broadcast_to(ai, (t, 1, S5_GROUPS, S5_STATE))
    _, _, xr, xi = lax.associative_scan(_cplx_combine, (a_re, a_im, e_re, e_im), axis=0)
    y = (jnp.einsum('ghp,tngp->ntgh', c_re.astype(F32), xr)
         - jnp.einsum('ghp,tngp->ntgh', c_im.astype(F32), xi))
    y = y.reshape(n, t, S5_WIDTH) + d_skip.astype(F32) * uf
    z = jax.nn.gelu(y)
    out = z * jax.nn.sigmoid(z @ w_glu.astype(F32) + b_glu.astype(F32))
    return out, xr[-1], xi[-1]


def _gla_chunked(q, k, v, lg, s0):
    n, t, h, _ = q.shape
    nc = t // GLA_CHUNK

    def rs(a):
        return a.reshape(n, nc, GLA_CHUNK, h, a.shape[-1]).transpose(1, 0, 3, 2, 4)

    qc, kc, vc, gc = rs(q), rs(k), rs(v), rs(lg)
    b = jnp.cumsum(gc, axis=3)
    causal = jnp.tril(jnp.ones((GLA_CHUNK, GLA_CHUNK), bool))[..., None]
    diff = b[..., :, None, :] - b[..., None, :, :]
    decay = jnp.exp(jnp.where(causal, diff, -jnp.inf))
    att = jnp.einsum('znhtd,znhsd,znhtsd->znhts', qc, kc, decay)
    intra = jnp.einsum('znhts,znhsv->znhtv', att, vc)
    q_inter = qc * jnp.exp(b)
    k_upd = kc * jnp.exp(b[..., -1:, :] - b)
    chunk_decay = jnp.exp(b[..., -1, :])

    def step(s, inp):
        qi, ku, vi, cd = inp
        o = jnp.einsum('nhtd,nhdv->nhtv', qi, s)
        s = cd[..., None] * s + jnp.einsum('nhsd,nhsv->nhdv', ku, vi)
        return s, o

    s_fin, inter = lax.scan(step, s0, (q_inter, k_upd, vc, chunk_decay))
    o = (intra + inter).transpose(1, 0, 3, 2, 4).reshape(n, t, h, v.shape[-1])
    return o, s_fin


def _gla_recurrent(q, k, v, lg, s0):
    def step(s, inp):
        qt, kt, vt, gt = inp
        s = jnp.exp(gt)[..., None] * s + kt[..., :, None] * vt[..., None, :]
        return s, jnp.einsum('nhd,nhdv->nhv', qt, s)

    tf = lambda a: a.transpose(1, 0, 2, 3)
    s_fin, o = lax.scan(step, s0, (tf(q), tf(k), tf(v), tf(lg)))
    return tf(o), s_fin


def _gla(q, k, v, a_in, r, w_gate2, b_gate2, g_out, s0, prompt):
    n, t, _ = q.shape
    q = q.astype(F32).reshape(n, t, GLA_HEADS, GLA_DK) * (GLA_DK ** -0.5)
    k = k.astype(F32).reshape(n, t, GLA_HEADS, GLA_DK)
    v = v.astype(F32).reshape(n, t, GLA_HEADS, GLA_DV)
    lg = jax.nn.log_sigmoid(a_in.astype(F32) @ w_gate2.astype(F32) + b_gate2.astype(F32)) / GLA_TAU
    lg = lg.reshape(n, t, GLA_HEADS, GLA_DK)
    if prompt:
        o, s = _gla_chunked(q, k, v, lg, s0.astype(F32))
    else:
        o, s = _gla_recurrent(q, k, v, lg, s0.astype(F32))
    o = _rmsnorm(o, g_out).reshape(n, t, GLA_HEADS * GLA_DV)
    return o * jax.nn.silu(r.astype(F32)), s


def _t5_bucket(dist):
    max_exact = N_BUCKETS // 2
    d = jnp.maximum(dist, 0)
    large = max_exact + (jnp.log(jnp.maximum(d, 1).astype(F32) / max_exact)
                         / math.log(MAX_DISTANCE / max_exact) * (N_BUCKETS - max_exact)).astype(jnp.int32)
    large = jnp.minimum(large, N_BUCKETS - 1)
    return jnp.where(d < max_exact, d, large)


def _rel_bias(rel_bias, dist):
    bias = rel_bias.astype(F32)[_t5_bucket(dist)]
    return bias.transpose(2, 0, 1).reshape(SWA_KVH, SWA_GRP, dist.shape[0], dist.shape[1])


def _sink_softmax(s, sink):
    sk = sink.astype(F32).reshape(SWA_KVH, SWA_GRP)[:, :, None, None]
    m = jnp.maximum(jnp.max(s, axis=-1, keepdims=True), sk)
    p = jnp.exp(s - m)
    return p / (jnp.sum(p, axis=-1, keepdims=True) + jnp.exp(sk - m))


def _swa_prompt(q, k, v, rel_bias, sinks):
    n, t, _ = q.shape
    pad = (-t) % WINDOW
    tp = t + pad
    nb = tp // WINDOW
    qf, kf, vf = q.astype(F32), k.astype(F32), v.astype(F32)
    qp = jnp.pad(qf, ((0, 0), (pad, 0), (0, 0))).reshape(n, nb, WINDOW, SWA_KVH, SWA_GRP, SWA_HD)
    kp = jnp.pad(kf, ((0, 0), (pad, 0), (0, 0))).reshape(n, nb, WINDOW, SWA_KVH, SWA_HD)
    vp = jnp.pad(vf, ((0, 0), (pad, 0), (0, 0))).reshape(n, nb, WINDOW, SWA_KVH, SWA_HD)
    shift = lambda a: jnp.pad(a, ((0, 0), (1, 0), (0, 0), (0, 0), (0, 0)))[:, :-1]
    kb = jnp.concatenate([shift(kp), kp], axis=2)
    vb = jnp.concatenate([shift(vp), vp], axis=2)
    s = jnp.einsum('nbqkgd,nbskd->nbkgqs', qp, kb) * (SWA_HD ** -0.5)
    dist = (jnp.arange(WINDOW)[:, None] + WINDOW) - jnp.arange(2 * WINDOW)[None, :]
    band = (dist >= 0) & (dist < WINDOW)
    kabs = jnp.arange(nb)[:, None] * WINDOW - WINDOW + jnp.arange(2 * WINDOW)[None, :]
    mask = band[None] & (kabs >= pad)[:, None, :]
    s = jnp.where(mask[None, :, None, None], s + _rel_bias(rel_bias, dist), -jnp.inf)
    p = _sink_softmax(s, sinks)
    o = jnp.einsum('nbkgqs,nbskd->nbqkgd', p, vb).reshape(n, tp, SWA_QH * SWA_HD)[:, pad:]
    wb = min(WINDOW, t)
    k_last = k.reshape(n, t, SWA_KVH, SWA_HD)[:, t - wb:]
    v_last = v.reshape(n, t, SWA_KVH, SWA_HD)[:, t - wb:]
    return o, k_last, v_last


def _swa_decode(q, k, v, k_buf, v_buf, rel_bias, sinks):
    n, s_new, _ = q.shape
    wb = k_buf.shape[1]
    kk = jnp.concatenate([k_buf, k.reshape(n, s_new, SWA_KVH, SWA_HD).astype(k_buf.dtype)], axis=1)
    vv = jnp.concatenate([v_buf, v.reshape(n, s_new, SWA_KVH, SWA_HD).astype(v_buf.dtype)], axis=1)
    qq = q.astype(F32).reshape(n, s_new, SWA_KVH, SWA_GRP, SWA_HD)
    s = jnp.einsum('nqkgd,nskd->nkgqs', qq, kk.astype(F32)) * (SWA_HD ** -0.5)
    dist = (wb + jnp.arange(s_new)[:, None]) - jnp.arange(wb + s_new)[None, :]
    band = (dist >= 0) & (dist < WINDOW)
    s = jnp.where(band, s + _rel_bias(rel_bias, dist), -jnp.inf)
    p = _sink_softmax(s, sinks)
    o = jnp.einsum('nkgqs,nskd->nqkgd', p, vv.astype(F32)).reshape(n, s_new, SWA_QH * SWA_HD)
    return o, kk[:, s_new:], vv[:, s_new:]


def _mixer(h, lp, rel_bias, st, prompt):
    n, t, _ = h.shape
    dt = h.dtype
    u_a, q_b, k_b, v_b, a_b, r_b, q_c, k_c, v_c, gates = _split_cols(h @ lp['w_in'])
    if prompt:
        s5r0 = jnp.zeros((n, S5_GROUPS, S5_STATE), F32)
        s5i0 = jnp.zeros((n, S5_GROUPS, S5_STATE), F32)
        gla0 = jnp.zeros((n, GLA_HEADS, GLA_DK, GLA_DV), F32)
    else:
        s5r0, s5i0, gla0, k_buf, v_buf = st
    y_a, s5r, s5i = _s5(u_a, s5r0, s5i0, lp['s5_lam_re'], lp['s5_lam_im'], lp['s5_log_step'],
                        lp['s5_b_re'], lp['s5_b_im'], lp['s5_c_re'], lp['s5_c_im'], lp['s5_d'],
                        lp['s5_w_glu'], lp['s5_b_glu'])
    y_b, gla1 = _gla(q_b, k_b, v_b, a_b, r_b, lp['gla_w_gate2'], lp['gla_b_gate2'], lp['gla_g_out'],
                     gla0, prompt)
    if prompt:
        y_c, kn, vn = _swa_prompt(q_c, k_c, v_c, rel_bias, lp['swa_sinks'])
    else:
        y_c, kn, vn = _swa_decode(q_c, k_c, v_c, k_buf, v_buf, rel_bias, lp['swa_sinks'])
    g = jax.nn.sigmoid(gates.astype(F32)).reshape(n, t, N_BRANCH, D_MODEL)
    merged = (g[..., 0, :] * (y_a.astype(dt) @ lp['w_up_s5'])
              + g[..., 1, :] * (y_b.astype(dt) @ lp['w_up_gla'])
              + g[..., 2, :] * (y_c.astype(dt) @ lp['w_up_swa']))
    return merged.astype(dt) @ lp['w_out'], (s5r, s5i, gla1, kn, vn)


def _trunk(x, w, rel_bias, states, prompt):
    new = ([], [], [], [], [])
    for l in range(DEPTH):
        lp = {name: arr[l] for name, arr in w.items()}
        st = None if prompt else tuple(s[l] for s in states)
        h = _rmsnorm(x, lp['norm_pre_mix'])
        mix, ns = _mixer(h, lp, rel_bias, st, prompt)
        x = x + _rmsnorm(mix, lp['norm_post_mix'])
        h = _rmsnorm(x, lp['norm_pre_ffn'])
        ff = jnp.square(jax.nn.relu(h @ lp['w_ff1'])) @ lp['w_ff2']
        x = x + _rmsnorm(ff, lp['norm_post_ffn'])
        for lst, a in zip(new, ns):
            lst.append(a)
    return x, [jnp.stack(a) for a in new]


def setup_inputs(seed: int = 0) -> dict:
    key = jax.random.key(seed)
    ks = iter(jax.random.split(key, 48))
    nrm = lambda shape, scale: jax.random.normal(next(ks), shape, F32) * scale
    wb = min(WINDOW, PAST_LEN)
    d = {}
    d['x_prompt'] = nrm((BATCH, SEQ, D_MODEL), 1.0)
    d['x_sample'] = nrm((DEC_BATCH, DEC_SEQ, D_MODEL), 1.0)
    d['state_s5_re'] = nrm((DEPTH, DEC_BATCH, S5_GROUPS, S5_STATE), 0.1)
    d['state_s5_im'] = nrm((DEPTH, DEC_BATCH, S5_GROUPS, S5_STATE), 0.1)
    d['state_gla'] = nrm((DEPTH, DEC_BATCH, GLA_HEADS, GLA_DK, GLA_DV), 1.0)
    d['cache_swa_k'] = nrm((DEPTH, DEC_BATCH, wb, SWA_KVH, SWA_HD), 1.0)
    d['cache_swa_v'] = nrm((DEPTH, DEC_BATCH, wb, SWA_KVH, SWA_HD), 1.0)
    d['meta_tokens'] = nrm((N_META, D_MODEL), 1.0)
    d['rel_bias'] = nrm((N_BUCKETS, SWA_QH), 0.5)
    for name in ('norm_pre_mix', 'norm_post_mix', 'norm_pre_ffn', 'norm_post_ffn'):
        d[name] = 1.0 + nrm((DEPTH, D_MODEL), 0.05)
    d['w_in'] = nrm((DEPTH, D_MODEL, IN_WIDTH), D_MODEL ** -0.5)
    d['s5_lam_re'] = -0.5 + nrm((DEPTH, S5_GROUPS, S5_STATE), 0.01)
    d['s5_lam_im'] = (jnp.pi * jnp.arange(S5_STATE, dtype=F32))[None, None, :] + nrm((DEPTH, S5_GROUPS, S5_STATE), 0.01)
    d['s5_log_step'] = jax.random.uniform(next(ks), (DEPTH, S5_GROUPS), F32, math.log(1e-3), math.log(1e-1))
    d['s5_b_re'] = nrm((DEPTH, S5_GROUPS, S5_STATE, S5_GROUP), (2.0 * S5_GROUP) ** -0.5)
    d['s5_b_im'] = nrm((DEPTH, S5_GROUPS, S5_STATE, S5_GROUP), (2.0 * S5_GROUP) ** -0.5)
    d['s5_c_re'] = nrm((DEPTH, S5_GROUPS, S5_GROUP, S5_STATE), (2.0 * S5_STATE) ** -0.5)
    d['s5_c_im'] = nrm((DEPTH, S5_GROUPS, S5_GROUP, S5_STATE), (2.0 * S5_STATE) ** -0.5)
    d['s5_d'] = nrm((DEPTH, S5_WIDTH), 1.0)
    d['s5_w_glu'] = nrm((DEPTH, S5_WIDTH, S5_WIDTH), S5_WIDTH ** -0.5)
    d['s5_b_glu'] = nrm((DEPTH, S5_WIDTH), 0.01)
    d['gla_w_gate2'] = nrm((DEPTH, GLA_RANK, GLA_HEADS * GLA_DK), GLA_RANK ** -0.5)
    d['gla_b_gate2'] = 1.0 + nrm((DEPTH, GLA_HEADS * GLA_DK), 0.5)
    d['gla_g_out'] = 1.0 + nrm((DEPTH, GLA_DV), 0.05)
    d['swa_sinks'] = nrm((DEPTH, SWA_QH), 0.5)
    d['w_up_s5'] = nrm((DEPTH, S5_WIDTH, D_MODEL), S5_WIDTH ** -0.5)
    d['w_up_gla'] = nrm((DEPTH, GLA_HEADS * GLA_DV, D_MODEL), (GLA_HEADS * GLA_DV) ** -0.5)
    d['w_up_swa'] = nrm((DEPTH, SWA_QH * SWA_HD, D_MODEL), (SWA_QH * SWA_HD) ** -0.5)
    d['w_out'] = nrm((DEPTH, D_MODEL, D_MODEL), D_MODEL ** -0.5)
    d['w_ff1'] = nrm((DEPTH, D_MODEL, D_FF), D_MODEL ** -0.5)
    d['w_ff2'] = nrm((DEPTH, D_FF, D_MODEL), D_FF ** -0.5)
    return d


def reference(x_prompt, x_sample, state_s5_re, state_s5_im, state_gla, cache_swa_k, cache_swa_v,
              meta_tokens, rel_bias, norm_pre_mix, norm_post_mix, norm_pre_ffn, norm_post_ffn,
              w_in, s5_lam_re, s5_lam_im, s5_log_step, s5_b_re, s5_b_im, s5_c_re, s5_c_im, s5_d,
              s5_w_glu, s5_b_glu, gla_w_gate2, gla_b_gate2, gla_g_out, swa_sinks,
              w_up_s5, w_up_gla, w_up_swa, w_out, w_ff1, w_ff2):
    w = {'norm_pre_mix': norm_pre_mix, 'norm_post_mix': norm_post_mix,
         'norm_pre_ffn': norm_pre_ffn, 'norm_post_ffn': norm_post_ffn,
         'w_in': w_in, 's5_lam_re': s5_lam_re, 's5_lam_im': s5_lam_im, 's5_log_step': s5_log_step,
         's5_b_re': s5_b_re, 's5_b_im': s5_b_im, 's5_c_re': s5_c_re, 's5_c_im': s5_c_im,
         's5_d': s5_d, 's5_w_glu': s5_w_glu, 's5_b_glu': s5_b_glu,
         'gla_w_gate2': gla_w_gate2, 'gla_b_gate2': gla_b_gate2, 'gla_g_out': gla_g_out,
         'swa_sinks': swa_sinks, 'w_up_s5': w_up_s5, 'w_up_gla': w_up_gla, 'w_up_swa': w_up_swa,
         'w_out': w_out, 'w_ff1': w_ff1, 'w_ff2': w_ff2}
    b = x_prompt.shape[0]
    meta = jnp.broadcast_to(meta_tokens.astype(x_prompt.dtype)[None], (b, N_META, D_MODEL))
    xp = jnp.concatenate([meta, x_prompt], axis=1)
    yp_full, pst = _trunk(xp, w, rel_bias, None, True)
    y_prompt = yp_full[:, N_META:]
    y_sample, sst = _trunk(x_sample, w, rel_bias,
                           (state_s5_re, state_s5_im, state_gla, cache_swa_k, cache_swa_v), False)
    p_s5_re, p_s5_im, p_gla, p_swa_k, p_swa_v = pst
    s_s5_re, s_s5_im, s_gla, s_swa_k, s_swa_v = sst
    return (y_prompt, y_sample, p_s5_re, p_s5_im, p_gla, p_swa_k, p_swa_v,
            s_s5_re, s_s5_im, s_gla, s_swa_k, s_swa_v)
```

```python
import functools
import math

import numpy as np
import jax
import jax.numpy as jnp
from jax import lax
from jax.experimental import pallas as pl
from jax.experimental.pallas import tpu as pltpu

F32 = jnp.float32
BF16 = jnp.bfloat16

D_MODEL = 2048
N_META = 16
S5_GROUP = 16
S5_WIDTH = D_MODEL // 2
S5_GROUPS = S5_WIDTH // S5_GROUP
S5_STATE = 64
GLA_HEADS = 4
GLA_DK = D_MODEL // 16
GLA_DV = D_MODEL // 8
GLA_RANK = 16
GLA_TAU = 16.0
GLA_CHUNK = 16
SWA_QH = 16
SWA_KVH = 4
SWA_GRP = SWA_QH // SWA_KVH
SWA_HD = 64
WINDOW = 128
N_BUCKETS = 32
MAX_DISTANCE = WINDOW
D_FF = 4 * D_MODEL
N_BRANCH = 3
EPS = 1e-6

GLA_QK = GLA_HEADS * GLA_DK
GLA_V = GLA_HEADS * GLA_DV
SWA_Q = SWA_QH * SWA_HD
SWA_KV = SWA_KVH * SWA_HD

COL_U = 0
COL_GQ = COL_U + S5_WIDTH
COL_GK = COL_GQ + GLA_QK
COL_GV = COL_GK + GLA_QK
COL_GA = COL_GV + GLA_V
LANES = 128
SUBLANES = 8
HEAD_COLS = COL_GA + LANES
TAIL_START = COL_GA + GLA_RANK
TCOL_R = 0
TCOL_SQ = TCOL_R + GLA_V
TCOL_SK = TCOL_SQ + SWA_Q
TCOL_SV = TCOL_SK + SWA_KV
TCOL_GATE = TCOL_SV + SWA_KV
TAIL_COLS = TCOL_GATE + N_BRANCH * D_MODEL

S5_GB = 8
S5_NGB = S5_GROUPS // S5_GB
S5_HALF = S5_GB // 2
S5_HW = S5_HALF * S5_STATE

VMEM_LIMIT = 60 * 1024 * 1024


def _cparams(sem):
    return pltpu.CompilerParams(dimension_semantics=sem, vmem_limit_bytes=VMEM_LIMIT)


def _rms(x, g):
    ms = jnp.mean(x * x, axis=-1, keepdims=True)
    return (x * lax.rsqrt(ms + EPS)) * g


def _row_tile(m):
    for t in (688, 512, 256, 128, 64, 32, 16, 8):
        if m % t == 0:
            return t
    return m


def _norm_mm_kernel(x_ref, g_ref, w_ref, o_ref, h_ref):
    @pl.when(pl.program_id(1) == 0)
    def _():
        h_ref[...] = _rms(x_ref[...], g_ref[...]).astype(BF16)

    o_ref[...] = jnp.dot(h_ref[...], w_ref[...], preferred_element_type=F32)


def _norm_mm(x, g, w, n_cols, tn):
    m, k = x.shape
    tm = _row_tile(m)
    return pl.pallas_call(
        _norm_mm_kernel,
        grid=(m // tm, n_cols // tn),
        in_specs=[pl.BlockSpec((tm, k), lambda i, j: (i, 0)),
                  pl.BlockSpec((1, k), lambda i, j: (0, 0)),
                  pl.BlockSpec((k, tn), lambda i, j: (0, j))],
        out_specs=pl.BlockSpec((tm, tn), lambda i, j: (i, j)),
        out_shape=jax.ShapeDtypeStruct((m, n_cols), F32),
        scratch_shapes=[pltpu.VMEM((tm, k), BF16)],
        compiler_params=_cparams(("parallel", "arbitrary")),
        name="norm_mm",
    )(x, g.reshape(1, k), w)


def _glu_kernel(z_ref, zc_ref, w_ref, b_ref, o_ref, h_ref):
    @pl.when(pl.program_id(1) == 0)
    def _():
        h_ref[...] = z_ref[...].astype(BF16)

    a = jnp.dot(h_ref[...], w_ref[...], preferred_element_type=F32) + b_ref[...]
    o_ref[...] = (zc_ref[...] * jax.nn.sigmoid(a)).astype(o_ref.dtype)


def _glu(z, w, b, tn=512):
    m, k = z.shape
    tm = _row_tile(m)
    return pl.pallas_call(
        _glu_kernel,
        grid=(m // tm, k // tn),
        in_specs=[pl.BlockSpec((tm, k), lambda i, j: (i, 0)),
                  pl.BlockSpec((tm, tn), lambda i, j: (i, j)),
                  pl.BlockSpec((k, tn), lambda i, j: (0, j)),
                  pl.BlockSpec((1, tn), lambda i, j: (0, j))],
        out_specs=pl.BlockSpec((tm, tn), lambda i, j: (i, j)),
        out_shape=jax.ShapeDtypeStruct((m, k), BF16),
        scratch_shapes=[pltpu.VMEM((tm, k), BF16)],
        compiler_params=_cparams(("parallel", "arbitrary")),
        name="s5_glu",
    )(z, z, w, b.reshape(1, k))


def _merge_out_kernel(ya_ref, yb_ref, yc_ref, ga_ref, gb_ref, gc_ref, wa_ref, wb_ref, wc_ref,
                      wo_ref, x_ref, g_ref, o_ref, acc_ref):
    kk = pl.program_id(1)

    @pl.when(kk == 0)
    def _():
        acc_ref[...] = jnp.zeros_like(acc_ref)

    m = (jax.nn.sigmoid(ga_ref[...]) * jnp.dot(ya_ref[...], wa_ref[...], preferred_element_type=F32)
         + jax.nn.sigmoid(gb_ref[...]) * jnp.dot(yb_ref[...], wb_ref[...], preferred_element_type=F32)
         + jax.nn.sigmoid(gc_ref[...]) * jnp.dot(yc_ref[...], wc_ref[...], preferred_element_type=F32))
    acc_ref[...] += jnp.dot(m.astype(BF16), wo_ref[...], preferred_element_type=F32)

    @pl.when(kk == pl.num_programs(1) - 1)
    def _():
        o_ref[...] = x_ref[...] + _rms(acc_ref[...], g_ref[...])


def _merge_out(ya, yb, yc, zt, wa, wb, wc, wo, x, g, tk=256):
    m, d = x.shape
    tm = _row_tile(m)
    kb = ya.shape[1]
    gate0 = TCOL_GATE // tk
    per = d // tk
    gate_spec = lambda br: pl.BlockSpec((tm, tk), lambda i, k: (i, gate0 + br * per + k))
    y_spec = pl.BlockSpec((tm, kb), lambda i, k: (i, 0))
    w_spec = pl.BlockSpec((kb, tk), lambda i, k: (0, k))
    return pl.pallas_call(
        _merge_out_kernel,
        grid=(m // tm, d // tk),
        in_specs=[y_spec, y_spec, y_spec, gate_spec(0), gate_spec(1), gate_spec(2),
                  w_spec, w_spec, w_spec,
                  pl.BlockSpec((tk, d), lambda i, k: (k, 0)),
                  pl.BlockSpec((tm, d), lambda i, k: (i, 0)),
                  pl.BlockSpec((1, d), lambda i, k: (0, 0))],
        out_specs=pl.BlockSpec((tm, d), lambda i, k: (i, 0)),
        out_shape=jax.ShapeDtypeStruct((m, d), F32),
        scratch_shapes=[pltpu.VMEM((tm, d), F32)],
        compiler_params=_cparams(("parallel", "arbitrary")),
        name="merge_out",
    )(ya, yb, yc, zt, zt, zt, wa, wb, wc, wo, x, g.reshape(1, d))


def _ffn_kernel(x_ref, g1_ref, w1_ref, w2_ref, g2_ref, o_ref, h_ref, acc_ref):
    f = pl.program_id(1)

    @pl.when(f == 0)
    def _():
        h_ref[...] = _rms(x_ref[...], g1_ref[...]).astype(BF16)
        acc_ref[...] = jnp.zeros_like(acc_ref)

    a = jnp.dot(h_ref[...], w1_ref[...], preferred_element_type=F32)
    a = jnp.square(jnp.maximum(a, 0.0)).astype(BF16)
    acc_ref[...] += jnp.dot(a, w2_ref[...], preferred_element_type=F32)

    @pl.when(f == pl.num_programs(1) - 1)
    def _():
        o_ref[...] = x_ref[...] + _rms(acc_ref[...], g2_ref[...])


def _ffn(x, g1, w1, w2, g2, tf=512):
    m, d = x.shape
    tm = _row_tile(m)
    ff = w1.shape[1]
    return pl.pallas_call(
        _ffn_kernel,
        grid=(m // tm, ff // tf),
        in_specs=[pl.BlockSpec((tm, d), lambda i, f: (i, 0)),
                  pl.BlockSpec((1, d), lambda i, f: (0, 0)),
                  pl.BlockSpec((d, tf), lambda i, f: (0, f)),
                  pl.BlockSpec((tf, d), lambda i, f: (f, 0)),
                  pl.BlockSpec((1, d), lambda i, f: (0, 0))],
        out_specs=pl.BlockSpec((tm, d), lambda i, f: (i, 0)),
        out_shape=jax.ShapeDtypeStruct((m, d), F32),
        scratch_shapes=[pltpu.VMEM((tm, d), BF16), pltpu.VMEM((tm, d), F32)],
        compiler_params=_cparams(("parallel", "arbitrary")),
        name="ffn",
    )(x, g1.reshape(1, d), w1, w2, g2.reshape(1, d))


def _s5_disc_kernel(lr_ref, li_ref, ls_ref, br_ref, bi_ref, ar_ref, ai_ref, bbr_ref, bbi_ref):
    lr = lr_ref[0]
    li = li_ref[0]
    dt = jnp.exp(ls_ref[0])
    mag = jnp.exp(lr * dt)
    ar = mag * jnp.cos(li * dt)
    ai = mag * jnp.sin(li * dt)
    den = lr * lr + li * li
    fr = ((ar - 1.0) * lr + ai * li) / den
    fi = (ai * lr - (ar - 1.0) * li) / den
    ar_ref[0] = ar
    ai_ref[0] = ai
    br = br_ref[0]
    bi = bi_ref[0]
    bbr_ref[0] = fr[:, None, :] * br - fi[:, None, :] * bi
    bbi_ref[0] = fr[:, None, :] * bi + fi[:, None, :] * br


def _s5_discretize(lam_re, lam_im, log_step, b_re_t, b_im_t):
    depth = lam_re.shape[0]
    gp = pl.BlockSpec((1, S5_GROUPS, S5_STATE), lambda l: (l, 0, 0))
    bspec = pl.BlockSpec((1, S5_GROUPS, S5_GROUP, S5_STATE), lambda l: (l, 0, 0, 0))
    return pl.pallas_call(
        _s5_disc_kernel,
        grid=(depth,),
        in_specs=[gp, gp, pl.BlockSpec((1, S5_GROUPS, 1), lambda l: (l, 0, 0)), bspec, bspec],
        out_specs=[gp, gp, bspec, bspec],
        out_shape=[jax.ShapeDtypeStruct(lam_re.shape, F32), jax.ShapeDtypeStruct(lam_re.shape, F32),
                   jax.ShapeDtypeStruct(b_re_t.shape, F32), jax.ShapeDtypeStruct(b_re_t.shape, F32)],
        compiler_params=_cparams(("parallel",)),
        name="s5_discretize",
    )(lam_re, lam_im, log_step.reshape(depth, S5_GROUPS, 1), b_re_t, b_im_t)


def _blockdiag(r):
    eye = jnp.eye(S5_HALF, dtype=r.dtype)
    z = jnp.einsum('ab,lgqbhp->lgqahbp', eye, r)
    s = r.shape
    return z.reshape(s[0], s[1], s[2], S5_HALF * s[4], S5_HALF * s[5])


def _s5_matrices(bbr_t, bbi_t, c_re, c_im):
    depth = bbr_t.shape[0]
    shp = (depth, S5_NGB, 2, S5_HALF, S5_GROUP, S5_STATE)
    bre = _blockdiag(bbr_t.reshape(shp))
    bim = _blockdiag(bbi_t.reshape(shp))
    bcat = jnp.concatenate([bre, bim], axis=-1)
    zb = jnp.zeros_like(bcat[:, :, 0])
    bmat = jnp.stack([jnp.concatenate([bcat[:, :, 0], zb], axis=-2),
                      jnp.concatenate([zb, bcat[:, :, 1]], axis=-2)], axis=2)
    cre = jnp.swapaxes(_blockdiag(c_re.reshape(shp)), -1, -2)
    cim = jnp.swapaxes(_blockdiag(c_im.reshape(shp)), -1, -2)
    ccat = jnp.concatenate([cre, -cim], axis=-2)
    zc = jnp.zeros_like(ccat[:, :, 0])
    cmat = jnp.stack([jnp.concatenate([ccat[:, :, 0], zc], axis=-1),
                      jnp.concatenate([zc, ccat[:, :, 1]], axis=-1)], axis=2)
    return bmat.astype(BF16), cmat.astype(BF16)


def _s5_prompt_kernel(u_ref, b_ref, c_ref, a_ref, d_ref, z_ref, st_ref, e_ref, x_ref, *, nb, tc):
    @pl.when(pl.program_id(1) == 0)
    def _():
        x_ref[...] = jnp.zeros_like(x_ref)

    nlb = 2 * S5_HW // LANES
    hlb = nlb // 2
    for n in range(nb):
        ub = u_ref[n].astype(BF16)
        for q in range(2):
            e = jnp.dot(ub, b_ref[0, q], preferred_element_type=F32)
            for j in range(nlb):
                e_ref[j, pl.ds(n * 2 + q, tc, stride=SUBLANES), :] = e[:, j * LANES:(j + 1) * LANES]

    a = a_ref[0]
    ar = [a[:, j * LANES:(j + 1) * LANES] for j in range(hlb)]
    ai = [a[:, (hlb + j) * LANES:(hlb + j + 1) * LANES] for j in range(hlb)]

    def step(t, carry):
        row = pl.multiple_of(t * SUBLANES, SUBLANES)
        new = []
        for j in range(hlb):
            xr, xi = carry[2 * j], carry[2 * j + 1]
            nxr = ar[j] * xr - ai[j] * xi + e_ref[j, pl.ds(row, SUBLANES), :]
            nxi = ar[j] * xi + ai[j] * xr + e_ref[hlb + j, pl.ds(row, SUBLANES), :]
            e_ref[j, pl.ds(row, SUBLANES), :] = nxr
            e_ref[hlb + j, pl.ds(row, SUBLANES), :] = nxi
            new += [nxr, nxi]
        return tuple(new)

    x0 = x_ref[...]
    init = []
    for j in range(hlb):
        init += [x0[:, j * LANES:(j + 1) * LANES], x0[:, (hlb + j) * LANES:(hlb + j + 1) * LANES]]
    fin = lax.fori_loop(0, tc, step, tuple(init), unroll=8)
    for j in range(hlb):
        x_ref[:, j * LANES:(j + 1) * LANES] = fin[2 * j]
        x_ref[:, (hlb + j) * LANES:(hlb + j + 1) * LANES] = fin[2 * j + 1]
    st_ref[0] = x_ref[...]

    for n in range(nb):
        y = d_ref[...] * u_ref[n]
        for q in range(2):
            xs = jnp.concatenate(
                [e_ref[j, pl.ds(n * 2 + q, tc, stride=SUBLANES), :] for j in range(nlb)], axis=1)
            y = y + jnp.dot(xs.astype(BF16), c_ref[0, q], preferred_element_type=F32)
        z_ref[n] = jax.nn.gelu(y)


def _s5_prompt(zh3, bmat, cmat, arow, d, tc):
    nb, t, _ = zh3.shape
    kern = functools.partial(_s5_prompt_kernel, nb=nb, tc=tc)
    return pl.pallas_call(
        kern,
        grid=(S5_NGB, t // tc),
        in_specs=[pl.BlockSpec((nb, tc, LANES), lambda g, c: (0, c, g)),
                  pl.BlockSpec((1, 2, LANES, 2 * S5_HW), lambda g, c: (g, 0, 0, 0)),
                  pl.BlockSpec((1, 2, 2 * S5_HW, LANES), lambda g, c: (g, 0, 0, 0)),
                  pl.BlockSpec((1, SUBLANES, 2 * S5_HW), lambda g, c: (g, 0, 0)),
                  pl.BlockSpec((1, LANES), lambda g, c: (0, g))],
        out_specs=[pl.BlockSpec((nb, tc, LANES), lambda g, c: (0, c, g)),
                   pl.BlockSpec((1, SUBLANES, 2 * S5_HW), lambda g, c: (g, 0, 0))],
        out_shape=[jax.ShapeDtypeStruct((nb, t, S5_WIDTH), F32),
                   jax.ShapeDtypeStruct((S5_NGB, SUBLANES, 2 * S5_HW), F32)],
        scratch_shapes=[pltpu.VMEM((2 * S5_HW // LANES, tc * SUBLANES, LANES), F32),
                        pltpu.VMEM((SUBLANES, 2 * S5_HW), F32)],
        compiler_params=_cparams(("parallel", "arbitrary")),
        name="s5_prompt",
    )(zh3, bmat, cmat, arow, d.reshape(1, S5_WIDTH))


def _s5_decode_kernel(u_ref, b_ref, c_ref, a_ref, d_ref, sr_ref, si_ref, z_ref, nr_ref, ni_ref):
    for g in range(S5_NGB):
        u = u_ref[:, g * LANES:(g + 1) * LANES]
        ub = u.astype(BF16)
        y = d_ref[:, g * LANES:(g + 1) * LANES] * u
        for q in range(2):
            c0 = (g * 2 + q) * S5_HW
            e = jnp.dot(ub, b_ref[g, q], preferred_element_type=F32)
            ar = a_ref[0:1, c0:c0 + S5_HW]
            ai = a_ref[1:2, c0:c0 + S5_HW]
            x0r = sr_ref[:, c0:c0 + S5_HW]
            x0i = si_ref[:, c0:c0 + S5_HW]
            nr = ar * x0r - ai * x0i + e[:, :S5_HW]
            ni = ar * x0i + ai * x0r + e[:, S5_HW:]
            nr_ref[:, c0:c0 + S5_HW] = nr
            ni_ref[:, c0:c0 + S5_HW] = ni
            xs = jnp.concatenate([nr, ni], axis=1).astype(BF16)
            y = y + jnp.dot(xs, c_ref[g, q], preferred_element_type=F32)
        z_ref[:, g * LANES:(g + 1) * LANES] = jax.nn.gelu(y)


def _s5_decode(zh, bmat, cmat, adec, d, sr, si):
    nb = zh.shape[0]
    ns = S5_GROUPS * S5_STATE
    full = lambda shape: pl.BlockSpec(shape, lambda i: (0,) * len(shape))
    return pl.pallas_call(
        _s5_decode_kernel,
        grid=(1,),
        in_specs=[full((nb, S5_WIDTH)), full(bmat.shape), full(cmat.shape), full((2, ns)),
                  full((1, S5_WIDTH)), full((nb, ns)), full((nb, ns))],
        out_specs=[full((nb, S5_WIDTH)), full((nb, ns)), full((nb, ns))],
        out_shape=[jax.ShapeDtypeStruct((nb, S5_WIDTH), F32),
                   jax.ShapeDtypeStruct((nb, ns), F32), jax.ShapeDtypeStruct((nb, ns), F32)],
        compiler_params=_cparams(("arbitrary",)),
        name="s5_decode",
    )(zh, bmat, cmat, adec, d.reshape(1, S5_WIDTH), sr, si)


def _gla_finish(o, r, g):
    return _rms(o, g) * (r * jax.nn.sigmoid(r))


def _gla_prompt_kernel(q_ref, k_ref, v_ref, a_ref, r_ref, wg_ref, bg_ref, go_ref, y_ref, st_ref,
                       s_ref, lg_ref, o_ref, *, tc):
    c = pl.program_id(1)

    @pl.when(c == 0)
    def _():
        s_ref[...] = jnp.zeros_like(s_ref)

    a16 = a_ref[0][:, :GLA_RANK].astype(BF16)
    lg_ref[...] = jax.nn.log_sigmoid(
        jnp.dot(a16, wg_ref[...], preferred_element_type=F32) + bg_ref[...]) / GLA_TAU

    ch = GLA_CHUNK
    ri = lax.broadcasted_iota(jnp.int32, (ch, ch), 0)
    ci = lax.broadcasted_iota(jnp.int32, (ch, ch), 1)
    tril = (ri >= ci).astype(F32)
    rowi = lax.broadcasted_iota(jnp.int32, (ch, GLA_DK), 0)
    scale = GLA_DK ** -0.5

    def chunk(idx, carry):
        r0 = pl.multiple_of(idx * ch, ch)
        for h in range(GLA_HEADS):
            ks = slice(h * GLA_DK, (h + 1) * GLA_DK)
            vs = slice(h * GLA_DV, (h + 1) * GLA_DV)
            g = lg_ref[pl.ds(r0, ch), ks]
            b = jnp.dot(tril, g, preferred_element_type=F32, precision=lax.Precision.HIGHEST)
            qh = q_ref[0, pl.ds(r0, ch), ks] * scale
            kh = k_ref[0, pl.ds(r0, ch), ks]
            vh = v_ref[0, pl.ds(r0, ch), vs].astype(BF16)
            att = jnp.zeros((ch, ch), F32)
            for s in range(ch):
                dec = jnp.exp(jnp.where(rowi >= s, b - b[s:s + 1, :], -jnp.inf))
                col = jnp.sum(qh * kh[s:s + 1, :] * dec, axis=1, keepdims=True)
                att = jnp.where(ci == s, col, att)
            intra = jnp.dot(att.astype(BF16), vh, preferred_element_type=F32)
            blast = b[ch - 1:ch, :]
            qi = (qh * jnp.exp(b)).astype(BF16)
            ku = (kh * jnp.exp(blast - b)).astype(BF16)
            st = s_ref[h]
            inter = lax.dot_general(qi, st.astype(BF16), (((1,), (1,)), ((), ())),
                                    preferred_element_type=F32)
            s_ref[h] = jnp.exp(blast) * st + lax.dot_general(
                vh, ku, (((0,), (0,)), ((), ())), preferred_element_type=F32)
            o_ref[pl.ds(r0, ch), vs] = intra + inter
        return carry

    lax.fori_loop(0, tc // ch, chunk, 0)

    for h in range(GLA_HEADS):
        vs = slice(h * GLA_DV, (h + 1) * GLA_DV)
        y_ref[0, :, vs] = _gla_finish(o_ref[:, vs], r_ref[0, :, vs], go_ref[...]).astype(y_ref.dtype)
        st_ref[0, h] = s_ref[h].T


def _gla_prompt(zh3, zt3, wg, bg, go, tc):
    nb, t, _ = zh3.shape
    kern = functools.partial(_gla_prompt_kernel, tc=tc)
    return pl.pallas_call(
        kern,
        grid=(nb, t // tc),
        in_specs=[pl.BlockSpec((1, tc, GLA_QK), lambda n, c: (n, c, COL_GQ // GLA_QK)),
                  pl.BlockSpec((1, tc, GLA_QK), lambda n, c: (n, c, COL_GK // GLA_QK)),
                  pl.BlockSpec((1, tc, GLA_V), lambda n, c: (n, c, COL_GV // GLA_V)),
                  pl.BlockSpec((1, tc, LANES), lambda n, c: (n, c, COL_GA // LANES)),
                  pl.BlockSpec((1, tc, GLA_V), lambda n, c: (n, c, TCOL_R // GLA_V)),
                  pl.BlockSpec((GLA_RANK, GLA_QK), lambda n, c: (0, 0)),
                  pl.BlockSpec((1, GLA_QK), lambda n, c: (0, 0)),
                  pl.BlockSpec((1, GLA_DV), lambda n, c: (0, 0))],
        out_specs=[pl.BlockSpec((1, tc, GLA_V), lambda n, c: (n, c, 0)),
                   pl.BlockSpec((1, GLA_HEADS, GLA_DK, GLA_DV), lambda n, c: (n, 0, 0, 0))],
        out_shape=[jax.ShapeDtypeStruct((nb, t, GLA_V), BF16),
                   jax.ShapeDtypeStruct((nb, GLA_HEADS, GLA_DK, GLA_DV), F32)],
        scratch_shapes=[pltpu.VMEM((GLA_HEADS, GLA_DV, GLA_DK), F32),
                        pltpu.VMEM((tc, GLA_QK), F32),
                        pltpu.VMEM((tc, GLA_V), F32)],
        compiler_params=_cparams(("parallel", "arbitrary")),
        name="gla_prompt",
    )(zh3, zh3, zh3, zh3, zt3, wg, bg.reshape(1, GLA_QK), go.reshape(1, GLA_DV))


def _gla_decode_kernel(zh_ref, r_ref, wg_ref, bg_ref, go_ref, s_ref, y_ref, so_ref):
    n = pl.program_id(0)
    row = zh_ref[pl.ds(n, 1), :]
    a16 = row[:, COL_GA:COL_GA + GLA_RANK].astype(BF16)
    lg = jax.nn.log_sigmoid(
        jnp.dot(a16, wg_ref[...], preferred_element_type=F32) + bg_ref[...]) / GLA_TAU
    eg = jnp.exp(lg)
    rr = r_ref[pl.ds(n, 1), :]
    ri = lax.broadcasted_iota(jnp.int32, (GLA_DK, GLA_DK), 0)
    ci = lax.broadcasted_iota(jnp.int32, (GLA_DK, GLA_DK), 1)
    eye = (ri == ci).astype(F32)

    def col(v):
        return jnp.sum(eye * v, axis=1, keepdims=True)

    for h in range(GLA_HEADS):
        ks = slice(h * GLA_DK, (h + 1) * GLA_DK)
        qh = row[:, COL_GQ + h * GLA_DK:COL_GQ + (h + 1) * GLA_DK] * (GLA_DK ** -0.5)
        kh = row[:, COL_GK + h * GLA_DK:COL_GK + (h + 1) * GLA_DK]
        vh = row[:, COL_GV + h * GLA_DV:COL_GV + (h + 1) * GLA_DV]
        s = col(eg[:, ks]) * s_ref[0, h] + col(kh) * vh
        so_ref[0, h] = s
        o = jnp.sum(col(qh) * s, axis=0, keepdims=True)
        vs = slice(h * GLA_DV, (h + 1) * GLA_DV)
        y_ref[0, :, vs] = _gla_finish(o, rr[:, vs], go_ref[...]).astype(y_ref.dtype)


def _gla_decode(zh, zt, wg, bg, go, s0):
    nb = zh.shape[0]
    return pl.pallas_call(
        _gla_decode_kernel,
        grid=(nb,),
        in_specs=[pl.BlockSpec((nb, HEAD_COLS), lambda n: (0, 0)),
                  pl.BlockSpec((nb, GLA_V), lambda n: (0, TCOL_R // GLA_V)),
                  pl.BlockSpec((GLA_RANK, GLA_QK), lambda n: (0, 0)),
                  pl.BlockSpec((1, GLA_QK), lambda n: (0, 0)),
                  pl.BlockSpec((1, GLA_DV), lambda n: (0, 0)),
                  pl.BlockSpec((1, GLA_HEADS, GLA_DK, GLA_DV), lambda n: (n, 0, 0, 0))],
        out_specs=[pl.BlockSpec((1, 1, GLA_V), lambda n: (n, 0, 0)),
                   pl.BlockSpec((1, GLA_HEADS, GLA_DK, GLA_DV), lambda n: (n, 0, 0, 0))],
        out_shape=[jax.ShapeDtypeStruct((nb, 1, GLA_V), BF16),
                   jax.ShapeDtypeStruct(s0.shape, F32)],
        compiler_params=_cparams(("arbitrary",)),
        name="gla_decode",
    )(zh, zt, wg, bg.reshape(1, GLA_QK), go.reshape(1, GLA_DV), s0)


def _bucket_table():
    dist = (np.arange(WINDOW)[:, None] + WINDOW) - np.arange(2 * WINDOW)[None, :]
    max_exact = N_BUCKETS // 2
    d = np.maximum(dist, 0)
    ratio = np.log(np.maximum(d, 1).astype(np.float32) / np.float32(max_exact)) / np.float32(
        math.log(MAX_DISTANCE / max_exact))
    large = max_exact + (ratio.astype(np.float32) * np.float32(N_BUCKETS - max_exact)).astype(np.int32)
    large = np.minimum(large, N_BUCKETS - 1)
    return np.where(d < max_exact, d, large).astype(np.int32)


def _bias_kernel(bk_ref, rb_ref, o_ref):
    h = pl.program_id(0)
    bk = bk_ref[...]
    acc = jnp.zeros(bk.shape, F32)
    for b in range(N_BUCKETS):
        acc = jnp.where(bk == b, rb_ref[b, h], acc)
    o_ref[0] = acc


def _bias_table(rel_bias):
    bk = jnp.asarray(_bucket_table())
    return pl.pallas_call(
        _bias_kernel,
        grid=(SWA_QH,),
        in_specs=[pl.BlockSpec((WINDOW, 2 * WINDOW), lambda h: (0, 0)),
                  pl.BlockSpec(memory_space=pltpu.SMEM)],
        out_specs=pl.BlockSpec((1, WINDOW, 2 * WINDOW), lambda h: (h, 0, 0)),
        out_shape=jax.ShapeDtypeStruct((SWA_QH, WINDOW, 2 * WINDOW), F32),
        compiler_params=_cparams(("parallel",)),
        name="swa_bias",
    )(bk, rel_bias)


def _sink_attend(s, sink, v):
    m = jnp.maximum(jnp.max(s, axis=-1, keepdims=True), sink)
    p = jnp.exp(s - m)
    den = jnp.sum(p, axis=-1, keepdims=True) + jnp.exp(sink - m)
    return jnp.dot((p / den).astype(BF16), v, preferred_element_type=F32)


def _swa_prompt_kernel(q_ref, kp_ref, kc_ref, vp_ref, vc_ref, bias_ref, sink_ref, y_ref, *, t):
    i = pl.program_id(1)
    w = WINDOW
    rows = lax.broadcasted_iota(jnp.int32, (w, 2 * w), 0)
    cols = lax.broadcasted_iota(jnp.int32, (w, 2 * w), 1)
    dist = w + rows - cols
    kidx = (i - 1) * w + cols
    mask = (dist >= 0) & (dist < w) & (kidx >= 0)
    krow = (i - 1) * w + lax.broadcasted_iota(jnp.int32, (2 * w, 1), 0)
    kk = jnp.concatenate([kp_ref[0], kc_ref[0]], axis=0)
    vv = jnp.concatenate([vp_ref[0], vc_ref[0]], axis=0)
    vv = jnp.where(krow < t, vv, 0.0)
    q = q_ref[0]
    scale = SWA_HD ** -0.5
    for kv in range(SWA_KVH):
        hs = slice(kv * SWA_HD, (kv + 1) * SWA_HD)
        kh = kk[:, hs].astype(BF16)
        vh = vv[:, hs].astype(BF16)
        for g in range(SWA_GRP):
            h = kv * SWA_GRP + g
            qh = q[:, h * SWA_HD:(h + 1) * SWA_HD].astype(BF16)
            s = lax.dot_general(qh, kh, (((1,), (1,)), ((), ())), preferred_element_type=F32)
            s = jnp.where(mask, s * scale + bias_ref[h], -jnp.inf)
            o = _sink_attend(s, sink_ref[h], vh)
            y_ref[0, :, h * SWA_HD:(h + 1) * SWA_HD] = o.astype(y_ref.dtype)


def _swa_prompt(zt3, bias, sinks):
    nb, t, _ = zt3.shape
    w = WINDOW
    nblk = pl.cdiv(t, w)
    kern = functools.partial(_swa_prompt_kernel, t=t)
    kcol = TCOL_SK // SWA_KV
    vcol = TCOL_SV // SWA_KV
    prev = lambda i: jnp.maximum(i - 1, 0)
    return pl.pallas_call(
        kern,
        grid=(nb, nblk),
        in_specs=[pl.BlockSpec((1, w, SWA_Q), lambda n, i: (n, i, TCOL_SQ // SWA_Q)),
                  pl.BlockSpec((1, w, SWA_KV), lambda n, i: (n, prev(i), kcol)),
                  pl.BlockSpec((1, w, SWA_KV), lambda n, i: (n, i, kcol)),
                  pl.BlockSpec((1, w, SWA_KV), lambda n, i: (n, prev(i), vcol)),
                  pl.BlockSpec((1, w, SWA_KV), lambda n, i: (n, i, vcol)),
                  pl.BlockSpec((SWA_QH, w, 2 * w), lambda n, i: (0, 0, 0)),
                  pl.BlockSpec(memory_space=pltpu.SMEM)],
        out_specs=pl.BlockSpec((1, w, SWA_Q), lambda n, i: (n, i, 0)),
        out_shape=jax.ShapeDtypeStruct((nb, t, SWA_Q), BF16),
        compiler_params=_cparams(("parallel", "arbitrary")),
        name="swa_prompt",
    )(zt3, zt3, zt3, zt3, zt3, bias, sinks)


def _swa_decode_kernel(zt_ref, kb_ref, vb_ref, bias_ref, sink_ref, y_ref, ko_ref, vo_ref):
    n = pl.program_id(0)
    row = zt_ref[pl.ds(n, 1), :]
    w = WINDOW
    kn = jnp.concatenate([kb_ref[0, 1:, :], row[:, SWA_Q:SWA_Q + SWA_KV]], axis=0)
    vn = jnp.concatenate([vb_ref[0, 1:, :], row[:, SWA_Q + SWA_KV:SWA_Q + 2 * SWA_KV]], axis=0)
    ko_ref[0] = kn
    vo_ref[0] = vn
    scale = SWA_HD ** -0.5
    for kv in range(SWA_KVH):
        hs = slice(kv * SWA_HD, (kv + 1) * SWA_HD)
        kh = kn[:, hs].astype(BF16)
        vh = vn[:, hs].astype(BF16)
        q4 = jnp.concatenate(
            [row[:, (kv * SWA_GRP + g) * SWA_HD:(kv * SWA_GRP + g + 1) * SWA_HD] for g in range(SWA_GRP)],
            axis=0).astype(BF16)
        s = lax.dot_general(q4, kh, (((1,), (1,)), ((), ())), preferred_element_type=F32)
        s = s * scale + bias_ref[kv]
        o = _sink_attend(s, sink_ref[kv * SWA_GRP:(kv + 1) * SWA_GRP, :], vh)
        for g in range(SWA_GRP):
            h = kv * SWA_GRP + g
            y_ref[0, :, h * SWA_HD:(h + 1) * SWA_HD] = o[g:g + 1, :].astype(y_ref.dtype)


def _swa_decode(zt, kbuf, vbuf, bias_dec, sinks):
    nb = zt.shape[0]
    w = WINDOW
    width = SWA_Q + 2 * SWA_KV
    return pl.pallas_call(
        _swa_decode_kernel,
        grid=(nb,),
        in_specs=[pl.BlockSpec((nb, width), lambda n: (0, 0)),
                  pl.BlockSpec((1, w, SWA_KV), lambda n: (n, 0, 0)),
                  pl.BlockSpec((1, w, SWA_KV), lambda n: (n, 0, 0)),
                  pl.BlockSpec((SWA_KVH, SWA_GRP, w), lambda n: (0, 0, 0)),
                  pl.BlockSpec((SWA_QH, 1), lambda n: (0, 0))],
        out_specs=[pl.BlockSpec((1, 1, SWA_Q), lambda n: (n, 0, 0)),
                   pl.BlockSpec((1, w, SWA_KV), lambda n: (n, 0, 0)),
                   pl.BlockSpec((1, w, SWA_KV), lambda n: (n, 0, 0))],
        out_shape=[jax.ShapeDtypeStruct((nb, 1, SWA_Q), BF16),
                   jax.ShapeDtypeStruct(kbuf.shape, F32), jax.ShapeDtypeStruct(vbuf.shape, F32)],
        compiler_params=_cparams(("arbitrary",)),
        name="swa_decode",
    )(zt[:, TCOL_SQ:TCOL_SQ + width], kbuf, vbuf, bias_dec, sinks.reshape(SWA_QH, 1))


def _seq_chunk(t):
    for c in (688, 512, 256, 128, 64, 32, 16):
        if t % c == 0:
            return c
    raise ValueError(f"sequence length {t} must be a multiple of {GLA_CHUNK}")


def kernel(x_prompt, x_sample, state_s5_re, state_s5_im, state_gla, cache_swa_k, cache_swa_v, meta_tokens, rel_bias, norm_pre_mix, norm_post_mix, norm_pre_ffn, norm_post_ffn, w_in, s5_lam_re, s5_lam_im, s5_log_step, s5_b_re, s5_b_im, s5_c_re, s5_c_im, s5_d, s5_w_glu, s5_b_glu, gla_w_gate2, gla_b_gate2, gla_g_out, swa_sinks, w_up_s5, w_up_gla, w_up_swa, w_out, w_ff1, w_ff2):
    depth = w_in.shape[0]
    nb, seq, d = x_prompt.shape
    t = seq + N_META
    nd = x_sample.shape[0]
    assert x_sample.shape[1] == 1 and cache_swa_k.shape[2] == WINDOW and t % GLA_CHUNK == 0
    tc = _seq_chunk(t)

    w_head = w_in[:, :, :HEAD_COLS].astype(BF16)
    w_tail = w_in[:, :, TAIL_START:].astype(BF16)
    w_glu = s5_w_glu.astype(BF16)
    wg2 = gla_w_gate2.astype(BF16)
    wu_a, wu_b, wu_c = w_up_s5.astype(BF16), w_up_gla.astype(BF16), w_up_swa.astype(BF16)
    wo, w1, w2 = w_out.astype(BF16), w_ff1.astype(BF16), w_ff2.astype(BF16)

    ar, ai, bbr_t, bbi_t = _s5_discretize(s5_lam_re, s5_lam_im, s5_log_step,
                                          jnp.swapaxes(s5_b_re, -1, -2), jnp.swapaxes(s5_b_im, -1, -2))
    bmat, cmat = _s5_matrices(bbr_t, bbi_t, s5_c_re, s5_c_im)
    a_half = jnp.concatenate([ar.reshape(depth, S5_NGB, 2, S5_HW), ai.reshape(depth, S5_NGB, 2, S5_HW)],
                             axis=-1)
    arow = jnp.tile(a_half, (1, 1, SUBLANES // 2, 1))
    adec = jnp.stack([ar.reshape(depth, -1), ai.reshape(depth, -1)], axis=1)

    bias = _bias_table(rel_bias)
    bias_dec = bias[:, WINDOW - 1, WINDOW:].reshape(SWA_KVH, SWA_GRP, WINDOW)

    meta = jnp.broadcast_to(meta_tokens.astype(x_prompt.dtype)[None], (nb, N_META, d))
    xp = jnp.concatenate([meta, x_prompt], axis=1).reshape(nb * t, d)
    xs = x_sample.reshape(nd, d)

    outs = {k: [] for k in ('p_re', 'p_im', 'p_gla', 'p_k', 'p_v', 's_re', 's_im', 's_gla', 's_k', 's_v')}
    for l in range(depth):
        zh = _norm_mm(xp, norm_pre_mix[l], w_head[l], HEAD_COLS, 640)
        zt = _norm_mm(xp, norm_pre_mix[l], w_tail[l], TAIL_COLS, 512)
        zh3 = zh.reshape(nb, t, HEAD_COLS)
        zt3 = zt.reshape(nb, t, TAIL_COLS)
        z5, st5 = _s5_prompt(zh3, bmat[l], cmat[l], arow[l], s5_d[l], tc)
        ya = _glu(z5.reshape(nb * t, S5_WIDTH), w_glu[l], s5_b_glu[l])
        yb, pg = _gla_prompt(zh3, zt3, wg2[l], gla_b_gate2[l], gla_g_out[l], tc)
        yc = _swa_prompt(zt3, bias, swa_sinks[l])
        xp = _merge_out(ya, yb.reshape(nb * t, GLA_V), yc.reshape(nb * t, SWA_Q), zt,
                        wu_a[l], wu_b[l], wu_c[l], wo[l], xp, norm_post_mix[l])
        xp = _ffn(xp, norm_pre_ffn[l], w1[l], w2[l], norm_post_ffn[l])
        st5 = st5.reshape(S5_NGB, nb, 2, 2, S5_HALF, S5_STATE).transpose(3, 1, 0, 2, 4, 5)
        st5 = st5.reshape(2, nb, S5_GROUPS, S5_STATE)
        outs['p_re'].append(st5[0])
        outs['p_im'].append(st5[1])
        outs['p_gla'].append(pg)
        outs['p_k'].append(zt3[:, t - WINDOW:, TCOL_SK:TCOL_SK + SWA_KV].reshape(nb, WINDOW, SWA_KVH, SWA_HD))
        outs['p_v'].append(zt3[:, t - WINDOW:, TCOL_SV:TCOL_SV + SWA_KV].reshape(nb, WINDOW, SWA_KVH, SWA_HD))

        zh = _norm_mm(xs, norm_pre_mix[l], w_head[l], HEAD_COLS, 640)
        zt = _norm_mm(xs, norm_pre_mix[l], w_tail[l], TAIL_COLS, 512)
        z5, nr, ni = _s5_decode(zh, bmat[l], cmat[l], adec[l], s5_d[l],
                                state_s5_re[l].reshape(nd, -1), state_s5_im[l].reshape(nd, -1))
        ya = _glu(z5, w_glu[l], s5_b_glu[l])
        yb, sg = _gla_decode(zh, zt, wg2[l], gla_b_gate2[l], gla_g_out[l], state_gla[l])
        yc, sk, sv = _swa_decode(zt, cache_swa_k[l].reshape(nd, WINDOW, SWA_KV),
                                 cache_swa_v[l].reshape(nd, WINDOW, SWA_KV), bias_dec, swa_sinks[l])
        xs = _merge_out(ya, yb.reshape(nd, GLA_V), yc.reshape(nd, SWA_Q), zt, wu_a[l], wu_b[l], wu_c[l], wo[l], xs, norm_post_mix[l])
        xs = _ffn(xs, norm_pre_ffn[l], w1[l], w2[l], norm_post_ffn[l])
        outs['s_re'].append(nr.reshape(nd, S5_GROUPS, S5_STATE))
        outs['s_im'].append(ni.reshape(nd, S5_GROUPS, S5_STATE))
        outs['s_gla'].append(sg)
        outs['s_k'].append(sk.reshape(nd, WINDOW, SWA_KVH, SWA_HD))
        outs['s_v'].append(sv.reshape(nd, WINDOW, SWA_KVH, SWA_HD))

    y_prompt = xp.reshape(nb, t, d)[:, N_META:]
    y_sample = xs.reshape(nd, 1, d)
    st = {k: jnp.stack(v) for k, v in outs.items()}
    return (y_prompt, y_sample, st['p_re'], st['p_im'], st['p_gla'], st['p_k'], st['p_v'],
            st['s_re'], st['s_im'], st['s_gla'], st['s_k'], st['s_v'])
```

```python
import functools
import math

import numpy as np
import jax
import jax.numpy as jnp
from jax import lax
from jax.experimental import pallas as pl
from jax.experimental.pallas import tpu as pltpu

F32 = jnp.float32
BF16 = jnp.bfloat16

D_MODEL = 2048
N_META = 16
S5_GROUP = 16
S5_WIDTH = D_MODEL // 2
S5_GROUPS = S5_WIDTH // S5_GROUP
S5_STATE = 64
GLA_HEADS = 4
GLA_DK = D_MODEL // 16
GLA_DV = D_MODEL // 8
GLA_RANK = 16
GLA_TAU = 16.0
GLA_CHUNK = 16
SWA_QH = 16
SWA_KVH = 4
SWA_GRP = SWA_QH // SWA_KVH
SWA_HD = 64
WINDOW = 128
N_BUCKETS = 32
MAX_DISTANCE = WINDOW
D_FF = 4 * D_MODEL
N_BRANCH = 3
EPS = 1e-6

GLA_QK = GLA_HEADS * GLA_DK
GLA_V = GLA_HEADS * GLA_DV
SWA_Q = SWA_QH * SWA_HD
SWA_KV = SWA_KVH * SWA_HD

COL_U = 0
COL_GQ = COL_U + S5_WIDTH
COL_GK = COL_GQ + GLA_QK
COL_GV = COL_GK + GLA_QK
COL_GA = COL_GV + GLA_V
LANES = 128
SUBLANES = 8
HEAD_COLS = COL_GA + LANES
TAIL_START = COL_GA + GLA_RANK
TCOL_R = 0
TCOL_SQ = TCOL_R + GLA_V
TCOL_SK = TCOL_SQ + SWA_Q
TCOL_SV = TCOL_SK + SWA_KV
TCOL_GATE = TCOL_SV + SWA_KV
TAIL_COLS = TCOL_GATE + N_BRANCH * D_MODEL

S5_GB = 8
S5_NGB = S5_GROUPS // S5_GB
S5_HALF = S5_GB // 2
S5_HW = S5_HALF * S5_STATE

VMEM_LIMIT = 60 * 1024 * 1024


def _cparams(sem):
    return pltpu.CompilerParams(dimension_semantics=sem, vmem_limit_bytes=VMEM_LIMIT)


def _rms(x, g):
    ms = jnp.mean(x * x, axis=-1, keepdims=True)
    return (x * lax.rsqrt(ms + EPS)) * g


def _row_tile(m):
    for t in (688, 512, 256, 128, 64, 32, 16, 8):
        if m % t == 0:
            return t
    return m


def _lspec(l, shape, imap):
    return pl.BlockSpec((None,) + tuple(shape), lambda *idx: (l,) + tuple(imap(*idx)))


def _norm_mm_kernel(x_ref, g_ref, w_ref, o_ref, h_ref):
    @pl.when(pl.program_id(1) == 0)
    def _():
        h_ref[...] = _rms(x_ref[...], g_ref[...]).astype(BF16)

    o_ref[...] = jnp.dot(h_ref[...], w_ref[...].astype(BF16), preferred_element_type=F32)


def _norm_mm(x, g, w, l, n_cols, tn):
    m, k = x.shape
    tm = _row_tile(m)
    return pl.pallas_call(
        _norm_mm_kernel,
        grid=(m // tm, n_cols // tn),
        in_specs=[pl.BlockSpec((tm, k), lambda i, j: (i, 0)),
                  _lspec(l, (1, k), lambda i, j: (0, 0)),
                  _lspec(l, (k, tn), lambda i, j: (0, j))],
        out_specs=pl.BlockSpec((tm, tn), lambda i, j: (i, j)),
        out_shape=jax.ShapeDtypeStruct((m, n_cols), F32),
        scratch_shapes=[pltpu.VMEM((tm, k), BF16)],
        compiler_params=_cparams(("parallel", "arbitrary")),
        name="norm_mm",
    )(x, g, w)


def _glu_kernel(z_ref, zc_ref, w_ref, b_ref, o_ref, h_ref):
    @pl.when(pl.program_id(1) == 0)
    def _():
        h_ref[...] = z_ref[...].astype(BF16)

    a = jnp.dot(h_ref[...], w_ref[...], preferred_element_type=F32) + b_ref[...]
    o_ref[...] = (zc_ref[...] * jax.nn.sigmoid(a)).astype(o_ref.dtype)


def _glu(z, w, b, l, tn=512):
    m, k = z.shape
    tm = _row_tile(m)
    return pl.pallas_call(
        _glu_kernel,
        grid=(m // tm, k // tn),
        in_specs=[pl.BlockSpec((tm, k), lambda i, j: (i, 0)),
                  pl.BlockSpec((tm, tn), lambda i, j: (i, j)),
                  _lspec(l, (k, tn), lambda i, j: (0, j)),
                  _lspec(l, (1, tn), lambda i, j: (0, j))],
        out_specs=pl.BlockSpec((tm, tn), lambda i, j: (i, j)),
        out_shape=jax.ShapeDtypeStruct((m, k), BF16),
        scratch_shapes=[pltpu.VMEM((tm, k), BF16)],
        compiler_params=_cparams(("parallel", "arbitrary")),
        name="s5_glu",
    )(z, z, w, b)


def _merge_out_kernel(ya_ref, yb_ref, yc_ref, ga_ref, gb_ref, gc_ref, wa_ref, wb_ref, wc_ref,
                      wo_ref, x_ref, g_ref, o_ref, acc_ref):
    kk = pl.program_id(1)

    @pl.when(kk == 0)
    def _():
        acc_ref[...] = jnp.zeros_like(acc_ref)

    m = (jax.nn.sigmoid(ga_ref[...]) * jnp.dot(ya_ref[...], wa_ref[...], preferred_element_type=F32)
         + jax.nn.sigmoid(gb_ref[...]) * jnp.dot(yb_ref[...], wb_ref[...], preferred_element_type=F32)
         + jax.nn.sigmoid(gc_ref[...]) * jnp.dot(yc_ref[...], wc_ref[...], preferred_element_type=F32))
    acc_ref[...] += jnp.dot(m.astype(BF16), wo_ref[...], preferred_element_type=F32)

    @pl.when(kk == pl.num_programs(1) - 1)
    def _():
        o_ref[...] = x_ref[...] + _rms(acc_ref[...], g_ref[...])


def _merge_out(ya, yb, yc, zt, wa, wb, wc, wo, x, g, l, tk=256):
    m, d = x.shape
    tm = _row_tile(m)
    kb = ya.shape[1]
    gate0 = TCOL_GATE // tk
    per = d // tk
    gate_spec = lambda br: pl.BlockSpec((tm, tk), lambda i, k: (i, gate0 + br * per + k))
    y_spec = pl.BlockSpec((tm, kb), lambda i, k: (i, 0))
    w_spec = _lspec(l, (kb, tk), lambda i, k: (0, k))
    return pl.pallas_call(
        _merge_out_kernel,
        grid=(m // tm, d // tk),
        in_specs=[y_spec, y_spec, y_spec, gate_spec(0), gate_spec(1), gate_spec(2),
                  w_spec, w_spec, w_spec,
                  _lspec(l, (tk, d), lambda i, k: (k, 0)),
                  pl.BlockSpec((tm, d), lambda i, k: (i, 0)),
                  _lspec(l, (1, d), lambda i, k: (0, 0))],
        out_specs=pl.BlockSpec((tm, d), lambda i, k: (i, 0)),
        out_shape=jax.ShapeDtypeStruct((m, d), F32),
        scratch_shapes=[pltpu.VMEM((tm, d), F32)],
        compiler_params=_cparams(("parallel", "arbitrary")),
        name="merge_out",
    )(ya, yb, yc, zt, zt, zt, wa, wb, wc, wo, x, g)


def _ffn_kernel(x_ref, g1_ref, w1_ref, w2_ref, g2_ref, o_ref, h_ref, acc_ref):
    f = pl.program_id(1)

    @pl.when(f == 0)
    def _():
        h_ref[...] = _rms(x_ref[...], g1_ref[...]).astype(BF16)
        acc_ref[...] = jnp.zeros_like(acc_ref)

    a = jnp.dot(h_ref[...], w1_ref[...], preferred_element_type=F32)
    a = jnp.square(jnp.maximum(a, 0.0)).astype(BF16)
    acc_ref[...] += jnp.dot(a, w2_ref[...], preferred_element_type=F32)

    @pl.when(f == pl.num_programs(1) - 1)
    def _():
        o_ref[...] = x_ref[...] + _rms(acc_ref[...], g2_ref[...])


def _ffn(x, g1, w1, w2, g2, l, tf=512):
    m, d = x.shape
    tm = _row_tile(m)
    ff = w1.shape[2]
    return pl.pallas_call(
        _ffn_kernel,
        grid=(m // tm, ff // tf),
        in_specs=[pl.BlockSpec((tm, d), lambda i, f: (i, 0)),
                  _lspec(l, (1, d), lambda i, f: (0, 0)),
                  _lspec(l, (d, tf), lambda i, f: (0, f)),
                  _lspec(l, (tf, d), lambda i, f: (f, 0)),
                  _lspec(l, (1, d), lambda i, f: (0, 0))],
        out_specs=pl.BlockSpec((tm, d), lambda i, f: (i, 0)),
        out_shape=jax.ShapeDtypeStruct((m, d), F32),
        scratch_shapes=[pltpu.VMEM((tm, d), BF16), pltpu.VMEM((tm, d), F32)],
        compiler_params=_cparams(("parallel", "arbitrary")),
        name="ffn",
    )(x, g1, w1, w2, g2)


def _s5_disc_kernel(lr_ref, li_ref, ls_ref, br_ref, bi_ref, ar_ref, ai_ref, bbr_ref, bbi_ref):
    lr = lr_ref[0]
    li = li_ref[0]
    dt = jnp.exp(ls_ref[0])
    mag = jnp.exp(lr * dt)
    ar = mag * jnp.cos(li * dt)
    ai = mag * jnp.sin(li * dt)
    den = lr * lr + li * li
    fr = ((ar - 1.0) * lr + ai * li) / den
    fi = (ai * lr - (ar - 1.0) * li) / den
    ar_ref[0] = ar
    ai_ref[0] = ai
    br = br_ref[0]
    bi = bi_ref[0]
    bbr_ref[0] = fr[:, None, :] * br - fi[:, None, :] * bi
    bbi_ref[0] = fr[:, None, :] * bi + fi[:, None, :] * br


def _s5_discretize(lam_re, lam_im, log_step, b_re_t, b_im_t):
    depth = lam_re.shape[0]
    gp = pl.BlockSpec((1, S5_GROUPS, S5_STATE), lambda l: (l, 0, 0))
    bspec = pl.BlockSpec((1, S5_GROUPS, S5_GROUP, S5_STATE), lambda l: (l, 0, 0, 0))
    return pl.pallas_call(
        _s5_disc_kernel,
        grid=(depth,),
        in_specs=[gp, gp, pl.BlockSpec((1, S5_GROUPS, 1), lambda l: (l, 0, 0)), bspec, bspec],
        out_specs=[gp, gp, bspec, bspec],
        out_shape=[jax.ShapeDtypeStruct(lam_re.shape, F32), jax.ShapeDtypeStruct(lam_re.shape, F32),
                   jax.ShapeDtypeStruct(b_re_t.shape, F32), jax.ShapeDtypeStruct(b_re_t.shape, F32)],
        compiler_params=_cparams(("parallel",)),
        name="s5_discretize",
    )(lam_re, lam_im, log_step.reshape(depth, S5_GROUPS, 1), b_re_t, b_im_t)


def _blockdiag(r):
    eye = jnp.eye(S5_HALF, dtype=r.dtype)
    z = jnp.einsum('ab,lgqbhp->lgqahbp', eye, r)
    s = r.shape
    return z.reshape(s[0], s[1], s[2], S5_HALF * s[4], S5_HALF * s[5])


def _s5_matrices(bbr_t, bbi_t, c_re, c_im):
    depth = bbr_t.shape[0]
    shp = (depth, S5_NGB, 2, S5_HALF, S5_GROUP, S5_STATE)
    bre = _blockdiag(bbr_t.reshape(shp))
    bim = _blockdiag(bbi_t.reshape(shp))
    bcat = jnp.concatenate([bre, bim], axis=-1)
    zb = jnp.zeros_like(bcat[:, :, 0])
    bmat = jnp.stack([jnp.concatenate([bcat[:, :, 0], zb], axis=-2),
                      jnp.concatenate([zb, bcat[:, :, 1]], axis=-2)], axis=2)
    cre = jnp.swapaxes(_blockdiag(c_re.reshape(shp)), -1, -2)
    cim = jnp.swapaxes(_blockdiag(c_im.reshape(shp)), -1, -2)
    ccat = jnp.concatenate([cre, -cim], axis=-2)
    zc = jnp.zeros_like(ccat[:, :, 0])
    cmat = jnp.stack([jnp.concatenate([ccat[:, :, 0], zc], axis=-1),
                      jnp.concatenate([zc, ccat[:, :, 1]], axis=-1)], axis=2)
    return bmat.astype(BF16), cmat.astype(BF16)


def _s5_prompt_kernel(u_ref, b_ref, c_ref, a_ref, d_ref, z_ref, st_ref, e_ref, x_ref, *, nb, tc):
    @pl.when(pl.program_id(1) == 0)
    def _():
        x_ref[...] = jnp.zeros_like(x_ref)

    nlb = 2 * S5_HW // LANES
    hlb = nlb // 2
    for n in range(nb):
        ub = u_ref[n].astype(BF16)
        for q in range(2):
            e = jnp.dot(ub, b_ref[0, q], preferred_element_type=F32)
            for j in range(nlb):
                e_ref[j, pl.ds(n * 2 + q, tc, stride=SUBLANES), :] = e[:, j * LANES:(j + 1) * LANES]

    a = a_ref[0]
    ar = [a[:, j * LANES:(j + 1) * LANES] for j in range(hlb)]
    ai = [a[:, (hlb + j) * LANES:(hlb + j + 1) * LANES] for j in range(hlb)]

    def step(t, carry):
        row = pl.multiple_of(t * SUBLANES, SUBLANES)
        new = []
        for j in range(hlb):
            xr, xi = carry[2 * j], carry[2 * j + 1]
            nxr = ar[j] * xr - ai[j] * xi + e_ref[j, pl.ds(row, SUBLANES), :]
            nxi = ar[j] * xi + ai[j] * xr + e_ref[hlb + j, pl.ds(row, SUBLANES), :]
            e_ref[j, pl.ds(row, SUBLANES), :] = nxr
            e_ref[hlb + j, pl.ds(row, SUBLANES), :] = nxi
            new += [nxr, nxi]
        return tuple(new)

    x0 = x_ref[...]
    init = []
    for j in range(hlb):
        init += [x0[:, j * LANES:(j + 1) * LANES], x0[:, (hlb + j) * LANES:(hlb + j + 1) * LANES]]
    fin = lax.fori_loop(0, tc, step, tuple(init), unroll=8)
    for j in range(hlb):
        x_ref[:, j * LANES:(j + 1) * LANES] = fin[2 * j]
        x_ref[:, (hlb + j) * LANES:(hlb + j + 1) * LANES] = fin[2 * j + 1]
    st_ref[0] = x_ref[...]

    for n in range(nb):
        y = d_ref[...] * u_ref[n]
        for q in range(2):
            xs = jnp.concatenate(
                [e_ref[j, pl.ds(n * 2 + q, tc, stride=SUBLANES), :] for j in range(nlb)], axis=1)
            y = y + jnp.dot(xs.astype(BF16), c_ref[0, q], preferred_element_type=F32)
        z_ref[n] = jax.nn.gelu(y)


def _s5_prompt(zh3, bmat, cmat, arow, d, l, tc):
    nb, t, _ = zh3.shape
    assert 2 * nb == SUBLANES, "state rows are (batch, half) pairs filling one sublane tile"
    kern = functools.partial(_s5_prompt_kernel, nb=nb, tc=tc)
    return pl.pallas_call(
        kern,
        grid=(S5_NGB, t // tc),
        in_specs=[pl.BlockSpec((nb, tc, LANES), lambda g, c: (0, c, g)),
                  _lspec(l, (1, 2, LANES, 2 * S5_HW), lambda g, c: (g, 0, 0, 0)),
                  _lspec(l, (1, 2, 2 * S5_HW, LANES), lambda g, c: (g, 0, 0, 0)),
                  _lspec(l, (1, SUBLANES, 2 * S5_HW), lambda g, c: (g, 0, 0)),
                  _lspec(l, (1, LANES), lambda g, c: (0, g))],
        out_specs=[pl.BlockSpec((nb, tc, LANES), lambda g, c: (0, c, g)),
                   pl.BlockSpec((1, SUBLANES, 2 * S5_HW), lambda g, c: (g, 0, 0))],
        out_shape=[jax.ShapeDtypeStruct((nb, t, S5_WIDTH), F32),
                   jax.ShapeDtypeStruct((S5_NGB, SUBLANES, 2 * S5_HW), F32)],
        scratch_shapes=[pltpu.VMEM((2 * S5_HW // LANES, tc * SUBLANES, LANES), F32),
                        pltpu.VMEM((SUBLANES, 2 * S5_HW), F32)],
        compiler_params=_cparams(("parallel", "arbitrary")),
        name="s5_prompt",
    )(zh3, bmat, cmat, arow, d)


def _s5_decode_kernel(u_ref, b_ref, c_ref, a_ref, d_ref, sr_ref, si_ref, z_ref, nr_ref, ni_ref):
    for g in range(S5_NGB):
        u = u_ref[:, g * LANES:(g + 1) * LANES]
        ub = u.astype(BF16)
        y = d_ref[:, g * LANES:(g + 1) * LANES] * u
        for q in range(2):
            c0 = (g * 2 + q) * S5_HW
            e = jnp.dot(ub, b_ref[g, q], preferred_element_type=F32)
            ar = a_ref[0:1, c0:c0 + S5_HW]
            ai = a_ref[1:2, c0:c0 + S5_HW]
            x0r = sr_ref[:, c0:c0 + S5_HW]
            x0i = si_ref[:, c0:c0 + S5_HW]
            nr = ar * x0r - ai * x0i + e[:, :S5_HW]
            ni = ar * x0i + ai * x0r + e[:, S5_HW:]
            nr_ref[:, c0:c0 + S5_HW] = nr
            ni_ref[:, c0:c0 + S5_HW] = ni
            xs = jnp.concatenate([nr, ni], axis=1).astype(BF16)
            y = y + jnp.dot(xs, c_ref[g, q], preferred_element_type=F32)
        z_ref[:, g * LANES:(g + 1) * LANES] = jax.nn.gelu(y)


def _s5_decode(zh, bmat, cmat, adec, d, sr, si, l):
    nb = zh.shape[0]
    ns = S5_GROUPS * S5_STATE
    full = lambda shape: pl.BlockSpec(shape, lambda i: (0,) * len(shape))
    lfull = lambda shape: _lspec(l, shape, lambda i: (0,) * len(shape))
    return pl.pallas_call(
        _s5_decode_kernel,
        grid=(1,),
        in_specs=[full((nb, S5_WIDTH)), lfull(bmat.shape[1:]), lfull(cmat.shape[1:]), lfull((2, ns)),
                  lfull((1, S5_WIDTH)), lfull((nb, ns)), lfull((nb, ns))],
        out_specs=[full((nb, S5_WIDTH)), full((nb, ns)), full((nb, ns))],
        out_shape=[jax.ShapeDtypeStruct((nb, S5_WIDTH), F32),
                   jax.ShapeDtypeStruct((nb, ns), F32), jax.ShapeDtypeStruct((nb, ns), F32)],
        compiler_params=_cparams(("arbitrary",)),
        name="s5_decode",
    )(zh, bmat, cmat, adec, d, sr, si)


def _gla_finish(o, r, g):
    return _rms(o, g) * (r * jax.nn.sigmoid(r))


def _gla_prompt_kernel(q_ref, k_ref, v_ref, a_ref, r_ref, wg_ref, bg_ref, go_ref, y_ref, st_ref,
                       s_ref, lg_ref, o_ref, qi_ref, ku_ref, cd_ref, *, tc):
    c = pl.program_id(1)

    @pl.when(c == 0)
    def _():
        s_ref[...] = jnp.zeros_like(s_ref)

    a16 = a_ref[0][:, :GLA_RANK].astype(BF16)
    lg_ref[...] = jax.nn.log_sigmoid(
        jnp.dot(a16, wg_ref[...], preferred_element_type=F32) + bg_ref[...]) / GLA_TAU

    ch = GLA_CHUNK
    hs = SUBLANES
    rowi = lax.broadcasted_iota(jnp.int32, (ch, GLA_DK), 0)
    rowh = lax.broadcasted_iota(jnp.int32, (hs, GLA_DK), 0)
    coli = lax.broadcasted_iota(jnp.int32, (hs, ch), 1)
    scale = GLA_DK ** -0.5

    def prep(idx, carry):
        r0 = pl.multiple_of(idx * ch, ch)
        for h in range(GLA_HEADS):
            ks = slice(h * GLA_DK, (h + 1) * GLA_DK)
            vs = slice(h * GLA_DV, (h + 1) * GLA_DV)
            b = lg_ref[pl.ds(r0, ch), ks]
            for sh in (1, 2, 4, 8):
                b = b + jnp.where(rowi >= sh, pltpu.roll(b, sh, axis=0), 0.0)
            qh = q_ref[0, pl.ds(r0, ch), ks] * scale
            kh = k_ref[0, pl.ds(r0, ch), ks]
            halves = []
            for half in range(ch // hs):
                bq = b[half * hs:(half + 1) * hs]
                qq = qh[half * hs:(half + 1) * hs]
                att = jnp.zeros((hs, ch), F32)
                for s in range((half + 1) * hs):
                    diff = bq - b[s:s + 1, :]
                    if s >= half * hs:
                        diff = jnp.where(rowh + half * hs >= s, diff, -jnp.inf)
                    col = jnp.sum(qq * kh[s:s + 1, :] * jnp.exp(diff), axis=1, keepdims=True)
                    att = jnp.where(coli == s, col, att)
                halves.append(att)
            att = jnp.concatenate(halves, axis=0).astype(BF16)
            vh = v_ref[0, pl.ds(r0, ch), vs].astype(BF16)
            o_ref[pl.ds(r0, ch), vs] = jnp.dot(att, vh, preferred_element_type=F32)
            blast = b[ch - 1:ch, :]
            qi_ref[pl.ds(r0, ch), ks] = (qh * jnp.exp(b)).astype(BF16)
            ku_ref[pl.ds(r0, ch), ks] = (kh * jnp.exp(blast - b)).astype(BF16)
            cd_ref[pl.ds(pl.multiple_of(idx * hs, hs), hs), ks] = jnp.broadcast_to(
                jnp.exp(blast), (hs, GLA_DK))
        return carry

    lax.fori_loop(0, tc // ch, prep, 0, unroll=2)

    def scan(idx, carry):
        r0 = pl.multiple_of(idx * ch, ch)
        for h in range(GLA_HEADS):
            ks = slice(h * GLA_DK, (h + 1) * GLA_DK)
            vs = slice(h * GLA_DV, (h + 1) * GLA_DV)
            st = s_ref[h]
            inter = lax.dot_general(qi_ref[pl.ds(r0, ch), ks], st.astype(BF16),
                                    (((1,), (1,)), ((), ())), preferred_element_type=F32)
            o_ref[pl.ds(r0, ch), vs] += inter
            vh = v_ref[0, pl.ds(r0, ch), vs].astype(BF16)
            kv = lax.dot_general(vh, ku_ref[pl.ds(r0, ch), ks], (((0,), (0,)), ((), ())),
                                 preferred_element_type=F32)
            cd = cd_ref[pl.ds(pl.multiple_of(idx * hs, hs), 1), ks]
            s_ref[h] = cd * st + kv
        return carry

    lax.fori_loop(0, tc // ch, scan, 0, unroll=2)

    for h in range(GLA_HEADS):
        vs = slice(h * GLA_DV, (h + 1) * GLA_DV)
        y_ref[0, :, vs] = _gla_finish(o_ref[:, vs], r_ref[0, :, vs], go_ref[...]).astype(y_ref.dtype)
        st_ref[0, h] = s_ref[h].T


def _gla_prompt(zh3, zt3, wg, bg, go, l, tc):
    nb, t, _ = zh3.shape
    kern = functools.partial(_gla_prompt_kernel, tc=tc)
    return pl.pallas_call(
        kern,
        grid=(nb, t // tc),
        in_specs=[pl.BlockSpec((1, tc, GLA_QK), lambda n, c: (n, c, COL_GQ // GLA_QK)),
                  pl.BlockSpec((1, tc, GLA_QK), lambda n, c: (n, c, COL_GK // GLA_QK)),
                  pl.BlockSpec((1, tc, GLA_V), lambda n, c: (n, c, COL_GV // GLA_V)),
                  pl.BlockSpec((1, tc, LANES), lambda n, c: (n, c, COL_GA // LANES)),
                  pl.BlockSpec((1, tc, GLA_V), lambda n, c: (n, c, TCOL_R // GLA_V)),
                  _lspec(l, (GLA_RANK, GLA_QK), lambda n, c: (0, 0)),
                  _lspec(l, (1, GLA_QK), lambda n, c: (0, 0)),
                  _lspec(l, (1, GLA_DV), lambda n, c: (0, 0))],
        out_specs=[pl.BlockSpec((1, tc, GLA_V), lambda n, c: (n, c, 0)),
                   pl.BlockSpec((1, GLA_HEADS, GLA_DK, GLA_DV), lambda n, c: (n, 0, 0, 0))],
        out_shape=[jax.ShapeDtypeStruct((nb, t, GLA_V), BF16),
                   jax.ShapeDtypeStruct((nb, GLA_HEADS, GLA_DK, GLA_DV), F32)],
        scratch_shapes=[pltpu.VMEM((GLA_HEADS, GLA_DV, GLA_DK), F32),
                        pltpu.VMEM((tc, GLA_QK), F32),
                        pltpu.VMEM((tc, GLA_V), F32),
                        pltpu.VMEM((tc, GLA_QK), BF16),
                        pltpu.VMEM((tc, GLA_QK), BF16),
                        pltpu.VMEM((tc // GLA_CHUNK * SUBLANES, GLA_QK), F32)],
        compiler_params=_cparams(("parallel", "arbitrary")),
        name="gla_prompt",
    )(zh3, zh3, zh3, zh3, zt3, wg, bg, go)


def _gla_decode_kernel(zh_ref, r_ref, wg_ref, bg_ref, go_ref, s_ref, y_ref, so_ref):
    n = pl.program_id(0)
    row = zh_ref[pl.ds(n, 1), :]
    a16 = row[:, COL_GA:COL_GA + GLA_RANK].astype(BF16)
    lg = jax.nn.log_sigmoid(
        jnp.dot(a16, wg_ref[...], preferred_element_type=F32) + bg_ref[...]) / GLA_TAU
    eg = jnp.exp(lg)
    rr = r_ref[pl.ds(n, 1), :]
    ri = lax.broadcasted_iota(jnp.int32, (GLA_DK, GLA_DK), 0)
    ci = lax.broadcasted_iota(jnp.int32, (GLA_DK, GLA_DK), 1)
    eye = (ri == ci).astype(F32)

    def col(v):
        return jnp.sum(eye * v, axis=1, keepdims=True)

    for h in range(GLA_HEADS):
        ks = slice(h * GLA_DK, (h + 1) * GLA_DK)
        qh = row[:, COL_GQ + h * GLA_DK:COL_GQ + (h + 1) * GLA_DK] * (GLA_DK ** -0.5)
        kh = row[:, COL_GK + h * GLA_DK:COL_GK + (h + 1) * GLA_DK]
        vh = row[:, COL_GV + h * GLA_DV:COL_GV + (h + 1) * GLA_DV]
        s = col(eg[:, ks]) * s_ref[0, h] + col(kh) * vh
        so_ref[0, h] = s
        o = jnp.sum(col(qh) * s, axis=0, keepdims=True)
        vs = slice(h * GLA_DV, (h + 1) * GLA_DV)
        y_ref[0, :, vs] = _gla_finish(o, rr[:, vs], go_ref[...]).astype(y_ref.dtype)


def _gla_decode(zh, zt, wg, bg, go, s0, l):
    nb = zh.shape[0]
    return pl.pallas_call(
        _gla_decode_kernel,
        grid=(nb,),
        in_specs=[pl.BlockSpec((nb, HEAD_COLS), lambda n: (0, 0)),
                  pl.BlockSpec((nb, GLA_V), lambda n: (0, TCOL_R // GLA_V)),
                  _lspec(l, (GLA_RANK, GLA_QK), lambda n: (0, 0)),
                  _lspec(l, (1, GLA_QK), lambda n: (0, 0)),
                  _lspec(l, (1, GLA_DV), lambda n: (0, 0)),
                  _lspec(l, (1, GLA_HEADS, GLA_DK, GLA_DV), lambda n: (n, 0, 0, 0))],
        out_specs=[pl.BlockSpec((1, 1, GLA_V), lambda n: (n, 0, 0)),
                   pl.BlockSpec((1, GLA_HEADS, GLA_DK, GLA_DV), lambda n: (n, 0, 0, 0))],
        out_shape=[jax.ShapeDtypeStruct((nb, 1, GLA_V), BF16),
                   jax.ShapeDtypeStruct(s0.shape[1:], F32)],
        compiler_params=_cparams(("arbitrary",)),
        name="gla_decode",
    )(zh, zt, wg, bg, go, s0)


def _bucket_table():
    dist = (np.arange(WINDOW)[:, None] + WINDOW) - np.arange(2 * WINDOW)[None, :]
    max_exact = N_BUCKETS // 2
    d = np.maximum(dist, 0)
    ratio = np.log(np.maximum(d, 1).astype(np.float32) / np.float32(max_exact)) / np.float32(
        math.log(MAX_DISTANCE / max_exact))
    large = max_exact + (ratio.astype(np.float32) * np.float32(N_BUCKETS - max_exact)).astype(np.int32)
    large = np.minimum(large, N_BUCKETS - 1)
    return np.where(d < max_exact, d, large).astype(np.int32)


def _bias_kernel(bk_ref, rb_ref, o_ref):
    h = pl.program_id(0)
    bk = bk_ref[...]
    acc = jnp.zeros(bk.shape, F32)
    for b in range(N_BUCKETS):
        acc = jnp.where(bk == b, rb_ref[b, h], acc)
    o_ref[0] = acc


def _bias_table(rel_bias):
    bk = jnp.asarray(_bucket_table())
    return pl.pallas_call(
        _bias_kernel,
        grid=(SWA_QH,),
        in_specs=[pl.BlockSpec((WINDOW, 2 * WINDOW), lambda h: (0, 0)),
                  pl.BlockSpec(memory_space=pltpu.SMEM)],
        out_specs=pl.BlockSpec((1, WINDOW, 2 * WINDOW), lambda h: (h, 0, 0)),
        out_shape=jax.ShapeDtypeStruct((SWA_QH, WINDOW, 2 * WINDOW), F32),
        compiler_params=_cparams(("parallel",)),
        name="swa_bias",
    )(bk, rel_bias)


def _sink_attend(s, sink, v):
    m = jnp.maximum(jnp.max(s, axis=-1, keepdims=True), sink)
    p = jnp.exp(s - m)
    den = jnp.sum(p, axis=-1, keepdims=True) + jnp.exp(sink - m)
    return jnp.dot((p / den).astype(BF16), v, preferred_element_type=F32)


def _swa_prompt_kernel(q_ref, kp_ref, kc_ref, vp_ref, vc_ref, bias_ref, sink_ref, y_ref, *, t, l):
    i = pl.program_id(1)
    w = WINDOW
    rows = lax.broadcasted_iota(jnp.int32, (w, 2 * w), 0)
    cols = lax.broadcasted_iota(jnp.int32, (w, 2 * w), 1)
    dist = w + rows - cols
    kidx = (i - 1) * w + cols
    mask = (dist >= 0) & (dist < w) & (kidx >= 0)
    krow = (i - 1) * w + lax.broadcasted_iota(jnp.int32, (2 * w, 1), 0)
    kk = jnp.concatenate([kp_ref[0], kc_ref[0]], axis=0)
    vv = jnp.concatenate([vp_ref[0], vc_ref[0]], axis=0)
    vv = jnp.where(krow < t, vv, 0.0)
    q = q_ref[0]
    scale = SWA_HD ** -0.5
    for kv in range(SWA_KVH):
        hs = slice(kv * SWA_HD, (kv + 1) * SWA_HD)
        kh = kk[:, hs].astype(BF16)
        vh = vv[:, hs].astype(BF16)
        for g in range(SWA_GRP):
            h = kv * SWA_GRP + g
            qh = q[:, h * SWA_HD:(h + 1) * SWA_HD].astype(BF16)
            s = lax.dot_general(qh, kh, (((1,), (1,)), ((), ())), preferred_element_type=F32)
            s = jnp.where(mask, s * scale + bias_ref[h], -jnp.inf)
            o = _sink_attend(s, sink_ref[l, h], vh)
            y_ref[0, :, h * SWA_HD:(h + 1) * SWA_HD] = o.astype(y_ref.dtype)


def _swa_prompt(zt3, bias, sinks, l):
    nb, t, _ = zt3.shape
    w = WINDOW
    nblk = pl.cdiv(t, w)
    kern = functools.partial(_swa_prompt_kernel, t=t, l=l)
    kcol = TCOL_SK // SWA_KV
    vcol = TCOL_SV // SWA_KV
    prev = lambda i: jnp.maximum(i - 1, 0)
    return pl.pallas_call(
        kern,
        grid=(nb, nblk),
        in_specs=[pl.BlockSpec((1, w, SWA_Q), lambda n, i: (n, i, TCOL_SQ // SWA_Q)),
                  pl.BlockSpec((1, w, SWA_KV), lambda n, i: (n, prev(i), kcol)),
                  pl.BlockSpec((1, w, SWA_KV), lambda n, i: (n, i, kcol)),
                  pl.BlockSpec((1, w, SWA_KV), lambda n, i: (n, prev(i), vcol)),
                  pl.BlockSpec((1, w, SWA_KV), lambda n, i: (n, i, vcol)),
                  pl.BlockSpec((SWA_QH, w, 2 * w), lambda n, i: (0, 0, 0)),
                  pl.BlockSpec(memory_space=pltpu.SMEM)],
        out_specs=pl.BlockSpec((1, w, SWA_Q), lambda n, i: (n, i, 0)),
        out_shape=jax.ShapeDtypeStruct((nb, t, SWA_Q), BF16),
        compiler_params=_cparams(("parallel", "arbitrary")),
        name="swa_prompt",
    )(zt3, zt3, zt3, zt3, zt3, bias, sinks)


def _swa_decode_kernel(q_ref, kv_ref, kb_ref, vb_ref, bias_ref, sink_ref, y_ref, ko_ref, vo_ref, *, nstep):
    n0 = pl.program_id(0) * nstep
    scale = SWA_HD ** -0.5
    for j in range(nstep):
        qrow = q_ref[pl.ds(n0 + j, 1), :]
        kvrow = kv_ref[pl.ds(n0 + j, 1), :]
        kn = jnp.concatenate([kb_ref[j, 1:, :], kvrow[:, :SWA_KV]], axis=0)
        vn = jnp.concatenate([vb_ref[j, 1:, :], kvrow[:, SWA_KV:]], axis=0)
        ko_ref[j] = kn
        vo_ref[j] = vn
        for kv in range(SWA_KVH):
            hs = slice(kv * SWA_HD, (kv + 1) * SWA_HD)
            kh = kn[:, hs].astype(BF16)
            vh = vn[:, hs].astype(BF16)
            q4 = jnp.concatenate(
                [qrow[:, (kv * SWA_GRP + g) * SWA_HD:(kv * SWA_GRP + g + 1) * SWA_HD]
                 for g in range(SWA_GRP)], axis=0).astype(BF16)
            s = lax.dot_general(q4, kh, (((1,), (1,)), ((), ())), preferred_element_type=F32)
            s = s * scale + bias_ref[kv]
            o = _sink_attend(s, sink_ref[kv * SWA_GRP:(kv + 1) * SWA_GRP, :], vh)
            for g in range(SWA_GRP):
                h = kv * SWA_GRP + g
                y_ref[j, :, h * SWA_HD:(h + 1) * SWA_HD] = o[g:g + 1, :].astype(y_ref.dtype)


def _swa_decode(zt, kbuf, vbuf, bias_dec, sinks, l, nstep=8):
    nb = zt.shape[0]
    w = WINDOW
    assert nb % nstep == 0
    kern = functools.partial(_swa_decode_kernel, nstep=nstep)
    return pl.pallas_call(
        kern,
        grid=(nb // nstep,),
        in_specs=[pl.BlockSpec((nb, SWA_Q), lambda n: (0, TCOL_SQ // SWA_Q)),
                  pl.BlockSpec((nb, 2 * SWA_KV), lambda n: (0, TCOL_SK // (2 * SWA_KV))),
                  _lspec(l, (nstep, w, SWA_KV), lambda n: (n, 0, 0)),
                  _lspec(l, (nstep, w, SWA_KV), lambda n: (n, 0, 0)),
                  pl.BlockSpec((SWA_KVH, SWA_GRP, w), lambda n: (0, 0, 0)),
                  _lspec(l, (SWA_QH, 1), lambda n: (0, 0))],
        out_specs=[pl.BlockSpec((nstep, 1, SWA_Q), lambda n: (n, 0, 0)),
                   pl.BlockSpec((nstep, w, SWA_KV), lambda n: (n, 0, 0)),
                   pl.BlockSpec((nstep, w, SWA_KV), lambda n: (n, 0, 0))],
        out_shape=[jax.ShapeDtypeStruct((nb, 1, SWA_Q), BF16),
                   jax.ShapeDtypeStruct(kbuf.shape[1:], F32), jax.ShapeDtypeStruct(vbuf.shape[1:], F32)],
        compiler_params=_cparams(("arbitrary",)),
        name="swa_decode",
    )(zt, zt, kbuf, vbuf, bias_dec, sinks)


def _seq_chunk(t):
    for c in (688, 512, 256, 128, 64, 32, 16):
        if t % c == 0:
            return c
    raise ValueError(f"sequence length {t} must be a multiple of {GLA_CHUNK}")


def kernel(x_prompt, x_sample, state_s5_re, state_s5_im, state_gla, cache_swa_k, cache_swa_v, meta_tokens, rel_bias, norm_pre_mix, norm_post_mix, norm_pre_ffn, norm_post_ffn, w_in, s5_lam_re, s5_lam_im, s5_log_step, s5_b_re, s5_b_im, s5_c_re, s5_c_im, s5_d, s5_w_glu, s5_b_glu, gla_w_gate2, gla_b_gate2, gla_g_out, swa_sinks, w_up_s5, w_up_gla, w_up_swa, w_out, w_ff1, w_ff2):
    depth = w_in.shape[0]
    nb, seq, d = x_prompt.shape
    t = seq + N_META
    nd = x_sample.shape[0]
    assert x_sample.shape[1] == 1 and cache_swa_k.shape[2] == WINDOW and t % GLA_CHUNK == 0
    tc = _seq_chunk(t)

    w_tail = w_in[:, :, TAIL_START:].astype(BF16)
    w_glu = s5_w_glu.astype(BF16)
    wg2 = gla_w_gate2.astype(BF16)
    wu_a, wu_b, wu_c = w_up_s5.astype(BF16), w_up_gla.astype(BF16), w_up_swa.astype(BF16)
    wo, w1, w2 = w_out.astype(BF16), w_ff1.astype(BF16), w_ff2.astype(BF16)
    row3 = lambda a: a.reshape(depth, 1, a.shape[-1])
    g_pre, g_post, g_pre_f, g_post_f = (row3(a) for a in (norm_pre_mix, norm_post_mix,
                                                           norm_pre_ffn, norm_post_ffn))
    b_glu, d5, bg2, go = row3(s5_b_glu), row3(s5_d), row3(gla_b_gate2), row3(gla_g_out)
    sinks3 = swa_sinks.reshape(depth, SWA_QH, 1)
    s5r = state_s5_re.reshape(depth, nd, -1)
    s5i = state_s5_im.reshape(depth, nd, -1)
    ckb = cache_swa_k.reshape(depth, nd, WINDOW, SWA_KV)
    cvb = cache_swa_v.reshape(depth, nd, WINDOW, SWA_KV)

    ar, ai, bbr_t, bbi_t = _s5_discretize(s5_lam_re, s5_lam_im, s5_log_step,
                                          jnp.swapaxes(s5_b_re, -1, -2), jnp.swapaxes(s5_b_im, -1, -2))
    bmat, cmat = _s5_matrices(bbr_t, bbi_t, s5_c_re, s5_c_im)
    a_half = jnp.concatenate([ar.reshape(depth, S5_NGB, 2, S5_HW), ai.reshape(depth, S5_NGB, 2, S5_HW)],
                             axis=-1)
    arow = jnp.tile(a_half, (1, 1, SUBLANES // 2, 1))
    adec = jnp.stack([ar.reshape(depth, -1), ai.reshape(depth, -1)], axis=1)

    bias = _bias_table(rel_bias)
    bias_dec = bias[:, WINDOW - 1, WINDOW:].reshape(SWA_KVH, SWA_GRP, WINDOW)

    meta = jnp.broadcast_to(meta_tokens.astype(x_prompt.dtype)[None], (nb, N_META, d))
    xp = jnp.concatenate([meta, x_prompt], axis=1).reshape(nb * t, d)
    xs = x_sample.reshape(nd, d)

    outs = {k: [] for k in ('p_re', 'p_im', 'p_gla', 'p_k', 'p_v', 's_re', 's_im', 's_gla', 's_k', 's_v')}
    for l in range(depth):
        zh = _norm_mm(xp, g_pre, w_in, l, HEAD_COLS, 640)
        zt = _norm_mm(xp, g_pre, w_tail, l, TAIL_COLS, 512)
        zh3 = zh.reshape(nb, t, HEAD_COLS)
        zt3 = zt.reshape(nb, t, TAIL_COLS)
        z5, st5 = _s5_prompt(zh3, bmat, cmat, arow, d5, l, tc)
        ya = _glu(z5.reshape(nb * t, S5_WIDTH), w_glu, b_glu, l)
        yb, pg = _gla_prompt(zh3, zt3, wg2, bg2, go, l, tc)
        yc = _swa_prompt(zt3, bias, swa_sinks, l)
        xp = _merge_out(ya, yb.reshape(nb * t, GLA_V), yc.reshape(nb * t, SWA_Q), zt,
                        wu_a, wu_b, wu_c, wo, xp, g_post, l)
        xp = _ffn(xp, g_pre_f, w1, w2, g_post_f, l)
        st5 = st5.reshape(S5_NGB, nb, 2, 2, S5_HALF, S5_STATE).transpose(3, 1, 0, 2, 4, 5)
        st5 = st5.reshape(2, nb, S5_GROUPS, S5_STATE)
        outs['p_re'].append(st5[0])
        outs['p_im'].append(st5[1])
        outs['p_gla'].append(pg)
        outs['p_k'].append(zt3[:, t - WINDOW:, TCOL_SK:TCOL_SK + SWA_KV].reshape(nb, WINDOW, SWA_KVH, SWA_HD))
        outs['p_v'].append(zt3[:, t - WINDOW:, TCOL_SV:TCOL_SV + SWA_KV].reshape(nb, WINDOW, SWA_KVH, SWA_HD))

        zh = _norm_mm(xs, g_pre, w_in, l, HEAD_COLS, 640)
        zt = _norm_mm(xs, g_pre, w_tail, l, TAIL_COLS, 512)
        z5, nr, ni = _s5_decode(zh, bmat, cmat, adec, d5, s5r, s5i, l)
        ya = _glu(z5, w_glu, b_glu, l)
        yb, sg = _gla_decode(zh, zt, wg2, bg2, go, state_gla, l)
        yc, sk, sv = _swa_decode(zt, ckb, cvb, bias_dec, sinks3, l)
        xs = _merge_out(ya, yb.reshape(nd, GLA_V), yc.reshape(nd, SWA_Q), zt,
                        wu_a, wu_b, wu_c, wo, xs, g_post, l)
        xs = _ffn(xs, g_pre_f, w1, w2, g_post_f, l)
        outs['s_re'].append(nr.reshape(nd, S5_GROUPS, S5_STATE))
        outs['s_im'].append(ni.reshape(nd, S5_GROUPS, S5_STATE))
        outs['s_gla'].append(sg)
        outs['s_k'].append(sk.reshape(nd, WINDOW, SWA_KVH, SWA_HD))
        outs['s_v'].append(sv.reshape(nd, WINDOW, SWA_KVH, SWA_HD))

    y_prompt = xp.reshape(nb, t, d)[:, N_META:]
    y_sample = xs.reshape(nd, 1, d)
    st = {k: jnp.stack(v) for k, v in outs.items()}
    return (y_prompt, y_sample, st['p_re'], st['p_im'], st['p_gla'], st['p_k'], st['p_v'],
            st['s_re'], st['s_im'], st['s_gla'], st['s_k'], st['s_v'])
```

```python
import functools
import math

import numpy as np
import jax
import jax.numpy as jnp
from jax import lax
from jax.experimental import pallas as pl
from jax.experimental.pallas import tpu as pltpu

F32 = jnp.float32
BF16 = jnp.bfloat16

D_MODEL = 2048
N_META = 16
S5_GROUP = 16
S5_WIDTH = D_MODEL // 2
S5_GROUPS = S5_WIDTH // S5_GROUP
S5_STATE = 64
GLA_HEADS = 4
GLA_DK = D_MODEL // 16
GLA_DV = D_MODEL // 8
GLA_RANK = 16
GLA_TAU = 16.0
GLA_CHUNK = 16
SWA_QH = 16
SWA_KVH = 4
SWA_GRP = SWA_QH // SWA_KVH
SWA_HD = 64
WINDOW = 128
N_BUCKETS = 32
MAX_DISTANCE = WINDOW
D_FF = 4 * D_MODEL
N_BRANCH = 3
EPS = 1e-6

GLA_QK = GLA_HEADS * GLA_DK
GLA_V = GLA_HEADS * GLA_DV
SWA_Q = SWA_QH * SWA_HD
SWA_KV = SWA_KVH * SWA_HD

COL_U = 0
COL_GQ = COL_U + S5_WIDTH
COL_GK = COL_GQ + GLA_QK
COL_GV = COL_GK + GLA_QK
COL_GA = COL_GV + GLA_V
LANES = 128
SUBLANES = 8
HEAD_COLS = COL_GA + LANES
TAIL_START = COL_GA + GLA_RANK
TCOL_R = 0
TCOL_SQ = TCOL_R + GLA_V
TCOL_SK = TCOL_SQ + SWA_Q
TCOL_SV = TCOL_SK + SWA_KV
TCOL_GATE = TCOL_SV + SWA_KV
TAIL_COLS = TCOL_GATE + N_BRANCH * D_MODEL

S5_GB = 8
S5_NGB = S5_GROUPS // S5_GB
S5_HALF = S5_GB // 2
S5_HW = S5_HALF * S5_STATE

VMEM_LIMIT = 60 * 1024 * 1024


def _cparams(sem):
    return pltpu.CompilerParams(dimension_semantics=sem, vmem_limit_bytes=VMEM_LIMIT)


def _rms(x, g):
    ms = jnp.mean(x * x, axis=-1, keepdims=True)
    return (x * lax.rsqrt(ms + EPS)) * g


def _row_tile(m):
    for t in (688, 512, 256, 128, 64, 32, 16, 8):
        if m % t == 0:
            return t
    return m


def _lspec(l, shape, imap):
    return pl.BlockSpec((None,) + tuple(shape), lambda *idx: (l,) + tuple(imap(*idx)))


def _norm_mm_kernel(x_ref, g_ref, w_ref, o_ref, h_ref):
    @pl.when(pl.program_id(1) == 0)
    def _():
        h_ref[...] = _rms(x_ref[...], g_ref[...]).astype(BF16)

    o_ref[...] = lax.dot_general(h_ref[...], w_ref[0], (((1,), (1,)), ((), ())),
                                 preferred_element_type=F32)


def _norm_mm(x, g, wt, l, col0, n_cols, tn):
    m, k = x.shape
    tm = _row_tile(m)
    return pl.pallas_call(
        _norm_mm_kernel,
        grid=(m // tm, n_cols // tn),
        in_specs=[pl.BlockSpec((tm, k), lambda i, j: (i, 0)),
                  _lspec(l, (1, k), lambda i, j: (0, 0)),
                  pl.BlockSpec((pl.Element(1), pl.Element(tn), pl.Element(k)),
                               lambda i, j: (l, pl.multiple_of(col0 + j * tn, 16), 0))],
        out_specs=pl.BlockSpec((tm, tn), lambda i, j: (i, j)),
        out_shape=jax.ShapeDtypeStruct((m, n_cols), F32),
        scratch_shapes=[pltpu.VMEM((tm, k), BF16)],
        compiler_params=_cparams(("parallel", "arbitrary")),
        name="norm_mm",
    )(x, g, wt)


def _glu_kernel(z_ref, zc_ref, w_ref, b_ref, o_ref, h_ref):
    @pl.when(pl.program_id(1) == 0)
    def _():
        h_ref[...] = z_ref[...].astype(BF16)

    a = jnp.dot(h_ref[...], w_ref[...], preferred_element_type=F32) + b_ref[...]
    o_ref[...] = (zc_ref[...] * jax.nn.sigmoid(a)).astype(o_ref.dtype)


def _glu(z, w, b, l, tn=512):
    m, k = z.shape
    tm = _row_tile(m)
    return pl.pallas_call(
        _glu_kernel,
        grid=(m // tm, k // tn),
        in_specs=[pl.BlockSpec((tm, k), lambda i, j: (i, 0)),
                  pl.BlockSpec((tm, tn), lambda i, j: (i, j)),
                  _lspec(l, (k, tn), lambda i, j: (0, j)),
                  _lspec(l, (1, tn), lambda i, j: (0, j))],
        out_specs=pl.BlockSpec((tm, tn), lambda i, j: (i, j)),
        out_shape=jax.ShapeDtypeStruct((m, k), BF16),
        scratch_shapes=[pltpu.VMEM((tm, k), BF16)],
        compiler_params=_cparams(("parallel", "arbitrary")),
        name="s5_glu",
    )(z, z, w, b)


def _merge_out_kernel(ya_ref, yb_ref, yc_ref, ga_ref, gb_ref, gc_ref, wa_ref, wb_ref, wc_ref,
                      wo_ref, x_ref, g_ref, o_ref, m_ref, *, nk, tk, tn):
    s = pl.program_id(1)

    @pl.when(s < nk)
    def _():
        m = (jax.nn.sigmoid(ga_ref[...]) * jnp.dot(ya_ref[...], wa_ref[...], preferred_element_type=F32)
             + jax.nn.sigmoid(gb_ref[...]) * jnp.dot(yb_ref[...], wb_ref[...], preferred_element_type=F32)
             + jax.nn.sigmoid(gc_ref[...]) * jnp.dot(yc_ref[...], wc_ref[...], preferred_element_type=F32))
        m_ref[:, pl.ds(pl.multiple_of(s * tk, tk), tk)] = m.astype(BF16)

    @pl.when(s >= nk)
    def _():
        c0 = pl.multiple_of((s - nk) * tn, tn)
        o_ref[:, pl.ds(c0, tn)] = jnp.dot(m_ref[...], wo_ref[...], preferred_element_type=F32)

    @pl.when(s == pl.num_programs(1) - 1)
    def _():
        o_ref[...] = x_ref[...] + _rms(o_ref[...], g_ref[...])


def _merge_out(ya, yb, yc, zt, wa, wb, wc, wo, x, g, l, tk=256, tn=512):
    m, d = x.shape
    tm = _row_tile(m)
    kb = ya.shape[1]
    gate0 = TCOL_GATE // tk
    nk = d // tk
    kidx = lambda s: jnp.minimum(s, nk - 1)
    nidx = lambda s: jnp.maximum(s - nk, 0)
    gate_spec = lambda br: pl.BlockSpec((tm, tk), lambda i, s: (i, gate0 + br * nk + kidx(s)))
    y_spec = pl.BlockSpec((tm, kb), lambda i, s: (i, 0))
    w_spec = _lspec(l, (kb, tk), lambda i, s: (0, kidx(s)))
    kern = functools.partial(_merge_out_kernel, nk=nk, tk=tk, tn=tn)
    return pl.pallas_call(
        kern,
        grid=(m // tm, nk + d // tn),
        in_specs=[y_spec, y_spec, y_spec, gate_spec(0), gate_spec(1), gate_spec(2),
                  w_spec, w_spec, w_spec,
                  _lspec(l, (d, tn), lambda i, s: (0, nidx(s))),
                  pl.BlockSpec((tm, d), lambda i, s: (i, 0)),
                  _lspec(l, (1, d), lambda i, s: (0, 0))],
        out_specs=pl.BlockSpec((tm, d), lambda i, s: (i, 0)),
        out_shape=jax.ShapeDtypeStruct((m, d), F32),
        scratch_shapes=[pltpu.VMEM((tm, d), BF16)],
        compiler_params=_cparams(("parallel", "arbitrary")),
        name="merge_out",
    )(ya, yb, yc, zt, zt, zt, wa, wb, wc, wo, x, g)


def _ffn_kernel(x_ref, g1_ref, w1_ref, w2_ref, g2_ref, o_ref, h_ref, acc_ref):
    f = pl.program_id(1)

    @pl.when(f == 0)
    def _():
        h_ref[...] = _rms(x_ref[...], g1_ref[...]).astype(BF16)
        acc_ref[...] = jnp.zeros_like(acc_ref)

    a = jnp.dot(h_ref[...], w1_ref[...], preferred_element_type=F32)
    a = jnp.square(jnp.maximum(a, 0.0)).astype(BF16)
    acc_ref[...] += jnp.dot(a, w2_ref[...], preferred_element_type=F32)

    @pl.when(f == pl.num_programs(1) - 1)
    def _():
        o_ref[...] = x_ref[...] + _rms(acc_ref[...], g2_ref[...])


def _ffn(x, g1, w1, w2, g2, l, tf=512):
    m, d = x.shape
    tm = _row_tile(m)
    ff = w1.shape[2]
    return pl.pallas_call(
        _ffn_kernel,
        grid=(m // tm, ff // tf),
        in_specs=[pl.BlockSpec((tm, d), lambda i, f: (i, 0)),
                  _lspec(l, (1, d), lambda i, f: (0, 0)),
                  _lspec(l, (d, tf), lambda i, f: (0, f)),
                  _lspec(l, (tf, d), lambda i, f: (f, 0)),
                  _lspec(l, (1, d), lambda i, f: (0, 0))],
        out_specs=pl.BlockSpec((tm, d), lambda i, f: (i, 0)),
        out_shape=jax.ShapeDtypeStruct((m, d), F32),
        scratch_shapes=[pltpu.VMEM((tm, d), BF16), pltpu.VMEM((tm, d), F32)],
        compiler_params=_cparams(("parallel", "arbitrary")),
        name="ffn",
    )(x, g1, w1, w2, g2)


def _s5_disc_kernel(lr_ref, li_ref, ls_ref, br_ref, bi_ref, ar_ref, ai_ref, bbr_ref, bbi_ref):
    lr = lr_ref[0]
    li = li_ref[0]
    dt = jnp.exp(ls_ref[0])
    mag = jnp.exp(lr * dt)
    ar = mag * jnp.cos(li * dt)
    ai = mag * jnp.sin(li * dt)
    den = lr * lr + li * li
    fr = ((ar - 1.0) * lr + ai * li) / den
    fi = (ai * lr - (ar - 1.0) * li) / den
    ar_ref[0] = ar
    ai_ref[0] = ai
    br = br_ref[0]
    bi = bi_ref[0]
    bbr_ref[0] = fr[:, None, :] * br - fi[:, None, :] * bi
    bbi_ref[0] = fr[:, None, :] * bi + fi[:, None, :] * br


def _s5_discretize(lam_re, lam_im, log_step, b_re_t, b_im_t):
    depth = lam_re.shape[0]
    gp = pl.BlockSpec((1, S5_GROUPS, S5_STATE), lambda l: (l, 0, 0))
    bspec = pl.BlockSpec((1, S5_GROUPS, S5_GROUP, S5_STATE), lambda l: (l, 0, 0, 0))
    return pl.pallas_call(
        _s5_disc_kernel,
        grid=(depth,),
        in_specs=[gp, gp, pl.BlockSpec((1, S5_GROUPS, 1), lambda l: (l, 0, 0)), bspec, bspec],
        out_specs=[gp, gp, bspec, bspec],
        out_shape=[jax.ShapeDtypeStruct(lam_re.shape, F32), jax.ShapeDtypeStruct(lam_re.shape, F32),
                   jax.ShapeDtypeStruct(b_re_t.shape, F32), jax.ShapeDtypeStruct(b_re_t.shape, F32)],
        compiler_params=_cparams(("parallel",)),
        name="s5_discretize",
    )(lam_re, lam_im, log_step.reshape(depth, S5_GROUPS, 1), b_re_t, b_im_t)


def _blockdiag(r):
    eye = jnp.eye(S5_HALF, dtype=r.dtype)
    z = jnp.einsum('ab,lgqbhp->lgqahbp', eye, r)
    s = r.shape
    return z.reshape(s[0], s[1], s[2], S5_HALF * s[4], S5_HALF * s[5])


def _s5_matrices(bbr_t, bbi_t, c_re, c_im):
    depth = bbr_t.shape[0]
    shp = (depth, S5_NGB, 2, S5_HALF, S5_GROUP, S5_STATE)
    bre = _blockdiag(bbr_t.reshape(shp))
    bim = _blockdiag(bbi_t.reshape(shp))
    bcat = jnp.concatenate([bre, bim], axis=-1)
    zb = jnp.zeros_like(bcat[:, :, 0])
    bmat = jnp.stack([jnp.concatenate([bcat[:, :, 0], zb], axis=-2),
                      jnp.concatenate([zb, bcat[:, :, 1]], axis=-2)], axis=2)
    cre = jnp.swapaxes(_blockdiag(c_re.reshape(shp)), -1, -2)
    cim = jnp.swapaxes(_blockdiag(c_im.reshape(shp)), -1, -2)
    ccat = jnp.concatenate([cre, -cim], axis=-2)
    zc = jnp.zeros_like(ccat[:, :, 0])
    cmat = jnp.stack([jnp.concatenate([ccat[:, :, 0], zc], axis=-1),
                      jnp.concatenate([zc, ccat[:, :, 1]], axis=-1)], axis=2)
    return bmat.astype(BF16), cmat.astype(BF16)


def _s5_prompt_kernel(u_ref, b_ref, c_ref, a_ref, d_ref, z_ref, st_ref, e_ref, x_ref, *, nb, tc):
    @pl.when(pl.program_id(1) == 0)
    def _():
        x_ref[...] = jnp.zeros_like(x_ref)

    nlb = 2 * S5_HW // LANES
    hlb = nlb // 2
    for n in range(nb):
        ub = u_ref[n].astype(BF16)
        for q in range(2):
            e = jnp.dot(ub, b_ref[0, q], preferred_element_type=F32)
            for j in range(nlb):
                e_ref[j, pl.ds(n * 2 + q, tc, stride=SUBLANES), :] = e[:, j * LANES:(j + 1) * LANES]

    a = a_ref[0]
    ar = [a[:, j * LANES:(j + 1) * LANES] for j in range(hlb)]
    ai = [a[:, (hlb + j) * LANES:(hlb + j + 1) * LANES] for j in range(hlb)]

    def step(t, carry):
        row = pl.multiple_of(t * SUBLANES, SUBLANES)
        new = []
        for j in range(hlb):
            xr, xi = carry[2 * j], carry[2 * j + 1]
            nxr = ar[j] * xr - ai[j] * xi + e_ref[j, pl.ds(row, SUBLANES), :]
            nxi = ar[j] * xi + ai[j] * xr + e_ref[hlb + j, pl.ds(row, SUBLANES), :]
            e_ref[j, pl.ds(row, SUBLANES), :] = nxr
            e_ref[hlb + j, pl.ds(row, SUBLANES), :] = nxi
            new += [nxr, nxi]
        return tuple(new)

    x0 = x_ref[...]
    init = []
    for j in range(hlb):
        init += [x0[:, j * LANES:(j + 1) * LANES], x0[:, (hlb + j) * LANES:(hlb + j + 1) * LANES]]
    fin = lax.fori_loop(0, tc, step, tuple(init), unroll=8)
    for j in range(hlb):
        x_ref[:, j * LANES:(j + 1) * LANES] = fin[2 * j]
        x_ref[:, (hlb + j) * LANES:(hlb + j + 1) * LANES] = fin[2 * j + 1]
    st_ref[0] = x_ref[...]

    for n in range(nb):
        y = d_ref[...] * u_ref[n]
        for q in range(2):
            xs = jnp.concatenate(
                [e_ref[j, pl.ds(n * 2 + q, tc, stride=SUBLANES), :] for j in range(nlb)], axis=1)
            y = y + jnp.dot(xs.astype(BF16), c_ref[0, q], preferred_element_type=F32)
        z_ref[n] = jax.nn.gelu(y)


def _s5_prompt(zh3, bmat, cmat, arow, d, l, tc):
    nb, t, _ = zh3.shape
    assert 2 * nb == SUBLANES, "state rows are (batch, half) pairs filling one sublane tile"
    kern = functools.partial(_s5_prompt_kernel, nb=nb, tc=tc)
    return pl.pallas_call(
        kern,
        grid=(S5_NGB, t // tc),
        in_specs=[pl.BlockSpec((nb, tc, LANES), lambda g, c: (0, c, g)),
                  _lspec(l, (1, 2, LANES, 2 * S5_HW), lambda g, c: (g, 0, 0, 0)),
                  _lspec(l, (1, 2, 2 * S5_HW, LANES), lambda g, c: (g, 0, 0, 0)),
                  _lspec(l, (1, SUBLANES, 2 * S5_HW), lambda g, c: (g, 0, 0)),
                  _lspec(l, (1, LANES), lambda g, c: (0, g))],
        out_specs=[pl.BlockSpec((nb, tc, LANES), lambda g, c: (0, c, g)),
                   pl.BlockSpec((1, SUBLANES, 2 * S5_HW), lambda g, c: (g, 0, 0))],
        out_shape=[jax.ShapeDtypeStruct((nb, t, S5_WIDTH), F32),
                   jax.ShapeDtypeStruct((S5_NGB, SUBLANES, 2 * S5_HW), F32)],
        scratch_shapes=[pltpu.VMEM((2 * S5_HW // LANES, tc * SUBLANES, LANES), F32),
                        pltpu.VMEM((SUBLANES, 2 * S5_HW), F32)],
        compiler_params=_cparams(("parallel", "arbitrary")),
        name="s5_prompt",
    )(zh3, bmat, cmat, arow, d)


def _s5_decode_kernel(u_ref, b_ref, c_ref, a_ref, d_ref, sr_ref, si_ref, z_ref, nr_ref, ni_ref):
    for g in range(S5_NGB):
        u = u_ref[:, g * LANES:(g + 1) * LANES]
        ub = u.astype(BF16)
        y = d_ref[:, g * LANES:(g + 1) * LANES] * u
        for q in range(2):
            c0 = (g * 2 + q) * S5_HW
            e = jnp.dot(ub, b_ref[g, q], preferred_element_type=F32)
            ar = a_ref[0:1, c0:c0 + S5_HW]
            ai = a_ref[1:2, c0:c0 + S5_HW]
            x0r = sr_ref[:, c0:c0 + S5_HW]
            x0i = si_ref[:, c0:c0 + S5_HW]
            nr = ar * x0r - ai * x0i + e[:, :S5_HW]
            ni = ar * x0i + ai * x0r + e[:, S5_HW:]
            nr_ref[:, c0:c0 + S5_HW] = nr
            ni_ref[:, c0:c0 + S5_HW] = ni
            xs = jnp.concatenate([nr, ni], axis=1).astype(BF16)
            y = y + jnp.dot(xs, c_ref[g, q], preferred_element_type=F32)
        z_ref[:, g * LANES:(g + 1) * LANES] = jax.nn.gelu(y)


def _s5_decode(zh, bmat, cmat, adec, d, sr, si, l):
    nb = zh.shape[0]
    ns = S5_GROUPS * S5_STATE
    full = lambda shape: pl.BlockSpec(shape, lambda i: (0,) * len(shape))
    lfull = lambda shape: _lspec(l, shape, lambda i: (0,) * len(shape))
    return pl.pallas_call(
        _s5_decode_kernel,
        grid=(1,),
        in_specs=[full((nb, S5_WIDTH)), lfull(bmat.shape[1:]), lfull(cmat.shape[1:]), lfull((2, ns)),
                  lfull((1, S5_WIDTH)), lfull((nb, ns)), lfull((nb, ns))],
        out_specs=[full((nb, S5_WIDTH)), full((nb, ns)), full((nb, ns))],
        out_shape=[jax.ShapeDtypeStruct((nb, S5_WIDTH), F32),
                   jax.ShapeDtypeStruct((nb, ns), F32), jax.ShapeDtypeStruct((nb, ns), F32)],
        compiler_params=_cparams(("arbitrary",)),
        name="s5_decode",
    )(zh, bmat, cmat, adec, d, sr, si)


def _gla_finish(o, r, g):
    return _rms(o, g) * (r * jax.nn.sigmoid(r))


def _gla_prompt_kernel(q_ref, k_ref, v_ref, a_ref, r_ref, wg_ref, bg_ref, go_ref, y_ref, st_ref,
                       s_ref, lg_ref, o_ref, qi_ref, ku_ref, cd_ref, *, tc):
    c = pl.program_id(1)

    @pl.when(c == 0)
    def _():
        s_ref[...] = jnp.zeros_like(s_ref)

    a16 = a_ref[0][:, :GLA_RANK].astype(BF16)
    lg_ref[...] = jax.nn.log_sigmoid(
        jnp.dot(a16, wg_ref[...], preferred_element_type=F32) + bg_ref[...]) / GLA_TAU

    ch = GLA_CHUNK
    hs = SUBLANES
    rowi = lax.broadcasted_iota(jnp.int32, (ch, GLA_DK), 0)
    rowh = lax.broadcasted_iota(jnp.int32, (hs, GLA_DK), 0)
    coli = lax.broadcasted_iota(jnp.int32, (hs, ch), 1)
    scale = GLA_DK ** -0.5

    def prep(idx, carry):
        r0 = pl.multiple_of(idx * ch, ch)
        for h in range(GLA_HEADS):
            ks = slice(h * GLA_DK, (h + 1) * GLA_DK)
            vs = slice(h * GLA_DV, (h + 1) * GLA_DV)
            b = lg_ref[pl.ds(r0, ch), ks]
            for sh in (1, 2, 4, 8):
                b = b + jnp.where(rowi >= sh, pltpu.roll(b, sh, axis=0), 0.0)
            qh = q_ref[0, pl.ds(r0, ch), ks] * scale
            kh = k_ref[0, pl.ds(r0, ch), ks]
            halves = []
            for half in range(ch // hs):
                bq = b[half * hs:(half + 1) * hs]
                qq = qh[half * hs:(half + 1) * hs]
                att = jnp.zeros((hs, ch), F32)
                for s in range((half + 1) * hs):
                    diff = bq - b[s:s + 1, :]
                    if s >= half * hs:
                        diff = jnp.where(rowh + half * hs >= s, diff, -jnp.inf)
                    col = jnp.sum(qq * kh[s:s + 1, :] * jnp.exp(diff), axis=1, keepdims=True)
                    att = jnp.where(coli == s, col, att)
                halves.append(att)
            att = jnp.concatenate(halves, axis=0).astype(BF16)
            vh = v_ref[0, pl.ds(r0, ch), vs].astype(BF16)
            o_ref[pl.ds(r0, ch), vs] = jnp.dot(att, vh, preferred_element_type=F32)
            blast = b[ch - 1:ch, :]
            qi_ref[pl.ds(r0, ch), ks] = (qh * jnp.exp(b)).astype(BF16)
            ku_ref[pl.ds(r0, ch), ks] = (kh * jnp.exp(blast - b)).astype(BF16)
            cd_ref[pl.ds(pl.multiple_of(idx * hs, hs), hs), ks] = jnp.broadcast_to(
                jnp.exp(blast), (hs, GLA_DK))
        return carry

    lax.fori_loop(0, tc // ch, prep, 0, unroll=2)

    def scan(idx, carry):
        r0 = pl.multiple_of(idx * ch, ch)
        for h in range(GLA_HEADS):
            ks = slice(h * GLA_DK, (h + 1) * GLA_DK)
            vs = slice(h * GLA_DV, (h + 1) * GLA_DV)
            st = s_ref[h]
            inter = lax.dot_general(qi_ref[pl.ds(r0, ch), ks], st.astype(BF16),
                                    (((1,), (1,)), ((), ())), preferred_element_type=F32)
            o_ref[pl.ds(r0, ch), vs] += inter
            vh = v_ref[0, pl.ds(r0, ch), vs].astype(BF16)
            kv = lax.dot_general(vh, ku_ref[pl.ds(r0, ch), ks], (((0,), (0,)), ((), ())),
                                 preferred_element_type=F32)
            cd = cd_ref[pl.ds(pl.multiple_of(idx * hs, hs), 1), ks]
            s_ref[h] = cd * st + kv
        return carry

    lax.fori_loop(0, tc // ch, scan, 0, unroll=2)

    for h in range(GLA_HEADS):
        vs = slice(h * GLA_DV, (h + 1) * GLA_DV)
        y_ref[0, :, vs] = _gla_finish(o_ref[:, vs], r_ref[0, :, vs], go_ref[...]).astype(y_ref.dtype)
        st_ref[0, h] = s_ref[h].T


def _gla_prompt(zh3, zt3, wg, bg, go, l, tc):
    nb, t, _ = zh3.shape
    kern = functools.partial(_gla_prompt_kernel, tc=tc)
    return pl.pallas_call(
        kern,
        grid=(nb, t // tc),
        in_specs=[pl.BlockSpec((1, tc, GLA_QK), lambda n, c: (n, c, COL_GQ // GLA_QK)),
                  pl.BlockSpec((1, tc, GLA_QK), lambda n, c: (n, c, COL_GK // GLA_QK)),
                  pl.BlockSpec((1, tc, GLA_V), lambda n, c: (n, c, COL_GV // GLA_V)),
                  pl.BlockSpec((1, tc, LANES), lambda n, c: (n, c, COL_GA // LANES)),
                  pl.BlockSpec((1, tc, GLA_V), lambda n, c: (n, c, TCOL_R // GLA_V)),
                  _lspec(l, (GLA_RANK, GLA_QK), lambda n, c: (0, 0)),
                  _lspec(l, (1, GLA_QK), lambda n, c: (0, 0)),
                  _lspec(l, (1, GLA_DV), lambda n, c: (0, 0))],
        out_specs=[pl.BlockSpec((1, tc, GLA_V), lambda n, c: (n, c, 0)),
                   pl.BlockSpec((1, GLA_HEADS, GLA_DK, GLA_DV), lambda n, c: (n, 0, 0, 0))],
        out_shape=[jax.ShapeDtypeStruct((nb, t, GLA_V), BF16),
                   jax.ShapeDtypeStruct((nb, GLA_HEADS, GLA_DK, GLA_DV), F32)],
        scratch_shapes=[pltpu.VMEM((GLA_HEADS, GLA_DV, GLA_DK), F32),
                        pltpu.VMEM((tc, GLA_QK), F32),
                        pltpu.VMEM((tc, GLA_V), F32),
                        pltpu.VMEM((tc, GLA_QK), BF16),
                        pltpu.VMEM((tc, GLA_QK), BF16),
                        pltpu.VMEM((tc // GLA_CHUNK * SUBLANES, GLA_QK), F32)],
        compiler_params=_cparams(("parallel", "arbitrary")),
        name="gla_prompt",
    )(zh3, zh3, zh3, zh3, zt3, wg, bg, go)


def _gla_decode_kernel(zh_ref, r_ref, wg_ref, bg_ref, go_ref, s_ref, y_ref, so_ref):
    n = pl.program_id(0)
    row = zh_ref[pl.ds(n, 1), :]
    a16 = row[:, COL_GA:COL_GA + GLA_RANK].astype(BF16)
    lg = jax.nn.log_sigmoid(
        jnp.dot(a16, wg_ref[...], preferred_element_type=F32) + bg_ref[...]) / GLA_TAU
    eg = jnp.exp(lg)
    rr = r_ref[pl.ds(n, 1), :]
    ri = lax.broadcasted_iota(jnp.int32, (GLA_DK, GLA_DK), 0)
    ci = lax.broadcasted_iota(jnp.int32, (GLA_DK, GLA_DK), 1)
    eye = (ri == ci).astype(F32)

    def col(v):
        return jnp.sum(eye * v, axis=1, keepdims=True)

    for h in range(GLA_HEADS):
        ks = slice(h * GLA_DK, (h + 1) * GLA_DK)
        qh = row[:, COL_GQ + h * GLA_DK:COL_GQ + (h + 1) * GLA_DK] * (GLA_DK ** -0.5)
        kh = row[:, COL_GK + h * GLA_DK:COL_GK + (h + 1) * GLA_DK]
        vh = row[:, COL_GV + h * GLA_DV:COL_GV + (h + 1) * GLA_DV]
        s = col(eg[:, ks]) * s_ref[0, h] + col(kh) * vh
        so_ref[0, h] = s
        o = jnp.sum(col(qh) * s, axis=0, keepdims=True)
        vs = slice(h * GLA_DV, (h + 1) * GLA_DV)
        y_ref[0, :, vs] = _gla_finish(o, rr[:, vs], go_ref[...]).astype(y_ref.dtype)


def _gla_decode(zh, zt, wg, bg, go, s0, l):
    nb = zh.shape[0]
    return pl.pallas_call(
        _gla_decode_kernel,
        grid=(nb,),
        in_specs=[pl.BlockSpec((nb, HEAD_COLS), lambda n: (0, 0)),
                  pl.BlockSpec((nb, GLA_V), lambda n: (0, TCOL_R // GLA_V)),
                  _lspec(l, (GLA_RANK, GLA_QK), lambda n: (0, 0)),
                  _lspec(l, (1, GLA_QK), lambda n: (0, 0)),
                  _lspec(l, (1, GLA_DV), lambda n: (0, 0)),
                  _lspec(l, (1, GLA_HEADS, GLA_DK, GLA_DV), lambda n: (n, 0, 0, 0))],
        out_specs=[pl.BlockSpec((1, 1, GLA_V), lambda n: (n, 0, 0)),
                   pl.BlockSpec((1, GLA_HEADS, GLA_DK, GLA_DV), lambda n: (n, 0, 0, 0))],
        out_shape=[jax.ShapeDtypeStruct((nb, 1, GLA_V), BF16),
                   jax.ShapeDtypeStruct(s0.shape[1:], F32)],
        compiler_params=_cparams(("arbitrary",)),
        name="gla_decode",
    )(zh, zt, wg, bg, go, s0)


def _bucket_table():
    dist = (np.arange(WINDOW)[:, None] + WINDOW) - np.arange(2 * WINDOW)[None, :]
    max_exact = N_BUCKETS // 2
    d = np.maximum(dist, 0)
    ratio = np.log(np.maximum(d, 1).astype(np.float32) / np.float32(max_exact)) / np.float32(
        math.log(MAX_DISTANCE / max_exact))
    large = max_exact + (ratio.astype(np.float32) * np.float32(N_BUCKETS - max_exact)).astype(np.int32)
    large = np.minimum(large, N_BUCKETS - 1)
    return np.where(d < max_exact, d, large).astype(np.int32)


def _bias_kernel(bk_ref, rb_ref, o_ref):
    h = pl.program_id(0)
    bk = bk_ref[...]
    acc = jnp.zeros(bk.shape, F32)
    for b in range(N_BUCKETS):
        acc = jnp.where(bk == b, rb_ref[b, h], acc)
    dist = (WINDOW + lax.broadcasted_iota(jnp.int32, bk.shape, 0)
            - lax.broadcasted_iota(jnp.int32, bk.shape, 1))
    o_ref[0] = jnp.where((dist >= 0) & (dist < WINDOW), acc, -jnp.inf)


def _bias_table(rel_bias):
    bk = jnp.asarray(_bucket_table())
    return pl.pallas_call(
        _bias_kernel,
        grid=(SWA_QH,),
        in_specs=[pl.BlockSpec((WINDOW, 2 * WINDOW), lambda h: (0, 0)),
                  pl.BlockSpec(memory_space=pltpu.SMEM)],
        out_specs=pl.BlockSpec((1, WINDOW, 2 * WINDOW), lambda h: (h, 0, 0)),
        out_shape=jax.ShapeDtypeStruct((SWA_QH, WINDOW, 2 * WINDOW), F32),
        compiler_params=_cparams(("parallel",)),
        name="swa_bias",
    )(bk, rel_bias)


def _sink_attend(s, sink, v):
    m = jnp.maximum(jnp.max(s, axis=-1, keepdims=True), sink)
    p = jnp.exp(s - m)
    den = jnp.sum(p, axis=-1, keepdims=True) + jnp.exp(sink - m)
    return jnp.dot((p / den).astype(BF16), v, preferred_element_type=F32)


def _swa_prompt_kernel(q_ref, kp_ref, kc_ref, vp_ref, vc_ref, bias_ref, sink_ref, y_ref, *, t, l):
    i = pl.program_id(1)
    w = WINDOW
    cols = lax.broadcasted_iota(jnp.int32, (w, 2 * w), 1)
    valid = (cols >= w) | (i > 0)
    inseq = i * w + lax.broadcasted_iota(jnp.int32, (w, 1), 0) < t
    scale = SWA_HD ** -0.5
    for kv in range(SWA_KVH):
        hs = slice(kv * SWA_HD, (kv + 1) * SWA_HD)
        kh = jnp.concatenate([kp_ref[0, :, hs], jnp.where(inseq, kc_ref[0, :, hs], 0.0)],
                             axis=0).astype(BF16)
        vh = jnp.concatenate([vp_ref[0, :, hs], jnp.where(inseq, vc_ref[0, :, hs], 0.0)],
                             axis=0).astype(BF16)
        for g in range(SWA_GRP):
            h = kv * SWA_GRP + g
            qh = q_ref[0, :, h * SWA_HD:(h + 1) * SWA_HD].astype(BF16)
            s = lax.dot_general(qh, kh, (((1,), (1,)), ((), ())), preferred_element_type=F32)
            s = jnp.where(valid, s * scale + bias_ref[h], -jnp.inf)
            sink = sink_ref[l, h]
            m = jnp.maximum(jnp.max(s, axis=-1, keepdims=True), sink)
            p = jnp.exp(s - m)
            den = jnp.sum(p, axis=-1, keepdims=True) + jnp.exp(sink - m)
            o = jnp.dot(p.astype(BF16), vh, preferred_element_type=F32) * (1.0 / den)
            y_ref[0, :, h * SWA_HD:(h + 1) * SWA_HD] = o.astype(y_ref.dtype)


def _swa_prompt(zt3, bias, sinks, l):
    nb, t, _ = zt3.shape
    w = WINDOW
    nblk = pl.cdiv(t, w)
    kern = functools.partial(_swa_prompt_kernel, t=t, l=l)
    kcol = TCOL_SK // SWA_KV
    vcol = TCOL_SV // SWA_KV
    prev = lambda i: jnp.maximum(i - 1, 0)
    return pl.pallas_call(
        kern,
        grid=(nb, nblk),
        in_specs=[pl.BlockSpec((1, w, SWA_Q), lambda n, i: (n, i, TCOL_SQ // SWA_Q)),
                  pl.BlockSpec((1, w, SWA_KV), lambda n, i: (n, prev(i), kcol)),
                  pl.BlockSpec((1, w, SWA_KV), lambda n, i: (n, i, kcol)),
                  pl.BlockSpec((1, w, SWA_KV), lambda n, i: (n, prev(i), vcol)),
                  pl.BlockSpec((1, w, SWA_KV), lambda n, i: (n, i, vcol)),
                  pl.BlockSpec((SWA_QH, w, 2 * w), lambda n, i: (0, 0, 0)),
                  pl.BlockSpec(memory_space=pltpu.SMEM)],
        out_specs=pl.BlockSpec((1, w, SWA_Q), lambda n, i: (n, i, 0)),
        out_shape=jax.ShapeDtypeStruct((nb, t, SWA_Q), BF16),
        compiler_params=_cparams(("parallel", "arbitrary")),
        name="swa_prompt",
    )(zt3, zt3, zt3, zt3, zt3, bias, sinks)


def _swa_decode_kernel(q_ref, kv_ref, kb_ref, vb_ref, bias_ref, sink_ref, y_ref, ko_ref, vo_ref, *, nstep):
    n0 = pl.program_id(0) * nstep
    scale = SWA_HD ** -0.5
    w = WINDOW
    ri = lax.broadcasted_iota(jnp.int32, (SWA_HD, SWA_HD), 0)
    ci = lax.broadcasted_iota(jnp.int32, (SWA_HD, SWA_HD), 1)
    eye = (ri == ci).astype(F32)
    last = lax.broadcasted_iota(jnp.int32, (SWA_HD, w), 1) == w - 1

    def push(cache_t, new_row):
        col = jnp.sum(eye * new_row, axis=1, keepdims=True)
        return jnp.where(last, col, pltpu.roll(cache_t, w - 1, axis=1))

    for j in range(nstep):
        qrow = q_ref[pl.ds(n0 + j, 1), :]
        kvrow = kv_ref[pl.ds(n0 + j, 1), :]
        for kv in range(SWA_KVH):
            hs = slice(kv * SWA_HD, (kv + 1) * SWA_HD)
            kt = push(kb_ref[j, kv], kvrow[:, hs])
            vt = push(vb_ref[j, kv], kvrow[:, SWA_KV + kv * SWA_HD:SWA_KV + (kv + 1) * SWA_HD])
            ko_ref[j, kv] = kt
            vo_ref[j, kv] = vt
            q4 = jnp.concatenate(
                [qrow[:, (kv * SWA_GRP + g) * SWA_HD:(kv * SWA_GRP + g + 1) * SWA_HD]
                 for g in range(SWA_GRP)], axis=0).astype(BF16)
            s = jnp.dot(q4, kt.astype(BF16), preferred_element_type=F32) * scale + bias_ref[kv]
            sink = sink_ref[kv * SWA_GRP:(kv + 1) * SWA_GRP, :]
            m = jnp.maximum(jnp.max(s, axis=-1, keepdims=True), sink)
            p = jnp.exp(s - m)
            den = jnp.sum(p, axis=-1, keepdims=True) + jnp.exp(sink - m)
            o = lax.dot_general((p / den).astype(BF16), vt.astype(BF16), (((1,), (1,)), ((), ())),
                                preferred_element_type=F32)
            for g in range(SWA_GRP):
                h = kv * SWA_GRP + g
                y_ref[j, :, h * SWA_HD:(h + 1) * SWA_HD] = o[g:g + 1, :].astype(y_ref.dtype)


def _swa_decode(zt, kbuf_t, vbuf_t, bias_dec, sinks, l, nstep=8):
    nb = zt.shape[0]
    w = WINDOW
    assert nb % nstep == 0
    kern = functools.partial(_swa_decode_kernel, nstep=nstep)
    cspec = lambda: _lspec(l, (nstep, SWA_KVH, SWA_HD, w), lambda n: (n, 0, 0, 0))
    ospec = lambda: pl.BlockSpec((nstep, SWA_KVH, SWA_HD, w), lambda n: (n, 0, 0, 0))
    return pl.pallas_call(
        kern,
        grid=(nb // nstep,),
        in_specs=[pl.BlockSpec((nb, SWA_Q), lambda n: (0, TCOL_SQ // SWA_Q)),
                  pl.BlockSpec((nb, 2 * SWA_KV), lambda n: (0, TCOL_SK // (2 * SWA_KV))),
                  cspec(), cspec(),
                  pl.BlockSpec((SWA_KVH, SWA_GRP, w), lambda n: (0, 0, 0)),
                  _lspec(l, (SWA_QH, 1), lambda n: (0, 0))],
        out_specs=[pl.BlockSpec((nstep, 1, SWA_Q), lambda n: (n, 0, 0)), ospec(), ospec()],
        out_shape=[jax.ShapeDtypeStruct((nb, 1, SWA_Q), BF16),
                   jax.ShapeDtypeStruct(kbuf_t.shape[1:], F32), jax.ShapeDtypeStruct(vbuf_t.shape[1:], F32)],
        compiler_params=_cparams(("arbitrary",)),
        name="swa_decode",
    )(zt, zt, kbuf_t, vbuf_t, bias_dec, sinks)


def _seq_chunk(t):
    for c in (688, 512, 256, 128, 64, 32, 16):
        if t % c == 0:
            return c
    raise ValueError(f"sequence length {t} must be a multiple of {GLA_CHUNK}")


def kernel(x_prompt, x_sample, state_s5_re, state_s5_im, state_gla, cache_swa_k, cache_swa_v, meta_tokens, rel_bias, norm_pre_mix, norm_post_mix, norm_pre_ffn, norm_post_ffn, w_in, s5_lam_re, s5_lam_im, s5_log_step, s5_b_re, s5_b_im, s5_c_re, s5_c_im, s5_d, s5_w_glu, s5_b_glu, gla_w_gate2, gla_b_gate2, gla_g_out, swa_sinks, w_up_s5, w_up_gla, w_up_swa, w_out, w_ff1, w_ff2):
    depth = w_in.shape[0]
    nb, seq, d = x_prompt.shape
    t = seq + N_META
    nd = x_sample.shape[0]
    assert x_sample.shape[1] == 1 and cache_swa_k.shape[2] == WINDOW and t % GLA_CHUNK == 0
    tc = _seq_chunk(t)

    w_in_t = jnp.swapaxes(w_in.astype(BF16), 1, 2)
    w_glu = s5_w_glu.astype(BF16)
    wg2 = gla_w_gate2.astype(BF16)
    wu_a, wu_b, wu_c = w_up_s5.astype(BF16), w_up_gla.astype(BF16), w_up_swa.astype(BF16)
    wo, w1, w2 = w_out.astype(BF16), w_ff1.astype(BF16), w_ff2.astype(BF16)
    row3 = lambda a: a.reshape(depth, 1, a.shape[-1])
    g_pre, g_post, g_pre_f, g_post_f = (row3(a) for a in (norm_pre_mix, norm_post_mix,
                                                           norm_pre_ffn, norm_post_ffn))
    b_glu, d5, bg2, go = row3(s5_b_glu), row3(s5_d), row3(gla_b_gate2), row3(gla_g_out)
    sinks3 = swa_sinks.reshape(depth, SWA_QH, 1)
    s5r = state_s5_re.reshape(depth, nd, -1)
    s5i = state_s5_im.reshape(depth, nd, -1)
    ckb = jnp.transpose(cache_swa_k, (0, 1, 3, 4, 2))
    cvb = jnp.transpose(cache_swa_v, (0, 1, 3, 4, 2))

    ar, ai, bbr_t, bbi_t = _s5_discretize(s5_lam_re, s5_lam_im, s5_log_step,
                                          jnp.swapaxes(s5_b_re, -1, -2), jnp.swapaxes(s5_b_im, -1, -2))
    bmat, cmat = _s5_matrices(bbr_t, bbi_t, s5_c_re, s5_c_im)
    a_half = jnp.concatenate([ar.reshape(depth, S5_NGB, 2, S5_HW), ai.reshape(depth, S5_NGB, 2, S5_HW)],
                             axis=-1)
    arow = jnp.tile(a_half, (1, 1, SUBLANES // 2, 1))
    adec = jnp.stack([ar.reshape(depth, -1), ai.reshape(depth, -1)], axis=1)

    bias = _bias_table(rel_bias)
    bias_dec = bias[:, WINDOW - 1, WINDOW:].reshape(SWA_KVH, SWA_GRP, WINDOW)

    meta = jnp.broadcast_to(meta_tokens.astype(x_prompt.dtype)[None], (nb, N_META, d))
    xp = jnp.concatenate([meta, x_prompt], axis=1).reshape(nb * t, d)
    xs = x_sample.reshape(nd, d)

    outs = {k: [] for k in ('p_re', 'p_im', 'p_gla', 'p_k', 'p_v', 's_re', 's_im', 's_gla', 's_k', 's_v')}
    for l in range(depth):
        zh = _norm_mm(xp, g_pre, w_in_t, l, 0, HEAD_COLS, 640)
        zt = _norm_mm(xp, g_pre, w_in_t, l, TAIL_START, TAIL_COLS, 2176)
        zh3 = zh.reshape(nb, t, HEAD_COLS)
        zt3 = zt.reshape(nb, t, TAIL_COLS)
        z5, st5 = _s5_prompt(zh3, bmat, cmat, arow, d5, l, tc)
        ya = _glu(z5.reshape(nb * t, S5_WIDTH), w_glu, b_glu, l)
        yb, pg = _gla_prompt(zh3, zt3, wg2, bg2, go, l, tc)
        yc = _swa_prompt(zt3, bias, swa_sinks, l)
        xp = _merge_out(ya, yb.reshape(nb * t, GLA_V), yc.reshape(nb * t, SWA_Q), zt,
                        wu_a, wu_b, wu_c, wo, xp, g_post, l)
        xp = _ffn(xp, g_pre_f, w1, w2, g_post_f, l)
        st5 = st5.reshape(S5_NGB, nb, 2, 2, S5_HALF, S5_STATE).transpose(3, 1, 0, 2, 4, 5)
        st5 = st5.reshape(2, nb, S5_GROUPS, S5_STATE)
        outs['p_re'].append(st5[0])
        outs['p_im'].append(st5[1])
        outs['p_gla'].append(pg)
        outs['p_k'].append(zt3[:, t - WINDOW:, TCOL_SK:TCOL_SK + SWA_KV].reshape(nb, WINDOW, SWA_KVH, SWA_HD))
        outs['p_v'].append(zt3[:, t - WINDOW:, TCOL_SV:TCOL_SV + SWA_KV].reshape(nb, WINDOW, SWA_KVH, SWA_HD))

        zh = _norm_mm(xs, g_pre, w_in_t, l, 0, HEAD_COLS, 640)
        zt = _norm_mm(xs, g_pre, w_in_t, l, TAIL_START, TAIL_COLS, 2176)
        z5, nr, ni = _s5_decode(zh, bmat, cmat, adec, d5, s5r, s5i, l)
        ya = _glu(z5, w_glu, b_glu, l)
        yb, sg = _gla_decode(zh, zt, wg2, bg2, go, state_gla, l)
        yc, sk, sv = _swa_decode(zt, ckb, cvb, bias_dec, sinks3, l)
        xs = _merge_out(ya, yb.reshape(nd, GLA_V), yc.reshape(nd, SWA_Q), zt,
                        wu_a, wu_b, wu_c, wo, xs, g_post, l)
        xs = _ffn(xs, g_pre_f, w1, w2, g_post_f, l)
        outs['s_re'].append(nr.reshape(nd, S5_GROUPS, S5_STATE))
        outs['s_im'].append(ni.reshape(nd, S5_GROUPS, S5_STATE))
        outs['s_gla'].append(sg)
        outs['s_k'].append(jnp.transpose(sk, (0, 3, 1, 2)))
        outs['s_v'].append(jnp.transpose(sv, (0, 3, 1, 2)))

    y_prompt = xp.reshape(nb, t, d)[:, N_META:]
    y_sample = xs.reshape(nd, 1, d)
    st = {k: jnp.stack(v) for k, v in outs.items()}
    return (y_prompt, y_sample, st['p_re'], st['p_im'], st['p_gla'], st['p_k'], st['p_v'],
            st['s_re'], st['s_im'], st['s_gla'], st['s_k'], st['s_v'])
```

```python
import functools
import math

import numpy as np
import jax
import jax.numpy as jnp
from jax import lax
from jax.experimental import pallas as pl
from jax.experimental.pallas import tpu as pltpu

F32 = jnp.float32
BF16 = jnp.bfloat16

D_MODEL = 2048
N_META = 16
S5_GROUP = 16
S5_WIDTH = D_MODEL // 2
S5_GROUPS = S5_WIDTH // S5_GROUP
S5_STATE = 64
GLA_HEADS = 4
GLA_DK = D_MODEL // 16
GLA_DV = D_MODEL // 8
GLA_RANK = 16
GLA_TAU = 16.0
GLA_CHUNK = 16
SWA_QH = 16
SWA_KVH = 4
SWA_GRP = SWA_QH // SWA_KVH
SWA_HD = 64
WINDOW = 128
N_BUCKETS = 32
MAX_DISTANCE = WINDOW
D_FF = 4 * D_MODEL
N_BRANCH = 3
EPS = 1e-6

GLA_QK = GLA_HEADS * GLA_DK
GLA_V = GLA_HEADS * GLA_DV
SWA_Q = SWA_QH * SWA_HD
SWA_KV = SWA_KVH * SWA_HD

COL_U = 0
COL_GQ = COL_U + S5_WIDTH
COL_GK = COL_GQ + GLA_QK
COL_GV = COL_GK + GLA_QK
COL_GA = COL_GV + GLA_V
LANES = 128
SUBLANES = 8
HEAD_COLS = COL_GA + LANES
TAIL_START = COL_GA + GLA_RANK
TCOL_R = 0
TCOL_SQ = TCOL_R + GLA_V
TCOL_SK = TCOL_SQ + SWA_Q
TCOL_SV = TCOL_SK + SWA_KV
TCOL_GATE = TCOL_SV + SWA_KV
TAIL_COLS = TCOL_GATE + N_BRANCH * D_MODEL

S5_GB = 8
S5_NGB = S5_GROUPS // S5_GB
S5_HALF = S5_GB // 2
S5_HW = S5_HALF * S5_STATE

VMEM_LIMIT = 60 * 1024 * 1024


def _cparams(sem):
    return pltpu.CompilerParams(dimension_semantics=sem, vmem_limit_bytes=VMEM_LIMIT)


def _rms(x, g):
    ms = jnp.mean(x * x, axis=-1, keepdims=True)
    return (x * lax.rsqrt(ms + EPS)) * g


def _row_tile(m):
    for t in (688, 512, 256, 128, 64, 32, 16, 8):
        if m % t == 0:
            return t
    return m


def _lspec(l, shape, imap):
    return pl.BlockSpec((None,) + tuple(shape), lambda *idx: (l,) + tuple(imap(*idx)))


class _Tiles:
    def __init__(self, m, n_inner):
        self.tm = _row_tile(m)
        self.npt = m // self.tm
        self.n_inner = n_inner
        self.grid = (self.npt + 1, n_inner)

    def prow(self, i):
        return jnp.minimum(i, self.npt - 1)

    def pstep(self, i, j):
        return jnp.where(i == self.npt, self.n_inner - 1, j)

    def dstep(self, i, j):
        return jnp.where(i == self.npt, j, 0)


def _on_tile(npt, prompt_fn, decode_fn):
    i = pl.program_id(0)
    pl.when(i < npt)(prompt_fn)
    pl.when(i == npt)(decode_fn)


def _norm_mm_kernel(xp_ref, xd_ref, g_ref, w_ref, op_ref, od_ref, h_ref, *, npt):
    def run(x_ref, o_ref):
        rows = x_ref.shape[0]

        @pl.when(pl.program_id(1) == 0)
        def _():
            h_ref[:rows] = _rms(x_ref[...], g_ref[...]).astype(BF16)

        o_ref[...] = lax.dot_general(h_ref[:rows], w_ref[0], (((1,), (1,)), ((), ())),
                                     preferred_element_type=F32)

    _on_tile(npt, lambda: run(xp_ref, op_ref), lambda: run(xd_ref, od_ref))


def _norm_mm(xp, xd, g, wt, l, col0, n_cols, tn):
    m, k = xp.shape
    nd = xd.shape[0]
    tl = _Tiles(m, n_cols // tn)
    kern = functools.partial(_norm_mm_kernel, npt=tl.npt)
    return pl.pallas_call(
        kern,
        grid=tl.grid,
        in_specs=[pl.BlockSpec((tl.tm, k), lambda i, j: (tl.prow(i), 0)),
                  pl.BlockSpec((nd, k), lambda i, j: (0, 0)),
                  _lspec(l, (1, k), lambda i, j: (0, 0)),
                  pl.BlockSpec((pl.Element(1), pl.Element(tn), pl.Element(k)),
                               lambda i, j: (l, pl.multiple_of(col0 + j * tn, 16), 0))],
        out_specs=[pl.BlockSpec((tl.tm, tn), lambda i, j: (tl.prow(i), tl.pstep(i, j))),
                   pl.BlockSpec((nd, tn), lambda i, j: (0, tl.dstep(i, j)))],
        out_shape=[jax.ShapeDtypeStruct((m, n_cols), F32), jax.ShapeDtypeStruct((nd, n_cols), F32)],
        scratch_shapes=[pltpu.VMEM((tl.tm, k), BF16)],
        compiler_params=_cparams(("arbitrary", "arbitrary")),
        name="norm_mm",
    )(xp, xd, g, wt)


def _glu_kernel(zp_ref, zd_ref, zcp_ref, zcd_ref, w_ref, b_ref, op_ref, od_ref, h_ref, *, npt):
    def run(z_ref, zc_ref, o_ref):
        rows = z_ref.shape[0]

        @pl.when(pl.program_id(1) == 0)
        def _():
            h_ref[:rows] = z_ref[...].astype(BF16)

        a = jnp.dot(h_ref[:rows], w_ref[...], preferred_element_type=F32) + b_ref[...]
        o_ref[...] = (zc_ref[...] * jax.nn.sigmoid(a)).astype(o_ref.dtype)

    _on_tile(npt, lambda: run(zp_ref, zcp_ref, op_ref), lambda: run(zd_ref, zcd_ref, od_ref))


def _glu(zp, zd, w, b, l, tn=512):
    m, k = zp.shape
    nd = zd.shape[0]
    tl = _Tiles(m, k // tn)
    kern = functools.partial(_glu_kernel, npt=tl.npt)
    return pl.pallas_call(
        kern,
        grid=tl.grid,
        in_specs=[pl.BlockSpec((tl.tm, k), lambda i, j: (tl.prow(i), 0)),
                  pl.BlockSpec((nd, k), lambda i, j: (0, 0)),
                  pl.BlockSpec((tl.tm, tn), lambda i, j: (tl.prow(i), tl.pstep(i, j))),
                  pl.BlockSpec((nd, tn), lambda i, j: (0, tl.dstep(i, j))),
                  _lspec(l, (k, tn), lambda i, j: (0, j)),
                  _lspec(l, (1, tn), lambda i, j: (0, j))],
        out_specs=[pl.BlockSpec((tl.tm, tn), lambda i, j: (tl.prow(i), tl.pstep(i, j))),
                   pl.BlockSpec((nd, tn), lambda i, j: (0, tl.dstep(i, j)))],
        out_shape=[jax.ShapeDtypeStruct((m, k), BF16), jax.ShapeDtypeStruct((nd, k), BF16)],
        scratch_shapes=[pltpu.VMEM((tl.tm, k), BF16)],
        compiler_params=_cparams(("arbitrary", "arbitrary")),
        name="s5_glu",
    )(zp, zd, zp, zd, w, b)


def _merge_out_kernel(yap_ref, ybp_ref, ycp_ref, yad_ref, ybd_ref, ycd_ref,
                      gap_ref, gbp_ref, gcp_ref, gad_ref, gbd_ref, gcd_ref,
                      wa_ref, wb_ref, wc_ref, wo_ref, xp_ref, xd_ref, g_ref,
                      op_ref, od_ref, m_ref, *, npt, nk, tk, tn):
    s = pl.program_id(1)

    def run(ya_ref, yb_ref, yc_ref, ga_ref, gb_ref, gc_ref, x_ref, o_ref):
        rows = x_ref.shape[0]

        @pl.when(s < nk)
        def _():
            m = (jax.nn.sigmoid(ga_ref[...]) * jnp.dot(ya_ref[...], wa_ref[...], preferred_element_type=F32)
                 + jax.nn.sigmoid(gb_ref[...]) * jnp.dot(yb_ref[...], wb_ref[...], preferred_element_type=F32)
                 + jax.nn.sigmoid(gc_ref[...]) * jnp.dot(yc_ref[...], wc_ref[...], preferred_element_type=F32))
            m_ref[:rows, pl.ds(pl.multiple_of(s * tk, tk), tk)] = m.astype(BF16)

        @pl.when(s >= nk)
        def _():
            c0 = pl.multiple_of((s - nk) * tn, tn)
            o_ref[:, pl.ds(c0, tn)] = jnp.dot(m_ref[:rows], wo_ref[...], preferred_element_type=F32)

        @pl.when(s == pl.num_programs(1) - 1)
        def _():
            o_ref[...] = x_ref[...] + _rms(o_ref[...], g_ref[...])

    _on_tile(npt,
             lambda: run(yap_ref, ybp_ref, ycp_ref, gap_ref, gbp_ref, gcp_ref, xp_ref, op_ref),
             lambda: run(yad_ref, ybd_ref, ycd_ref, gad_ref, gbd_ref, gcd_ref, xd_ref, od_ref))


def _merge_out(yp, yd, ztp, ztd, wa, wb, wc, wo, xp, xd, g, l, tk=256, tn=512):
    m, d = xp.shape
    nd = xd.shape[0]
    kb = yp[0].shape[1]
    gate0 = TCOL_GATE // tk
    nk = d // tk
    tl = _Tiles(m, nk + d // tn)
    kidx = lambda s: jnp.minimum(s, nk - 1)
    nidx = lambda s: jnp.maximum(s - nk, 0)
    gp_spec = lambda br: pl.BlockSpec(
        (tl.tm, tk), lambda i, s: (tl.prow(i), gate0 + br * nk + kidx(tl.pstep(i, s))))
    gd_spec = lambda br: pl.BlockSpec(
        (nd, tk), lambda i, s: (0, gate0 + br * nk + kidx(tl.dstep(i, s))))
    yp_spec = pl.BlockSpec((tl.tm, kb), lambda i, s: (tl.prow(i), 0))
    yd_spec = pl.BlockSpec((nd, kb), lambda i, s: (0, 0))
    w_spec = _lspec(l, (kb, tk), lambda i, s: (0, kidx(s)))
    kern = functools.partial(_merge_out_kernel, npt=tl.npt, nk=nk, tk=tk, tn=tn)
    return pl.pallas_call(
        kern,
        grid=tl.grid,
        in_specs=[yp_spec, yp_spec, yp_spec, yd_spec, yd_spec, yd_spec,
                  gp_spec(0), gp_spec(1), gp_spec(2), gd_spec(0), gd_spec(1), gd_spec(2),
                  w_spec, w_spec, w_spec,
                  _lspec(l, (d, tn), lambda i, s: (0, nidx(s))),
                  pl.BlockSpec((tl.tm, d), lambda i, s: (tl.prow(i), 0)),
                  pl.BlockSpec((nd, d), lambda i, s: (0, 0)),
                  _lspec(l, (1, d), lambda i, s: (0, 0))],
        out_specs=[pl.BlockSpec((tl.tm, d), lambda i, s: (tl.prow(i), 0)),
                   pl.BlockSpec((nd, d), lambda i, s: (0, 0))],
        out_shape=[jax.ShapeDtypeStruct((m, d), F32), jax.ShapeDtypeStruct((nd, d), F32)],
        scratch_shapes=[pltpu.VMEM((tl.tm, d), BF16)],
        compiler_params=_cparams(("arbitrary", "arbitrary")),
        name="merge_out",
    )(*yp, *yd, ztp, ztp, ztp, ztd, ztd, ztd, wa, wb, wc, wo, xp, xd, g)


def _ffn_kernel(xp_ref, xd_ref, g1_ref, w1_ref, w2_ref, g2_ref, op_ref, od_ref, h_ref, acc_ref, *, npt):
    f = pl.program_id(1)

    def run(x_ref, o_ref):
        rows = x_ref.shape[0]

        @pl.when(f == 0)
        def _():
            h_ref[:rows] = _rms(x_ref[...], g1_ref[...]).astype(BF16)
            acc_ref[:rows] = jnp.zeros((rows, acc_ref.shape[1]), F32)

        a = jnp.dot(h_ref[:rows], w1_ref[...], preferred_element_type=F32)
        a = jnp.square(jnp.maximum(a, 0.0)).astype(BF16)
        acc_ref[:rows] += jnp.dot(a, w2_ref[...], preferred_element_type=F32)

        @pl.when(f == pl.num_programs(1) - 1)
        def _():
            o_ref[...] = x_ref[...] + _rms(acc_ref[:rows], g2_ref[...])

    _on_tile(npt, lambda: run(xp_ref, op_ref), lambda: run(xd_ref, od_ref))


def _ffn(xp, xd, g1, w1, w2, g2, l, tf=512):
    m, d = xp.shape
    nd = xd.shape[0]
    ff = w1.shape[2]
    tl = _Tiles(m, ff // tf)
    kern = functools.partial(_ffn_kernel, npt=tl.npt)
    return pl.pallas_call(
        kern,
        grid=tl.grid,
        in_specs=[pl.BlockSpec((tl.tm, d), lambda i, f: (tl.prow(i), 0)),
                  pl.BlockSpec((nd, d), lambda i, f: (0, 0)),
                  _lspec(l, (1, d), lambda i, f: (0, 0)),
                  _lspec(l, (d, tf), lambda i, f: (0, f)),
                  _lspec(l, (tf, d), lambda i, f: (f, 0)),
                  _lspec(l, (1, d), lambda i, f: (0, 0))],
        out_specs=[pl.BlockSpec((tl.tm, d), lambda i, f: (tl.prow(i), 0)),
                   pl.BlockSpec((nd, d), lambda i, f: (0, 0))],
        out_shape=[jax.ShapeDtypeStruct((m, d), F32), jax.ShapeDtypeStruct((nd, d), F32)],
        scratch_shapes=[pltpu.VMEM((tl.tm, d), BF16), pltpu.VMEM((tl.tm, d), F32)],
        compiler_params=_cparams(("arbitrary", "arbitrary")),
        name="ffn",
    )(xp, xd, g1, w1, w2, g2)


def _s5_disc_kernel(lr_ref, li_ref, ls_ref, br_ref, bi_ref, ar_ref, ai_ref, bbr_ref, bbi_ref):
    lr = lr_ref[0]
    li = li_ref[0]
    dt = jnp.exp(ls_ref[0])
    mag = jnp.exp(lr * dt)
    ar = mag * jnp.cos(li * dt)
    ai = mag * jnp.sin(li * dt)
    den = lr * lr + li * li
    fr = ((ar - 1.0) * lr + ai * li) / den
    fi = (ai * lr - (ar - 1.0) * li) / den
    ar_ref[0] = ar
    ai_ref[0] = ai
    br = br_ref[0]
    bi = bi_ref[0]
    bbr_ref[0] = fr[:, None, :] * br - fi[:, None, :] * bi
    bbi_ref[0] = fr[:, None, :] * bi + fi[:, None, :] * br


def _s5_discretize(lam_re, lam_im, log_step, b_re_t, b_im_t):
    depth = lam_re.shape[0]
    gp = pl.BlockSpec((1, S5_GROUPS, S5_STATE), lambda l: (l, 0, 0))
    bspec = pl.BlockSpec((1, S5_GROUPS, S5_GROUP, S5_STATE), lambda l: (l, 0, 0, 0))
    return pl.pallas_call(
        _s5_disc_kernel,
        grid=(depth,),
        in_specs=[gp, gp, pl.BlockSpec((1, S5_GROUPS, 1), lambda l: (l, 0, 0)), bspec, bspec],
        out_specs=[gp, gp, bspec, bspec],
        out_shape=[jax.ShapeDtypeStruct(lam_re.shape, F32), jax.ShapeDtypeStruct(lam_re.shape, F32),
                   jax.ShapeDtypeStruct(b_re_t.shape, F32), jax.ShapeDtypeStruct(b_re_t.shape, F32)],
        compiler_params=_cparams(("parallel",)),
        name="s5_discretize",
    )(lam_re, lam_im, log_step.reshape(depth, S5_GROUPS, 1), b_re_t, b_im_t)


def _blockdiag(r):
    eye = jnp.eye(S5_HALF, dtype=r.dtype)
    z = jnp.einsum('ab,lgqbhp->lgqahbp', eye, r)
    s = r.shape
    return z.reshape(s[0], s[1], s[2], S5_HALF * s[4], S5_HALF * s[5])


def _s5_matrices(bbr_t, bbi_t, c_re, c_im):
    depth = bbr_t.shape[0]
    shp = (depth, S5_NGB, 2, S5_HALF, S5_GROUP, S5_STATE)
    bre = _blockdiag(bbr_t.reshape(shp))
    bim = _blockdiag(bbi_t.reshape(shp))
    bcat = jnp.concatenate([bre, bim], axis=-1)
    zb = jnp.zeros_like(bcat[:, :, 0])
    bmat = jnp.stack([jnp.concatenate([bcat[:, :, 0], zb], axis=-2),
                      jnp.concatenate([zb, bcat[:, :, 1]], axis=-2)], axis=2)
    cre = jnp.swapaxes(_blockdiag(c_re.reshape(shp)), -1, -2)
    cim = jnp.swapaxes(_blockdiag(c_im.reshape(shp)), -1, -2)
    ccat = jnp.concatenate([cre, -cim], axis=-2)
    zc = jnp.zeros_like(ccat[:, :, 0])
    cmat = jnp.stack([jnp.concatenate([ccat[:, :, 0], zc], axis=-1),
                      jnp.concatenate([zc, ccat[:, :, 1]], axis=-1)], axis=2)
    return bmat.astype(BF16), cmat.astype(BF16)


def _s5_prompt_kernel(u_ref, b_ref, c_ref, a_ref, d_ref, z_ref, st_ref, e_ref, x_ref, *, nb, tc):
    @pl.when(pl.program_id(1) == 0)
    def _():
        x_ref[...] = jnp.zeros_like(x_ref)

    nlb = 2 * S5_HW // LANES
    hlb = nlb // 2
    for n in range(nb):
        ub = u_ref[n].astype(BF16)
        for q in range(2):
            e = jnp.dot(ub, b_ref[0, q], preferred_element_type=F32)
            for j in range(nlb):
                e_ref[j, pl.ds(n * 2 + q, tc, stride=SUBLANES), :] = e[:, j * LANES:(j + 1) * LANES]

    a = a_ref[0]
    ar = [a[:, j * LANES:(j + 1) * LANES] for j in range(hlb)]
    ai = [a[:, (hlb + j) * LANES:(hlb + j + 1) * LANES] for j in range(hlb)]

    def step(t, carry):
        row = pl.multiple_of(t * SUBLANES, SUBLANES)
        new = []
        for j in range(hlb):
            xr, xi = carry[2 * j], carry[2 * j + 1]
            nxr = ar[j] * xr - ai[j] * xi + e_ref[j, pl.ds(row, SUBLANES), :]
            nxi = ar[j] * xi + ai[j] * xr + e_ref[hlb + j, pl.ds(row, SUBLANES), :]
            e_ref[j, pl.ds(row, SUBLANES), :] = nxr
            e_ref[hlb + j, pl.ds(row, SUBLANES), :] = nxi
            new += [nxr, nxi]
        return tuple(new)

    x0 = x_ref[...]
    init = []
    for j in range(hlb):
        init += [x0[:, j * LANES:(j + 1) * LANES], x0[:, (hlb + j) * LANES:(hlb + j + 1) * LANES]]
    fin = lax.fori_loop(0, tc, step, tuple(init), unroll=8)
    for j in range(hlb):
        x_ref[:, j * LANES:(j + 1) * LANES] = fin[2 * j]
        x_ref[:, (hlb + j) * LANES:(hlb + j + 1) * LANES] = fin[2 * j + 1]
    st_ref[0] = x_ref[...]

    for n in range(nb):
        y = d_ref[...] * u_ref[n]
        for q in range(2):
            xs = jnp.concatenate(
                [e_ref[j, pl.ds(n * 2 + q, tc, stride=SUBLANES), :] for j in range(nlb)], axis=1)
            y = y + jnp.dot(xs.astype(BF16), c_ref[0, q], preferred_element_type=F32)
        z_ref[n] = jax.nn.gelu(y)


def _s5_prompt(zh3, bmat, cmat, arow, d, l, tc):
    nb, t, _ = zh3.shape
    assert 2 * nb == SUBLANES, "state rows are (batch, half) pairs filling one sublane tile"
    kern = functools.partial(_s5_prompt_kernel, nb=nb, tc=tc)
    return pl.pallas_call(
        kern,
        grid=(S5_NGB, t // tc),
        in_specs=[pl.BlockSpec((nb, tc, LANES), lambda g, c: (0, c, g)),
                  _lspec(l, (1, 2, LANES, 2 * S5_HW), lambda g, c: (g, 0, 0, 0)),
                  _lspec(l, (1, 2, 2 * S5_HW, LANES), lambda g, c: (g, 0, 0, 0)),
                  _lspec(l, (1, SUBLANES, 2 * S5_HW), lambda g, c: (g, 0, 0)),
                  _lspec(l, (1, LANES), lambda g, c: (0, g))],
        out_specs=[pl.BlockSpec((nb, tc, LANES), lambda g, c: (0, c, g)),
                   pl.BlockSpec((1, SUBLANES, 2 * S5_HW), lambda g, c: (g, 0, 0))],
        out_shape=[jax.ShapeDtypeStruct((nb, t, S5_WIDTH), F32),
                   jax.ShapeDtypeStruct((S5_NGB, SUBLANES, 2 * S5_HW), F32)],
        scratch_shapes=[pltpu.VMEM((2 * S5_HW // LANES, tc * SUBLANES, LANES), F32),
                        pltpu.VMEM((SUBLANES, 2 * S5_HW), F32)],
        compiler_params=_cparams(("parallel", "arbitrary")),
        name="s5_prompt",
    )(zh3, bmat, cmat, arow, d)


def _s5_decode_kernel(u_ref, b_ref, c_ref, a_ref, d_ref, sr_ref, si_ref, z_ref, nr_ref, ni_ref):
    for g in range(S5_NGB):
        u = u_ref[:, g * LANES:(g + 1) * LANES]
        ub = u.astype(BF16)
        y = d_ref[:, g * LANES:(g + 1) * LANES] * u
        for q in range(2):
            c0 = (g * 2 + q) * S5_HW
            e = jnp.dot(ub, b_ref[g, q], preferred_element_type=F32)
            ar = a_ref[0:1, c0:c0 + S5_HW]
            ai = a_ref[1:2, c0:c0 + S5_HW]
            x0r = sr_ref[:, c0:c0 + S5_HW]
            x0i = si_ref[:, c0:c0 + S5_HW]
            nr = ar * x0r - ai * x0i + e[:, :S5_HW]
            ni = ar * x0i + ai * x0r + e[:, S5_HW:]
            nr_ref[:, c0:c0 + S5_HW] = nr
            ni_ref[:, c0:c0 + S5_HW] = ni
            xs = jnp.concatenate([nr, ni], axis=1).astype(BF16)
            y = y + jnp.dot(xs, c_ref[g, q], preferred_element_type=F32)
        z_ref[:, g * LANES:(g + 1) * LANES] = jax.nn.gelu(y)


def _s5_decode(zh, bmat, cmat, adec, d, sr, si, l):
    nb = zh.shape[0]
    ns = S5_GROUPS * S5_STATE
    full = lambda shape: pl.BlockSpec(shape, lambda i: (0,) * len(shape))
    lfull = lambda shape: _lspec(l, shape, lambda i: (0,) * len(shape))
    return pl.pallas_call(
        _s5_decode_kernel,
        grid=(1,),
        in_specs=[full((nb, S5_WIDTH)), lfull(bmat.shape[1:]), lfull(cmat.shape[1:]), lfull((2, ns)),
                  lfull((1, S5_WIDTH)), lfull((nb, ns)), lfull((nb, ns))],
        out_specs=[full((nb, S5_WIDTH)), full((nb, ns)), full((nb, ns))],
        out_shape=[jax.ShapeDtypeStruct((nb, S5_WIDTH), F32),
                   jax.ShapeDtypeStruct((nb, ns), F32), jax.ShapeDtypeStruct((nb, ns), F32)],
        compiler_params=_cparams(("arbitrary",)),
        name="s5_decode",
    )(zh, bmat, cmat, adec, d, sr, si)


def _gla_finish(o, r, g):
    return _rms(o, g) * (r * jax.nn.sigmoid(r))


def _gla_prompt_kernel(q_ref, k_ref, v_ref, a_ref, r_ref, wg_ref, bg_ref, go_ref, y_ref, st_ref,
                       s_ref, lg_ref, o_ref, qi_ref, ku_ref, cd_ref, *, tc):
    c = pl.program_id(1)

    @pl.when(c == 0)
    def _():
        s_ref[...] = jnp.zeros_like(s_ref)

    a16 = a_ref[0][:, :GLA_RANK].astype(BF16)
    lg_ref[...] = jax.nn.log_sigmoid(
        jnp.dot(a16, wg_ref[...], preferred_element_type=F32) + bg_ref[...]) / GLA_TAU

    ch = GLA_CHUNK
    hs = SUBLANES
    rowi = lax.broadcasted_iota(jnp.int32, (ch, GLA_DK), 0)
    rowh = lax.broadcasted_iota(jnp.int32, (hs, GLA_DK), 0)
    coli = lax.broadcasted_iota(jnp.int32, (hs, ch), 1)
    scale = GLA_DK ** -0.5

    def prep(idx, carry):
        r0 = pl.multiple_of(idx * ch, ch)
        for h in range(GLA_HEADS):
            ks = slice(h * GLA_DK, (h + 1) * GLA_DK)
            vs = slice(h * GLA_DV, (h + 1) * GLA_DV)
            b = lg_ref[pl.ds(r0, ch), ks]
            for sh in (1, 2, 4, 8):
                b = b + jnp.where(rowi >= sh, pltpu.roll(b, sh, axis=0), 0.0)
            qh = q_ref[0, pl.ds(r0, ch), ks] * scale
            kh = k_ref[0, pl.ds(r0, ch), ks]
            halves = []
            for half in range(ch // hs):
                bq = b[half * hs:(half + 1) * hs]
                qq = qh[half * hs:(half + 1) * hs]
                att = jnp.zeros((hs, ch), F32)
                for s in range((half + 1) * hs):
                    diff = bq - b[s:s + 1, :]
                    if s >= half * hs:
                        diff = jnp.where(rowh + half * hs >= s, diff, -jnp.inf)
                    col = jnp.sum(qq * kh[s:s + 1, :] * jnp.exp(diff), axis=1, keepdims=True)
                    att = jnp.where(coli == s, col, att)
                halves.append(att)
            att = jnp.concatenate(halves, axis=0).astype(BF16)
            vh = v_ref[0, pl.ds(r0, ch), vs].astype(BF16)
            o_ref[pl.ds(r0, ch), vs] = jnp.dot(att, vh, preferred_element_type=F32)
            blast = b[ch - 1:ch, :]
            qi_ref[pl.ds(r0, ch), ks] = (qh * jnp.exp(b)).astype(BF16)
            ku_ref[pl.ds(r0, ch), ks] = (kh * jnp.exp(blast - b)).astype(BF16)
            cd_ref[pl.ds(pl.multiple_of(idx * hs, hs), hs), ks] = jnp.broadcast_to(
                jnp.exp(blast), (hs, GLA_DK))
        return carry

    lax.fori_loop(0, tc // ch, prep, 0, unroll=2)

    def scan(idx, carry):
        r0 = pl.multiple_of(idx * ch, ch)
        for h in range(GLA_HEADS):
            ks = slice(h * GLA_DK, (h + 1) * GLA_DK)
            vs = slice(h * GLA_DV, (h + 1) * GLA_DV)
            st = s_ref[h]
            inter = lax.dot_general(qi_ref[pl.ds(r0, ch), ks], st.astype(BF16),
                                    (((1,), (1,)), ((), ())), preferred_element_type=F32)
            o_ref[pl.ds(r0, ch), vs] += inter
            vh = v_ref[0, pl.ds(r0, ch), vs].astype(BF16)
            kv = lax.dot_general(vh, ku_ref[pl.ds(r0, ch), ks], (((0,), (0,)), ((), ())),
                                 preferred_element_type=F32)
            cd = cd_ref[pl.ds(pl.multiple_of(idx * hs, hs), 1), ks]
            s_ref[h] = cd * st + kv
        return carry

    lax.fori_loop(0, tc // ch, scan, 0, unroll=2)

    for h in range(GLA_HEADS):
        vs = slice(h * GLA_DV, (h + 1) * GLA_DV)
        y_ref[0, :, vs] = _gla_finish(o_ref[:, vs], r_ref[0, :, vs], go_ref[...]).astype(y_ref.dtype)
        st_ref[0, h] = s_ref[h].T


def _gla_prompt(zh3, zt3, wg, bg, go, l, tc):
    nb, t, _ = zh3.shape
    kern = functools.partial(_gla_prompt_kernel, tc=tc)
    return pl.pallas_call(
        kern,
        grid=(nb, t // tc),
        in_specs=[pl.BlockSpec((1, tc, GLA_QK), lambda n, c: (n, c, COL_GQ // GLA_QK)),
                  pl.BlockSpec((1, tc, GLA_QK), lambda n, c: (n, c, COL_GK // GLA_QK)),
                  pl.BlockSpec((1, tc, GLA_V), lambda n, c: (n, c, COL_GV // GLA_V)),
                  pl.BlockSpec((1, tc, LANES), lambda n, c: (n, c, COL_GA // LANES)),
                  pl.BlockSpec((1, tc, GLA_V), lambda n, c: (n, c, TCOL_R // GLA_V)),
                  _lspec(l, (GLA_RANK, GLA_QK), lambda n, c: (0, 0)),
                  _lspec(l, (1, GLA_QK), lambda n, c: (0, 0)),
                  _lspec(l, (1, GLA_DV), lambda n, c: (0, 0))],
        out_specs=[pl.BlockSpec((1, tc, GLA_V), lambda n, c: (n, c, 0)),
                   pl.BlockSpec((1, GLA_HEADS, GLA_DK, GLA_DV), lambda n, c: (n, 0, 0, 0))],
        out_shape=[jax.ShapeDtypeStruct((nb, t, GLA_V), BF16),
                   jax.ShapeDtypeStruct((nb, GLA_HEADS, GLA_DK, GLA_DV), F32)],
        scratch_shapes=[pltpu.VMEM((GLA_HEADS, GLA_DV, GLA_DK), F32),
                        pltpu.VMEM((tc, GLA_QK), F32),
                        pltpu.VMEM((tc, GLA_V), F32),
                        pltpu.VMEM((tc, GLA_QK), BF16),
                        pltpu.VMEM((tc, GLA_QK), BF16),
                        pltpu.VMEM((tc // GLA_CHUNK * SUBLANES, GLA_QK), F32)],
        compiler_params=_cparams(("parallel", "arbitrary")),
        name="gla_prompt",
    )(zh3, zh3, zh3, zh3, zt3, wg, bg, go)


def _gla_decode_kernel(zh_ref, r_ref, wg_ref, bg_ref, go_ref, s_ref, y_ref, so_ref):
    n = pl.program_id(0)
    row = zh_ref[pl.ds(n, 1), :]
    a16 = row[:, COL_GA:COL_GA + GLA_RANK].astype(BF16)
    lg = jax.nn.log_sigmoid(
        jnp.dot(a16, wg_ref[...], preferred_element_type=F32) + bg_ref[...]) / GLA_TAU
    eg = jnp.exp(lg)
    rr = r_ref[pl.ds(n, 1), :]
    ri = lax.broadcasted_iota(jnp.int32, (GLA_DK, GLA_DK), 0)
    ci = lax.broadcasted_iota(jnp.int32, (GLA_DK, GLA_DK), 1)
    eye = (ri == ci).astype(F32)

    def col(v):
        return jnp.sum(eye * v, axis=1, keepdims=True)

    for h in range(GLA_HEADS):
        ks = slice(h * GLA_DK, (h + 1) * GLA_DK)
        qh = row[:, COL_GQ + h * GLA_DK:COL_GQ + (h + 1) * GLA_DK] * (GLA_DK ** -0.5)
        kh = row[:, COL_GK + h * GLA_DK:COL_GK + (h + 1) * GLA_DK]
        vh = row[:, COL_GV + h * GLA_DV:COL_GV + (h + 1) * GLA_DV]
        s = col(eg[:, ks]) * s_ref[0, h] + col(kh) * vh
        so_ref[0, h] = s
        o = jnp.sum(col(qh) * s, axis=0, keepdims=True)
        vs = slice(h * GLA_DV, (h + 1) * GLA_DV)
        y_ref[0, :, vs] = _gla_finish(o, rr[:, vs], go_ref[...]).astype(y_ref.dtype)


def _gla_decode(zh, zt, wg, bg, go, s0, l):
    nb = zh.shape[0]
    return pl.pallas_call(
        _gla_decode_kernel,
        grid=(nb,),
        in_specs=[pl.BlockSpec((nb, HEAD_COLS), lambda n: (0, 0)),
                  pl.BlockSpec((nb, GLA_V), lambda n: (0, TCOL_R // GLA_V)),
                  _lspec(l, (GLA_RANK, GLA_QK), lambda n: (0, 0)),
                  _lspec(l, (1, GLA_QK), lambda n: (0, 0)),
                  _lspec(l, (1, GLA_DV), lambda n: (0, 0)),
                  _lspec(l, (1, GLA_HEADS, GLA_DK, GLA_DV), lambda n: (n, 0, 0, 0))],
        out_specs=[pl.BlockSpec((1, 1, GLA_V), lambda n: (n, 0, 0)),
                   pl.BlockSpec((1, GLA_HEADS, GLA_DK, GLA_DV), lambda n: (n, 0, 0, 0))],
        out_shape=[jax.ShapeDtypeStruct((nb, 1, GLA_V), BF16),
                   jax.ShapeDtypeStruct(s0.shape[1:], F32)],
        compiler_params=_cparams(("arbitrary",)),
        name="gla_decode",
    )(zh, zt, wg, bg, go, s0)


def _bucket_table():
    dist = (np.arange(WINDOW)[:, None] + WINDOW) - np.arange(2 * WINDOW)[None, :]
    max_exact = N_BUCKETS // 2
    d = np.maximum(dist, 0)
    ratio = np.log(np.maximum(d, 1).astype(np.float32) / np.float32(max_exact)) / np.float32(
        math.log(MAX_DISTANCE / max_exact))
    large = max_exact + (ratio.astype(np.float32) * np.float32(N_BUCKETS - max_exact)).astype(np.int32)
    large = np.minimum(large, N_BUCKETS - 1)
    return np.where(d < max_exact, d, large).astype(np.int32)


def _bias_kernel(bk_ref, rb_ref, o_ref):
    h = pl.program_id(1)
    bk = bk_ref[...]
    acc = jnp.zeros(bk.shape, F32)
    for b in range(N_BUCKETS):
        acc = jnp.where(bk == b, rb_ref[b, h], acc)
    cols = lax.broadcasted_iota(jnp.int32, bk.shape, 1)
    dist = WINDOW + lax.broadcasted_iota(jnp.int32, bk.shape, 0) - cols
    band = (dist >= 0) & (dist < WINDOW)
    keep = band & ((cols >= WINDOW) | (pl.program_id(0) == 1))
    o_ref[0, 0] = jnp.where(keep, acc, -jnp.inf)


def _bias_table(rel_bias):
    bk = jnp.asarray(_bucket_table())
    return pl.pallas_call(
        _bias_kernel,
        grid=(2, SWA_QH),
        in_specs=[pl.BlockSpec((WINDOW, 2 * WINDOW), lambda v, h: (0, 0)),
                  pl.BlockSpec(memory_space=pltpu.SMEM)],
        out_specs=pl.BlockSpec((1, 1, WINDOW, 2 * WINDOW), lambda v, h: (v, h, 0, 0)),
        out_shape=jax.ShapeDtypeStruct((2, SWA_QH, WINDOW, 2 * WINDOW), F32),
        compiler_params=_cparams(("parallel", "parallel")),
        name="swa_bias",
    )(bk, rel_bias)


def _swa_prompt_kernel(q_ref, kp_ref, kc_ref, vp_ref, vc_ref, bias_ref, sink_ref, y_ref, *, t, l):
    i = pl.program_id(1)
    w = WINDOW
    inseq = i * w + lax.broadcasted_iota(jnp.int32, (w, 1), 0) < t
    scale = SWA_HD ** -0.5
    for kv in range(SWA_KVH):
        hs = slice(kv * SWA_HD, (kv + 1) * SWA_HD)
        kh = jnp.concatenate([kp_ref[0, :, hs], jnp.where(inseq, kc_ref[0, :, hs], 0.0)],
                             axis=0).astype(BF16)
        vh = jnp.concatenate([vp_ref[0, :, hs], jnp.where(inseq, vc_ref[0, :, hs], 0.0)],
                             axis=0).astype(BF16)
        for g in range(SWA_GRP):
            h = kv * SWA_GRP + g
            qh = (q_ref[0, :, h * SWA_HD:(h + 1) * SWA_HD] * scale).astype(BF16)
            s = lax.dot_general(qh, kh, (((1,), (1,)), ((), ())), preferred_element_type=F32)
            s = s + bias_ref[h]
            sink = sink_ref[l, h]
            m = jnp.maximum(jnp.max(s, axis=-1, keepdims=True), sink)
            p = jnp.exp(s - m)
            den = jnp.sum(p, axis=-1, keepdims=True) + jnp.exp(sink - m)
            o = jnp.dot(p.astype(BF16), vh, preferred_element_type=F32) * (1.0 / den)
            y_ref[0, :, h * SWA_HD:(h + 1) * SWA_HD] = o.astype(y_ref.dtype)


def _swa_prompt(zt3, bias, sinks, l):
    nb, t, _ = zt3.shape
    w = WINDOW
    nblk = pl.cdiv(t, w)
    kern = functools.partial(_swa_prompt_kernel, t=t, l=l)
    kcol = TCOL_SK // SWA_KV
    vcol = TCOL_SV // SWA_KV
    prev = lambda i: jnp.maximum(i - 1, 0)
    return pl.pallas_call(
        kern,
        grid=(nb, nblk),
        in_specs=[pl.BlockSpec((1, w, SWA_Q), lambda n, i: (n, i, TCOL_SQ // SWA_Q)),
                  pl.BlockSpec((1, w, SWA_KV), lambda n, i: (n, prev(i), kcol)),
                  pl.BlockSpec((1, w, SWA_KV), lambda n, i: (n, i, kcol)),
                  pl.BlockSpec((1, w, SWA_KV), lambda n, i: (n, prev(i), vcol)),
                  pl.BlockSpec((1, w, SWA_KV), lambda n, i: (n, i, vcol)),
                  pl.BlockSpec((None, SWA_QH, w, 2 * w), lambda n, i: (jnp.minimum(i, 1), 0, 0, 0)),
                  pl.BlockSpec(memory_space=pltpu.SMEM)],
        out_specs=pl.BlockSpec((1, w, SWA_Q), lambda n, i: (n, i, 0)),
        out_shape=jax.ShapeDtypeStruct((nb, t, SWA_Q), BF16),
        compiler_params=_cparams(("parallel", "arbitrary")),
        name="swa_prompt",
    )(zt3, zt3, zt3, zt3, zt3, bias, sinks)


def _swa_decode_kernel(q_ref, kv_ref, kb_ref, vb_ref, bias_ref, sink_ref, y_ref, ko_ref, vo_ref, *, nstep):
    n0 = pl.program_id(0) * nstep
    scale = SWA_HD ** -0.5
    w = WINDOW
    ri = lax.broadcasted_iota(jnp.int32, (SWA_HD, SWA_HD), 0)
    ci = lax.broadcasted_iota(jnp.int32, (SWA_HD, SWA_HD), 1)
    eye = (ri == ci).astype(F32)
    last = lax.broadcasted_iota(jnp.int32, (SWA_HD, w), 1) == w - 1

    def push(cache_t, new_row):
        col = jnp.sum(eye * new_row, axis=1, keepdims=True)
        return jnp.where(last, col, pltpu.roll(cache_t, w - 1, axis=1))

    for j in range(nstep):
        qrow = q_ref[pl.ds(n0 + j, 1), :]
        kvrow = kv_ref[pl.ds(n0 + j, 1), :]
        for kv in range(SWA_KVH):
            hs = slice(kv * SWA_HD, (kv + 1) * SWA_HD)
            kt = push(kb_ref[j, kv], kvrow[:, hs])
            vt = push(vb_ref[j, kv], kvrow[:, SWA_KV + kv * SWA_HD:SWA_KV + (kv + 1) * SWA_HD])
            ko_ref[j, kv] = kt
            vo_ref[j, kv] = vt
            q4 = jnp.concatenate(
                [qrow[:, (kv * SWA_GRP + g) * SWA_HD:(kv * SWA_GRP + g + 1) * SWA_HD]
                 for g in range(SWA_GRP)], axis=0).astype(BF16)
            s = jnp.dot(q4, kt.astype(BF16), preferred_element_type=F32) * scale + bias_ref[kv]
            sink = sink_ref[kv * SWA_GRP:(kv + 1) * SWA_GRP, :]
            m = jnp.maximum(jnp.max(s, axis=-1, keepdims=True), sink)
            p = jnp.exp(s - m)
            den = jnp.sum(p, axis=-1, keepdims=True) + jnp.exp(sink - m)
            o = lax.dot_general((p / den).astype(BF16), vt.astype(BF16), (((1,), (1,)), ((), ())),
                                preferred_element_type=F32)
            for g in range(SWA_GRP):
                h = kv * SWA_GRP + g
                y_ref[j, :, h * SWA_HD:(h + 1) * SWA_HD] = o[g:g + 1, :].astype(y_ref.dtype)


def _swa_decode(zt, kbuf_t, vbuf_t, bias_dec, sinks, l, nstep=8):
    nb = zt.shape[0]
    w = WINDOW
    assert nb % nstep == 0
    kern = functools.partial(_swa_decode_kernel, nstep=nstep)
    cspec = lambda: _lspec(l, (nstep, SWA_KVH, SWA_HD, w), lambda n: (n, 0, 0, 0))
    ospec = lambda: pl.BlockSpec((nstep, SWA_KVH, SWA_HD, w), lambda n: (n, 0, 0, 0))
    return pl.pallas_call(
        kern,
        grid=(nb // nstep,),
        in_specs=[pl.BlockSpec((nb, SWA_Q), lambda n: (0, TCOL_SQ // SWA_Q)),
                  pl.BlockSpec((nb, 2 * SWA_KV), lambda n: (0, TCOL_SK // (2 * SWA_KV))),
                  cspec(), cspec(),
                  pl.BlockSpec((SWA_KVH, SWA_GRP, w), lambda n: (0, 0, 0)),
                  _lspec(l, (SWA_QH, 1), lambda n: (0, 0))],
        out_specs=[pl.BlockSpec((nstep, 1, SWA_Q), lambda n: (n, 0, 0)), ospec(), ospec()],
        out_shape=[jax.ShapeDtypeStruct((nb, 1, SWA_Q), BF16),
                   jax.ShapeDtypeStruct(kbuf_t.shape[1:], F32), jax.ShapeDtypeStruct(vbuf_t.shape[1:], F32)],
        compiler_params=_cparams(("arbitrary",)),
        name="swa_decode",
    )(zt, zt, kbuf_t, vbuf_t, bias_dec, sinks)


def _seq_chunk(t):
    for c in (688, 512, 256, 128, 64, 32, 16):
        if t % c == 0:
            return c
    raise ValueError(f"sequence length {t} must be a multiple of {GLA_CHUNK}")


def kernel(x_prompt, x_sample, state_s5_re, state_s5_im, state_gla, cache_swa_k, cache_swa_v, meta_tokens, rel_bias, norm_pre_mix, norm_post_mix, norm_pre_ffn, norm_post_ffn, w_in, s5_lam_re, s5_lam_im, s5_log_step, s5_b_re, s5_b_im, s5_c_re, s5_c_im, s5_d, s5_w_glu, s5_b_glu, gla_w_gate2, gla_b_gate2, gla_g_out, swa_sinks, w_up_s5, w_up_gla, w_up_swa, w_out, w_ff1, w_ff2):
    depth = w_in.shape[0]
    nb, seq, d = x_prompt.shape
    t = seq + N_META
    nd = x_sample.shape[0]
    assert x_sample.shape[1] == 1 and cache_swa_k.shape[2] == WINDOW and t % GLA_CHUNK == 0
    tc = _seq_chunk(t)

    w_in_t = jnp.swapaxes(w_in.astype(BF16), 1, 2)
    w_glu = s5_w_glu.astype(BF16)
    wg2 = gla_w_gate2.astype(BF16)
    wu_a, wu_b, wu_c = w_up_s5.astype(BF16), w_up_gla.astype(BF16), w_up_swa.astype(BF16)
    wo, w1, w2 = w_out.astype(BF16), w_ff1.astype(BF16), w_ff2.astype(BF16)
    row3 = lambda a: a.reshape(depth, 1, a.shape[-1])
    g_pre, g_post, g_pre_f, g_post_f = (row3(a) for a in (norm_pre_mix, norm_post_mix,
                                                           norm_pre_ffn, norm_post_ffn))
    b_glu, d5, bg2, go = row3(s5_b_glu), row3(s5_d), row3(gla_b_gate2), row3(gla_g_out)
    sinks3 = swa_sinks.reshape(depth, SWA_QH, 1)
    s5r = state_s5_re.reshape(depth, nd, -1)
    s5i = state_s5_im.reshape(depth, nd, -1)
    ckb = jnp.transpose(cache_swa_k, (0, 1, 3, 4, 2))
    cvb = jnp.transpose(cache_swa_v, (0, 1, 3, 4, 2))

    ar, ai, bbr_t, bbi_t = _s5_discretize(s5_lam_re, s5_lam_im, s5_log_step,
                                          jnp.swapaxes(s5_b_re, -1, -2), jnp.swapaxes(s5_b_im, -1, -2))
    bmat, cmat = _s5_matrices(bbr_t, bbi_t, s5_c_re, s5_c_im)
    a_half = jnp.concatenate([ar.reshape(depth, S5_NGB, 2, S5_HW), ai.reshape(depth, S5_NGB, 2, S5_HW)],
                             axis=-1)
    arow = jnp.tile(a_half, (1, 1, SUBLANES // 2, 1))
    adec = jnp.stack([ar.reshape(depth, -1), ai.reshape(depth, -1)], axis=1)

    bias = _bias_table(rel_bias)
    bias_dec = bias[1, :, WINDOW - 1, WINDOW:].reshape(SWA_KVH, SWA_GRP, WINDOW)

    meta = jnp.broadcast_to(meta_tokens.astype(x_prompt.dtype)[None], (nb, N_META, d))
    xp = jnp.concatenate([meta, x_prompt], axis=1).reshape(nb * t, d)
    xs = x_sample.reshape(nd, d)

    outs = {k: [] for k in ('p_re', 'p_im', 'p_gla', 'p_k', 'p_v', 's_re', 's_im', 's_gla', 's_k', 's_v')}
    for l in range(depth):
        zh, zhd = _norm_mm(xp, xs, g_pre, w_in_t, l, 0, HEAD_COLS, 640)
        zt, ztd = _norm_mm(xp, xs, g_pre, w_in_t, l, TAIL_START, TAIL_COLS, 2176)
        zh3 = zh.reshape(nb, t, HEAD_COLS)
        zt3 = zt.reshape(nb, t, TAIL_COLS)
        z5, st5 = _s5_prompt(zh3, bmat, cmat, arow, d5, l, tc)
        z5d, nr, ni = _s5_decode(zhd, bmat, cmat, adec, d5, s5r, s5i, l)
        ya, yad = _glu(z5.reshape(nb * t, S5_WIDTH), z5d, w_glu, b_glu, l)
        yb, pg = _gla_prompt(zh3, zt3, wg2, bg2, go, l, tc)
        ybd, sg = _gla_decode(zhd, ztd, wg2, bg2, go, state_gla, l)
        yc = _swa_prompt(zt3, bias, swa_sinks, l)
        ycd, sk, sv = _swa_decode(ztd, ckb, cvb, bias_dec, sinks3, l)
        xp, xs = _merge_out((ya, yb.reshape(nb * t, GLA_V), yc.reshape(nb * t, SWA_Q)),
                            (yad, ybd.reshape(nd, GLA_V), ycd.reshape(nd, SWA_Q)),
                            zt, ztd, wu_a, wu_b, wu_c, wo, xp, xs, g_post, l)
        xp, xs = _ffn(xp, xs, g_pre_f, w1, w2, g_post_f, l)
        st5 = st5.reshape(S5_NGB, nb, 2, 2, S5_HALF, S5_STATE).transpose(3, 1, 0, 2, 4, 5)
        st5 = st5.reshape(2, nb, S5_GROUPS, S5_STATE)
        outs['p_re'].append(st5[0])
        outs['p_im'].append(st5[1])
        outs['p_gla'].append(pg)
        outs['p_k'].append(zt3[:, t - WINDOW:, TCOL_SK:TCOL_SK + SWA_KV].reshape(nb, WINDOW, SWA_KVH, SWA_HD))
        outs['p_v'].append(zt3[:, t - WINDOW:, TCOL_SV:TCOL_SV + SWA_KV].reshape(nb, WINDOW, SWA_KVH, SWA_HD))
        outs['s_re'].append(nr.reshape(nd, S5_GROUPS, S5_STATE))
        outs['s_im'].append(ni.reshape(nd, S5_GROUPS, S5_STATE))
        outs['s_gla'].append(sg)
        outs['s_k'].append(jnp.transpose(sk, (0, 3, 1, 2)))
        outs['s_v'].append(jnp.transpose(sv, (0, 3, 1, 2)))

    y_prompt = xp.reshape(nb, t, d)[:, N_META:]
    y_sample = xs.reshape(nd, 1, d)
    st = {k: jnp.stack(v) for k, v in outs.items()}
    return (y_prompt, y_sample, st['p_re'], st['p_im'], st['p_gla'], st['p_k'], st['p_v'],
            st['s_re'], st['s_im'], st['s_gla'], st['s_k'], st['s_v'])
```

```python
import functools
import math

import numpy as np
import jax
import jax.numpy as jnp
from jax import lax
from jax.experimental import pallas as pl
from jax.experimental.pallas import tpu as pltpu

F32 = jnp.float32
BF16 = jnp.bfloat16

D_MODEL = 2048
N_META = 16
S5_GROUP = 16
S5_WIDTH = D_MODEL // 2
S5_GROUPS = S5_WIDTH // S5_GROUP
S5_STATE = 64
GLA_HEADS = 4
GLA_DK = D_MODEL // 16
GLA_DV = D_MODEL // 8
GLA_RANK = 16
GLA_TAU = 16.0
GLA_CHUNK = 16
SWA_QH = 16
SWA_KVH = 4
SWA_GRP = SWA_QH // SWA_KVH
SWA_HD = 64
WINDOW = 128
N_BUCKETS = 32
MAX_DISTANCE = WINDOW
D_FF = 4 * D_MODEL
N_BRANCH = 3
EPS = 1e-6

GLA_QK = GLA_HEADS * GLA_DK
GLA_V = GLA_HEADS * GLA_DV
SWA_Q = SWA_QH * SWA_HD
SWA_KV = SWA_KVH * SWA_HD

COL_U = 0
COL_GQ = COL_U + S5_WIDTH
COL_GK = COL_GQ + GLA_QK
COL_GV = COL_GK + GLA_QK
COL_GA = COL_GV + GLA_V
LANES = 128
SUBLANES = 8
HEAD_COLS = COL_GA + LANES
TAIL_START = COL_GA + GLA_RANK
TCOL_R = 0
TCOL_SQ = TCOL_R + GLA_V
TCOL_SK = TCOL_SQ + SWA_Q
TCOL_SV = TCOL_SK + SWA_KV
TCOL_GATE = TCOL_SV + SWA_KV
TAIL_COLS = TCOL_GATE + N_BRANCH * D_MODEL

S5_GB = 8
S5_NGB = S5_GROUPS // S5_GB
S5_HALF = S5_GB // 2
S5_HW = S5_HALF * S5_STATE

VMEM_LIMIT = 60 * 1024 * 1024


def _cparams(sem):
    return pltpu.CompilerParams(dimension_semantics=sem, vmem_limit_bytes=VMEM_LIMIT)


def _rms(x, g):
    ms = jnp.mean(x * x, axis=-1, keepdims=True)
    return (x * lax.rsqrt(ms + EPS)) * g


def _row_tile(m):
    for t in (688, 512, 256, 128, 64, 32, 16, 8):
        if m % t == 0:
            return t
    return m


def _lspec(l, shape, imap):
    return pl.BlockSpec((None,) + tuple(shape), lambda *idx: (l,) + tuple(imap(*idx)))


class _Tiles:
    def __init__(self, m, n_inner):
        self.tm = _row_tile(m)
        self.npt = m // self.tm
        self.grid = (self.npt, n_inner)

    def dstep(self, i, j):
        return jnp.where(i == self.npt - 1, j, 0)


def _with_decode(npt, prompt_fn, decode_fn):
    prompt_fn()
    pl.when(pl.program_id(0) == npt - 1)(decode_fn)


def _norm_mm_kernel(xp_ref, xd_ref, g_ref, w_ref, op_ref, od_ref, hp_ref, hd_ref, *, npt):
    def run(x_ref, o_ref, h_ref):
        @pl.when(pl.program_id(1) == 0)
        def _():
            h_ref[...] = _rms(x_ref[...], g_ref[...]).astype(BF16)

        o_ref[...] = lax.dot_general(h_ref[...], w_ref[0], (((1,), (1,)), ((), ())),
                                     preferred_element_type=F32)

    _with_decode(npt, lambda: run(xp_ref, op_ref, hp_ref), lambda: run(xd_ref, od_ref, hd_ref))


def _norm_mm(xp, xd, g, wt, l, col0, n_cols, tn):
    m, k = xp.shape
    nd = xd.shape[0]
    tl = _Tiles(m, n_cols // tn)
    kern = functools.partial(_norm_mm_kernel, npt=tl.npt)
    return pl.pallas_call(
        kern,
        grid=tl.grid,
        in_specs=[pl.BlockSpec((tl.tm, k), lambda i, j: (i, 0)),
                  pl.BlockSpec((nd, k), lambda i, j: (0, 0)),
                  _lspec(l, (1, k), lambda i, j: (0, 0)),
                  pl.BlockSpec((pl.Element(1), pl.Element(tn), pl.Element(k)),
                               lambda i, j: (l, pl.multiple_of(col0 + j * tn, 16), 0))],
        out_specs=[pl.BlockSpec((tl.tm, tn), lambda i, j: (i, j)),
                   pl.BlockSpec((nd, tn), lambda i, j: (0, tl.dstep(i, j)))],
        out_shape=[jax.ShapeDtypeStruct((m, n_cols), F32), jax.ShapeDtypeStruct((nd, n_cols), F32)],
        scratch_shapes=[pltpu.VMEM((tl.tm, k), BF16), pltpu.VMEM((nd, k), BF16)],
        compiler_params=_cparams(("arbitrary", "arbitrary")),
        name="norm_mm",
    )(xp, xd, g, wt)


def _glu_kernel(zp_ref, zd_ref, zcp_ref, zcd_ref, w_ref, b_ref, op_ref, od_ref, hp_ref, hd_ref, *, npt):
    def run(z_ref, zc_ref, o_ref, h_ref):
        @pl.when(pl.program_id(1) == 0)
        def _():
            h_ref[...] = z_ref[...].astype(BF16)

        a = jnp.dot(h_ref[...], w_ref[...], preferred_element_type=F32) + b_ref[...]
        o_ref[...] = (zc_ref[...] * jax.nn.sigmoid(a)).astype(o_ref.dtype)

    _with_decode(npt, lambda: run(zp_ref, zcp_ref, op_ref, hp_ref),
                 lambda: run(zd_ref, zcd_ref, od_ref, hd_ref))


def _glu(zp, zd, w, b, l, tn=512):
    m, k = zp.shape
    nd = zd.shape[0]
    tl = _Tiles(m, k // tn)
    kern = functools.partial(_glu_kernel, npt=tl.npt)
    return pl.pallas_call(
        kern,
        grid=tl.grid,
        in_specs=[pl.BlockSpec((tl.tm, k), lambda i, j: (i, 0)),
                  pl.BlockSpec((nd, k), lambda i, j: (0, 0)),
                  pl.BlockSpec((tl.tm, tn), lambda i, j: (i, j)),
                  pl.BlockSpec((nd, tn), lambda i, j: (0, tl.dstep(i, j))),
                  _lspec(l, (k, tn), lambda i, j: (0, j)),
                  _lspec(l, (1, tn), lambda i, j: (0, j))],
        out_specs=[pl.BlockSpec((tl.tm, tn), lambda i, j: (i, j)),
                   pl.BlockSpec((nd, tn), lambda i, j: (0, tl.dstep(i, j)))],
        out_shape=[jax.ShapeDtypeStruct((m, k), BF16), jax.ShapeDtypeStruct((nd, k), BF16)],
        scratch_shapes=[pltpu.VMEM((tl.tm, k), BF16), pltpu.VMEM((nd, k), BF16)],
        compiler_params=_cparams(("arbitrary", "arbitrary")),
        name="s5_glu",
    )(zp, zd, zp, zd, w, b)


def _merge_out_kernel(yap_ref, ybp_ref, ycp_ref, yad_ref, ybd_ref, ycd_ref,
                      gap_ref, gbp_ref, gcp_ref, gad_ref, gbd_ref, gcd_ref,
                      wa_ref, wb_ref, wc_ref, wo_ref, xp_ref, xd_ref, g_ref,
                      op_ref, od_ref, mp_ref, md_ref, *, npt, nk, tk, tn):
    s = pl.program_id(1)

    def run(ya_ref, yb_ref, yc_ref, ga_ref, gb_ref, gc_ref, x_ref, o_ref, m_ref):
        @pl.when(s < nk)
        def _():
            m = (jax.nn.sigmoid(ga_ref[...]) * jnp.dot(ya_ref[...], wa_ref[...], preferred_element_type=F32)
                 + jax.nn.sigmoid(gb_ref[...]) * jnp.dot(yb_ref[...], wb_ref[...], preferred_element_type=F32)
                 + jax.nn.sigmoid(gc_ref[...]) * jnp.dot(yc_ref[...], wc_ref[...], preferred_element_type=F32))
            m_ref[:, pl.ds(pl.multiple_of(s * tk, tk), tk)] = m.astype(BF16)

        @pl.when(s >= nk)
        def _():
            c0 = pl.multiple_of((s - nk) * tn, tn)
            o_ref[:, pl.ds(c0, tn)] = jnp.dot(m_ref[...], wo_ref[...], preferred_element_type=F32)

        @pl.when(s == pl.num_programs(1) - 1)
        def _():
            o_ref[...] = x_ref[...] + _rms(o_ref[...], g_ref[...])

    _with_decode(
        npt,
        lambda: run(yap_ref, ybp_ref, ycp_ref, gap_ref, gbp_ref, gcp_ref, xp_ref, op_ref, mp_ref),
        lambda: run(yad_ref, ybd_ref, ycd_ref, gad_ref, gbd_ref, gcd_ref, xd_ref, od_ref, md_ref))


def _merge_out(yp, yd, ztp, ztd, wa, wb, wc, wo, xp, xd, g, l, tk=256, tn=512):
    m, d = xp.shape
    nd = xd.shape[0]
    kb = yp[0].shape[1]
    gate0 = TCOL_GATE // tk
    nk = d // tk
    tl = _Tiles(m, nk + d // tn)
    kidx = lambda s: jnp.minimum(s, nk - 1)
    nidx = lambda s: jnp.maximum(s - nk, 0)
    gp_spec = lambda br: pl.BlockSpec((tl.tm, tk), lambda i, s: (i, gate0 + br * nk + kidx(s)))
    gd_spec = lambda br: pl.BlockSpec(
        (nd, tk), lambda i, s: (0, gate0 + br * nk + kidx(tl.dstep(i, s))))
    yp_spec = pl.BlockSpec((tl.tm, kb), lambda i, s: (i, 0))
    yd_spec = pl.BlockSpec((nd, kb), lambda i, s: (0, 0))
    w_spec = _lspec(l, (kb, tk), lambda i, s: (0, kidx(s)))
    kern = functools.partial(_merge_out_kernel, npt=tl.npt, nk=nk, tk=tk, tn=tn)
    return pl.pallas_call(
        kern,
        grid=tl.grid,
        in_specs=[yp_spec, yp_spec, yp_spec, yd_spec, yd_spec, yd_spec,
                  gp_spec(0), gp_spec(1), gp_spec(2), gd_spec(0), gd_spec(1), gd_spec(2),
                  w_spec, w_spec, w_spec,
                  _lspec(l, (d, tn), lambda i, s: (0, nidx(s))),
                  pl.BlockSpec((tl.tm, d), lambda i, s: (i, 0)),
                  pl.BlockSpec((nd, d), lambda i, s: (0, 0)),
                  _lspec(l, (1, d), lambda i, s: (0, 0))],
        out_specs=[pl.BlockSpec((tl.tm, d), lambda i, s: (i, 0)),
                   pl.BlockSpec((nd, d), lambda i, s: (0, 0))],
        out_shape=[jax.ShapeDtypeStruct((m, d), F32), jax.ShapeDtypeStruct((nd, d), F32)],
        scratch_shapes=[pltpu.VMEM((tl.tm, d), BF16), pltpu.VMEM((nd, d), BF16)],
        compiler_params=_cparams(("arbitrary", "arbitrary")),
        name="merge_out",
    )(*yp, *yd, ztp, ztp, ztp, ztd, ztd, ztd, wa, wb, wc, wo, xp, xd, g)


def _ffn_kernel(xp_ref, xd_ref, g1_ref, w1_ref, w2_ref, g2_ref, op_ref, od_ref,
                hp_ref, accp_ref, hd_ref, accd_ref, *, npt):
    f = pl.program_id(1)

    def run(x_ref, o_ref, h_ref, acc_ref):
        @pl.when(f == 0)
        def _():
            h_ref[...] = _rms(x_ref[...], g1_ref[...]).astype(BF16)
            acc_ref[...] = jnp.zeros_like(acc_ref)

        a = jnp.dot(h_ref[...], w1_ref[...], preferred_element_type=F32)
        a = jnp.square(jnp.maximum(a, 0.0)).astype(BF16)
        acc_ref[...] += jnp.dot(a, w2_ref[...], preferred_element_type=F32)

        @pl.when(f == pl.num_programs(1) - 1)
        def _():
            o_ref[...] = x_ref[...] + _rms(acc_ref[...], g2_ref[...])

    _with_decode(npt, lambda: run(xp_ref, op_ref, hp_ref, accp_ref),
                 lambda: run(xd_ref, od_ref, hd_ref, accd_ref))


def _ffn(xp, xd, g1, w1, w2, g2, l, tf=1024):
    m, d = xp.shape
    nd = xd.shape[0]
    ff = w1.shape[2]
    tl = _Tiles(m, ff // tf)
    kern = functools.partial(_ffn_kernel, npt=tl.npt)
    return pl.pallas_call(
        kern,
        grid=tl.grid,
        in_specs=[pl.BlockSpec((tl.tm, d), lambda i, f: (i, 0)),
                  pl.BlockSpec((nd, d), lambda i, f: (0, 0)),
                  _lspec(l, (1, d), lambda i, f: (0, 0)),
                  _lspec(l, (d, tf), lambda i, f: (0, f)),
                  _lspec(l, (tf, d), lambda i, f: (f, 0)),
                  _lspec(l, (1, d), lambda i, f: (0, 0))],
        out_specs=[pl.BlockSpec((tl.tm, d), lambda i, f: (i, 0)),
                   pl.BlockSpec((nd, d), lambda i, f: (0, 0))],
        out_shape=[jax.ShapeDtypeStruct((m, d), F32), jax.ShapeDtypeStruct((nd, d), F32)],
        scratch_shapes=[pltpu.VMEM((tl.tm, d), BF16), pltpu.VMEM((tl.tm, d), F32),
                        pltpu.VMEM((nd, d), BF16), pltpu.VMEM((nd, d), F32)],
        compiler_params=_cparams(("arbitrary", "arbitrary")),
        name="ffn",
    )(xp, xd, g1, w1, w2, g2)


def _s5_disc_kernel(lr_ref, li_ref, ls_ref, br_ref, bi_ref, ar_ref, ai_ref, bbr_ref, bbi_ref):
    lr = lr_ref[0]
    li = li_ref[0]
    dt = jnp.exp(ls_ref[0])
    mag = jnp.exp(lr * dt)
    ar = mag * jnp.cos(li * dt)
    ai = mag * jnp.sin(li * dt)
    den = lr * lr + li * li
    fr = ((ar - 1.0) * lr + ai * li) / den
    fi = (ai * lr - (ar - 1.0) * li) / den
    ar_ref[0] = ar
    ai_ref[0] = ai
    br = br_ref[0]
    bi = bi_ref[0]
    bbr_ref[0] = fr[:, None, :] * br - fi[:, None, :] * bi
    bbi_ref[0] = fr[:, None, :] * bi + fi[:, None, :] * br


def _s5_discretize(lam_re, lam_im, log_step, b_re_t, b_im_t):
    depth = lam_re.shape[0]
    gp = pl.BlockSpec((1, S5_GROUPS, S5_STATE), lambda l: (l, 0, 0))
    bspec = pl.BlockSpec((1, S5_GROUPS, S5_GROUP, S5_STATE), lambda l: (l, 0, 0, 0))
    return pl.pallas_call(
        _s5_disc_kernel,
        grid=(depth,),
        in_specs=[gp, gp, pl.BlockSpec((1, S5_GROUPS, 1), lambda l: (l, 0, 0)), bspec, bspec],
        out_specs=[gp, gp, bspec, bspec],
        out_shape=[jax.ShapeDtypeStruct(lam_re.shape, F32), jax.ShapeDtypeStruct(lam_re.shape, F32),
                   jax.ShapeDtypeStruct(b_re_t.shape, F32), jax.ShapeDtypeStruct(b_re_t.shape, F32)],
        compiler_params=_cparams(("parallel",)),
        name="s5_discretize",
    )(lam_re, lam_im, log_step.reshape(depth, S5_GROUPS, 1), b_re_t, b_im_t)


def _blockdiag(r):
    eye = jnp.eye(S5_HALF, dtype=r.dtype)
    z = jnp.einsum('ab,lgqbhp->lgqahbp', eye, r)
    s = r.shape
    return z.reshape(s[0], s[1], s[2], S5_HALF * s[4], S5_HALF * s[5])


def _s5_matrices(bbr_t, bbi_t, c_re, c_im):
    depth = bbr_t.shape[0]
    shp = (depth, S5_NGB, 2, S5_HALF, S5_GROUP, S5_STATE)
    bre = _blockdiag(bbr_t.reshape(shp))
    bim = _blockdiag(bbi_t.reshape(shp))
    bcat = jnp.concatenate([bre, bim], axis=-1)
    zb = jnp.zeros_like(bcat[:, :, 0])
    bmat = jnp.stack([jnp.concatenate([bcat[:, :, 0], zb], axis=-2),
                      jnp.concatenate([zb, bcat[:, :, 1]], axis=-2)], axis=2)
    cre = jnp.swapaxes(_blockdiag(c_re.reshape(shp)), -1, -2)
    cim = jnp.swapaxes(_blockdiag(c_im.reshape(shp)), -1, -2)
    ccat = jnp.concatenate([cre, -cim], axis=-2)
    zc = jnp.zeros_like(ccat[:, :, 0])
    cmat = jnp.stack([jnp.concatenate([ccat[:, :, 0], zc], axis=-1),
                      jnp.concatenate([zc, ccat[:, :, 1]], axis=-1)], axis=2)
    return bmat.astype(BF16), cmat.astype(BF16)


def _s5_prompt_kernel(u_ref, b_ref, c_ref, a_ref, d_ref, z_ref, st_ref, e_ref, x_ref, *, nb, tc):
    @pl.when(pl.program_id(1) == 0)
    def _():
        x_ref[...] = jnp.zeros_like(x_ref)

    nlb = 2 * S5_HW // LANES
    hlb = nlb // 2
    for n in range(nb):
        ub = u_ref[n].astype(BF16)
        for q in range(2):
            e = jnp.dot(ub, b_ref[0, q], preferred_element_type=F32)
            for j in range(nlb):
                e_ref[j, pl.ds(n * 2 + q, tc, stride=SUBLANES), :] = e[:, j * LANES:(j + 1) * LANES]

    a = a_ref[0]
    ar = [a[:, j * LANES:(j + 1) * LANES] for j in range(hlb)]
    ai = [a[:, (hlb + j) * LANES:(hlb + j + 1) * LANES] for j in range(hlb)]

    def step(t, carry):
        row = pl.multiple_of(t * SUBLANES, SUBLANES)
        new = []
        for j in range(hlb):
            xr, xi = carry[2 * j], carry[2 * j + 1]
            nxr = ar[j] * xr - ai[j] * xi + e_ref[j, pl.ds(row, SUBLANES), :]
            nxi = ar[j] * xi + ai[j] * xr + e_ref[hlb + j, pl.ds(row, SUBLANES), :]
            e_ref[j, pl.ds(row, SUBLANES), :] = nxr
            e_ref[hlb + j, pl.ds(row, SUBLANES), :] = nxi
            new += [nxr, nxi]
        return tuple(new)

    x0 = x_ref[...]
    init = []
    for j in range(hlb):
        init += [x0[:, j * LANES:(j + 1) * LANES], x0[:, (hlb + j) * LANES:(hlb + j + 1) * LANES]]
    fin = lax.fori_loop(0, tc, step, tuple(init), unroll=8)
    for j in range(hlb):
        x_ref[:, j * LANES:(j + 1) * LANES] = fin[2 * j]
        x_ref[:, (hlb + j) * LANES:(hlb + j + 1) * LANES] = fin[2 * j + 1]
    st_ref[0] = x_ref[...]

    for n in range(nb):
        y = d_ref[...] * u_ref[n]
        for q in range(2):
            xs = jnp.concatenate(
                [e_ref[j, pl.ds(n * 2 + q, tc, stride=SUBLANES), :] for j in range(nlb)], axis=1)
            y = y + jnp.dot(xs.astype(BF16), c_ref[0, q], preferred_element_type=F32)
        z_ref[n] = jax.nn.gelu(y)


def _s5_prompt(zh3, bmat, cmat, arow, d, l, tc):
    nb, t, _ = zh3.shape
    assert 2 * nb == SUBLANES, "state rows are (batch, half) pairs filling one sublane tile"
    kern = functools.partial(_s5_prompt_kernel, nb=nb, tc=tc)
    return pl.pallas_call(
        kern,
        grid=(S5_NGB, t // tc),
        in_specs=[pl.BlockSpec((nb, tc, LANES), lambda g, c: (0, c, g)),
                  _lspec(l, (1, 2, LANES, 2 * S5_HW), lambda g, c: (g, 0, 0, 0)),
                  _lspec(l, (1, 2, 2 * S5_HW, LANES), lambda g, c: (g, 0, 0, 0)),
                  _lspec(l, (1, SUBLANES, 2 * S5_HW), lambda g, c: (g, 0, 0)),
                  _lspec(l, (1, LANES), lambda g, c: (0, g))],
        out_specs=[pl.BlockSpec((nb, tc, LANES), lambda g, c: (0, c, g)),
                   pl.BlockSpec((1, SUBLANES, 2 * S5_HW), lambda g, c: (g, 0, 0))],
        out_shape=[jax.ShapeDtypeStruct((nb, t, S5_WIDTH), F32),
                   jax.ShapeDtypeStruct((S5_NGB, SUBLANES, 2 * S5_HW), F32)],
        scratch_shapes=[pltpu.VMEM((2 * S5_HW // LANES, tc * SUBLANES, LANES), F32),
                        pltpu.VMEM((SUBLANES, 2 * S5_HW), F32)],
        compiler_params=_cparams(("parallel", "arbitrary")),
        name="s5_prompt",
    )(zh3, bmat, cmat, arow, d)


def _s5_decode_kernel(u_ref, b_ref, c_ref, a_ref, d_ref, sr_ref, si_ref, z_ref, nr_ref, ni_ref):
    for g in range(S5_NGB):
        u = u_ref[:, g * LANES:(g + 1) * LANES]
        ub = u.astype(BF16)
        y = d_ref[:, g * LANES:(g + 1) * LANES] * u
        for q in range(2):
            c0 = (g * 2 + q) * S5_HW
            e = jnp.dot(ub, b_ref[g, q], preferred_element_type=F32)
            ar = a_ref[0:1, c0:c0 + S5_HW]
            ai = a_ref[1:2, c0:c0 + S5_HW]
            x0r = sr_ref[:, c0:c0 + S5_HW]
            x0i = si_ref[:, c0:c0 + S5_HW]
            nr = ar * x0r - ai * x0i + e[:, :S5_HW]
            ni = ar * x0i + ai * x0r + e[:, S5_HW:]
            nr_ref[:, c0:c0 + S5_HW] = nr
            ni_ref[:, c0:c0 + S5_HW] = ni
            xs = jnp.concatenate([nr, ni], axis=1).astype(BF16)
            y = y + jnp.dot(xs, c_ref[g, q], preferred_element_type=F32)
        z_ref[:, g * LANES:(g + 1) * LANES] = jax.nn.gelu(y)


def _s5_decode(zh, bmat, cmat, adec, d, sr, si, l):
    nb = zh.shape[0]
    ns = S5_GROUPS * S5_STATE
    full = lambda shape: pl.BlockSpec(shape, lambda i: (0,) * len(shape))
    lfull = lambda shape: _lspec(l, shape, lambda i: (0,) * len(shape))
    return pl.pallas_call(
        _s5_decode_kernel,
        grid=(1,),
        in_specs=[full((nb, S5_WIDTH)), lfull(bmat.shape[1:]), lfull(cmat.shape[1:]), lfull((2, ns)),
                  lfull((1, S5_WIDTH)), lfull((nb, ns)), lfull((nb, ns))],
        out_specs=[full((nb, S5_WIDTH)), full((nb, ns)), full((nb, ns))],
        out_shape=[jax.ShapeDtypeStruct((nb, S5_WIDTH), F32),
                   jax.ShapeDtypeStruct((nb, ns), F32), jax.ShapeDtypeStruct((nb, ns), F32)],
        compiler_params=_cparams(("arbitrary",)),
        name="s5_decode",
    )(zh, bmat, cmat, adec, d, sr, si)


def _gla_finish(o, r, g):
    return _rms(o, g) * (r * jax.nn.sigmoid(r))


def _gla_prompt_kernel(q_ref, k_ref, v_ref, a_ref, r_ref, wg_ref, bg_ref, go_ref, y_ref, st_ref,
                       s_ref, lg_ref, o_ref, qi_ref, ku_ref, cd_ref, *, tc):
    c = pl.program_id(1)

    @pl.when(c == 0)
    def _():
        s_ref[...] = jnp.zeros_like(s_ref)

    a16 = a_ref[0][:, :GLA_RANK].astype(BF16)
    lg_ref[...] = jax.nn.log_sigmoid(
        jnp.dot(a16, wg_ref[...], preferred_element_type=F32) + bg_ref[...]) / GLA_TAU

    ch = GLA_CHUNK
    hs = SUBLANES
    rowi = lax.broadcasted_iota(jnp.int32, (ch, GLA_DK), 0)
    rowh = lax.broadcasted_iota(jnp.int32, (hs, GLA_DK), 0)
    coli = lax.broadcasted_iota(jnp.int32, (hs, ch), 1)
    scale = GLA_DK ** -0.5

    def prep(idx, carry):
        r0 = pl.multiple_of(idx * ch, ch)
        for h in range(GLA_HEADS):
            ks = slice(h * GLA_DK, (h + 1) * GLA_DK)
            vs = slice(h * GLA_DV, (h + 1) * GLA_DV)
            b = lg_ref[pl.ds(r0, ch), ks]
            for sh in (1, 2, 4, 8):
                b = b + jnp.where(rowi >= sh, pltpu.roll(b, sh, axis=0), 0.0)
            qh = q_ref[0, pl.ds(r0, ch), ks] * scale
            kh = k_ref[0, pl.ds(r0, ch), ks]
            halves = []
            for half in range(ch // hs):
                bq = b[half * hs:(half + 1) * hs]
                qq = qh[half * hs:(half + 1) * hs]
                att = jnp.zeros((hs, ch), F32)
                for s in range((half + 1) * hs):
                    diff = bq - b[s:s + 1, :]
                    if s >= half * hs:
                        diff = jnp.where(rowh + half * hs >= s, diff, -jnp.inf)
                    col = jnp.sum(qq * kh[s:s + 1, :] * jnp.exp(diff), axis=1, keepdims=True)
                    att = jnp.where(coli == s, col, att)
                halves.append(att)
            att = jnp.concatenate(halves, axis=0).astype(BF16)
            vh = v_ref[0, pl.ds(r0, ch), vs].astype(BF16)
            o_ref[pl.ds(r0, ch), vs] = jnp.dot(att, vh, preferred_element_type=F32)
            blast = b[ch - 1:ch, :]
            qi_ref[pl.ds(r0, ch), ks] = (qh * jnp.exp(b)).astype(BF16)
            ku_ref[pl.ds(r0, ch), ks] = (kh * jnp.exp(blast - b)).astype(BF16)
            cd_ref[pl.ds(pl.multiple_of(idx * hs, hs), hs), ks] = jnp.broadcast_to(
                jnp.exp(blast), (hs, GLA_DK))
        return carry

    lax.fori_loop(0, tc // ch, prep, 0, unroll=2)

    def scan(idx, carry):
        r0 = pl.multiple_of(idx * ch, ch)
        for h in range(GLA_HEADS):
            ks = slice(h * GLA_DK, (h + 1) * GLA_DK)
            vs = slice(h * GLA_DV, (h + 1) * GLA_DV)
            st = s_ref[h]
            inter = lax.dot_general(qi_ref[pl.ds(r0, ch), ks], st.astype(BF16),
                                    (((1,), (1,)), ((), ())), preferred_element_type=F32)
            o_ref[pl.ds(r0, ch), vs] += inter
            vh = v_ref[0, pl.ds(r0, ch), vs].astype(BF16)
            kv = lax.dot_general(vh, ku_ref[pl.ds(r0, ch), ks], (((0,), (0,)), ((), ())),
                                 preferred_element_type=F32)
            cd = cd_ref[pl.ds(pl.multiple_of(idx * hs, hs), 1), ks]
            s_ref[h] = cd * st + kv
        return carry

    lax.fori_loop(0, tc // ch, scan, 0, unroll=2)

    for h in range(GLA_HEADS):
        vs = slice(h * GLA_DV, (h + 1) * GLA_DV)
        y_ref[0, :, vs] = _gla_finish(o_ref[:, vs], r_ref[0, :, vs], go_ref[...]).astype(y_ref.dtype)
        st_ref[0, h] = s_ref[h].T


def _gla_prompt(zh3, zt3, wg, bg, go, l, tc):
    nb, t, _ = zh3.shape
    kern = functools.partial(_gla_prompt_kernel, tc=tc)
    return pl.pallas_call(
        kern,
        grid=(nb, t // tc),
        in_specs=[pl.BlockSpec((1, tc, GLA_QK), lambda n, c: (n, c, COL_GQ // GLA_QK)),
                  pl.BlockSpec((1, tc, GLA_QK), lambda n, c: (n, c, COL_GK // GLA_QK)),
                  pl.BlockSpec((1, tc, GLA_V), lambda n, c: (n, c, COL_GV // GLA_V)),
                  pl.BlockSpec((1, tc, LANES), lambda n, c: (n, c, COL_GA // LANES)),
                  pl.BlockSpec((1, tc, GLA_V), lambda n, c: (n, c, TCOL_R // GLA_V)),
                  _lspec(l, (GLA_RANK, GLA_QK), lambda n, c: (0, 0)),
                  _lspec(l, (1, GLA_QK), lambda n, c: (0, 0)),
                  _lspec(l, (1, GLA_DV), lambda n, c: (0, 0))],
        out_specs=[pl.BlockSpec((1, tc, GLA_V), lambda n, c: (n, c, 0)),
                   pl.BlockSpec((1, GLA_HEADS, GLA_DK, GLA_DV), lambda n, c: (n, 0, 0, 0))],
        out_shape=[jax.ShapeDtypeStruct((nb, t, GLA_V), BF16),
                   jax.ShapeDtypeStruct((nb, GLA_HEADS, GLA_DK, GLA_DV), F32)],
        scratch_shapes=[pltpu.VMEM((GLA_HEADS, GLA_DV, GLA_DK), F32),
                        pltpu.VMEM((tc, GLA_QK), F32),
                        pltpu.VMEM((tc, GLA_V), F32),
                        pltpu.VMEM((tc, GLA_QK), BF16),
                        pltpu.VMEM((tc, GLA_QK), BF16),
                        pltpu.VMEM((tc // GLA_CHUNK * SUBLANES, GLA_QK), F32)],
        compiler_params=_cparams(("parallel", "arbitrary")),
        name="gla_prompt",
    )(zh3, zh3, zh3, zh3, zt3, wg, bg, go)


def _gla_decode_kernel(zh_ref, r_ref, wg_ref, bg_ref, go_ref, s_ref, y_ref, so_ref, *, nstep):
    n0 = pl.multiple_of(pl.program_id(0) * nstep, nstep)
    rows = zh_ref[pl.ds(n0, nstep), :]
    a16 = rows[:, COL_GA:COL_GA + GLA_RANK].astype(BF16)
    lg = jax.nn.log_sigmoid(
        jnp.dot(a16, wg_ref[...], preferred_element_type=F32) + bg_ref[...]) / GLA_TAU
    eg = jnp.exp(lg)
    rr = r_ref[pl.ds(n0, nstep), :]
    ri = lax.broadcasted_iota(jnp.int32, (GLA_DK, GLA_DK), 0)
    ci = lax.broadcasted_iota(jnp.int32, (GLA_DK, GLA_DK), 1)
    eye = (ri == ci).astype(F32)

    def col(v):
        return jnp.sum(eye * v, axis=1, keepdims=True)

    for j in range(nstep):
        row = rows[j:j + 1]
        for h in range(GLA_HEADS):
            ks = slice(h * GLA_DK, (h + 1) * GLA_DK)
            qh = row[:, COL_GQ + h * GLA_DK:COL_GQ + (h + 1) * GLA_DK] * (GLA_DK ** -0.5)
            kh = row[:, COL_GK + h * GLA_DK:COL_GK + (h + 1) * GLA_DK]
            vh = row[:, COL_GV + h * GLA_DV:COL_GV + (h + 1) * GLA_DV]
            s = col(eg[j:j + 1, ks]) * s_ref[j, h] + col(kh) * vh
            so_ref[j, h] = s
            o = jnp.sum(col(qh) * s, axis=0, keepdims=True)
            vs = slice(h * GLA_DV, (h + 1) * GLA_DV)
            y_ref[j, :, vs] = _gla_finish(o, rr[j:j + 1, vs], go_ref[...]).astype(y_ref.dtype)


def _gla_decode(zh, zt, wg, bg, go, s0, l, nstep=SUBLANES):
    nb = zh.shape[0]
    assert nb % nstep == 0
    kern = functools.partial(_gla_decode_kernel, nstep=nstep)
    sblock = (nstep, GLA_HEADS, GLA_DK, GLA_DV)
    return pl.pallas_call(
        kern,
        grid=(nb // nstep,),
        in_specs=[pl.BlockSpec((nb, HEAD_COLS), lambda n: (0, 0)),
                  pl.BlockSpec((nb, GLA_V), lambda n: (0, TCOL_R // GLA_V)),
                  _lspec(l, (GLA_RANK, GLA_QK), lambda n: (0, 0)),
                  _lspec(l, (1, GLA_QK), lambda n: (0, 0)),
                  _lspec(l, (1, GLA_DV), lambda n: (0, 0)),
                  _lspec(l, sblock, lambda n: (n, 0, 0, 0))],
        out_specs=[pl.BlockSpec((nstep, 1, GLA_V), lambda n: (n, 0, 0)),
                   pl.BlockSpec(sblock, lambda n: (n, 0, 0, 0))],
        out_shape=[jax.ShapeDtypeStruct((nb, 1, GLA_V), BF16),
                   jax.ShapeDtypeStruct(s0.shape[1:], F32)],
        compiler_params=_cparams(("arbitrary",)),
        name="gla_decode",
    )(zh, zt, wg, bg, go, s0)


def _bucket_table():
    dist = (np.arange(WINDOW)[:, None] + WINDOW) - np.arange(2 * WINDOW)[None, :]
    max_exact = N_BUCKETS // 2
    d = np.maximum(dist, 0)
    ratio = np.log(np.maximum(d, 1).astype(np.float32) / np.float32(max_exact)) / np.float32(
        math.log(MAX_DISTANCE / max_exact))
    large = max_exact + (ratio.astype(np.float32) * np.float32(N_BUCKETS - max_exact)).astype(np.int32)
    large = np.minimum(large, N_BUCKETS - 1)
    return np.where(d < max_exact, d, large).astype(np.int32)


def _bias_kernel(bk_ref, rb_ref, o_ref):
    h = pl.program_id(1)
    bk = bk_ref[...]
    acc = jnp.zeros(bk.shape, F32)
    for b in range(N_BUCKETS):
        acc = jnp.where(bk == b, rb_ref[b, h], acc)
    cols = lax.broadcasted_iota(jnp.int32, bk.shape, 1)
    dist = WINDOW + lax.broadcasted_iota(jnp.int32, bk.shape, 0) - cols
    band = (dist >= 0) & (dist < WINDOW)
    keep = band & ((cols >= WINDOW) | (pl.program_id(0) == 1))
    o_ref[0, 0] = jnp.where(keep, acc, -jnp.inf)


def _bias_table(rel_bias):
    bk = jnp.asarray(_bucket_table())
    return pl.pallas_call(
        _bias_kernel,
        grid=(2, SWA_QH),
        in_specs=[pl.BlockSpec((WINDOW, 2 * WINDOW), lambda v, h: (0, 0)),
                  pl.BlockSpec(memory_space=pltpu.SMEM)],
        out_specs=pl.BlockSpec((1, 1, WINDOW, 2 * WINDOW), lambda v, h: (v, h, 0, 0)),
        out_shape=jax.ShapeDtypeStruct((2, SWA_QH, WINDOW, 2 * WINDOW), F32),
        compiler_params=_cparams(("parallel", "parallel")),
        name="swa_bias",
    )(bk, rel_bias)


def _swa_prompt_kernel(q_ref, kp_ref, kc_ref, vp_ref, vc_ref, bias_ref, sink_ref, y_ref, *, t, l):
    i = pl.program_id(1)
    w = WINDOW
    inseq = i * w + lax.broadcasted_iota(jnp.int32, (w, 1), 0) < t
    scale = SWA_HD ** -0.5
    for kv in range(SWA_KVH):
        hs = slice(kv * SWA_HD, (kv + 1) * SWA_HD)
        kh = jnp.concatenate([kp_ref[0, :, hs], jnp.where(inseq, kc_ref[0, :, hs], 0.0)],
                             axis=0).astype(BF16)
        vh = jnp.concatenate([vp_ref[0, :, hs], jnp.where(inseq, vc_ref[0, :, hs], 0.0)],
                             axis=0).astype(BF16)
        for g in range(SWA_GRP):
            h = kv * SWA_GRP + g
            qh = (q_ref[0, :, h * SWA_HD:(h + 1) * SWA_HD] * scale).astype(BF16)
            s = lax.dot_general(qh, kh, (((1,), (1,)), ((), ())), preferred_element_type=F32)
            s = s + bias_ref[h]
            sink = sink_ref[l, h]
            m = jnp.maximum(jnp.max(s, axis=-1, keepdims=True), sink)
            p = jnp.exp(s - m)
            den = jnp.sum(p, axis=-1, keepdims=True) + jnp.exp(sink - m)
            o = jnp.dot(p.astype(BF16), vh, preferred_element_type=F32) * (1.0 / den)
            y_ref[0, :, h * SWA_HD:(h + 1) * SWA_HD] = o.astype(y_ref.dtype)


def _swa_prompt(zt3, bias, sinks, l):
    nb, t, _ = zt3.shape
    w = WINDOW
    nblk = pl.cdiv(t, w)
    kern = functools.partial(_swa_prompt_kernel, t=t, l=l)
    kcol = TCOL_SK // SWA_KV
    vcol = TCOL_SV // SWA_KV
    prev = lambda i: jnp.maximum(i - 1, 0)
    return pl.pallas_call(
        kern,
        grid=(nb, nblk),
        in_specs=[pl.BlockSpec((1, w, SWA_Q), lambda n, i: (n, i, TCOL_SQ // SWA_Q)),
                  pl.BlockSpec((1, w, SWA_KV), lambda n, i: (n, prev(i), kcol)),
                  pl.BlockSpec((1, w, SWA_KV), lambda n, i: (n, i, kcol)),
                  pl.BlockSpec((1, w, SWA_KV), lambda n, i: (n, prev(i), vcol)),
                  pl.BlockSpec((1, w, SWA_KV), lambda n, i: (n, i, vcol)),
                  pl.BlockSpec((None, SWA_QH, w, 2 * w), lambda n, i: (jnp.minimum(i, 1), 0, 0, 0)),
                  pl.BlockSpec(memory_space=pltpu.SMEM)],
        out_specs=pl.BlockSpec((1, w, SWA_Q), lambda n, i: (n, i, 0)),
        out_shape=jax.ShapeDtypeStruct((nb, t, SWA_Q), BF16),
        compiler_params=_cparams(("parallel", "arbitrary")),
        name="swa_prompt",
    )(zt3, zt3, zt3, zt3, zt3, bias, sinks)


def _swa_decode_kernel(q_ref, kv_ref, kb_ref, vb_ref, bias_ref, sink_ref, y_ref, ko_ref, vo_ref, *, nstep):
    n0 = pl.program_id(0) * nstep
    scale = SWA_HD ** -0.5
    w = WINDOW
    ri = lax.broadcasted_iota(jnp.int32, (SWA_HD, SWA_HD), 0)
    ci = lax.broadcasted_iota(jnp.int32, (SWA_HD, SWA_HD), 1)
    eye = (ri == ci).astype(F32)
    last = lax.broadcasted_iota(jnp.int32, (SWA_HD, w), 1) == w - 1

    def push(cache_t, new_row):
        col = jnp.sum(eye * new_row, axis=1, keepdims=True)
        return jnp.where(last, col, pltpu.roll(cache_t, w - 1, axis=1))

    for j in range(nstep):
        qrow = q_ref[pl.ds(n0 + j, 1), :]
        kvrow = kv_ref[pl.ds(n0 + j, 1), :]
        for kv in range(SWA_KVH):
            hs = slice(kv * SWA_HD, (kv + 1) * SWA_HD)
            kt = push(kb_ref[j, kv], kvrow[:, hs])
            vt = push(vb_ref[j, kv], kvrow[:, SWA_KV + kv * SWA_HD:SWA_KV + (kv + 1) * SWA_HD])
            ko_ref[j, kv] = kt
            vo_ref[j, kv] = vt
            q4 = jnp.concatenate(
                [qrow[:, (kv * SWA_GRP + g) * SWA_HD:(kv * SWA_GRP + g + 1) * SWA_HD]
                 for g in range(SWA_GRP)], axis=0).astype(BF16)
            s = jnp.dot(q4, kt.astype(BF16), preferred_element_type=F32) * scale + bias_ref[kv]
            sink = sink_ref[kv * SWA_GRP:(kv + 1) * SWA_GRP, :]
            m = jnp.maximum(jnp.max(s, axis=-1, keepdims=True), sink)
            p = jnp.exp(s - m)
            den = jnp.sum(p, axis=-1, keepdims=True) + jnp.exp(sink - m)
            o = lax.dot_general((p / den).astype(BF16), vt.astype(BF16), (((1,), (1,)), ((), ())),
                                preferred_element_type=F32)
            for g in range(SWA_GRP):
                h = kv * SWA_GRP + g
                y_ref[j, :, h * SWA_HD:(h + 1) * SWA_HD] = o[g:g + 1, :].astype(y_ref.dtype)


def _swa_decode(zt, kbuf_t, vbuf_t, bias_dec, sinks, l, nstep=8):
    nb = zt.shape[0]
    w = WINDOW
    assert nb % nstep == 0
    kern = functools.partial(_swa_decode_kernel, nstep=nstep)
    cspec = lambda: _lspec(l, (nstep, SWA_KVH, SWA_HD, w), lambda n: (n, 0, 0, 0))
    ospec = lambda: pl.BlockSpec((nstep, SWA_KVH, SWA_HD, w), lambda n: (n, 0, 0, 0))
    return pl.pallas_call(
        kern,
        grid=(nb // nstep,),
        in_specs=[pl.BlockSpec((nb, SWA_Q), lambda n: (0, TCOL_SQ // SWA_Q)),
                  pl.BlockSpec((nb, 2 * SWA_KV), lambda n: (0, TCOL_SK // (2 * SWA_KV))),
                  cspec(), cspec(),
                  pl.BlockSpec((SWA_KVH, SWA_GRP, w), lambda n: (0, 0, 0)),
                  _lspec(l, (SWA_QH, 1), lambda n: (0, 0))],
        out_specs=[pl.BlockSpec((nstep, 1, SWA_Q), lambda n: (n, 0, 0)), ospec(), ospec()],
        out_shape=[jax.ShapeDtypeStruct((nb, 1, SWA_Q), BF16),
                   jax.ShapeDtypeStruct(kbuf_t.shape[1:], F32), jax.ShapeDtypeStruct(vbuf_t.shape[1:], F32)],
        compiler_params=_cparams(("arbitrary",)),
        name="swa_decode",
    )(zt, zt, kbuf_t, vbuf_t, bias_dec, sinks)


def _seq_chunk(t):
    for c in (688, 512, 256, 128, 64, 32, 16):
        if t % c == 0:
            return c
    raise ValueError(f"sequence length {t} must be a multiple of {GLA_CHUNK}")


def kernel(x_prompt, x_sample, state_s5_re, state_s5_im, state_gla, cache_swa_k, cache_swa_v, meta_tokens, rel_bias, norm_pre_mix, norm_post_mix, norm_pre_ffn, norm_post_ffn, w_in, s5_lam_re, s5_lam_im, s5_log_step, s5_b_re, s5_b_im, s5_c_re, s5_c_im, s5_d, s5_w_glu, s5_b_glu, gla_w_gate2, gla_b_gate2, gla_g_out, swa_sinks, w_up_s5, w_up_gla, w_up_swa, w_out, w_ff1, w_ff2):
    depth = w_in.shape[0]
    nb, seq, d = x_prompt.shape
    t = seq + N_META
    nd = x_sample.shape[0]
    assert x_sample.shape[1] == 1 and cache_swa_k.shape[2] == WINDOW and t % GLA_CHUNK == 0
    tc = _seq_chunk(t)

    w_in_t = jnp.swapaxes(w_in.astype(BF16), 1, 2)
    w_glu = s5_w_glu.astype(BF16)
    wg2 = gla_w_gate2.astype(BF16)
    wu_a, wu_b, wu_c = w_up_s5.astype(BF16), w_up_gla.astype(BF16), w_up_swa.astype(BF16)
    wo, w1, w2 = w_out.astype(BF16), w_ff1.astype(BF16), w_ff2.astype(BF16)
    row3 = lambda a: a.reshape(depth, 1, a.shape[-1])
    g_pre, g_post, g_pre_f, g_post_f = (row3(a) for a in (norm_pre_mix, norm_post_mix,
                                                           norm_pre_ffn, norm_post_ffn))
    b_glu, d5, bg2, go = row3(s5_b_glu), row3(s5_d), row3(gla_b_gate2), row3(gla_g_out)
    sinks3 = swa_sinks.reshape(depth, SWA_QH, 1)
    s5r = state_s5_re.reshape(depth, nd, -1)
    s5i = state_s5_im.reshape(depth, nd, -1)
    ckb = jnp.transpose(cache_swa_k, (0, 1, 3, 4, 2))
    cvb = jnp.transpose(cache_swa_v, (0, 1, 3, 4, 2))

    ar, ai, bbr_t, bbi_t = _s5_discretize(s5_lam_re, s5_lam_im, s5_log_step,
                                          jnp.swapaxes(s5_b_re, -1, -2), jnp.swapaxes(s5_b_im, -1, -2))
    bmat, cmat = _s5_matrices(bbr_t, bbi_t, s5_c_re, s5_c_im)
    a_half = jnp.concatenate([ar.reshape(depth, S5_NGB, 2, S5_HW), ai.reshape(depth, S5_NGB, 2, S5_HW)],
                             axis=-1)
    arow = jnp.tile(a_half, (1, 1, SUBLANES // 2, 1))
    adec = jnp.stack([ar.reshape(depth, -1), ai.reshape(depth, -1)], axis=1)

    bias = _bias_table(rel_bias)
    bias_dec = bias[1, :, WINDOW - 1, WINDOW:].reshape(SWA_KVH, SWA_GRP, WINDOW)

    meta = jnp.broadcast_to(meta_tokens.astype(x_prompt.dtype)[None], (nb, N_META, d))
    xp = jnp.concatenate([meta, x_prompt], axis=1).reshape(nb * t, d)
    xs = x_sample.reshape(nd, d)

    outs = {k: [] for k in ('p_re', 'p_im', 'p_gla', 'p_k', 'p_v', 's_re', 's_im', 's_gla', 's_k', 's_v')}
    for l in range(depth):
        zh, zhd = _norm_mm(xp, xs, g_pre, w_in_t, l, 0, HEAD_COLS, 640)
        zt, ztd = _norm_mm(xp, xs, g_pre, w_in_t, l, TAIL_START, TAIL_COLS, 2176)
        zh3 = zh.reshape(nb, t, HEAD_COLS)
        zt3 = zt.reshape(nb, t, TAIL_COLS)
        z5, st5 = _s5_prompt(zh3, bmat, cmat, arow, d5, l, tc)
        z5d, nr, ni = _s5_decode(zhd, bmat, cmat, adec, d5, s5r, s5i, l)
        ya, yad = _glu(z5.reshape(nb * t, S5_WIDTH), z5d, w_glu, b_glu, l)
        yb, pg = _gla_prompt(zh3, zt3, wg2, bg2, go, l, tc)
        ybd, sg = _gla_decode(zhd, ztd, wg2, bg2, go, state_gla, l)
        yc = _swa_prompt(zt3, bias, swa_sinks, l)
        ycd, sk, sv = _swa_decode(ztd, ckb, cvb, bias_dec, sinks3, l)
        xp, xs = _merge_out((ya, yb.reshape(nb * t, GLA_V), yc.reshape(nb * t, SWA_Q)),
                            (yad, ybd.reshape(nd, GLA_V), ycd.reshape(nd, SWA_Q)),
                            zt, ztd, wu_a, wu_b, wu_c, wo, xp, xs, g_post, l)
        xp, xs = _ffn(xp, xs, g_pre_f, w1, w2, g_post_f, l)
        st5 = st5.reshape(S5_NGB, nb, 2, 2, S5_HALF, S5_STATE).transpose(3, 1, 0, 2, 4, 5)
        st5 = st5.reshape(2, nb, S5_GROUPS, S5_STATE)
        outs['p_re'].append(st5[0])
        outs['p_im'].append(st5[1])
        outs['p_gla'].append(pg)
        outs['p_k'].append(zt3[:, t - WINDOW:, TCOL_SK:TCOL_SK + SWA_KV].reshape(nb, WINDOW, SWA_KVH, SWA_HD))
        outs['p_v'].append(zt3[:, t - WINDOW:, TCOL_SV:TCOL_SV + SWA_KV].reshape(nb, WINDOW, SWA_KVH, SWA_HD))
        outs['s_re'].append(nr.reshape(nd, S5_GROUPS, S5_STATE))
        outs['s_im'].append(ni.reshape(nd, S5_GROUPS, S5_STATE))
        outs['s_gla'].append(sg)
        outs['s_k'].append(jnp.transpose(sk, (0, 3, 1, 2)))
        outs['s_v'].append(jnp.transpose(sv, (0, 3, 1, 2)))

    y_prompt = xp.reshape(nb, t, d)[:, N_META:]
    y_sample = xs.reshape(nd, 1, d)
    st = {k: jnp.stack(v) for k, v in outs.items()}
    return (y_prompt, y_sample, st['p_re'], st['p_im'], st['p_gla'], st['p_k'], st['p_v'],
            st['s_re'], st['s_im'], st['s_gla'], st['s_k'], st['s_v'])
```

```python
import functools
import math

import numpy as np
import jax
import jax.numpy as jnp
from jax import lax
from jax.experimental import pallas as pl
from jax.experimental.pallas import tpu as pltpu

F32 = jnp.float32
BF16 = jnp.bfloat16

D_MODEL = 2048
N_META = 16
S5_GROUP = 16
S5_WIDTH = D_MODEL // 2
S5_GROUPS = S5_WIDTH // S5_GROUP
S5_STATE = 64
GLA_HEADS = 4
GLA_DK = D_MODEL // 16
GLA_DV = D_MODEL // 8
GLA_RANK = 16
GLA_TAU = 16.0
GLA_CHUNK = 16
SWA_QH = 16
SWA_KVH = 4
SWA_GRP = SWA_QH // SWA_KVH
SWA_HD = 64
WINDOW = 128
N_BUCKETS = 32
MAX_DISTANCE = WINDOW
D_FF = 4 * D_MODEL
N_BRANCH = 3
EPS = 1e-6

GLA_QK = GLA_HEADS * GLA_DK
GLA_V = GLA_HEADS * GLA_DV
SWA_Q = SWA_QH * SWA_HD
SWA_KV = SWA_KVH * SWA_HD

COL_U = 0
COL_GQ = COL_U + S5_WIDTH
COL_GK = COL_GQ + GLA_QK
COL_GV = COL_GK + GLA_QK
COL_GA = COL_GV + GLA_V
LANES = 128
SUBLANES = 8
HEAD_COLS = COL_GA
TAIL_START = COL_GA + GLA_RANK
TCOL_R = 0
TCOL_SQ = TCOL_R + GLA_V
TCOL_SK = TCOL_SQ + SWA_Q
TCOL_SV = TCOL_SK + SWA_KV
TCOL_GATE = TCOL_SV + SWA_KV
MIX_COLS = TCOL_GATE
GATE_START = TAIL_START + TCOL_GATE
GATE_COLS = N_BRANCH * D_MODEL

S5_GB = 8
S5_NGB = S5_GROUPS // S5_GB
S5_HALF = S5_GB // 2
S5_HW = S5_HALF * S5_STATE

VMEM_LIMIT = 60 * 1024 * 1024


def _cparams(sem):
    return pltpu.CompilerParams(dimension_semantics=sem, vmem_limit_bytes=VMEM_LIMIT)


def _rms(x, g):
    ms = jnp.mean(x * x, axis=-1, keepdims=True)
    return (x * lax.rsqrt(ms + EPS)) * g


def _row_tile(m):
    for t in (688, 512, 256, 128, 64, 32, 16, 8):
        if m % t == 0:
            return t
    return m


def _lspec(l, shape, imap):
    return pl.BlockSpec((None,) + tuple(shape), lambda *idx: (l,) + tuple(imap(*idx)))


class _Tiles:
    def __init__(self, m, n_inner):
        self.tm = _row_tile(m)
        self.npt = m // self.tm
        self.grid = (self.npt, n_inner)

    def dstep(self, i, j):
        return jnp.where(i == self.npt - 1, j, 0)


def _with_decode(npt, prompt_fn, decode_fn):
    prompt_fn()
    pl.when(pl.program_id(0) == npt - 1)(decode_fn)


def _nt_dot(h, wt):
    return lax.dot_general(h, wt, (((1,), (1,)), ((), ())), preferred_element_type=F32)


def _norm_mm_kernel(*refs, npt, gate, extra):
    if extra:
        xp_ref, xd_ref, g_ref, w_ref, we_ref, op_ref, od_ref, ep_ref, ed_ref, hp_ref, hd_ref = refs
    else:
        xp_ref, xd_ref, g_ref, w_ref, op_ref, od_ref, hp_ref, hd_ref = refs
        we_ref = ep_ref = ed_ref = None

    def run(x_ref, o_ref, e_ref, h_ref):
        @pl.when(pl.program_id(1) == 0)
        def _():
            h_ref[...] = _rms(x_ref[...], g_ref[...]).astype(BF16)
            if extra:
                e_ref[...] = _nt_dot(h_ref[...], we_ref[0])

        z = _nt_dot(h_ref[...], w_ref[0])
        o_ref[...] = (jax.nn.sigmoid(z) if gate else z).astype(o_ref.dtype)

    _with_decode(npt, lambda: run(xp_ref, op_ref, ep_ref, hp_ref),
                 lambda: run(xd_ref, od_ref, ed_ref, hd_ref))


def _norm_mm(xp, xd, g, wt, l, col0, n_cols, tn, gate=False, extra_col0=None):
    m, k = xp.shape
    nd = xd.shape[0]
    tl = _Tiles(m, n_cols // tn)
    extra = extra_col0 is not None
    odt = BF16 if gate else F32
    kern = functools.partial(_norm_mm_kernel, npt=tl.npt, gate=gate, extra=extra)
    wspec = lambda width, off: pl.BlockSpec(
        (pl.Element(1), pl.Element(width), pl.Element(k)),
        lambda i, j: (l, pl.multiple_of(off(j), 16), 0))
    in_specs = [pl.BlockSpec((tl.tm, k), lambda i, j: (i, 0)),
                pl.BlockSpec((nd, k), lambda i, j: (0, 0)),
                _lspec(l, (1, k), lambda i, j: (0, 0)),
                wspec(tn, lambda j: col0 + j * tn)]
    out_specs = [pl.BlockSpec((tl.tm, tn), lambda i, j: (i, j)),
                 pl.BlockSpec((nd, tn), lambda i, j: (0, tl.dstep(i, j)))]
    out_shape = [jax.ShapeDtypeStruct((m, n_cols), odt), jax.ShapeDtypeStruct((nd, n_cols), odt)]
    args = [xp, xd, g, wt]
    if extra:
        in_specs.append(wspec(LANES, lambda j: extra_col0 + 0 * j))
        out_specs += [pl.BlockSpec((tl.tm, LANES), lambda i, j: (i, 0)),
                      pl.BlockSpec((nd, LANES), lambda i, j: (0, 0))]
        out_shape += [jax.ShapeDtypeStruct((m, LANES), F32), jax.ShapeDtypeStruct((nd, LANES), F32)]
        args.append(wt)
    return pl.pallas_call(
        kern,
        grid=tl.grid,
        in_specs=in_specs,
        out_specs=out_specs,
        out_shape=out_shape,
        scratch_shapes=[pltpu.VMEM((tl.tm, k), BF16), pltpu.VMEM((nd, k), BF16)],
        compiler_params=_cparams(("arbitrary", "arbitrary")),
        name="norm_mm",
    )(*args)


def _glu_kernel(zp_ref, zd_ref, zcp_ref, zcd_ref, w_ref, b_ref, op_ref, od_ref, hp_ref, hd_ref, *, npt):
    def run(z_ref, zc_ref, o_ref, h_ref):
        @pl.when(pl.program_id(1) == 0)
        def _():
            h_ref[...] = z_ref[...].astype(BF16)

        a = jnp.dot(h_ref[...], w_ref[...], preferred_element_type=F32) + b_ref[...]
        o_ref[...] = (zc_ref[...] * jax.nn.sigmoid(a)).astype(o_ref.dtype)

    _with_decode(npt, lambda: run(zp_ref, zcp_ref, op_ref, hp_ref),
                 lambda: run(zd_ref, zcd_ref, od_ref, hd_ref))


def _glu(zp, zd, w, b, l, tn=512):
    m, k = zp.shape
    nd = zd.shape[0]
    tl = _Tiles(m, k // tn)
    kern = functools.partial(_glu_kernel, npt=tl.npt)
    return pl.pallas_call(
        kern,
        grid=tl.grid,
        in_specs=[pl.BlockSpec((tl.tm, k), lambda i, j: (i, 0)),
                  pl.BlockSpec((nd, k), lambda i, j: (0, 0)),
                  pl.BlockSpec((tl.tm, tn), lambda i, j: (i, j)),
                  pl.BlockSpec((nd, tn), lambda i, j: (0, tl.dstep(i, j))),
                  _lspec(l, (k, tn), lambda i, j: (0, j)),
                  _lspec(l, (1, tn), lambda i, j: (0, j))],
        out_specs=[pl.BlockSpec((tl.tm, tn), lambda i, j: (i, j)),
                   pl.BlockSpec((nd, tn), lambda i, j: (0, tl.dstep(i, j)))],
        out_shape=[jax.ShapeDtypeStruct((m, k), BF16), jax.ShapeDtypeStruct((nd, k), BF16)],
        scratch_shapes=[pltpu.VMEM((tl.tm, k), BF16), pltpu.VMEM((nd, k), BF16)],
        compiler_params=_cparams(("arbitrary", "arbitrary")),
        name="s5_glu",
    )(zp, zd, zp, zd, w, b)


def _merge_out_kernel(yap_ref, ybp_ref, ycp_ref, yad_ref, ybd_ref, ycd_ref,
                      gap_ref, gbp_ref, gcp_ref, gad_ref, gbd_ref, gcd_ref,
                      wa_ref, wb_ref, wc_ref, wo_ref, xp_ref, xd_ref, g_ref,
                      op_ref, od_ref, mp_ref, md_ref, *, npt, nk, tk, tn):
    s = pl.program_id(1)

    def run(ya_ref, yb_ref, yc_ref, ga_ref, gb_ref, gc_ref, x_ref, o_ref, m_ref):
        @pl.when(s < nk)
        def _():
            m = (ga_ref[...].astype(F32) * jnp.dot(ya_ref[...], wa_ref[...], preferred_element_type=F32)
                 + gb_ref[...].astype(F32) * jnp.dot(yb_ref[...], wb_ref[...], preferred_element_type=F32)
                 + gc_ref[...].astype(F32) * jnp.dot(yc_ref[...], wc_ref[...], preferred_element_type=F32))
            m_ref[:, pl.ds(pl.multiple_of(s * tk, tk), tk)] = m.astype(BF16)

        @pl.when(s >= nk)
        def _():
            c0 = pl.multiple_of((s - nk) * tn, tn)
            o_ref[:, pl.ds(c0, tn)] = jnp.dot(m_ref[...], wo_ref[...], preferred_element_type=F32)

        @pl.when(s == pl.num_programs(1) - 1)
        def _():
            o_ref[...] = x_ref[...] + _rms(o_ref[...], g_ref[...])

    _with_decode(
        npt,
        lambda: run(yap_ref, ybp_ref, ycp_ref, gap_ref, gbp_ref, gcp_ref, xp_ref, op_ref, mp_ref),
        lambda: run(yad_ref, ybd_ref, ycd_ref, gad_ref, gbd_ref, gcd_ref, xd_ref, od_ref, md_ref))


def _merge_out(yp, yd, ztp, ztd, wa, wb, wc, wo, xp, xd, g, l, tk=256, tn=512):
    m, d = xp.shape
    nd = xd.shape[0]
    kb = yp[0].shape[1]
    gate0 = 0
    nk = d // tk
    tl = _Tiles(m, nk + d // tn)
    kidx = lambda s: jnp.minimum(s, nk - 1)
    nidx = lambda s: jnp.maximum(s - nk, 0)
    gp_spec = lambda br: pl.BlockSpec((tl.tm, tk), lambda i, s: (i, gate0 + br * nk + kidx(s)))
    gd_spec = lambda br: pl.BlockSpec(
        (nd, tk), lambda i, s: (0, gate0 + br * nk + kidx(tl.dstep(i, s))))
    yp_spec = pl.BlockSpec((tl.tm, kb), lambda i, s: (i, 0))
    yd_spec = pl.BlockSpec((nd, kb), lambda i, s: (0, 0))
    w_spec = _lspec(l, (kb, tk), lambda i, s: (0, kidx(s)))
    kern = functools.partial(_merge_out_kernel, npt=tl.npt, nk=nk, tk=tk, tn=tn)
    return pl.pallas_call(
        kern,
        grid=tl.grid,
        in_specs=[yp_spec, yp_spec, yp_spec, yd_spec, yd_spec, yd_spec,
                  gp_spec(0), gp_spec(1), gp_spec(2), gd_spec(0), gd_spec(1), gd_spec(2),
                  w_spec, w_spec, w_spec,
                  _lspec(l, (d, tn), lambda i, s: (0, nidx(s))),
                  pl.BlockSpec((tl.tm, d), lambda i, s: (i, 0)),
                  pl.BlockSpec((nd, d), lambda i, s: (0, 0)),
                  _lspec(l, (1, d), lambda i, s: (0, 0))],
        out_specs=[pl.BlockSpec((tl.tm, d), lambda i, s: (i, 0)),
                   pl.BlockSpec((nd, d), lambda i, s: (0, 0))],
        out_shape=[jax.ShapeDtypeStruct((m, d), F32), jax.ShapeDtypeStruct((nd, d), F32)],
        scratch_shapes=[pltpu.VMEM((tl.tm, d), BF16), pltpu.VMEM((nd, d), BF16)],
        compiler_params=_cparams(("arbitrary", "arbitrary")),
        name="merge_out",
    )(*yp, *yd, ztp, ztp, ztp, ztd, ztd, ztd, wa, wb, wc, wo, xp, xd, g)


def _ffn_kernel(xp_ref, xd_ref, g1_ref, w1_ref, w2_ref, g2_ref, op_ref, od_ref,
                hp_ref, accp_ref, hd_ref, accd_ref, *, npt):
    f = pl.program_id(1)

    def run(x_ref, o_ref, h_ref, acc_ref):
        @pl.when(f == 0)
        def _():
            h_ref[...] = _rms(x_ref[...], g1_ref[...]).astype(BF16)
            acc_ref[...] = jnp.zeros_like(acc_ref)

        a = jnp.dot(h_ref[...], w1_ref[...], preferred_element_type=F32)
        a = jnp.square(jnp.maximum(a, 0.0)).astype(BF16)
        acc_ref[...] += jnp.dot(a, w2_ref[...], preferred_element_type=F32)

        @pl.when(f == pl.num_programs(1) - 1)
        def _():
            o_ref[...] = x_ref[...] + _rms(acc_ref[...], g2_ref[...])

    _with_decode(npt, lambda: run(xp_ref, op_ref, hp_ref, accp_ref),
                 lambda: run(xd_ref, od_ref, hd_ref, accd_ref))


def _ffn(xp, xd, g1, w1, w2, g2, l, tf=1024):
    m, d = xp.shape
    nd = xd.shape[0]
    ff = w1.shape[2]
    tl = _Tiles(m, ff // tf)
    kern = functools.partial(_ffn_kernel, npt=tl.npt)
    return pl.pallas_call(
        kern,
        grid=tl.grid,
        in_specs=[pl.BlockSpec((tl.tm, d), lambda i, f: (i, 0)),
                  pl.BlockSpec((nd, d), lambda i, f: (0, 0)),
                  _lspec(l, (1, d), lambda i, f: (0, 0)),
                  _lspec(l, (d, tf), lambda i, f: (0, f)),
                  _lspec(l, (tf, d), lambda i, f: (f, 0)),
                  _lspec(l, (1, d), lambda i, f: (0, 0))],
        out_specs=[pl.BlockSpec((tl.tm, d), lambda i, f: (i, 0)),
                   pl.BlockSpec((nd, d), lambda i, f: (0, 0))],
        out_shape=[jax.ShapeDtypeStruct((m, d), F32), jax.ShapeDtypeStruct((nd, d), F32)],
        scratch_shapes=[pltpu.VMEM((tl.tm, d), BF16), pltpu.VMEM((tl.tm, d), F32),
                        pltpu.VMEM((nd, d), BF16), pltpu.VMEM((nd, d), F32)],
        compiler_params=_cparams(("arbitrary", "arbitrary")),
        name="ffn",
    )(xp, xd, g1, w1, w2, g2)


def _s5_disc_kernel(lr_ref, li_ref, ls_ref, br_ref, bi_ref, ar_ref, ai_ref, bbr_ref, bbi_ref):
    lr = lr_ref[0]
    li = li_ref[0]
    dt = jnp.exp(ls_ref[0])
    mag = jnp.exp(lr * dt)
    ar = mag * jnp.cos(li * dt)
    ai = mag * jnp.sin(li * dt)
    den = lr * lr + li * li
    fr = ((ar - 1.0) * lr + ai * li) / den
    fi = (ai * lr - (ar - 1.0) * li) / den
    ar_ref[0] = ar
    ai_ref[0] = ai
    br = br_ref[0]
    bi = bi_ref[0]
    bbr_ref[0] = fr[:, None, :] * br - fi[:, None, :] * bi
    bbi_ref[0] = fr[:, None, :] * bi + fi[:, None, :] * br


def _s5_discretize(lam_re, lam_im, log_step, b_re_t, b_im_t):
    depth = lam_re.shape[0]
    gp = pl.BlockSpec((1, S5_GROUPS, S5_STATE), lambda l: (l, 0, 0))
    bspec = pl.BlockSpec((1, S5_GROUPS, S5_GROUP, S5_STATE), lambda l: (l, 0, 0, 0))
    return pl.pallas_call(
        _s5_disc_kernel,
        grid=(depth,),
        in_specs=[gp, gp, pl.BlockSpec((1, S5_GROUPS, 1), lambda l: (l, 0, 0)), bspec, bspec],
        out_specs=[gp, gp, bspec, bspec],
        out_shape=[jax.ShapeDtypeStruct(lam_re.shape, F32), jax.ShapeDtypeStruct(lam_re.shape, F32),
                   jax.ShapeDtypeStruct(b_re_t.shape, F32), jax.ShapeDtypeStruct(b_re_t.shape, F32)],
        compiler_params=_cparams(("parallel",)),
        name="s5_discretize",
    )(lam_re, lam_im, log_step.reshape(depth, S5_GROUPS, 1), b_re_t, b_im_t)


def _blockdiag(r):
    eye = jnp.eye(S5_HALF, dtype=r.dtype)
    z = jnp.einsum('ab,lgqbhp->lgqahbp', eye, r)
    s = r.shape
    return z.reshape(s[0], s[1], s[2], S5_HALF * s[4], S5_HALF * s[5])


def _s5_matrices(bbr_t, bbi_t, c_re, c_im):
    depth = bbr_t.shape[0]
    shp = (depth, S5_NGB, 2, S5_HALF, S5_GROUP, S5_STATE)
    bre = _blockdiag(bbr_t.reshape(shp))
    bim = _blockdiag(bbi_t.reshape(shp))
    bcat = jnp.concatenate([bre, bim], axis=-1)
    zb = jnp.zeros_like(bcat[:, :, 0])
    bmat = jnp.stack([jnp.concatenate([bcat[:, :, 0], zb], axis=-2),
                      jnp.concatenate([zb, bcat[:, :, 1]], axis=-2)], axis=2)
    cre = jnp.swapaxes(_blockdiag(c_re.reshape(shp)), -1, -2)
    cim = jnp.swapaxes(_blockdiag(c_im.reshape(shp)), -1, -2)
    ccat = jnp.concatenate([cre, -cim], axis=-2)
    zc = jnp.zeros_like(ccat[:, :, 0])
    cmat = jnp.stack([jnp.concatenate([ccat[:, :, 0], zc], axis=-1),
                      jnp.concatenate([zc, ccat[:, :, 1]], axis=-1)], axis=2)
    return bmat.astype(BF16), cmat.astype(BF16)


def _s5_prompt_kernel(u_ref, b_ref, c_ref, a_ref, d_ref, z_ref, st_ref, e_ref, x_ref, *, nb, tc):
    @pl.when(pl.program_id(1) == 0)
    def _():
        x_ref[...] = jnp.zeros_like(x_ref)

    nlb = 2 * S5_HW // LANES
    hlb = nlb // 2
    for n in range(nb):
        ub = u_ref[n].astype(BF16)
        for q in range(2):
            e = jnp.dot(ub, b_ref[0, q], preferred_element_type=F32)
            for j in range(nlb):
                e_ref[j, pl.ds(n * 2 + q, tc, stride=SUBLANES), :] = e[:, j * LANES:(j + 1) * LANES]

    a = a_ref[0]
    ar = [a[:, j * LANES:(j + 1) * LANES] for j in range(hlb)]
    ai = [a[:, (hlb + j) * LANES:(hlb + j + 1) * LANES] for j in range(hlb)]

    def step(t, carry):
        row = pl.multiple_of(t * SUBLANES, SUBLANES)
        new = []
        for j in range(hlb):
            xr, xi = carry[2 * j], carry[2 * j + 1]
            nxr = ar[j] * xr - ai[j] * xi + e_ref[j, pl.ds(row, SUBLANES), :]
            nxi = ar[j] * xi + ai[j] * xr + e_ref[hlb + j, pl.ds(row, SUBLANES), :]
            e_ref[j, pl.ds(row, SUBLANES), :] = nxr
            e_ref[hlb + j, pl.ds(row, SUBLANES), :] = nxi
            new += [nxr, nxi]
        return tuple(new)

    x0 = x_ref[...]
    init = []
    for j in range(hlb):
        init += [x0[:, j * LANES:(j + 1) * LANES], x0[:, (hlb + j) * LANES:(hlb + j + 1) * LANES]]
    fin = lax.fori_loop(0, tc, step, tuple(init), unroll=8)
    for j in range(hlb):
        x_ref[:, j * LANES:(j + 1) * LANES] = fin[2 * j]
        x_ref[:, (hlb + j) * LANES:(hlb + j + 1) * LANES] = fin[2 * j + 1]
    st_ref[0] = x_ref[...]

    for n in range(nb):
        y = d_ref[...] * u_ref[n]
        for q in range(2):
            xs = jnp.concatenate(
                [e_ref[j, pl.ds(n * 2 + q, tc, stride=SUBLANES), :] for j in range(nlb)], axis=1)
            y = y + jnp.dot(xs.astype(BF16), c_ref[0, q], preferred_element_type=F32)
        z_ref[n] = jax.nn.gelu(y)


def _s5_prompt(zh3, bmat, cmat, arow, d, l, tc):
    nb, t, _ = zh3.shape
    assert 2 * nb == SUBLANES, "state rows are (batch, half) pairs filling one sublane tile"
    kern = functools.partial(_s5_prompt_kernel, nb=nb, tc=tc)
    return pl.pallas_call(
        kern,
        grid=(S5_NGB, t // tc),
        in_specs=[pl.BlockSpec((nb, tc, LANES), lambda g, c: (0, c, g)),
                  _lspec(l, (1, 2, LANES, 2 * S5_HW), lambda g, c: (g, 0, 0, 0)),
                  _lspec(l, (1, 2, 2 * S5_HW, LANES), lambda g, c: (g, 0, 0, 0)),
                  _lspec(l, (1, SUBLANES, 2 * S5_HW), lambda g, c: (g, 0, 0)),
                  _lspec(l, (1, LANES), lambda g, c: (0, g))],
        out_specs=[pl.BlockSpec((nb, tc, LANES), lambda g, c: (0, c, g)),
                   pl.BlockSpec((1, SUBLANES, 2 * S5_HW), lambda g, c: (g, 0, 0))],
        out_shape=[jax.ShapeDtypeStruct((nb, t, S5_WIDTH), F32),
                   jax.ShapeDtypeStruct((S5_NGB, SUBLANES, 2 * S5_HW), F32)],
        scratch_shapes=[pltpu.VMEM((2 * S5_HW // LANES, tc * SUBLANES, LANES), F32),
                        pltpu.VMEM((SUBLANES, 2 * S5_HW), F32)],
        compiler_params=_cparams(("parallel", "arbitrary")),
        name="s5_prompt",
    )(zh3, bmat, cmat, arow, d)


def _s5_decode_kernel(u_ref, b_ref, c_ref, a_ref, d_ref, sr_ref, si_ref, z_ref, nr_ref, ni_ref):
    for g in range(S5_NGB):
        u = u_ref[:, g * LANES:(g + 1) * LANES]
        ub = u.astype(BF16)
        y = d_ref[:, g * LANES:(g + 1) * LANES] * u
        for q in range(2):
            c0 = (g * 2 + q) * S5_HW
            e = jnp.dot(ub, b_ref[g, q], preferred_element_type=F32)
            ar = a_ref[0:1, c0:c0 + S5_HW]
            ai = a_ref[1:2, c0:c0 + S5_HW]
            x0r = sr_ref[:, c0:c0 + S5_HW]
            x0i = si_ref[:, c0:c0 + S5_HW]
            nr = ar * x0r - ai * x0i + e[:, :S5_HW]
            ni = ar * x0i + ai * x0r + e[:, S5_HW:]
            nr_ref[:, c0:c0 + S5_HW] = nr
            ni_ref[:, c0:c0 + S5_HW] = ni
            xs = jnp.concatenate([nr, ni], axis=1).astype(BF16)
            y = y + jnp.dot(xs, c_ref[g, q], preferred_element_type=F32)
        z_ref[:, g * LANES:(g + 1) * LANES] = jax.nn.gelu(y)


def _s5_decode(zh, bmat, cmat, adec, d, sr, si, l):
    nb = zh.shape[0]
    ns = S5_GROUPS * S5_STATE
    full = lambda shape: pl.BlockSpec(shape, lambda i: (0,) * len(shape))
    lfull = lambda shape: _lspec(l, shape, lambda i: (0,) * len(shape))
    return pl.pallas_call(
        _s5_decode_kernel,
        grid=(1,),
        in_specs=[full((nb, S5_WIDTH)), lfull(bmat.shape[1:]), lfull(cmat.shape[1:]), lfull((2, ns)),
                  lfull((1, S5_WIDTH)), lfull((nb, ns)), lfull((nb, ns))],
        out_specs=[full((nb, S5_WIDTH)), full((nb, ns)), full((nb, ns))],
        out_shape=[jax.ShapeDtypeStruct((nb, S5_WIDTH), F32),
                   jax.ShapeDtypeStruct((nb, ns), F32), jax.ShapeDtypeStruct((nb, ns), F32)],
        compiler_params=_cparams(("arbitrary",)),
        name="s5_decode",
    )(zh, bmat, cmat, adec, d, sr, si)


def _gla_finish(o, r, g):
    return _rms(o, g) * (r * jax.nn.sigmoid(r))


def _gla_prompt_kernel(q_ref, k_ref, v_ref, a_ref, r_ref, wg_ref, bg_ref, go_ref, y_ref, st_ref,
                       s_ref, lg_ref, o_ref, qi_ref, ku_ref, cd_ref, *, tc):
    c = pl.program_id(1)

    @pl.when(c == 0)
    def _():
        s_ref[...] = jnp.zeros_like(s_ref)

    a16 = a_ref[0][:, :GLA_RANK].astype(BF16)
    lg_ref[...] = jax.nn.log_sigmoid(
        jnp.dot(a16, wg_ref[...], preferred_element_type=F32) + bg_ref[...]) / GLA_TAU

    ch = GLA_CHUNK
    hs = SUBLANES
    rowi = lax.broadcasted_iota(jnp.int32, (ch, GLA_DK), 0)
    rowh = lax.broadcasted_iota(jnp.int32, (hs, GLA_DK), 0)
    coli = lax.broadcasted_iota(jnp.int32, (hs, ch), 1)
    scale = GLA_DK ** -0.5

    def prep(idx, carry):
        r0 = pl.multiple_of(idx * ch, ch)
        for h in range(GLA_HEADS):
            ks = slice(h * GLA_DK, (h + 1) * GLA_DK)
            vs = slice(h * GLA_DV, (h + 1) * GLA_DV)
            b = lg_ref[pl.ds(r0, ch), ks]
            for sh in (1, 2, 4, 8):
                b = b + jnp.where(rowi >= sh, pltpu.roll(b, sh, axis=0), 0.0)
            qh = q_ref[0, pl.ds(r0, ch), ks] * scale
            kh = k_ref[0, pl.ds(r0, ch), ks]
            halves = []
            for half in range(ch // hs):
                bq = b[half * hs:(half + 1) * hs]
                qq = qh[half * hs:(half + 1) * hs]
                att = jnp.zeros((hs, ch), F32)
                for s in range((half + 1) * hs):
                    diff = bq - b[s:s + 1, :]
                    if s >= half * hs:
                        diff = jnp.where(rowh + half * hs >= s, diff, -jnp.inf)
                    col = jnp.sum(qq * kh[s:s + 1, :] * jnp.exp(diff), axis=1, keepdims=True)
                    att = jnp.where(coli == s, col, att)
                halves.append(att)
            att = jnp.concatenate(halves, axis=0).astype(BF16)
            vh = v_ref[0, pl.ds(r0, ch), vs].astype(BF16)
            o_ref[pl.ds(r0, ch), vs] = jnp.dot(att, vh, preferred_element_type=F32)
            blast = b[ch - 1:ch, :]
            qi_ref[pl.ds(r0, ch), ks] = (qh * jnp.exp(b)).astype(BF16)
            ku_ref[pl.ds(r0, ch), ks] = (kh * jnp.exp(blast - b)).astype(BF16)
            cd_ref[pl.ds(pl.multiple_of(idx * hs, hs), hs), ks] = jnp.broadcast_to(
                jnp.exp(blast), (hs, GLA_DK))
        return carry

    lax.fori_loop(0, tc // ch, prep, 0, unroll=2)

    def scan(idx, carry):
        r0 = pl.multiple_of(idx * ch, ch)
        for h in range(GLA_HEADS):
            ks = slice(h * GLA_DK, (h + 1) * GLA_DK)
            vs = slice(h * GLA_DV, (h + 1) * GLA_DV)
            st = s_ref[h]
            inter = lax.dot_general(qi_ref[pl.ds(r0, ch), ks], st.astype(BF16),
                                    (((1,), (1,)), ((), ())), preferred_element_type=F32)
            o_ref[pl.ds(r0, ch), vs] += inter
            vh = v_ref[0, pl.ds(r0, ch), vs].astype(BF16)
            kv = lax.dot_general(vh, ku_ref[pl.ds(r0, ch), ks], (((0,), (0,)), ((), ())),
                                 preferred_element_type=F32)
            cd = cd_ref[pl.ds(pl.multiple_of(idx * hs, hs), 1), ks]
            s_ref[h] = cd * st + kv
        return carry

    lax.fori_loop(0, tc // ch, scan, 0, unroll=4)

    for h in range(GLA_HEADS):
        vs = slice(h * GLA_DV, (h + 1) * GLA_DV)
        y_ref[0, :, vs] = _gla_finish(o_ref[:, vs], r_ref[0, :, vs], go_ref[...]).astype(y_ref.dtype)
        st_ref[0, h] = s_ref[h].T


def _gla_prompt(zh3, za3, zt3, wg, bg, go, l, tc):
    nb, t, _ = zh3.shape
    kern = functools.partial(_gla_prompt_kernel, tc=tc)
    return pl.pallas_call(
        kern,
        grid=(nb, t // tc),
        in_specs=[pl.BlockSpec((1, tc, GLA_QK), lambda n, c: (n, c, COL_GQ // GLA_QK)),
                  pl.BlockSpec((1, tc, GLA_QK), lambda n, c: (n, c, COL_GK // GLA_QK)),
                  pl.BlockSpec((1, tc, GLA_V), lambda n, c: (n, c, COL_GV // GLA_V)),
                  pl.BlockSpec((1, tc, LANES), lambda n, c: (n, c, 0)),
                  pl.BlockSpec((1, tc, GLA_V), lambda n, c: (n, c, TCOL_R // GLA_V)),
                  _lspec(l, (GLA_RANK, GLA_QK), lambda n, c: (0, 0)),
                  _lspec(l, (1, GLA_QK), lambda n, c: (0, 0)),
                  _lspec(l, (1, GLA_DV), lambda n, c: (0, 0))],
        out_specs=[pl.BlockSpec((1, tc, GLA_V), lambda n, c: (n, c, 0)),
                   pl.BlockSpec((1, GLA_HEADS, GLA_DK, GLA_DV), lambda n, c: (n, 0, 0, 0))],
        out_shape=[jax.ShapeDtypeStruct((nb, t, GLA_V), BF16),
                   jax.ShapeDtypeStruct((nb, GLA_HEADS, GLA_DK, GLA_DV), F32)],
        scratch_shapes=[pltpu.VMEM((GLA_HEADS, GLA_DV, GLA_DK), F32),
                        pltpu.VMEM((tc, GLA_QK), F32),
                        pltpu.VMEM((tc, GLA_V), F32),
                        pltpu.VMEM((tc, GLA_QK), BF16),
                        pltpu.VMEM((tc, GLA_QK), BF16),
                        pltpu.VMEM((tc // GLA_CHUNK * SUBLANES, GLA_QK), F32)],
        compiler_params=_cparams(("parallel", "arbitrary")),
        name="gla_prompt",
    )(zh3, zh3, zh3, za3, zt3, wg, bg, go)


def _gla_decode_kernel(zh_ref, za_ref, r_ref, wg_ref, bg_ref, go_ref, s_ref, y_ref, so_ref, *, nstep):
    n0 = pl.multiple_of(pl.program_id(0) * nstep, nstep)
    rows = zh_ref[pl.ds(n0, nstep), :]
    a16 = za_ref[pl.ds(n0, nstep), :GLA_RANK].astype(BF16)
    lg = jax.nn.log_sigmoid(
        jnp.dot(a16, wg_ref[...], preferred_element_type=F32) + bg_ref[...]) / GLA_TAU
    eg = jnp.exp(lg)
    rr = r_ref[pl.ds(n0, nstep), :]
    ri = lax.broadcasted_iota(jnp.int32, (GLA_DK, GLA_DK), 0)
    ci = lax.broadcasted_iota(jnp.int32, (GLA_DK, GLA_DK), 1)
    eye = (ri == ci).astype(F32)

    def col(v):
        return jnp.sum(eye * v, axis=1, keepdims=True)

    for j in range(nstep):
        row = rows[j:j + 1]
        for h in range(GLA_HEADS):
            ks = slice(h * GLA_DK, (h + 1) * GLA_DK)
            qh = row[:, COL_GQ + h * GLA_DK:COL_GQ + (h + 1) * GLA_DK] * (GLA_DK ** -0.5)
            kh = row[:, COL_GK + h * GLA_DK:COL_GK + (h + 1) * GLA_DK]
            vh = row[:, COL_GV + h * GLA_DV:COL_GV + (h + 1) * GLA_DV]
            s = col(eg[j:j + 1, ks]) * s_ref[j, h] + col(kh) * vh
            so_ref[j, h] = s
            o = jnp.sum(col(qh) * s, axis=0, keepdims=True)
            vs = slice(h * GLA_DV, (h + 1) * GLA_DV)
            y_ref[j, :, vs] = _gla_finish(o, rr[j:j + 1, vs], go_ref[...]).astype(y_ref.dtype)


def _gla_decode(zh, za, zt, wg, bg, go, s0, l, nstep=SUBLANES):
    nb = zh.shape[0]
    assert nb % nstep == 0
    kern = functools.partial(_gla_decode_kernel, nstep=nstep)
    sblock = (nstep, GLA_HEADS, GLA_DK, GLA_DV)
    return pl.pallas_call(
        kern,
        grid=(nb // nstep,),
        in_specs=[pl.BlockSpec((nb, HEAD_COLS), lambda n: (0, 0)),
                  pl.BlockSpec((nb, LANES), lambda n: (0, 0)),
                  pl.BlockSpec((nb, GLA_V), lambda n: (0, TCOL_R // GLA_V)),
                  _lspec(l, (GLA_RANK, GLA_QK), lambda n: (0, 0)),
                  _lspec(l, (1, GLA_QK), lambda n: (0, 0)),
                  _lspec(l, (1, GLA_DV), lambda n: (0, 0)),
                  _lspec(l, sblock, lambda n: (n, 0, 0, 0))],
        out_specs=[pl.BlockSpec((nstep, 1, GLA_V), lambda n: (n, 0, 0)),
                   pl.BlockSpec(sblock, lambda n: (n, 0, 0, 0))],
        out_shape=[jax.ShapeDtypeStruct((nb, 1, GLA_V), BF16),
                   jax.ShapeDtypeStruct(s0.shape[1:], F32)],
        compiler_params=_cparams(("arbitrary",)),
        name="gla_decode",
    )(zh, za, zt, wg, bg, go, s0)


def _bucket_table():
    dist = (np.arange(WINDOW)[:, None] + WINDOW) - np.arange(2 * WINDOW)[None, :]
    max_exact = N_BUCKETS // 2
    d = np.maximum(dist, 0)
    ratio = np.log(np.maximum(d, 1).astype(np.float32) / np.float32(max_exact)) / np.float32(
        math.log(MAX_DISTANCE / max_exact))
    large = max_exact + (ratio.astype(np.float32) * np.float32(N_BUCKETS - max_exact)).astype(np.int32)
    large = np.minimum(large, N_BUCKETS - 1)
    return np.where(d < max_exact, d, large).astype(np.int32)


def _bias_kernel(bk_ref, rb_ref, o_ref):
    h = pl.program_id(1)
    bk = bk_ref[...]
    acc = jnp.zeros(bk.shape, F32)
    for b in range(N_BUCKETS):
        acc = jnp.where(bk == b, rb_ref[b, h], acc)
    cols = lax.broadcasted_iota(jnp.int32, bk.shape, 1)
    dist = WINDOW + lax.broadcasted_iota(jnp.int32, bk.shape, 0) - cols
    band = (dist >= 0) & (dist < WINDOW)
    keep = band & ((cols >= WINDOW) | (pl.program_id(0) == 1))
    o_ref[0, 0] = jnp.where(keep, acc, -jnp.inf)


def _bias_table(rel_bias):
    bk = jnp.asarray(_bucket_table())
    return pl.pallas_call(
        _bias_kernel,
        grid=(2, SWA_QH),
        in_specs=[pl.BlockSpec((WINDOW, 2 * WINDOW), lambda v, h: (0, 0)),
                  pl.BlockSpec(memory_space=pltpu.SMEM)],
        out_specs=pl.BlockSpec((1, 1, WINDOW, 2 * WINDOW), lambda v, h: (v, h, 0, 0)),
        out_shape=jax.ShapeDtypeStruct((2, SWA_QH, WINDOW, 2 * WINDOW), F32),
        compiler_params=_cparams(("parallel", "parallel")),
        name="swa_bias",
    )(bk, rel_bias)


def _swa_prompt_kernel(q_ref, kp_ref, kc_ref, vp_ref, vc_ref, bias_ref, sink_ref, y_ref, *, t, l):
    i = pl.program_id(1)
    w = WINDOW
    inseq = i * w + lax.broadcasted_iota(jnp.int32, (w, 1), 0) < t
    scale = SWA_HD ** -0.5
    for kv in range(SWA_KVH):
        hs = slice(kv * SWA_HD, (kv + 1) * SWA_HD)
        kh = jnp.concatenate([kp_ref[0, :, hs], jnp.where(inseq, kc_ref[0, :, hs], 0.0)],
                             axis=0).astype(BF16)
        vh = jnp.concatenate([vp_ref[0, :, hs], jnp.where(inseq, vc_ref[0, :, hs], 0.0)],
                             axis=0).astype(BF16)
        for g in range(SWA_GRP):
            h = kv * SWA_GRP + g
            qh = (q_ref[0, :, h * SWA_HD:(h + 1) * SWA_HD] * scale).astype(BF16)
            s = lax.dot_general(qh, kh, (((1,), (1,)), ((), ())), preferred_element_type=F32)
            s = s + bias_ref[h]
            sink = sink_ref[l, h]
            m = jnp.maximum(jnp.max(s, axis=-1, keepdims=True), sink)
            p = jnp.exp(s - m)
            den = jnp.sum(p, axis=-1, keepdims=True) + jnp.exp(sink - m)
            o = jnp.dot(p.astype(BF16), vh, preferred_element_type=F32) * (1.0 / den)
            y_ref[0, :, h * SWA_HD:(h + 1) * SWA_HD] = o.astype(y_ref.dtype)


def _swa_prompt(zt3, bias, sinks, l):
    nb, t, _ = zt3.shape
    w = WINDOW
    nblk = pl.cdiv(t, w)
    kern = functools.partial(_swa_prompt_kernel, t=t, l=l)
    kcol = TCOL_SK // SWA_KV
    vcol = TCOL_SV // SWA_KV
    prev = lambda i: jnp.maximum(i - 1, 0)
    return pl.pallas_call(
        kern,
        grid=(nb, nblk),
        in_specs=[pl.BlockSpec((1, w, SWA_Q), lambda n, i: (n, i, TCOL_SQ // SWA_Q)),
                  pl.BlockSpec((1, w, SWA_KV), lambda n, i: (n, prev(i), kcol)),
                  pl.BlockSpec((1, w, SWA_KV), lambda n, i: (n, i, kcol)),
                  pl.BlockSpec((1, w, SWA_KV), lambda n, i: (n, prev(i), vcol)),
                  pl.BlockSpec((1, w, SWA_KV), lambda n, i: (n, i, vcol)),
                  pl.BlockSpec((None, SWA_QH, w, 2 * w), lambda n, i: (jnp.minimum(i, 1), 0, 0, 0)),
                  pl.BlockSpec(memory_space=pltpu.SMEM)],
        out_specs=pl.BlockSpec((1, w, SWA_Q), lambda n, i: (n, i, 0)),
        out_shape=jax.ShapeDtypeStruct((nb, t, SWA_Q), BF16),
        compiler_params=_cparams(("parallel", "arbitrary")),
        name="swa_prompt",
    )(zt3, zt3, zt3, zt3, zt3, bias, sinks)


def _swa_decode_kernel(q_ref, kv_ref, kb_ref, vb_ref, bias_ref, sink_ref, y_ref, ko_ref, vo_ref, *, nstep):
    n0 = pl.program_id(0) * nstep
    scale = SWA_HD ** -0.5
    w = WINDOW
    ri = lax.broadcasted_iota(jnp.int32, (SWA_HD, SWA_HD), 0)
    ci = lax.broadcasted_iota(jnp.int32, (SWA_HD, SWA_HD), 1)
    eye = (ri == ci).astype(F32)
    last = lax.broadcasted_iota(jnp.int32, (SWA_HD, w), 1) == w - 1

    def push(cache_t, new_row):
        col = jnp.sum(eye * new_row, axis=1, keepdims=True)
        return jnp.where(last, col, pltpu.roll(cache_t, w - 1, axis=1))

    for j in range(nstep):
        qrow = q_ref[pl.ds(n0 + j, 1), :]
        kvrow = kv_ref[pl.ds(n0 + j, 1), :]
        for kv in range(SWA_KVH):
            hs = slice(kv * SWA_HD, (kv + 1) * SWA_HD)
            kt = push(kb_ref[j, kv], kvrow[:, hs])
            vt = push(vb_ref[j, kv], kvrow[:, SWA_KV + kv * SWA_HD:SWA_KV + (kv + 1) * SWA_HD])
            ko_ref[j, kv] = kt
            vo_ref[j, kv] = vt
            q4 = jnp.concatenate(
                [qrow[:, (kv * SWA_GRP + g) * SWA_HD:(kv * SWA_GRP + g + 1) * SWA_HD]
                 for g in range(SWA_GRP)], axis=0).astype(BF16)
            s = jnp.dot(q4, kt.astype(BF16), preferred_element_type=F32) * scale + bias_ref[kv]
            sink = sink_ref[kv * SWA_GRP:(kv + 1) * SWA_GRP, :]
            m = jnp.maximum(jnp.max(s, axis=-1, keepdims=True), sink)
            p = jnp.exp(s - m)
            den = jnp.sum(p, axis=-1, keepdims=True) + jnp.exp(sink - m)
            o = lax.dot_general((p / den).astype(BF16), vt.astype(BF16), (((1,), (1,)), ((), ())),
                                preferred_element_type=F32)
            for g in range(SWA_GRP):
                h = kv * SWA_GRP + g
                y_ref[j, :, h * SWA_HD:(h + 1) * SWA_HD] = o[g:g + 1, :].astype(y_ref.dtype)


def _swa_decode(zt, kbuf_t, vbuf_t, bias_dec, sinks, l, nstep=8):
    nb = zt.shape[0]
    w = WINDOW
    assert nb % nstep == 0
    kern = functools.partial(_swa_decode_kernel, nstep=nstep)
    cspec = lambda: _lspec(l, (nstep, SWA_KVH, SWA_HD, w), lambda n: (n, 0, 0, 0))
    ospec = lambda: pl.BlockSpec((nstep, SWA_KVH, SWA_HD, w), lambda n: (n, 0, 0, 0))
    return pl.pallas_call(
        kern,
        grid=(nb // nstep,),
        in_specs=[pl.BlockSpec((nb, SWA_Q), lambda n: (0, TCOL_SQ // SWA_Q)),
                  pl.BlockSpec((nb, 2 * SWA_KV), lambda n: (0, TCOL_SK // (2 * SWA_KV))),
                  cspec(), cspec(),
                  pl.BlockSpec((SWA_KVH, SWA_GRP, w), lambda n: (0, 0, 0)),
                  _lspec(l, (SWA_QH, 1), lambda n: (0, 0))],
        out_specs=[pl.BlockSpec((nstep, 1, SWA_Q), lambda n: (n, 0, 0)), ospec(), ospec()],
        out_shape=[jax.ShapeDtypeStruct((nb, 1, SWA_Q), BF16),
                   jax.ShapeDtypeStruct(kbuf_t.shape[1:], F32), jax.ShapeDtypeStruct(vbuf_t.shape[1:], F32)],
        compiler_params=_cparams(("arbitrary",)),
        name="swa_decode",
    )(zt, zt, kbuf_t, vbuf_t, bias_dec, sinks)


def _seq_chunk(t):
    for c in (688, 512, 256, 128, 64, 32, 16):
        if t % c == 0:
            return c
    raise ValueError(f"sequence length {t} must be a multiple of {GLA_CHUNK}")


def kernel(x_prompt, x_sample, state_s5_re, state_s5_im, state_gla, cache_swa_k, cache_swa_v, meta_tokens, rel_bias, norm_pre_mix, norm_post_mix, norm_pre_ffn, norm_post_ffn, w_in, s5_lam_re, s5_lam_im, s5_log_step, s5_b_re, s5_b_im, s5_c_re, s5_c_im, s5_d, s5_w_glu, s5_b_glu, gla_w_gate2, gla_b_gate2, gla_g_out, swa_sinks, w_up_s5, w_up_gla, w_up_swa, w_out, w_ff1, w_ff2):
    depth = w_in.shape[0]
    nb, seq, d = x_prompt.shape
    t = seq + N_META
    nd = x_sample.shape[0]
    assert x_sample.shape[1] == 1 and cache_swa_k.shape[2] == WINDOW and t % GLA_CHUNK == 0
    tc = _seq_chunk(t)

    w_in_t = jnp.swapaxes(w_in.astype(BF16), 1, 2)
    w_glu = s5_w_glu.astype(BF16)
    wg2 = gla_w_gate2.astype(BF16)
    wu_a, wu_b, wu_c = w_up_s5.astype(BF16), w_up_gla.astype(BF16), w_up_swa.astype(BF16)
    wo, w1, w2 = w_out.astype(BF16), w_ff1.astype(BF16), w_ff2.astype(BF16)
    row3 = lambda a: a.reshape(depth, 1, a.shape[-1])
    g_pre, g_post, g_pre_f, g_post_f = (row3(a) for a in (norm_pre_mix, norm_post_mix,
                                                           norm_pre_ffn, norm_post_ffn))
    b_glu, d5, bg2, go = row3(s5_b_glu), row3(s5_d), row3(gla_b_gate2), row3(gla_g_out)
    sinks3 = swa_sinks.reshape(depth, SWA_QH, 1)
    s5r = state_s5_re.reshape(depth, nd, -1)
    s5i = state_s5_im.reshape(depth, nd, -1)
    ckb = jnp.transpose(cache_swa_k, (0, 1, 3, 4, 2))
    cvb = jnp.transpose(cache_swa_v, (0, 1, 3, 4, 2))

    ar, ai, bbr_t, bbi_t = _s5_discretize(s5_lam_re, s5_lam_im, s5_log_step,
                                          jnp.swapaxes(s5_b_re, -1, -2), jnp.swapaxes(s5_b_im, -1, -2))
    bmat, cmat = _s5_matrices(bbr_t, bbi_t, s5_c_re, s5_c_im)
    a_half = jnp.concatenate([ar.reshape(depth, S5_NGB, 2, S5_HW), ai.reshape(depth, S5_NGB, 2, S5_HW)],
                             axis=-1)
    arow = jnp.tile(a_half, (1, 1, SUBLANES // 2, 1))
    adec = jnp.stack([ar.reshape(depth, -1), ai.reshape(depth, -1)], axis=1)

    bias = _bias_table(rel_bias)
    bias_dec = bias[1, :, WINDOW - 1, WINDOW:].reshape(SWA_KVH, SWA_GRP, WINDOW)

    meta = jnp.broadcast_to(meta_tokens.astype(x_prompt.dtype)[None], (nb, N_META, d))
    xp = jnp.concatenate([meta, x_prompt], axis=1).reshape(nb * t, d)
    xs = x_sample.reshape(nd, d)

    outs = {k: [] for k in ('p_re', 'p_im', 'p_gla', 'p_k', 'p_v', 's_re', 's_im', 's_gla', 's_k', 's_v')}
    for l in range(depth):
        zh, zhd, za, zad = _norm_mm(xp, xs, g_pre, w_in_t, l, 0, HEAD_COLS, 1536, extra_col0=COL_GA)
        zt, ztd = _norm_mm(xp, xs, g_pre, w_in_t, l, TAIL_START, MIX_COLS, 1280)
        zg, zgd = _norm_mm(xp, xs, g_pre, w_in_t, l, GATE_START, GATE_COLS, 2048, gate=True)
        zh3 = zh.reshape(nb, t, HEAD_COLS)
        za3 = za.reshape(nb, t, LANES)
        zt3 = zt.reshape(nb, t, MIX_COLS)
        z5, st5 = _s5_prompt(zh3, bmat, cmat, arow, d5, l, tc)
        z5d, nr, ni = _s5_decode(zhd, bmat, cmat, adec, d5, s5r, s5i, l)
        ya, yad = _glu(z5.reshape(nb * t, S5_WIDTH), z5d, w_glu, b_glu, l)
        yb, pg = _gla_prompt(zh3, za3, zt3, wg2, bg2, go, l, tc)
        ybd, sg = _gla_decode(zhd, zad, ztd, wg2, bg2, go, state_gla, l)
        yc = _swa_prompt(zt3, bias, swa_sinks, l)
        ycd, sk, sv = _swa_decode(ztd, ckb, cvb, bias_dec, sinks3, l)
        xp, xs = _merge_out((ya, yb.reshape(nb * t, GLA_V), yc.reshape(nb * t, SWA_Q)),
                            (yad, ybd.reshape(nd, GLA_V), ycd.reshape(nd, SWA_Q)),
                            zg, zgd, wu_a, wu_b, wu_c, wo, xp, xs, g_post, l)
        xp, xs = _ffn(xp, xs, g_pre_f, w1, w2, g_post_f, l)
        st5 = st5.reshape(S5_NGB, nb, 2, 2, S5_HALF, S5_STATE).transpose(3, 1, 0, 2, 4, 5)
        st5 = st5.reshape(2, nb, S5_GROUPS, S5_STATE)
        outs['p_re'].append(st5[0])
        outs['p_im'].append(st5[1])
        outs['p_gla'].append(pg)
        outs['p_k'].append(zt3[:, t - WINDOW:, TCOL_SK:TCOL_SK + SWA_KV].reshape(nb, WINDOW, SWA_KVH, SWA_HD))
        outs['p_v'].append(zt3[:, t - WINDOW:, TCOL_SV:TCOL_SV + SWA_KV].reshape(nb, WINDOW, SWA_KVH, SWA_HD))
        outs['s_re'].append(nr.reshape(nd, S5_GROUPS, S5_STATE))
        outs['s_im'].append(ni.reshape(nd, S5_GROUPS, S5_STATE))
        outs['s_gla'].append(sg)
        outs['s_k'].append(jnp.transpose(sk, (0, 3, 1, 2)))
        outs['s_v'].append(jnp.transpose(sv, (0, 3, 1, 2)))

    y_prompt = xp.reshape(nb, t, d)[:, N_META:]
    y_sample = xs.reshape(nd, 1, d)
    st = {k: jnp.stack(v) for k, v in outs.items()}
    return (y_prompt, y_sample, st['p_re'], st['p_im'], st['p_gla'], st['p_k'], st['p_v'],
            st['s_re'], st['s_im'], st['s_gla'], st['s_k'], st['s_v'])
```

```python
import functools
import math

import numpy as np
import jax
import jax.numpy as jnp
from jax import lax
from jax.experimental import pallas as pl
from jax.experimental.pallas import tpu as pltpu

F32 = jnp.float32
BF16 = jnp.bfloat16

D_MODEL = 2048
N_META = 16
S5_GROUP = 16
S5_WIDTH = D_MODEL // 2
S5_GROUPS = S5_WIDTH // S5_GROUP
S5_STATE = 64
GLA_HEADS = 4
GLA_DK = D_MODEL // 16
GLA_DV = D_MODEL // 8
GLA_RANK = 16
GLA_TAU = 16.0
GLA_CHUNK = 16
SWA_QH = 16
SWA_KVH = 4
SWA_GRP = SWA_QH // SWA_KVH
SWA_HD = 64
WINDOW = 128
N_BUCKETS = 32
MAX_DISTANCE = WINDOW
D_FF = 4 * D_MODEL
N_BRANCH = 3
EPS = 1e-6

GLA_QK = GLA_HEADS * GLA_DK
GLA_V = GLA_HEADS * GLA_DV
SWA_Q = SWA_QH * SWA_HD
SWA_KV = SWA_KVH * SWA_HD

COL_U = 0
COL_GQ = COL_U + S5_WIDTH
COL_GK = COL_GQ + GLA_QK
COL_GV = COL_GK + GLA_QK
COL_GA = COL_GV + GLA_V
LANES = 128
SUBLANES = 8
HEAD_COLS = COL_GA
TAIL_START = COL_GA + GLA_RANK
TCOL_R = 0
TCOL_SQ = TCOL_R + GLA_V
TCOL_SK = TCOL_SQ + SWA_Q
TCOL_SV = TCOL_SK + SWA_KV
TCOL_GATE = TCOL_SV + SWA_KV
MIX_COLS = TCOL_GATE
GATE_START = TAIL_START + TCOL_GATE
GATE_COLS = N_BRANCH * D_MODEL

S5_GB = 8
S5_NGB = S5_GROUPS // S5_GB
S5_HALF = S5_GB // 2
S5_HW = S5_HALF * S5_STATE

VMEM_LIMIT = 60 * 1024 * 1024


def _cparams(sem):
    return pltpu.CompilerParams(dimension_semantics=sem, vmem_limit_bytes=VMEM_LIMIT)


def _rms(x, g):
    ms = jnp.mean(x * x, axis=-1, keepdims=True)
    return (x * lax.rsqrt(ms + EPS)) * g


def _row_tile(m):
    for t in (688, 512, 256, 128, 64, 32, 16, 8):
        if m % t == 0:
            return t
    return m


def _lspec(l, shape, imap):
    return pl.BlockSpec((None,) + tuple(shape), lambda *idx: (l,) + tuple(imap(*idx)))


class _Tiles:
    def __init__(self, m, n_inner):
        self.tm = _row_tile(m)
        self.npt = m // self.tm
        self.grid = (self.npt, n_inner)

    def dstep(self, i, j):
        return jnp.where(i == self.npt - 1, j, 0)


def _with_decode(npt, prompt_fn, decode_fn):
    prompt_fn()
    pl.when(pl.program_id(0) == npt - 1)(decode_fn)


def _nt_dot(h, wt):
    return lax.dot_general(h, wt, (((1,), (1,)), ((), ())), preferred_element_type=F32)


def _norm_mm_kernel(*refs, npt, gate, extra):
    if extra:
        xp_ref, xd_ref, g_ref, w_ref, we_ref, op_ref, od_ref, ep_ref, ed_ref, hp_ref, hd_ref = refs
    else:
        xp_ref, xd_ref, g_ref, w_ref, op_ref, od_ref, hp_ref, hd_ref = refs
        we_ref = ep_ref = ed_ref = None

    def run(x_ref, o_ref, e_ref, h_ref):
        @pl.when(pl.program_id(1) == 0)
        def _():
            h_ref[...] = _rms(x_ref[...], g_ref[...]).astype(BF16)
            if extra:
                e_ref[...] = _nt_dot(h_ref[...], we_ref[0])

        z = _nt_dot(h_ref[...], w_ref[0])
        o_ref[...] = (jax.nn.sigmoid(z) if gate else z).astype(o_ref.dtype)

    _with_decode(npt, lambda: run(xp_ref, op_ref, ep_ref, hp_ref),
                 lambda: run(xd_ref, od_ref, ed_ref, hd_ref))


def _norm_mm(xp, xd, g, wt, l, col0, n_cols, tn, gate=False, extra_col0=None):
    m, k = xp.shape
    nd = xd.shape[0]
    tl = _Tiles(m, n_cols // tn)
    extra = extra_col0 is not None
    odt = BF16 if gate else F32
    kern = functools.partial(_norm_mm_kernel, npt=tl.npt, gate=gate, extra=extra)
    wspec = lambda width, off: pl.BlockSpec(
        (pl.Element(1), pl.Element(width), pl.Element(k)),
        lambda i, j: (l, pl.multiple_of(off(j), 16), 0))
    in_specs = [pl.BlockSpec((tl.tm, k), lambda i, j: (i, 0)),
                pl.BlockSpec((nd, k), lambda i, j: (0, 0)),
                _lspec(l, (1, k), lambda i, j: (0, 0)),
                wspec(tn, lambda j: col0 + j * tn)]
    out_specs = [pl.BlockSpec((tl.tm, tn), lambda i, j: (i, j)),
                 pl.BlockSpec((nd, tn), lambda i, j: (0, tl.dstep(i, j)))]
    out_shape = [jax.ShapeDtypeStruct((m, n_cols), odt), jax.ShapeDtypeStruct((nd, n_cols), odt)]
    args = [xp, xd, g, wt]
    if extra:
        in_specs.append(wspec(LANES, lambda j: extra_col0 + 0 * j))
        out_specs += [pl.BlockSpec((tl.tm, LANES), lambda i, j: (i, 0)),
                      pl.BlockSpec((nd, LANES), lambda i, j: (0, 0))]
        out_shape += [jax.ShapeDtypeStruct((m, LANES), F32), jax.ShapeDtypeStruct((nd, LANES), F32)]
        args.append(wt)
    return pl.pallas_call(
        kern,
        grid=tl.grid,
        in_specs=in_specs,
        out_specs=out_specs,
        out_shape=out_shape,
        scratch_shapes=[pltpu.VMEM((tl.tm, k), BF16), pltpu.VMEM((nd, k), BF16)],
        compiler_params=_cparams(("arbitrary", "arbitrary")),
        name="norm_mm",
    )(*args)


def _merge_out_kernel(zp_ref, ybp_ref, ycp_ref, zd_ref, ybd_ref, ycd_ref,
                      gap_ref, gbp_ref, gcp_ref, gad_ref, gbd_ref, gcd_ref,
                      wglu_ref, bglu_ref, wa_ref, wb_ref, wc_ref, wo_ref, xp_ref, xd_ref, g_ref,
                      op_ref, od_ref, mp_ref, md_ref, yap_ref, yad_ref, *, npt, nk, tk, tn):
    s = pl.program_id(1)

    def run(z_ref, ya_ref, yb_ref, yc_ref, ga_ref, gb_ref, gc_ref, x_ref, o_ref, m_ref):
        @pl.when(s == 0)
        def _():
            z = z_ref[...]
            a = jnp.dot(z.astype(BF16), wglu_ref[...], preferred_element_type=F32) + bglu_ref[...]
            ya_ref[...] = (z * jax.nn.sigmoid(a)).astype(BF16)

        @pl.when(s < nk)
        def _():
            m = (ga_ref[...].astype(F32) * jnp.dot(ya_ref[...], wa_ref[...], preferred_element_type=F32)
                 + gb_ref[...].astype(F32) * jnp.dot(yb_ref[...], wb_ref[...], preferred_element_type=F32)
                 + gc_ref[...].astype(F32) * jnp.dot(yc_ref[...], wc_ref[...], preferred_element_type=F32))
            m_ref[:, pl.ds(pl.multiple_of(s * tk, tk), tk)] = m.astype(BF16)

        @pl.when(s >= nk)
        def _():
            c0 = pl.multiple_of((s - nk) * tn, tn)
            o_ref[:, pl.ds(c0, tn)] = jnp.dot(m_ref[...], wo_ref[...], preferred_element_type=F32)

        @pl.when(s == pl.num_programs(1) - 1)
        def _():
            o_ref[...] = x_ref[...] + _rms(o_ref[...], g_ref[...])

    _with_decode(
        npt,
        lambda: run(zp_ref, yap_ref, ybp_ref, ycp_ref, gap_ref, gbp_ref, gcp_ref, xp_ref, op_ref, mp_ref),
        lambda: run(zd_ref, yad_ref, ybd_ref, ycd_ref, gad_ref, gbd_ref, gcd_ref, xd_ref, od_ref, md_ref))


def _merge_out(yp, yd, ztp, ztd, wglu, bglu, wa, wb, wc, wo, xp, xd, g, l, tk=256, tn=512):
    m, d = xp.shape
    nd = xd.shape[0]
    kb = yp[0].shape[1]
    gate0 = 0
    nk = d // tk
    tl = _Tiles(m, nk + d // tn)
    kidx = lambda s: jnp.minimum(s, nk - 1)
    nidx = lambda s: jnp.maximum(s - nk, 0)
    gp_spec = lambda br: pl.BlockSpec((tl.tm, tk), lambda i, s: (i, gate0 + br * nk + kidx(s)))
    gd_spec = lambda br: pl.BlockSpec(
        (nd, tk), lambda i, s: (0, gate0 + br * nk + kidx(tl.dstep(i, s))))
    yp_spec = pl.BlockSpec((tl.tm, kb), lambda i, s: (i, 0))
    yd_spec = pl.BlockSpec((nd, kb), lambda i, s: (0, 0))
    w_spec = _lspec(l, (kb, tk), lambda i, s: (0, kidx(s)))
    kern = functools.partial(_merge_out_kernel, npt=tl.npt, nk=nk, tk=tk, tn=tn)
    return pl.pallas_call(
        kern,
        grid=tl.grid,
        in_specs=[yp_spec, yp_spec, yp_spec, yd_spec, yd_spec, yd_spec,
                  gp_spec(0), gp_spec(1), gp_spec(2), gd_spec(0), gd_spec(1), gd_spec(2),
                  _lspec(l, (kb, kb), lambda i, s: (0, 0)),
                  _lspec(l, (1, kb), lambda i, s: (0, 0)),
                  w_spec, w_spec, w_spec,
                  _lspec(l, (d, tn), lambda i, s: (0, nidx(s))),
                  pl.BlockSpec((tl.tm, d), lambda i, s: (i, 0)),
                  pl.BlockSpec((nd, d), lambda i, s: (0, 0)),
                  _lspec(l, (1, d), lambda i, s: (0, 0))],
        out_specs=[pl.BlockSpec((tl.tm, d), lambda i, s: (i, 0)),
                   pl.BlockSpec((nd, d), lambda i, s: (0, 0))],
        out_shape=[jax.ShapeDtypeStruct((m, d), F32), jax.ShapeDtypeStruct((nd, d), F32)],
        scratch_shapes=[pltpu.VMEM((tl.tm, d), BF16), pltpu.VMEM((nd, d), BF16),
                        pltpu.VMEM((tl.tm, kb), BF16), pltpu.VMEM((nd, kb), BF16)],
        compiler_params=_cparams(("arbitrary", "arbitrary")),
        name="merge_out",
    )(*yp, *yd, ztp, ztp, ztp, ztd, ztd, ztd, wglu, bglu, wa, wb, wc, wo, xp, xd, g)


def _ffn_kernel(xp_ref, xd_ref, g1_ref, w1_ref, w2_ref, g2_ref, op_ref, od_ref,
                hp_ref, accp_ref, hd_ref, accd_ref, *, npt):
    f = pl.program_id(1)

    def run(x_ref, o_ref, h_ref, acc_ref):
        @pl.when(f == 0)
        def _():
            h_ref[...] = _rms(x_ref[...], g1_ref[...]).astype(BF16)
            acc_ref[...] = jnp.zeros_like(acc_ref)

        a = jnp.dot(h_ref[...], w1_ref[...], preferred_element_type=F32)
        a = jnp.square(jnp.maximum(a, 0.0)).astype(BF16)
        acc_ref[...] += jnp.dot(a, w2_ref[...], preferred_element_type=F32)

        @pl.when(f == pl.num_programs(1) - 1)
        def _():
            o_ref[...] = x_ref[...] + _rms(acc_ref[...], g2_ref[...])

    _with_decode(npt, lambda: run(xp_ref, op_ref, hp_ref, accp_ref),
                 lambda: run(xd_ref, od_ref, hd_ref, accd_ref))


def _ffn(xp, xd, g1, w1, w2, g2, l, tf=1024):
    m, d = xp.shape
    nd = xd.shape[0]
    ff = w1.shape[2]
    tl = _Tiles(m, ff // tf)
    kern = functools.partial(_ffn_kernel, npt=tl.npt)
    return pl.pallas_call(
        kern,
        grid=tl.grid,
        in_specs=[pl.BlockSpec((tl.tm, d), lambda i, f: (i, 0)),
                  pl.BlockSpec((nd, d), lambda i, f: (0, 0)),
                  _lspec(l, (1, d), lambda i, f: (0, 0)),
                  _lspec(l, (d, tf), lambda i, f: (0, f)),
                  _lspec(l, (tf, d), lambda i, f: (f, 0)),
                  _lspec(l, (1, d), lambda i, f: (0, 0))],
        out_specs=[pl.BlockSpec((tl.tm, d), lambda i, f: (i, 0)),
                   pl.BlockSpec((nd, d), lambda i, f: (0, 0))],
        out_shape=[jax.ShapeDtypeStruct((m, d), F32), jax.ShapeDtypeStruct((nd, d), F32)],
        scratch_shapes=[pltpu.VMEM((tl.tm, d), BF16), pltpu.VMEM((tl.tm, d), F32),
                        pltpu.VMEM((nd, d), BF16), pltpu.VMEM((nd, d), F32)],
        compiler_params=_cparams(("arbitrary", "arbitrary")),
        name="ffn",
    )(xp, xd, g1, w1, w2, g2)


def _s5_disc_kernel(lr_ref, li_ref, ls_ref, br_ref, bi_ref, ar_ref, ai_ref, bbr_ref, bbi_ref):
    lr = lr_ref[0]
    li = li_ref[0]
    dt = jnp.exp(ls_ref[0])
    mag = jnp.exp(lr * dt)
    ar = mag * jnp.cos(li * dt)
    ai = mag * jnp.sin(li * dt)
    den = lr * lr + li * li
    fr = ((ar - 1.0) * lr + ai * li) / den
    fi = (ai * lr - (ar - 1.0) * li) / den
    ar_ref[0] = ar
    ai_ref[0] = ai
    br = br_ref[0]
    bi = bi_ref[0]
    bbr_ref[0] = fr[:, None, :] * br - fi[:, None, :] * bi
    bbi_ref[0] = fr[:, None, :] * bi + fi[:, None, :] * br


def _s5_discretize(lam_re, lam_im, log_step, b_re_t, b_im_t):
    depth = lam_re.shape[0]
    gp = pl.BlockSpec((1, S5_GROUPS, S5_STATE), lambda l: (l, 0, 0))
    bspec = pl.BlockSpec((1, S5_GROUPS, S5_GROUP, S5_STATE), lambda l: (l, 0, 0, 0))
    return pl.pallas_call(
        _s5_disc_kernel,
        grid=(depth,),
        in_specs=[gp, gp, pl.BlockSpec((1, S5_GROUPS, 1), lambda l: (l, 0, 0)), bspec, bspec],
        out_specs=[gp, gp, bspec, bspec],
        out_shape=[jax.ShapeDtypeStruct(lam_re.shape, F32), jax.ShapeDtypeStruct(lam_re.shape, F32),
                   jax.ShapeDtypeStruct(b_re_t.shape, F32), jax.ShapeDtypeStruct(b_re_t.shape, F32)],
        compiler_params=_cparams(("parallel",)),
        name="s5_discretize",
    )(lam_re, lam_im, log_step.reshape(depth, S5_GROUPS, 1), b_re_t, b_im_t)


def _blockdiag(r):
    eye = jnp.eye(S5_HALF, dtype=r.dtype)
    z = jnp.einsum('ab,lgqbhp->lgqahbp', eye, r)
    s = r.shape
    return z.reshape(s[0], s[1], s[2], S5_HALF * s[4], S5_HALF * s[5])


def _s5_matrices(bbr_t, bbi_t, c_re, c_im):
    depth = bbr_t.shape[0]
    shp = (depth, S5_NGB, 2, S5_HALF, S5_GROUP, S5_STATE)
    bre = _blockdiag(bbr_t.reshape(shp))
    bim = _blockdiag(bbi_t.reshape(shp))
    bcat = jnp.concatenate([bre, bim], axis=-1)
    zb = jnp.zeros_like(bcat[:, :, 0])
    bmat = jnp.stack([jnp.concatenate([bcat[:, :, 0], zb], axis=-2),
                      jnp.concatenate([zb, bcat[:, :, 1]], axis=-2)], axis=2)
    cre = jnp.swapaxes(_blockdiag(c_re.reshape(shp)), -1, -2)
    cim = jnp.swapaxes(_blockdiag(c_im.reshape(shp)), -1, -2)
    ccat = jnp.concatenate([cre, -cim], axis=-2)
    zc = jnp.zeros_like(ccat[:, :, 0])
    cmat = jnp.stack([jnp.concatenate([ccat[:, :, 0], zc], axis=-1),
                      jnp.concatenate([zc, ccat[:, :, 1]], axis=-1)], axis=2)
    return bmat.astype(BF16), cmat.astype(BF16)


def _s5_prompt_kernel(u_ref, b_ref, c_ref, a_ref, d_ref, z_ref, st_ref, e_ref, x_ref, *, nb, tc, ngb):
    @pl.when(pl.program_id(1) == 0)
    def _():
        x_ref[...] = jnp.zeros_like(x_ref)

    nlb = 2 * S5_HW // LANES
    hlb = nlb // 2
    for b in range(ngb):
        for n in range(nb):
            ub = u_ref[n, :, b * LANES:(b + 1) * LANES].astype(BF16)
            for q in range(2):
                e = jnp.dot(ub, b_ref[b, q], preferred_element_type=F32)
                for j in range(nlb):
                    e_ref[b * nlb + j, pl.ds(n * 2 + q, tc, stride=SUBLANES), :] = (
                        e[:, j * LANES:(j + 1) * LANES])

    ar, ai = [], []
    for b in range(ngb):
        a = a_ref[b]
        ar += [a[:, j * LANES:(j + 1) * LANES] for j in range(hlb)]
        ai += [a[:, (hlb + j) * LANES:(hlb + j + 1) * LANES] for j in range(hlb)]
    slabs = [(b * nlb + j, b * nlb + hlb + j) for b in range(ngb) for j in range(hlb)]

    def step(t, carry):
        row = pl.multiple_of(t * SUBLANES, SUBLANES)
        new = []
        for c, (sr, si) in enumerate(slabs):
            xr, xi = carry[2 * c], carry[2 * c + 1]
            nxr = ar[c] * xr - ai[c] * xi + e_ref[sr, pl.ds(row, SUBLANES), :]
            nxi = ar[c] * xi + ai[c] * xr + e_ref[si, pl.ds(row, SUBLANES), :]
            e_ref[sr, pl.ds(row, SUBLANES), :] = nxr
            e_ref[si, pl.ds(row, SUBLANES), :] = nxi
            new += [nxr, nxi]
        return tuple(new)

    init = []
    for b in range(ngb):
        x0 = x_ref[b]
        for j in range(hlb):
            init += [x0[:, j * LANES:(j + 1) * LANES], x0[:, (hlb + j) * LANES:(hlb + j + 1) * LANES]]
    fin = lax.fori_loop(0, tc, step, tuple(init), unroll=8)
    for b in range(ngb):
        for j in range(hlb):
            c = b * hlb + j
            x_ref[b, :, j * LANES:(j + 1) * LANES] = fin[2 * c]
            x_ref[b, :, (hlb + j) * LANES:(hlb + j + 1) * LANES] = fin[2 * c + 1]
    st_ref[...] = x_ref[...]

    for b in range(ngb):
        for n in range(nb):
            y = d_ref[:, b * LANES:(b + 1) * LANES] * u_ref[n, :, b * LANES:(b + 1) * LANES]
            for q in range(2):
                xs = jnp.concatenate(
                    [e_ref[b * nlb + j, pl.ds(n * 2 + q, tc, stride=SUBLANES), :] for j in range(nlb)],
                    axis=1)
                y = y + jnp.dot(xs.astype(BF16), c_ref[b, q], preferred_element_type=F32)
            z_ref[n, :, b * LANES:(b + 1) * LANES] = jax.nn.gelu(y)


def _s5_prompt(zh3, bmat, cmat, arow, d, l, tc):
    nb, t, _ = zh3.shape
    assert 2 * nb == SUBLANES, "state rows are (batch, half) pairs filling one sublane tile"
    ngb = 2
    kern = functools.partial(_s5_prompt_kernel, nb=nb, tc=tc, ngb=ngb)
    return pl.pallas_call(
        kern,
        grid=(S5_NGB // ngb, t // tc),
        in_specs=[pl.BlockSpec((nb, tc, ngb * LANES), lambda g, c: (0, c, g)),
                  _lspec(l, (ngb, 2, LANES, 2 * S5_HW), lambda g, c: (g, 0, 0, 0)),
                  _lspec(l, (ngb, 2, 2 * S5_HW, LANES), lambda g, c: (g, 0, 0, 0)),
                  _lspec(l, (ngb, SUBLANES, 2 * S5_HW), lambda g, c: (g, 0, 0)),
                  _lspec(l, (1, ngb * LANES), lambda g, c: (0, g))],
        out_specs=[pl.BlockSpec((nb, tc, ngb * LANES), lambda g, c: (0, c, g)),
                   pl.BlockSpec((ngb, SUBLANES, 2 * S5_HW), lambda g, c: (g, 0, 0))],
        out_shape=[jax.ShapeDtypeStruct((nb, t, S5_WIDTH), F32),
                   jax.ShapeDtypeStruct((S5_NGB, SUBLANES, 2 * S5_HW), F32)],
        scratch_shapes=[pltpu.VMEM((ngb * 2 * S5_HW // LANES, tc * SUBLANES, LANES), F32),
                        pltpu.VMEM((ngb, SUBLANES, 2 * S5_HW), F32)],
        compiler_params=_cparams(("parallel", "arbitrary")),
        name="s5_prompt",
    )(zh3, bmat, cmat, arow, d)


def _s5_decode_kernel(u_ref, b_ref, c_ref, a_ref, d_ref, sr_ref, si_ref, z_ref, nr_ref, ni_ref):
    for g in range(S5_NGB):
        u = u_ref[:, g * LANES:(g + 1) * LANES]
        ub = u.astype(BF16)
        y = d_ref[:, g * LANES:(g + 1) * LANES] * u
        for q in range(2):
            c0 = (g * 2 + q) * S5_HW
            e = jnp.dot(ub, b_ref[g, q], preferred_element_type=F32)
            ar = a_ref[0:1, c0:c0 + S5_HW]
            ai = a_ref[1:2, c0:c0 + S5_HW]
            x0r = sr_ref[:, c0:c0 + S5_HW]
            x0i = si_ref[:, c0:c0 + S5_HW]
            nr = ar * x0r - ai * x0i + e[:, :S5_HW]
            ni = ar * x0i + ai * x0r + e[:, S5_HW:]
            nr_ref[:, c0:c0 + S5_HW] = nr
            ni_ref[:, c0:c0 + S5_HW] = ni
            xs = jnp.concatenate([nr, ni], axis=1).astype(BF16)
            y = y + jnp.dot(xs, c_ref[g, q], preferred_element_type=F32)
        z_ref[:, g * LANES:(g + 1) * LANES] = jax.nn.gelu(y)


def _s5_decode(zh, bmat, cmat, adec, d, sr, si, l):
    nb = zh.shape[0]
    ns = S5_GROUPS * S5_STATE
    full = lambda shape: pl.BlockSpec(shape, lambda i: (0,) * len(shape))
    lfull = lambda shape: _lspec(l, shape, lambda i: (0,) * len(shape))
    return pl.pallas_call(
        _s5_decode_kernel,
        grid=(1,),
        in_specs=[full((nb, S5_WIDTH)), lfull(bmat.shape[1:]), lfull(cmat.shape[1:]), lfull((2, ns)),
                  lfull((1, S5_WIDTH)), lfull((nb, ns)), lfull((nb, ns))],
        out_specs=[full((nb, S5_WIDTH)), full((nb, ns)), full((nb, ns))],
        out_shape=[jax.ShapeDtypeStruct((nb, S5_WIDTH), F32),
                   jax.ShapeDtypeStruct((nb, ns), F32), jax.ShapeDtypeStruct((nb, ns), F32)],
        compiler_params=_cparams(("arbitrary",)),
        name="s5_decode",
    )(zh, bmat, cmat, adec, d, sr, si)


def _gla_finish(o, r, g):
    return _rms(o, g) * (r * jax.nn.sigmoid(r))


def _gla_prompt_kernel(q_ref, k_ref, v_ref, a_ref, r_ref, wg_ref, bg_ref, go_ref, y_ref, st_ref,
                       s_ref, lg_ref, o_ref, qi_ref, ku_ref, cd_ref, *, tc):
    c = pl.program_id(1)

    @pl.when(c == 0)
    def _():
        s_ref[...] = jnp.zeros_like(s_ref)

    a16 = a_ref[0][:, :GLA_RANK].astype(BF16)
    lg_ref[...] = jax.nn.log_sigmoid(
        jnp.dot(a16, wg_ref[...], preferred_element_type=F32) + bg_ref[...]) / GLA_TAU

    ch = GLA_CHUNK
    hs = SUBLANES
    rowi = lax.broadcasted_iota(jnp.int32, (ch, GLA_DK), 0)
    rowh = lax.broadcasted_iota(jnp.int32, (hs, GLA_DK), 0)
    coli = lax.broadcasted_iota(jnp.int32, (hs, ch), 1)
    scale = GLA_DK ** -0.5

    def prep(idx, carry):
        r0 = pl.multiple_of(idx * ch, ch)
        for h in range(GLA_HEADS):
            ks = slice(h * GLA_DK, (h + 1) * GLA_DK)
            vs = slice(h * GLA_DV, (h + 1) * GLA_DV)
            b = lg_ref[pl.ds(r0, ch), ks]
            for sh in (1, 2, 4, 8):
                b = b + jnp.where(rowi >= sh, pltpu.roll(b, sh, axis=0), 0.0)
            qh = q_ref[0, pl.ds(r0, ch), ks] * scale
            kh = k_ref[0, pl.ds(r0, ch), ks]
            halves = []
            for half in range(ch // hs):
                bq = b[half * hs:(half + 1) * hs]
                qq = qh[half * hs:(half + 1) * hs]
                att = jnp.zeros((hs, ch), F32)
                for s in range((half + 1) * hs):
                    diff = bq - b[s:s + 1, :]
                    if s >= half * hs:
                        diff = jnp.where(rowh + half * hs >= s, diff, -jnp.inf)
                    col = jnp.sum(qq * kh[s:s + 1, :] * jnp.exp(diff), axis=1, keepdims=True)
                    att = jnp.where(coli == s, col, att)
                halves.append(att)
            att = jnp.concatenate(halves, axis=0).astype(BF16)
            vh = v_ref[0, pl.ds(r0, ch), vs].astype(BF16)
            o_ref[pl.ds(r0, ch), vs] = jnp.dot(att, vh, preferred_element_type=F32)
            blast = b[ch - 1:ch, :]
            qi_ref[pl.ds(r0, ch), ks] = (qh * jnp.exp(b)).astype(BF16)
            ku_ref[pl.ds(r0, ch), ks] = (kh * jnp.exp(blast - b)).astype(BF16)
            cd_ref[pl.ds(pl.multiple_of(idx * hs, hs), hs), ks] = jnp.broadcast_to(
                jnp.exp(blast), (hs, GLA_DK))
        return carry

    lax.fori_loop(0, tc // ch, prep, 0, unroll=2)

    def scan(idx, carry):
        r0 = pl.multiple_of(idx * ch, ch)
        for h in range(GLA_HEADS):
            ks = slice(h * GLA_DK, (h + 1) * GLA_DK)
            vs = slice(h * GLA_DV, (h + 1) * GLA_DV)
            st = s_ref[h]
            inter = lax.dot_general(qi_ref[pl.ds(r0, ch), ks], st.astype(BF16),
                                    (((1,), (1,)), ((), ())), preferred_element_type=F32)
            o_ref[pl.ds(r0, ch), vs] += inter
            vh = v_ref[0, pl.ds(r0, ch), vs].astype(BF16)
            kv = lax.dot_general(vh, ku_ref[pl.ds(r0, ch), ks], (((0,), (0,)), ((), ())),
                                 preferred_element_type=F32)
            cd = cd_ref[pl.ds(pl.multiple_of(idx * hs, hs), 1), ks]
            s_ref[h] = cd * st + kv
        return carry

    lax.fori_loop(0, tc // ch, scan, 0, unroll=4)

    for h in range(GLA_HEADS):
        vs = slice(h * GLA_DV, (h + 1) * GLA_DV)
        y_ref[0, :, vs] = _gla_finish(o_ref[:, vs], r_ref[0, :, vs], go_ref[...]).astype(y_ref.dtype)
        st_ref[0, h] = s_ref[h].T


def _gla_prompt(zh3, za3, zt3, wg, bg, go, l, tc):
    nb, t, _ = zh3.shape
    kern = functools.partial(_gla_prompt_kernel, tc=tc)
    return pl.pallas_call(
        kern,
        grid=(nb, t // tc),
        in_specs=[pl.BlockSpec((1, tc, GLA_QK), lambda n, c: (n, c, COL_GQ // GLA_QK)),
                  pl.BlockSpec((1, tc, GLA_QK), lambda n, c: (n, c, COL_GK // GLA_QK)),
                  pl.BlockSpec((1, tc, GLA_V), lambda n, c: (n, c, COL_GV // GLA_V)),
                  pl.BlockSpec((1, tc, LANES), lambda n, c: (n, c, 0)),
                  pl.BlockSpec((1, tc, GLA_V), lambda n, c: (n, c, TCOL_R // GLA_V)),
                  _lspec(l, (GLA_RANK, GLA_QK), lambda n, c: (0, 0)),
                  _lspec(l, (1, GLA_QK), lambda n, c: (0, 0)),
                  _lspec(l, (1, GLA_DV), lambda n, c: (0, 0))],
        out_specs=[pl.BlockSpec((1, tc, GLA_V), lambda n, c: (n, c, 0)),
                   pl.BlockSpec((1, GLA_HEADS, GLA_DK, GLA_DV), lambda n, c: (n, 0, 0, 0))],
        out_shape=[jax.ShapeDtypeStruct((nb, t, GLA_V), BF16),
                   jax.ShapeDtypeStruct((nb, GLA_HEADS, GLA_DK, GLA_DV), F32)],
        scratch_shapes=[pltpu.VMEM((GLA_HEADS, GLA_DV, GLA_DK), F32),
                        pltpu.VMEM((tc, GLA_QK), F32),
                        pltpu.VMEM((tc, GLA_V), F32),
                        pltpu.VMEM((tc, GLA_QK), BF16),
                        pltpu.VMEM((tc, GLA_QK), BF16),
                        pltpu.VMEM((tc // GLA_CHUNK * SUBLANES, GLA_QK), F32)],
        compiler_params=_cparams(("parallel", "arbitrary")),
        name="gla_prompt",
    )(zh3, zh3, zh3, za3, zt3, wg, bg, go)


def _gla_decode_kernel(zh_ref, za_ref, r_ref, wg_ref, bg_ref, go_ref, s_ref, y_ref, so_ref, *, nstep):
    n0 = pl.multiple_of(pl.program_id(0) * nstep, nstep)
    rows = zh_ref[pl.ds(n0, nstep), :]
    a16 = za_ref[pl.ds(n0, nstep), :GLA_RANK].astype(BF16)
    lg = jax.nn.log_sigmoid(
        jnp.dot(a16, wg_ref[...], preferred_element_type=F32) + bg_ref[...]) / GLA_TAU
    eg = jnp.exp(lg)
    rr = r_ref[pl.ds(n0, nstep), :]
    ri = lax.broadcasted_iota(jnp.int32, (GLA_DK, GLA_DK), 0)
    ci = lax.broadcasted_iota(jnp.int32, (GLA_DK, GLA_DK), 1)
    eye = (ri == ci).astype(F32)

    def col(v):
        return jnp.sum(eye * v, axis=1, keepdims=True)

    for j in range(nstep):
        row = rows[j:j + 1]
        for h in range(GLA_HEADS):
            ks = slice(h * GLA_DK, (h + 1) * GLA_DK)
            qh = row[:, COL_GQ + h * GLA_DK:COL_GQ + (h + 1) * GLA_DK] * (GLA_DK ** -0.5)
            kh = row[:, COL_GK + h * GLA_DK:COL_GK + (h + 1) * GLA_DK]
            vh = row[:, COL_GV + h * GLA_DV:COL_GV + (h + 1) * GLA_DV]
            s = col(eg[j:j + 1, ks]) * s_ref[j, h] + col(kh) * vh
            so_ref[j, h] = s
            o = jnp.sum(col(qh) * s, axis=0, keepdims=True)
            vs = slice(h * GLA_DV, (h + 1) * GLA_DV)
            y_ref[j, :, vs] = _gla_finish(o, rr[j:j + 1, vs], go_ref[...]).astype(y_ref.dtype)


def _gla_decode(zh, za, zt, wg, bg, go, s0, l, nstep=SUBLANES):
    nb = zh.shape[0]
    assert nb % nstep == 0
    kern = functools.partial(_gla_decode_kernel, nstep=nstep)
    sblock = (nstep, GLA_HEADS, GLA_DK, GLA_DV)
    return pl.pallas_call(
        kern,
        grid=(nb // nstep,),
        in_specs=[pl.BlockSpec((nb, HEAD_COLS), lambda n: (0, 0)),
                  pl.BlockSpec((nb, LANES), lambda n: (0, 0)),
                  pl.BlockSpec((nb, GLA_V), lambda n: (0, TCOL_R // GLA_V)),
                  _lspec(l, (GLA_RANK, GLA_QK), lambda n: (0, 0)),
                  _lspec(l, (1, GLA_QK), lambda n: (0, 0)),
                  _lspec(l, (1, GLA_DV), lambda n: (0, 0)),
                  _lspec(l, sblock, lambda n: (n, 0, 0, 0))],
        out_specs=[pl.BlockSpec((nstep, 1, GLA_V), lambda n: (n, 0, 0)),
                   pl.BlockSpec(sblock, lambda n: (n, 0, 0, 0))],
        out_shape=[jax.ShapeDtypeStruct((nb, 1, GLA_V), BF16),
                   jax.ShapeDtypeStruct(s0.shape[1:], F32)],
        compiler_params=_cparams(("arbitrary",)),
        name="gla_decode",
    )(zh, za, zt, wg, bg, go, s0)


def _bucket_table():
    dist = (np.arange(WINDOW)[:, None] + WINDOW) - np.arange(2 * WINDOW)[None, :]
    max_exact = N_BUCKETS // 2
    d = np.maximum(dist, 0)
    ratio = np.log(np.maximum(d, 1).astype(np.float32) / np.float32(max_exact)) / np.float32(
        math.log(MAX_DISTANCE / max_exact))
    large = max_exact + (ratio.astype(np.float32) * np.float32(N_BUCKETS - max_exact)).astype(np.int32)
    large = np.minimum(large, N_BUCKETS - 1)
    return np.where(d < max_exact, d, large).astype(np.int32)


def _bias_kernel(bk_ref, rb_ref, o_ref):
    h = pl.program_id(1)
    bk = bk_ref[...]
    acc = jnp.zeros(bk.shape, F32)
    for b in range(N_BUCKETS):
        acc = jnp.where(bk == b, rb_ref[b, h], acc)
    cols = lax.broadcasted_iota(jnp.int32, bk.shape, 1)
    dist = WINDOW + lax.broadcasted_iota(jnp.int32, bk.shape, 0) - cols
    band = (dist >= 0) & (dist < WINDOW)
    keep = band & ((cols >= WINDOW) | (pl.program_id(0) == 1))
    o_ref[0, 0] = jnp.where(keep, acc, -jnp.inf)


def _bias_table(rel_bias):
    bk = jnp.asarray(_bucket_table())
    return pl.pallas_call(
        _bias_kernel,
        grid=(2, SWA_QH),
        in_specs=[pl.BlockSpec((WINDOW, 2 * WINDOW), lambda v, h: (0, 0)),
                  pl.BlockSpec(memory_space=pltpu.SMEM)],
        out_specs=pl.BlockSpec((1, 1, WINDOW, 2 * WINDOW), lambda v, h: (v, h, 0, 0)),
        out_shape=jax.ShapeDtypeStruct((2, SWA_QH, WINDOW, 2 * WINDOW), F32),
        compiler_params=_cparams(("parallel", "parallel")),
        name="swa_bias",
    )(bk, rel_bias)


def _swa_prompt_kernel(q_ref, kp_ref, kc_ref, vp_ref, vc_ref, bias_ref, sink_ref, y_ref, *, t, l):
    i = pl.program_id(1)
    w = WINDOW
    inseq = i * w + lax.broadcasted_iota(jnp.int32, (w, 1), 0) < t
    scale = SWA_HD ** -0.5
    for kv in range(SWA_KVH):
        hs = slice(kv * SWA_HD, (kv + 1) * SWA_HD)
        kh = jnp.concatenate([kp_ref[0, :, hs], jnp.where(inseq, kc_ref[0, :, hs], 0.0)],
                             axis=0).astype(BF16)
        vh = jnp.concatenate([vp_ref[0, :, hs], jnp.where(inseq, vc_ref[0, :, hs], 0.0)],
                             axis=0).astype(BF16)
        for g in range(SWA_GRP):
            h = kv * SWA_GRP + g
            qh = (q_ref[0, :, h * SWA_HD:(h + 1) * SWA_HD] * scale).astype(BF16)
            s = lax.dot_general(qh, kh, (((1,), (1,)), ((), ())), preferred_element_type=F32)
            s = s + bias_ref[h]
            sink = sink_ref[l, h]
            m = jnp.maximum(jnp.max(s, axis=-1, keepdims=True), sink)
            p = jnp.exp(s - m)
            den = jnp.sum(p, axis=-1, keepdims=True) + jnp.exp(sink - m)
            o = jnp.dot(p.astype(BF16), vh, preferred_element_type=F32) * (1.0 / den)
            y_ref[0, :, h * SWA_HD:(h + 1) * SWA_HD] = o.astype(y_ref.dtype)


def _swa_prompt(zt3, bias, sinks, l):
    nb, t, _ = zt3.shape
    w = WINDOW
    nblk = pl.cdiv(t, w)
    kern = functools.partial(_swa_prompt_kernel, t=t, l=l)
    kcol = TCOL_SK // SWA_KV
    vcol = TCOL_SV // SWA_KV
    prev = lambda i: jnp.maximum(i - 1, 0)
    return pl.pallas_call(
        kern,
        grid=(nb, nblk),
        in_specs=[pl.BlockSpec((1, w, SWA_Q), lambda n, i: (n, i, TCOL_SQ // SWA_Q)),
                  pl.BlockSpec((1, w, SWA_KV), lambda n, i: (n, prev(i), kcol)),
                  pl.BlockSpec((1, w, SWA_KV), lambda n, i: (n, i, kcol)),
                  pl.BlockSpec((1, w, SWA_KV), lambda n, i: (n, prev(i), vcol)),
                  pl.BlockSpec((1, w, SWA_KV), lambda n, i: (n, i, vcol)),
                  pl.BlockSpec((None, SWA_QH, w, 2 * w), lambda n, i: (jnp.minimum(i, 1), 0, 0, 0)),
                  pl.BlockSpec(memory_space=pltpu.SMEM)],
        out_specs=pl.BlockSpec((1, w, SWA_Q), lambda n, i: (n, i, 0)),
        out_shape=jax.ShapeDtypeStruct((nb, t, SWA_Q), BF16),
        compiler_params=_cparams(("parallel", "arbitrary")),
        name="swa_prompt",
    )(zt3, zt3, zt3, zt3, zt3, bias, sinks)


def _swa_decode_kernel(q_ref, kv_ref, kb_ref, vb_ref, bias_ref, sink_ref, y_ref, ko_ref, vo_ref, *, nstep):
    n0 = pl.program_id(0) * nstep
    scale = SWA_HD ** -0.5
    w = WINDOW
    ri = lax.broadcasted_iota(jnp.int32, (SWA_HD, SWA_HD), 0)
    ci = lax.broadcasted_iota(jnp.int32, (SWA_HD, SWA_HD), 1)
    eye = (ri == ci).astype(F32)
    last = lax.broadcasted_iota(jnp.int32, (SWA_HD, w), 1) == w - 1

    def push(cache_t, new_row):
        col = jnp.sum(eye * new_row, axis=1, keepdims=True)
        return jnp.where(last, col, pltpu.roll(cache_t, w - 1, axis=1))

    for j in range(nstep):
        qrow = q_ref[pl.ds(n0 + j, 1), :]
        kvrow = kv_ref[pl.ds(n0 + j, 1), :]
        for kv in range(SWA_KVH):
            hs = slice(kv * SWA_HD, (kv + 1) * SWA_HD)
            kt = push(kb_ref[j, kv], kvrow[:, hs])
            vt = push(vb_ref[j, kv], kvrow[:, SWA_KV + kv * SWA_HD:SWA_KV + (kv + 1) * SWA_HD])
            ko_ref[j, kv] = kt
            vo_ref[j, kv] = vt
            q4 = jnp.concatenate(
                [qrow[:, (kv * SWA_GRP + g) * SWA_HD:(kv * SWA_GRP + g + 1) * SWA_HD]
                 for g in range(SWA_GRP)], axis=0).astype(BF16)
            s = jnp.dot(q4, kt.astype(BF16), preferred_element_type=F32) * scale + bias_ref[kv]
            sink = sink_ref[kv * SWA_GRP:(kv + 1) * SWA_GRP, :]
            m = jnp.maximum(jnp.max(s, axis=-1, keepdims=True), sink)
            p = jnp.exp(s - m)
            den = jnp.sum(p, axis=-1, keepdims=True) + jnp.exp(sink - m)
            o = lax.dot_general((p / den).astype(BF16), vt.astype(BF16), (((1,), (1,)), ((), ())),
                                preferred_element_type=F32)
            for g in range(SWA_GRP):
                h = kv * SWA_GRP + g
                y_ref[j, :, h * SWA_HD:(h + 1) * SWA_HD] = o[g:g + 1, :].astype(y_ref.dtype)


def _swa_decode(zt, kbuf_t, vbuf_t, bias_dec, sinks, l, nstep=8):
    nb = zt.shape[0]
    w = WINDOW
    assert nb % nstep == 0
    kern = functools.partial(_swa_decode_kernel, nstep=nstep)
    cspec = lambda: _lspec(l, (nstep, SWA_KVH, SWA_HD, w), lambda n: (n, 0, 0, 0))
    ospec = lambda: pl.BlockSpec((nstep, SWA_KVH, SWA_HD, w), lambda n: (n, 0, 0, 0))
    return pl.pallas_call(
        kern,
        grid=(nb // nstep,),
        in_specs=[pl.BlockSpec((nb, SWA_Q), lambda n: (0, TCOL_SQ // SWA_Q)),
                  pl.BlockSpec((nb, 2 * SWA_KV), lambda n: (0, TCOL_SK // (2 * SWA_KV))),
                  cspec(), cspec(),
                  pl.BlockSpec((SWA_KVH, SWA_GRP, w), lambda n: (0, 0, 0)),
                  _lspec(l, (SWA_QH, 1), lambda n: (0, 0))],
        out_specs=[pl.BlockSpec((nstep, 1, SWA_Q), lambda n: (n, 0, 0)), ospec(), ospec()],
        out_shape=[jax.ShapeDtypeStruct((nb, 1, SWA_Q), BF16),
                   jax.ShapeDtypeStruct(kbuf_t.shape[1:], F32), jax.ShapeDtypeStruct(vbuf_t.shape[1:], F32)],
        compiler_params=_cparams(("arbitrary",)),
        name="swa_decode",
    )(zt, zt, kbuf_t, vbuf_t, bias_dec, sinks)


def _seq_chunk(t):
    for c in (688, 512, 256, 128, 64, 32, 16):
        if t % c == 0:
            return c
    raise ValueError(f"sequence length {t} must be a multiple of {GLA_CHUNK}")


def kernel(x_prompt, x_sample, state_s5_re, state_s5_im, state_gla, cache_swa_k, cache_swa_v, meta_tokens, rel_bias, norm_pre_mix, norm_post_mix, norm_pre_ffn, norm_post_ffn, w_in, s5_lam_re, s5_lam_im, s5_log_step, s5_b_re, s5_b_im, s5_c_re, s5_c_im, s5_d, s5_w_glu, s5_b_glu, gla_w_gate2, gla_b_gate2, gla_g_out, swa_sinks, w_up_s5, w_up_gla, w_up_swa, w_out, w_ff1, w_ff2):
    depth = w_in.shape[0]
    nb, seq, d = x_prompt.shape
    t = seq + N_META
    nd = x_sample.shape[0]
    assert x_sample.shape[1] == 1 and cache_swa_k.shape[2] == WINDOW and t % GLA_CHUNK == 0
    tc = _seq_chunk(t)

    w_in_t = jnp.swapaxes(w_in.astype(BF16), 1, 2)
    w_glu = s5_w_glu.astype(BF16)
    wg2 = gla_w_gate2.astype(BF16)
    wu_a, wu_b, wu_c = w_up_s5.astype(BF16), w_up_gla.astype(BF16), w_up_swa.astype(BF16)
    wo, w1, w2 = w_out.astype(BF16), w_ff1.astype(BF16), w_ff2.astype(BF16)
    row3 = lambda a: a.reshape(depth, 1, a.shape[-1])
    g_pre, g_post, g_pre_f, g_post_f = (row3(a) for a in (norm_pre_mix, norm_post_mix,
                                                           norm_pre_ffn, norm_post_ffn))
    b_glu, d5, bg2, go = row3(s5_b_glu), row3(s5_d), row3(gla_b_gate2), row3(gla_g_out)
    sinks3 = swa_sinks.reshape(depth, SWA_QH, 1)
    s5r = state_s5_re.reshape(depth, nd, -1)
    s5i = state_s5_im.reshape(depth, nd, -1)
    ckb = jnp.transpose(cache_swa_k, (0, 1, 3, 4, 2))
    cvb = jnp.transpose(cache_swa_v, (0, 1, 3, 4, 2))

    ar, ai, bbr_t, bbi_t = _s5_discretize(s5_lam_re, s5_lam_im, s5_log_step,
                                          jnp.swapaxes(s5_b_re, -1, -2), jnp.swapaxes(s5_b_im, -1, -2))
    bmat, cmat = _s5_matrices(bbr_t, bbi_t, s5_c_re, s5_c_im)
    a_half = jnp.concatenate([ar.reshape(depth, S5_NGB, 2, S5_HW), ai.reshape(depth, S5_NGB, 2, S5_HW)],
                             axis=-1)
    arow = jnp.tile(a_half, (1, 1, SUBLANES // 2, 1))
    adec = jnp.stack([ar.reshape(depth, -1), ai.reshape(depth, -1)], axis=1)

    bias = _bias_table(rel_bias)
    bias_dec = bias[1, :, WINDOW - 1, WINDOW:].reshape(SWA_KVH, SWA_GRP, WINDOW)

    meta = jnp.broadcast_to(meta_tokens.astype(x_prompt.dtype)[None], (nb, N_META, d))
    xp = jnp.concatenate([meta, x_prompt], axis=1).reshape(nb * t, d)
    xs = x_sample.reshape(nd, d)

    outs = {k: [] for k in ('p_re', 'p_im', 'p_gla', 'p_k', 'p_v', 's_re', 's_im', 's_gla', 's_k', 's_v')}
    for l in range(depth):
        zh, zhd, za, zad = _norm_mm(xp, xs, g_pre, w_in_t, l, 0, HEAD_COLS, 1536, extra_col0=COL_GA)
        zt, ztd = _norm_mm(xp, xs, g_pre, w_in_t, l, TAIL_START, MIX_COLS, 1280)
        zg, zgd = _norm_mm(xp, xs, g_pre, w_in_t, l, GATE_START, GATE_COLS, 2048, gate=True)
        zh3 = zh.reshape(nb, t, HEAD_COLS)
        za3 = za.reshape(nb, t, LANES)
        zt3 = zt.reshape(nb, t, MIX_COLS)
        z5, st5 = _s5_prompt(zh3, bmat, cmat, arow, d5, l, tc)
        z5d, nr, ni = _s5_decode(zhd, bmat, cmat, adec, d5, s5r, s5i, l)
        yb, pg = _gla_prompt(zh3, za3, zt3, wg2, bg2, go, l, tc)
        ybd, sg = _gla_decode(zhd, zad, ztd, wg2, bg2, go, state_gla, l)
        yc = _swa_prompt(zt3, bias, swa_sinks, l)
        ycd, sk, sv = _swa_decode(ztd, ckb, cvb, bias_dec, sinks3, l)
        xp, xs = _merge_out((z5.reshape(nb * t, S5_WIDTH), yb.reshape(nb * t, GLA_V),
                             yc.reshape(nb * t, SWA_Q)),
                            (z5d, ybd.reshape(nd, GLA_V), ycd.reshape(nd, SWA_Q)),
                            zg, zgd, w_glu, b_glu, wu_a, wu_b, wu_c, wo, xp, xs, g_post, l)
        xp, xs = _ffn(xp, xs, g_pre_f, w1, w2, g_post_f, l)
        st5 = st5.reshape(S5_NGB, nb, 2, 2, S5_HALF, S5_STATE).transpose(3, 1, 0, 2, 4, 5)
        st5 = st5.reshape(2, nb, S5_GROUPS, S5_STATE)
        outs['p_re'].append(st5[0])
        outs['p_im'].append(st5[1])
        outs['p_gla'].append(pg)
        outs['p_k'].append(zt3[:, t - WINDOW:, TCOL_SK:TCOL_SK + SWA_KV].reshape(nb, WINDOW, SWA_KVH, SWA_HD))
        outs['p_v'].append(zt3[:, t - WINDOW:, TCOL_SV:TCOL_SV + SWA_KV].reshape(nb, WINDOW, SWA_KVH, SWA_HD))
        outs['s_re'].append(nr.reshape(nd, S5_GROUPS, S5_STATE))
        outs['s_im'].append(ni.reshape(nd, S5_GROUPS, S5_STATE))
        outs['s_gla'].append(sg)
        outs['s_k'].append(jnp.transpose(sk, (0, 3, 1, 2)))
        outs['s_v'].append(jnp.transpose(sv, (0, 3, 1, 2)))

    y_prompt = xp.reshape(nb, t, d)[:, N_META:]
    y_sample = xs.reshape(nd, 1, d)
    st = {k: jnp.stack(v) for k, v in outs.items()}
    return (y_prompt, y_sample, st['p_re'], st['p_im'], st['p_gla'], st['p_k'], st['p_v'],
            st['s_re'], st['s_im'], st['s_gla'], st['s_k'], st['s_v'])
```

```python
import functools
import math

import numpy as np
import jax
import jax.numpy as jnp
from jax import lax
from jax.experimental import pallas as pl
from jax.experimental.pallas import tpu as pltpu

F32 = jnp.float32
BF16 = jnp.bfloat16

D_MODEL = 2048
N_META = 16
S5_GROUP = 16
S5_WIDTH = D_MODEL // 2
S5_GROUPS = S5_WIDTH // S5_GROUP
S5_STATE = 64
GLA_HEADS = 4
GLA_DK = D_MODEL // 16
GLA_DV = D_MODEL // 8
GLA_RANK = 16
GLA_TAU = 16.0
GLA_CHUNK = 16
SWA_QH = 16
SWA_KVH = 4
SWA_GRP = SWA_QH // SWA_KVH
SWA_HD = 64
WINDOW = 128
N_BUCKETS = 32
MAX_DISTANCE = WINDOW
D_FF = 4 * D_MODEL
N_BRANCH = 3
EPS = 1e-6

GLA_QK = GLA_HEADS * GLA_DK
GLA_V = GLA_HEADS * GLA_DV
SWA_Q = SWA_QH * SWA_HD
SWA_KV = SWA_KVH * SWA_HD

COL_U = 0
COL_GQ = COL_U + S5_WIDTH
COL_GK = COL_GQ + GLA_QK
COL_GV = COL_GK + GLA_QK
COL_GA = COL_GV + GLA_V
LANES = 128
SUBLANES = 8
HEAD_COLS = COL_GA
TAIL_START = COL_GA + GLA_RANK
TCOL_R = 0
TCOL_SQ = TCOL_R + GLA_V
TCOL_SK = TCOL_SQ + SWA_Q
TCOL_SV = TCOL_SK + SWA_KV
TCOL_GATE = TCOL_SV + SWA_KV
MIX_COLS = TCOL_GATE
GATE_START = TAIL_START + TCOL_GATE
GATE_COLS = N_BRANCH * D_MODEL

S5_GB = 8
S5_NGB = S5_GROUPS // S5_GB
S5_HALF = S5_GB // 2
S5_HW = S5_HALF * S5_STATE

VMEM_LIMIT = 60 * 1024 * 1024


def _cparams(sem):
    return pltpu.CompilerParams(dimension_semantics=sem, vmem_limit_bytes=VMEM_LIMIT)


def _rms(x, g):
    ms = jnp.mean(x * x, axis=-1, keepdims=True)
    return (x * lax.rsqrt(ms + EPS)) * g


def _row_tile(m):
    for t in (688, 512, 256, 128, 64, 32, 16, 8):
        if m % t == 0:
            return t
    return m


def _lspec(l, shape, imap):
    return pl.BlockSpec((None,) + tuple(shape), lambda *idx: (l,) + tuple(imap(*idx)))


class _Tiles:
    def __init__(self, m, n_inner):
        self.tm = _row_tile(m)
        self.npt = m // self.tm
        self.grid = (self.npt, n_inner)

    def dstep(self, i, j):
        return jnp.where(i == self.npt - 1, j, 0)


def _with_decode(npt, prompt_fn, decode_fn):
    prompt_fn()
    pl.when(pl.program_id(0) == npt - 1)(decode_fn)


def _nt_dot(h, wt):
    return lax.dot_general(h, wt, (((1,), (1,)), ((), ())), preferred_element_type=F32)


def _norm_mm_kernel(*refs, npt, gate, extra):
    if extra:
        xp_ref, xd_ref, g_ref, w_ref, we_ref, op_ref, od_ref, ep_ref, ed_ref, hp_ref, hd_ref = refs
    else:
        xp_ref, xd_ref, g_ref, w_ref, op_ref, od_ref, hp_ref, hd_ref = refs
        we_ref = ep_ref = ed_ref = None

    def run(x_ref, o_ref, e_ref, h_ref):
        @pl.when(pl.program_id(1) == 0)
        def _():
            h_ref[...] = _rms(x_ref[...], g_ref[...]).astype(BF16)
            if extra:
                e_ref[...] = _nt_dot(h_ref[...], we_ref[0])

        z = _nt_dot(h_ref[...], w_ref[0])
        o_ref[...] = (jax.nn.sigmoid(z) if gate else z).astype(o_ref.dtype)

    _with_decode(npt, lambda: run(xp_ref, op_ref, ep_ref, hp_ref),
                 lambda: run(xd_ref, od_ref, ed_ref, hd_ref))


def _norm_mm(xp, xd, g, wt, l, col0, n_cols, tn, gate=False, extra_col0=None):
    m, k = xp.shape
    nd = xd.shape[0]
    tl = _Tiles(m, n_cols // tn)
    extra = extra_col0 is not None
    odt = BF16 if gate else F32
    kern = functools.partial(_norm_mm_kernel, npt=tl.npt, gate=gate, extra=extra)
    wspec = lambda width, off: pl.BlockSpec(
        (pl.Element(1), pl.Element(width), pl.Element(k)),
        lambda i, j: (l, pl.multiple_of(off(j), 16), 0))
    in_specs = [pl.BlockSpec((tl.tm, k), lambda i, j: (i, 0)),
                pl.BlockSpec((nd, k), lambda i, j: (0, 0)),
                _lspec(l, (1, k), lambda i, j: (0, 0)),
                wspec(tn, lambda j: col0 + j * tn)]
    out_specs = [pl.BlockSpec((tl.tm, tn), lambda i, j: (i, j)),
                 pl.BlockSpec((nd, tn), lambda i, j: (0, tl.dstep(i, j)))]
    out_shape = [jax.ShapeDtypeStruct((m, n_cols), odt), jax.ShapeDtypeStruct((nd, n_cols), odt)]
    args = [xp, xd, g, wt]
    if extra:
        in_specs.append(wspec(LANES, lambda j: extra_col0 + 0 * j))
        out_specs += [pl.BlockSpec((tl.tm, LANES), lambda i, j: (i, 0)),
                      pl.BlockSpec((nd, LANES), lambda i, j: (0, 0))]
        out_shape += [jax.ShapeDtypeStruct((m, LANES), F32), jax.ShapeDtypeStruct((nd, LANES), F32)]
        args.append(wt)
    return pl.pallas_call(
        kern,
        grid=tl.grid,
        in_specs=in_specs,
        out_specs=out_specs,
        out_shape=out_shape,
        scratch_shapes=[pltpu.VMEM((tl.tm, k), BF16), pltpu.VMEM((nd, k), BF16)],
        compiler_params=_cparams(("arbitrary", "arbitrary")),
        name="norm_mm",
    )(*args)


def _merge_out_kernel(zp_ref, ybp_ref, ycp_ref, zd_ref, ybd_ref, ycd_ref,
                      gap_ref, gbp_ref, gcp_ref, gad_ref, gbd_ref, gcd_ref,
                      wglu_ref, bglu_ref, wa_ref, wb_ref, wc_ref, wo_ref, xp_ref, xd_ref, g_ref,
                      op_ref, od_ref, mp_ref, md_ref, yap_ref, yad_ref, *, npt, nk, tk, tn):
    s = pl.program_id(1)

    def run(z_ref, ya_ref, yb_ref, yc_ref, ga_ref, gb_ref, gc_ref, x_ref, o_ref, m_ref):
        @pl.when(s == 0)
        def _():
            z = z_ref[...]
            a = jnp.dot(z.astype(BF16), wglu_ref[...], preferred_element_type=F32) + bglu_ref[...]
            ya_ref[...] = (z * jax.nn.sigmoid(a)).astype(BF16)

        @pl.when(s < nk)
        def _():
            m = (ga_ref[...].astype(F32) * jnp.dot(ya_ref[...], wa_ref[...], preferred_element_type=F32)
                 + gb_ref[...].astype(F32) * jnp.dot(yb_ref[...], wb_ref[...], preferred_element_type=F32)
                 + gc_ref[...].astype(F32) * jnp.dot(yc_ref[...], wc_ref[...], preferred_element_type=F32))
            m_ref[:, pl.ds(pl.multiple_of(s * tk, tk), tk)] = m.astype(BF16)

        @pl.when(s >= nk)
        def _():
            c0 = pl.multiple_of((s - nk) * tn, tn)
            o_ref[:, pl.ds(c0, tn)] = jnp.dot(m_ref[...], wo_ref[...], preferred_element_type=F32)

        @pl.when(s == pl.num_programs(1) - 1)
        def _():
            o_ref[...] = x_ref[...] + _rms(o_ref[...], g_ref[...])

    _with_decode(
        npt,
        lambda: run(zp_ref, yap_ref, ybp_ref, ycp_ref, gap_ref, gbp_ref, gcp_ref, xp_ref, op_ref, mp_ref),
        lambda: run(zd_ref, yad_ref, ybd_ref, ycd_ref, gad_ref, gbd_ref, gcd_ref, xd_ref, od_ref, md_ref))


def _merge_out(yp, yd, ztp, ztd, wglu, bglu, wa, wb, wc, wo, xp, xd, g, l, tk=256, tn=512):
    m, d = xp.shape
    nd = xd.shape[0]
    kb = yp[0].shape[1]
    gate0 = 0
    nk = d // tk
    tl = _Tiles(m, nk + d // tn)
    kidx = lambda s: jnp.minimum(s, nk - 1)
    nidx = lambda s: jnp.maximum(s - nk, 0)
    gp_spec = lambda br: pl.BlockSpec((tl.tm, tk), lambda i, s: (i, gate0 + br * nk + kidx(s)))
    gd_spec = lambda br: pl.BlockSpec(
        (nd, tk), lambda i, s: (0, gate0 + br * nk + kidx(tl.dstep(i, s))))
    yp_spec = pl.BlockSpec((tl.tm, kb), lambda i, s: (i, 0))
    yd_spec = pl.BlockSpec((nd, kb), lambda i, s: (0, 0))
    w_spec = _lspec(l, (kb, tk), lambda i, s: (0, kidx(s)))
    kern = functools.partial(_merge_out_kernel, npt=tl.npt, nk=nk, tk=tk, tn=tn)
    return pl.pallas_call(
        kern,
        grid=tl.grid,
        in_specs=[yp_spec, yp_spec, yp_spec, yd_spec, yd_spec, yd_spec,
                  gp_spec(0), gp_spec(1), gp_spec(2), gd_spec(0), gd_spec(1), gd_spec(2),
                  _lspec(l, (kb, kb), lambda i, s: (0, 0)),
                  _lspec(l, (1, kb), lambda i, s: (0, 0)),
                  w_spec, w_spec, w_spec,
                  _lspec(l, (d, tn), lambda i, s: (0, nidx(s))),
                  pl.BlockSpec((tl.tm, d), lambda i, s: (i, 0)),
                  pl.BlockSpec((nd, d), lambda i, s: (0, 0)),
                  _lspec(l, (1, d), lambda i, s: (0, 0))],
        out_specs=[pl.BlockSpec((tl.tm, d), lambda i, s: (i, 0)),
                   pl.BlockSpec((nd, d), lambda i, s: (0, 0))],
        out_shape=[jax.ShapeDtypeStruct((m, d), F32), jax.ShapeDtypeStruct((nd, d), F32)],
        scratch_shapes=[pltpu.VMEM((tl.tm, d), BF16), pltpu.VMEM((nd, d), BF16),
                        pltpu.VMEM((tl.tm, kb), BF16), pltpu.VMEM((nd, kb), BF16)],
        compiler_params=_cparams(("arbitrary", "arbitrary")),
        name="merge_out",
    )(*yp, *yd, ztp, ztp, ztp, ztd, ztd, ztd, wglu, bglu, wa, wb, wc, wo, xp, xd, g)


def _ffn_kernel(xp_ref, xd_ref, g1_ref, w1_ref, w2_ref, g2_ref, op_ref, od_ref,
                hp_ref, accp_ref, hd_ref, accd_ref, *, npt):
    f = pl.program_id(1)

    def run(x_ref, o_ref, h_ref, acc_ref):
        @pl.when(f == 0)
        def _():
            h_ref[...] = _rms(x_ref[...], g1_ref[...]).astype(BF16)
            acc_ref[...] = jnp.zeros_like(acc_ref)

        a = jnp.dot(h_ref[...], w1_ref[...].astype(BF16), preferred_element_type=F32)
        a = jnp.square(jnp.maximum(a, 0.0)).astype(BF16)
        acc_ref[...] += jnp.dot(a, w2_ref[...].astype(BF16), preferred_element_type=F32)

        @pl.when(f == pl.num_programs(1) - 1)
        def _():
            o_ref[...] = x_ref[...] + _rms(acc_ref[...], g2_ref[...])

    _with_decode(npt, lambda: run(xp_ref, op_ref, hp_ref, accp_ref),
                 lambda: run(xd_ref, od_ref, hd_ref, accd_ref))


def _ffn(xp, xd, g1, w1, w2, g2, l, tf=512):
    m, d = xp.shape
    nd = xd.shape[0]
    ff = w1.shape[2]
    tl = _Tiles(m, ff // tf)
    kern = functools.partial(_ffn_kernel, npt=tl.npt)
    return pl.pallas_call(
        kern,
        grid=tl.grid,
        in_specs=[pl.BlockSpec((tl.tm, d), lambda i, f: (i, 0)),
                  pl.BlockSpec((nd, d), lambda i, f: (0, 0)),
                  _lspec(l, (1, d), lambda i, f: (0, 0)),
                  _lspec(l, (d, tf), lambda i, f: (0, f)),
                  _lspec(l, (tf, d), lambda i, f: (f, 0)),
                  _lspec(l, (1, d), lambda i, f: (0, 0))],
        out_specs=[pl.BlockSpec((tl.tm, d), lambda i, f: (i, 0)),
                   pl.BlockSpec((nd, d), lambda i, f: (0, 0))],
        out_shape=[jax.ShapeDtypeStruct((m, d), F32), jax.ShapeDtypeStruct((nd, d), F32)],
        scratch_shapes=[pltpu.VMEM((tl.tm, d), BF16), pltpu.VMEM((tl.tm, d), F32),
                        pltpu.VMEM((nd, d), BF16), pltpu.VMEM((nd, d), F32)],
        compiler_params=_cparams(("arbitrary", "arbitrary")),
        name="ffn",
    )(xp, xd, g1, w1, w2, g2)


def _s5_disc_kernel(lr_ref, li_ref, ls_ref, br_ref, bi_ref, ar_ref, ai_ref, bbr_ref, bbi_ref):
    lr = lr_ref[0]
    li = li_ref[0]
    dt = jnp.exp(ls_ref[0])
    mag = jnp.exp(lr * dt)
    ar = mag * jnp.cos(li * dt)
    ai = mag * jnp.sin(li * dt)
    den = lr * lr + li * li
    fr = ((ar - 1.0) * lr + ai * li) / den
    fi = (ai * lr - (ar - 1.0) * li) / den
    ar_ref[0] = ar
    ai_ref[0] = ai
    br = br_ref[0]
    bi = bi_ref[0]
    bbr_ref[0] = fr[:, None, :] * br - fi[:, None, :] * bi
    bbi_ref[0] = fr[:, None, :] * bi + fi[:, None, :] * br


def _s5_discretize(lam_re, lam_im, log_step, b_re_t, b_im_t):
    depth = lam_re.shape[0]
    gp = pl.BlockSpec((1, S5_GROUPS, S5_STATE), lambda l: (l, 0, 0))
    bspec = pl.BlockSpec((1, S5_GROUPS, S5_GROUP, S5_STATE), lambda l: (l, 0, 0, 0))
    return pl.pallas_call(
        _s5_disc_kernel,
        grid=(depth,),
        in_specs=[gp, gp, pl.BlockSpec((1, S5_GROUPS, 1), lambda l: (l, 0, 0)), bspec, bspec],
        out_specs=[gp, gp, bspec, bspec],
        out_shape=[jax.ShapeDtypeStruct(lam_re.shape, F32), jax.ShapeDtypeStruct(lam_re.shape, F32),
                   jax.ShapeDtypeStruct(b_re_t.shape, F32), jax.ShapeDtypeStruct(b_re_t.shape, F32)],
        compiler_params=_cparams(("parallel",)),
        name="s5_discretize",
    )(lam_re, lam_im, log_step.reshape(depth, S5_GROUPS, 1), b_re_t, b_im_t)


def _blockdiag(r):
    eye = jnp.eye(S5_HALF, dtype=r.dtype)
    z = jnp.einsum('ab,lgqbhp->lgqahbp', eye, r)
    s = r.shape
    return z.reshape(s[0], s[1], s[2], S5_HALF * s[4], S5_HALF * s[5])


def _s5_matrices(bbr_t, bbi_t, c_re, c_im):
    depth = bbr_t.shape[0]
    shp = (depth, S5_NGB, 2, S5_HALF, S5_GROUP, S5_STATE)
    bre = _blockdiag(bbr_t.reshape(shp))
    bim = _blockdiag(bbi_t.reshape(shp))
    bcat = jnp.concatenate([bre, bim], axis=-1)
    zb = jnp.zeros_like(bcat[:, :, 0])
    bmat = jnp.stack([jnp.concatenate([bcat[:, :, 0], zb], axis=-2),
                      jnp.concatenate([zb, bcat[:, :, 1]], axis=-2)], axis=2)
    cre = jnp.swapaxes(_blockdiag(c_re.reshape(shp)), -1, -2)
    cim = jnp.swapaxes(_blockdiag(c_im.reshape(shp)), -1, -2)
    ccat = jnp.concatenate([cre, -cim], axis=-2)
    zc = jnp.zeros_like(ccat[:, :, 0])
    cmat = jnp.stack([jnp.concatenate([ccat[:, :, 0], zc], axis=-1),
                      jnp.concatenate([zc, ccat[:, :, 1]], axis=-1)], axis=2)
    return bmat.astype(BF16), cmat.astype(BF16)


def _s5_prompt_kernel(u_ref, b_ref, c_ref, a_ref, d_ref, z_ref, st_ref, e_ref, x_ref, *, nb, tc, ngb):
    @pl.when(pl.program_id(1) == 0)
    def _():
        x_ref[...] = jnp.zeros_like(x_ref)

    nlb = 2 * S5_HW // LANES
    hlb = nlb // 2
    for b in range(ngb):
        for n in range(nb):
            ub = u_ref[n, :, b * LANES:(b + 1) * LANES].astype(BF16)
            for q in range(2):
                e = jnp.dot(ub, b_ref[b, q], preferred_element_type=F32)
                for j in range(nlb):
                    e_ref[b * nlb + j, pl.ds(n * 2 + q, tc, stride=SUBLANES), :] = (
                        e[:, j * LANES:(j + 1) * LANES])

    ar, ai = [], []
    for b in range(ngb):
        a = a_ref[b]
        ar += [a[:, j * LANES:(j + 1) * LANES] for j in range(hlb)]
        ai += [a[:, (hlb + j) * LANES:(hlb + j + 1) * LANES] for j in range(hlb)]
    slabs = [(b * nlb + j, b * nlb + hlb + j) for b in range(ngb) for j in range(hlb)]

    def step(t, carry):
        row = pl.multiple_of(t * SUBLANES, SUBLANES)
        new = []
        for c, (sr, si) in enumerate(slabs):
            xr, xi = carry[2 * c], carry[2 * c + 1]
            nxr = ar[c] * xr - ai[c] * xi + e_ref[sr, pl.ds(row, SUBLANES), :]
            nxi = ar[c] * xi + ai[c] * xr + e_ref[si, pl.ds(row, SUBLANES), :]
            e_ref[sr, pl.ds(row, SUBLANES), :] = nxr
            e_ref[si, pl.ds(row, SUBLANES), :] = nxi
            new += [nxr, nxi]
        return tuple(new)

    init = []
    for b in range(ngb):
        x0 = x_ref[b]
        for j in range(hlb):
            init += [x0[:, j * LANES:(j + 1) * LANES], x0[:, (hlb + j) * LANES:(hlb + j + 1) * LANES]]
    fin = lax.fori_loop(0, tc, step, tuple(init), unroll=8)
    for b in range(ngb):
        for j in range(hlb):
            c = b * hlb + j
            x_ref[b, :, j * LANES:(j + 1) * LANES] = fin[2 * c]
            x_ref[b, :, (hlb + j) * LANES:(hlb + j + 1) * LANES] = fin[2 * c + 1]
    st_ref[...] = x_ref[...]

    for b in range(ngb):
        for n in range(nb):
            y = d_ref[:, b * LANES:(b + 1) * LANES] * u_ref[n, :, b * LANES:(b + 1) * LANES]
            for q in range(2):
                xs = jnp.concatenate(
                    [e_ref[b * nlb + j, pl.ds(n * 2 + q, tc, stride=SUBLANES), :] for j in range(nlb)],
                    axis=1)
                y = y + jnp.dot(xs.astype(BF16), c_ref[b, q], preferred_element_type=F32)
            z_ref[n, :, b * LANES:(b + 1) * LANES] = jax.nn.gelu(y)


def _s5_prompt(zh3, bmat, cmat, arow, d, l, tc):
    nb, t, _ = zh3.shape
    assert 2 * nb == SUBLANES, "state rows are (batch, half) pairs filling one sublane tile"
    ngb = 2
    kern = functools.partial(_s5_prompt_kernel, nb=nb, tc=tc, ngb=ngb)
    return pl.pallas_call(
        kern,
        grid=(S5_NGB // ngb, t // tc),
        in_specs=[pl.BlockSpec((nb, tc, ngb * LANES), lambda g, c: (0, c, g)),
                  _lspec(l, (ngb, 2, LANES, 2 * S5_HW), lambda g, c: (g, 0, 0, 0)),
                  _lspec(l, (ngb, 2, 2 * S5_HW, LANES), lambda g, c: (g, 0, 0, 0)),
                  _lspec(l, (ngb, SUBLANES, 2 * S5_HW), lambda g, c: (g, 0, 0)),
                  _lspec(l, (1, ngb * LANES), lambda g, c: (0, g))],
        out_specs=[pl.BlockSpec((nb, tc, ngb * LANES), lambda g, c: (0, c, g)),
                   pl.BlockSpec((ngb, SUBLANES, 2 * S5_HW), lambda g, c: (g, 0, 0))],
        out_shape=[jax.ShapeDtypeStruct((nb, t, S5_WIDTH), F32),
                   jax.ShapeDtypeStruct((S5_NGB, SUBLANES, 2 * S5_HW), F32)],
        scratch_shapes=[pltpu.VMEM((ngb * 2 * S5_HW // LANES, tc * SUBLANES, LANES), F32),
                        pltpu.VMEM((ngb, SUBLANES, 2 * S5_HW), F32)],
        compiler_params=_cparams(("parallel", "arbitrary")),
        name="s5_prompt",
    )(zh3, bmat, cmat, arow, d)


def _s5_decode_kernel(u_ref, b_ref, c_ref, a_ref, d_ref, sr_ref, si_ref, z_ref, nr_ref, ni_ref):
    for g in range(S5_NGB):
        u = u_ref[:, g * LANES:(g + 1) * LANES]
        ub = u.astype(BF16)
        y = d_ref[:, g * LANES:(g + 1) * LANES] * u
        for q in range(2):
            c0 = (g * 2 + q) * S5_HW
            e = jnp.dot(ub, b_ref[g, q], preferred_element_type=F32)
            ar = a_ref[0:1, c0:c0 + S5_HW]
            ai = a_ref[1:2, c0:c0 + S5_HW]
            x0r = sr_ref[:, c0:c0 + S5_HW]
            x0i = si_ref[:, c0:c0 + S5_HW]
            nr = ar * x0r - ai * x0i + e[:, :S5_HW]
            ni = ar * x0i + ai * x0r + e[:, S5_HW:]
            nr_ref[:, c0:c0 + S5_HW] = nr
            ni_ref[:, c0:c0 + S5_HW] = ni
            xs = jnp.concatenate([nr, ni], axis=1).astype(BF16)
            y = y + jnp.dot(xs, c_ref[g, q], preferred_element_type=F32)
        z_ref[:, g * LANES:(g + 1) * LANES] = jax.nn.gelu(y)


def _s5_decode(zh, bmat, cmat, adec, d, sr, si, l):
    nb = zh.shape[0]
    ns = S5_GROUPS * S5_STATE
    full = lambda shape: pl.BlockSpec(shape, lambda i: (0,) * len(shape))
    lfull = lambda shape: _lspec(l, shape, lambda i: (0,) * len(shape))
    return pl.pallas_call(
        _s5_decode_kernel,
        grid=(1,),
        in_specs=[full((nb, S5_WIDTH)), lfull(bmat.shape[1:]), lfull(cmat.shape[1:]), lfull((2, ns)),
                  lfull((1, S5_WIDTH)), lfull((nb, ns)), lfull((nb, ns))],
        out_specs=[full((nb, S5_WIDTH)), full((nb, ns)), full((nb, ns))],
        out_shape=[jax.ShapeDtypeStruct((nb, S5_WIDTH), F32),
                   jax.ShapeDtypeStruct((nb, ns), F32), jax.ShapeDtypeStruct((nb, ns), F32)],
        compiler_params=_cparams(("arbitrary",)),
        name="s5_decode",
    )(zh, bmat, cmat, adec, d, sr, si)


def _gla_finish(o, r, g):
    return _rms(o, g) * (r * jax.nn.sigmoid(r))


def _gla_prompt_kernel(q_ref, k_ref, v_ref, a_ref, r_ref, wg_ref, bg_ref, go_ref, y_ref, st_ref,
                       s_ref, lg_ref, o_ref, qi_ref, ku_ref, cd_ref, *, tc):
    c = pl.program_id(1)

    @pl.when(c == 0)
    def _():
        s_ref[...] = jnp.zeros_like(s_ref)

    a16 = a_ref[0][:, :GLA_RANK].astype(BF16)
    lg_ref[...] = jax.nn.log_sigmoid(
        jnp.dot(a16, wg_ref[...], preferred_element_type=F32) + bg_ref[...]) / GLA_TAU

    ch = GLA_CHUNK
    hs = SUBLANES
    rowi = lax.broadcasted_iota(jnp.int32, (ch, GLA_DK), 0)
    rowh = lax.broadcasted_iota(jnp.int32, (hs, GLA_DK), 0)
    coli = lax.broadcasted_iota(jnp.int32, (hs, ch), 1)
    scale = GLA_DK ** -0.5

    def prep(idx, carry):
        r0 = pl.multiple_of(idx * ch, ch)
        for h in range(GLA_HEADS):
            ks = slice(h * GLA_DK, (h + 1) * GLA_DK)
            vs = slice(h * GLA_DV, (h + 1) * GLA_DV)
            b = lg_ref[pl.ds(r0, ch), ks]
            for sh in (1, 2, 4, 8):
                b = b + jnp.where(rowi >= sh, pltpu.roll(b, sh, axis=0), 0.0)
            qh = q_ref[0, pl.ds(r0, ch), ks] * scale
            kh = k_ref[0, pl.ds(r0, ch), ks]
            halves = []
            for half in range(ch // hs):
                bq = b[half * hs:(half + 1) * hs]
                qq = qh[half * hs:(half + 1) * hs]
                att = jnp.zeros((hs, ch), F32)
                for s in range((half + 1) * hs):
                    diff = bq - b[s:s + 1, :]
                    if s >= half * hs:
                        diff = jnp.where(rowh + half * hs >= s, diff, -jnp.inf)
                    col = jnp.sum(qq * kh[s:s + 1, :] * jnp.exp(diff), axis=1, keepdims=True)
                    att = jnp.where(coli == s, col, att)
                halves.append(att)
            att = jnp.concatenate(halves, axis=0).astype(BF16)
            vh = v_ref[0, pl.ds(r0, ch), vs].astype(BF16)
            o_ref[pl.ds(r0, ch), vs] = jnp.dot(att, vh, preferred_element_type=F32)
            blast = b[ch - 1:ch, :]
            qi_ref[pl.ds(r0, ch), ks] = (qh * jnp.exp(b)).astype(BF16)
            ku_ref[pl.ds(r0, ch), ks] = (kh * jnp.exp(blast - b)).astype(BF16)
            cd_ref[pl.ds(pl.multiple_of(idx * hs, hs), hs), ks] = jnp.broadcast_to(
                jnp.exp(blast), (hs, GLA_DK))
        return carry

    lax.fori_loop(0, tc // ch, prep, 0, unroll=2)

    def scan(idx, carry):
        r0 = pl.multiple_of(idx * ch, ch)
        for h in range(GLA_HEADS):
            ks = slice(h * GLA_DK, (h + 1) * GLA_DK)
            vs = slice(h * GLA_DV, (h + 1) * GLA_DV)
            st = s_ref[h]
            inter = lax.dot_general(qi_ref[pl.ds(r0, ch), ks], st.astype(BF16),
                                    (((1,), (1,)), ((), ())), preferred_element_type=F32)
            o_ref[pl.ds(r0, ch), vs] += inter
            vh = v_ref[0, pl.ds(r0, ch), vs].astype(BF16)
            kv = lax.dot_general(vh, ku_ref[pl.ds(r0, ch), ks], (((0,), (0,)), ((), ())),
                                 preferred_element_type=F32)
            cd = cd_ref[pl.ds(pl.multiple_of(idx * hs, hs), 1), ks]
            s_ref[h] = cd * st + kv
        return carry

    lax.fori_loop(0, tc // ch, scan, 0, unroll=4)

    for h in range(GLA_HEADS):
        vs = slice(h * GLA_DV, (h + 1) * GLA_DV)
        y_ref[0, :, vs] = _gla_finish(o_ref[:, vs], r_ref[0, :, vs], go_ref[...]).astype(y_ref.dtype)
        st_ref[0, h] = s_ref[h].T


def _gla_prompt(zh3, za3, zt3, wg, bg, go, l, tc):
    nb, t, _ = zh3.shape
    kern = functools.partial(_gla_prompt_kernel, tc=tc)
    return pl.pallas_call(
        kern,
        grid=(nb, t // tc),
        in_specs=[pl.BlockSpec((1, tc, GLA_QK), lambda n, c: (n, c, COL_GQ // GLA_QK)),
                  pl.BlockSpec((1, tc, GLA_QK), lambda n, c: (n, c, COL_GK // GLA_QK)),
                  pl.BlockSpec((1, tc, GLA_V), lambda n, c: (n, c, COL_GV // GLA_V)),
                  pl.BlockSpec((1, tc, LANES), lambda n, c: (n, c, 0)),
                  pl.BlockSpec((1, tc, GLA_V), lambda n, c: (n, c, TCOL_R // GLA_V)),
                  _lspec(l, (GLA_RANK, GLA_QK), lambda n, c: (0, 0)),
                  _lspec(l, (1, GLA_QK), lambda n, c: (0, 0)),
                  _lspec(l, (1, GLA_DV), lambda n, c: (0, 0))],
        out_specs=[pl.BlockSpec((1, tc, GLA_V), lambda n, c: (n, c, 0)),
                   pl.BlockSpec((1, GLA_HEADS, GLA_DK, GLA_DV), lambda n, c: (n, 0, 0, 0))],
        out_shape=[jax.ShapeDtypeStruct((nb, t, GLA_V), BF16),
                   jax.ShapeDtypeStruct((nb, GLA_HEADS, GLA_DK, GLA_DV), F32)],
        scratch_shapes=[pltpu.VMEM((GLA_HEADS, GLA_DV, GLA_DK), F32),
                        pltpu.VMEM((tc, GLA_QK), F32),
                        pltpu.VMEM((tc, GLA_V), F32),
                        pltpu.VMEM((tc, GLA_QK), BF16),
                        pltpu.VMEM((tc, GLA_QK), BF16),
                        pltpu.VMEM((tc // GLA_CHUNK * SUBLANES, GLA_QK), F32)],
        compiler_params=_cparams(("parallel", "arbitrary")),
        name="gla_prompt",
    )(zh3, zh3, zh3, za3, zt3, wg, bg, go)


def _gla_decode_kernel(zh_ref, za_ref, r_ref, wg_ref, bg_ref, go_ref, s_ref, y_ref, so_ref, *, nstep):
    n0 = pl.multiple_of(pl.program_id(0) * nstep, nstep)
    rows = zh_ref[pl.ds(n0, nstep), :]
    a16 = za_ref[pl.ds(n0, nstep), :GLA_RANK].astype(BF16)
    lg = jax.nn.log_sigmoid(
        jnp.dot(a16, wg_ref[...], preferred_element_type=F32) + bg_ref[...]) / GLA_TAU
    eg = jnp.exp(lg)
    rr = r_ref[pl.ds(n0, nstep), :]
    ri = lax.broadcasted_iota(jnp.int32, (GLA_DK, GLA_DK), 0)
    ci = lax.broadcasted_iota(jnp.int32, (GLA_DK, GLA_DK), 1)
    eye = (ri == ci).astype(F32)

    def col(v):
        return jnp.sum(eye * v, axis=1, keepdims=True)

    for j in range(nstep):
        row = rows[j:j + 1]
        for h in range(GLA_HEADS):
            ks = slice(h * GLA_DK, (h + 1) * GLA_DK)
            qh = row[:, COL_GQ + h * GLA_DK:COL_GQ + (h + 1) * GLA_DK] * (GLA_DK ** -0.5)
            kh = row[:, COL_GK + h * GLA_DK:COL_GK + (h + 1) * GLA_DK]
            vh = row[:, COL_GV + h * GLA_DV:COL_GV + (h + 1) * GLA_DV]
            s = col(eg[j:j + 1, ks]) * s_ref[j, h] + col(kh) * vh
            so_ref[j, h] = s
            o = jnp.sum(col(qh) * s, axis=0, keepdims=True)
            vs = slice(h * GLA_DV, (h + 1) * GLA_DV)
            y_ref[j, :, vs] = _gla_finish(o, rr[j:j + 1, vs], go_ref[...]).astype(y_ref.dtype)


def _gla_decode(zh, za, zt, wg, bg, go, s0, l, nstep=SUBLANES):
    nb = zh.shape[0]
    assert nb % nstep == 0
    kern = functools.partial(_gla_decode_kernel, nstep=nstep)
    sblock = (nstep, GLA_HEADS, GLA_DK, GLA_DV)
    return pl.pallas_call(
        kern,
        grid=(nb // nstep,),
        in_specs=[pl.BlockSpec((nb, HEAD_COLS), lambda n: (0, 0)),
                  pl.BlockSpec((nb, LANES), lambda n: (0, 0)),
                  pl.BlockSpec((nb, GLA_V), lambda n: (0, TCOL_R // GLA_V)),
                  _lspec(l, (GLA_RANK, GLA_QK), lambda n: (0, 0)),
                  _lspec(l, (1, GLA_QK), lambda n: (0, 0)),
                  _lspec(l, (1, GLA_DV), lambda n: (0, 0)),
                  _lspec(l, sblock, lambda n: (n, 0, 0, 0))],
        out_specs=[pl.BlockSpec((nstep, 1, GLA_V), lambda n: (n, 0, 0)),
                   pl.BlockSpec(sblock, lambda n: (n, 0, 0, 0))],
        out_shape=[jax.ShapeDtypeStruct((nb, 1, GLA_V), BF16),
                   jax.ShapeDtypeStruct(s0.shape[1:], F32)],
        compiler_params=_cparams(("arbitrary",)),
        name="gla_decode",
    )(zh, za, zt, wg, bg, go, s0)


def _bucket_table():
    dist = (np.arange(WINDOW)[:, None] + WINDOW) - np.arange(2 * WINDOW)[None, :]
    max_exact = N_BUCKETS // 2
    d = np.maximum(dist, 0)
    ratio = np.log(np.maximum(d, 1).astype(np.float32) / np.float32(max_exact)) / np.float32(
        math.log(MAX_DISTANCE / max_exact))
    large = max_exact + (ratio.astype(np.float32) * np.float32(N_BUCKETS - max_exact)).astype(np.int32)
    large = np.minimum(large, N_BUCKETS - 1)
    return np.where(d < max_exact, d, large).astype(np.int32)


def _bias_kernel(bk_ref, rb_ref, o_ref):
    h = pl.program_id(1)
    bk = bk_ref[...]
    acc = jnp.zeros(bk.shape, F32)
    for b in range(N_BUCKETS):
        acc = jnp.where(bk == b, rb_ref[b, h], acc)
    cols = lax.broadcasted_iota(jnp.int32, bk.shape, 1)
    dist = WINDOW + lax.broadcasted_iota(jnp.int32, bk.shape, 0) - cols
    band = (dist >= 0) & (dist < WINDOW)
    keep = band & ((cols >= WINDOW) | (pl.program_id(0) == 1))
    o_ref[0, 0] = jnp.where(keep, acc, -jnp.inf)


def _bias_table(rel_bias):
    bk = jnp.asarray(_bucket_table())
    return pl.pallas_call(
        _bias_kernel,
        grid=(2, SWA_QH),
        in_specs=[pl.BlockSpec((WINDOW, 2 * WINDOW), lambda v, h: (0, 0)),
                  pl.BlockSpec(memory_space=pltpu.SMEM)],
        out_specs=pl.BlockSpec((1, 1, WINDOW, 2 * WINDOW), lambda v, h: (v, h, 0, 0)),
        out_shape=jax.ShapeDtypeStruct((2, SWA_QH, WINDOW, 2 * WINDOW), F32),
        compiler_params=_cparams(("parallel", "parallel")),
        name="swa_bias",
    )(bk, rel_bias)


def _swa_prompt_kernel(q_ref, kp_ref, kc_ref, vp_ref, vc_ref, bias_ref, sink_ref, y_ref, *, t, l):
    i = pl.program_id(1)
    w = WINDOW
    inseq = i * w + lax.broadcasted_iota(jnp.int32, (w, 1), 0) < t
    scale = SWA_HD ** -0.5
    for kv in range(SWA_KVH):
        hs = slice(kv * SWA_HD, (kv + 1) * SWA_HD)
        kh = jnp.concatenate([kp_ref[0, :, hs], jnp.where(inseq, kc_ref[0, :, hs], 0.0)],
                             axis=0).astype(BF16)
        vh = jnp.concatenate([vp_ref[0, :, hs], jnp.where(inseq, vc_ref[0, :, hs], 0.0)],
                             axis=0).astype(BF16)
        for g in range(SWA_GRP):
            h = kv * SWA_GRP + g
            qh = (q_ref[0, :, h * SWA_HD:(h + 1) * SWA_HD] * scale).astype(BF16)
            s = lax.dot_general(qh, kh, (((1,), (1,)), ((), ())), preferred_element_type=F32)
            s = s + bias_ref[h]
            sink = sink_ref[l, h]
            m = jnp.maximum(jnp.max(s, axis=-1, keepdims=True), sink)
            p = jnp.exp(s - m)
            den = jnp.sum(p, axis=-1, keepdims=True) + jnp.exp(sink - m)
            o = jnp.dot(p.astype(BF16), vh, preferred_element_type=F32) * (1.0 / den)
            y_ref[0, :, h * SWA_HD:(h + 1) * SWA_HD] = o.astype(y_ref.dtype)


def _swa_prompt(zt3, bias, sinks, l):
    nb, t, _ = zt3.shape
    w = WINDOW
    nblk = pl.cdiv(t, w)
    kern = functools.partial(_swa_prompt_kernel, t=t, l=l)
    kcol = TCOL_SK // SWA_KV
    vcol = TCOL_SV // SWA_KV
    prev = lambda i: jnp.maximum(i - 1, 0)
    return pl.pallas_call(
        kern,
        grid=(nb, nblk),
        in_specs=[pl.BlockSpec((1, w, SWA_Q), lambda n, i: (n, i, TCOL_SQ // SWA_Q)),
                  pl.BlockSpec((1, w, SWA_KV), lambda n, i: (n, prev(i), kcol)),
                  pl.BlockSpec((1, w, SWA_KV), lambda n, i: (n, i, kcol)),
                  pl.BlockSpec((1, w, SWA_KV), lambda n, i: (n, prev(i), vcol)),
                  pl.BlockSpec((1, w, SWA_KV), lambda n, i: (n, i, vcol)),
                  pl.BlockSpec((None, SWA_QH, w, 2 * w), lambda n, i: (jnp.minimum(i, 1), 0, 0, 0)),
                  pl.BlockSpec(memory_space=pltpu.SMEM)],
        out_specs=pl.BlockSpec((1, w, SWA_Q), lambda n, i: (n, i, 0)),
        out_shape=jax.ShapeDtypeStruct((nb, t, SWA_Q), BF16),
        compiler_params=_cparams(("parallel", "arbitrary")),
        name="swa_prompt",
    )(zt3, zt3, zt3, zt3, zt3, bias, sinks)


def _swa_decode_kernel(q_ref, kv_ref, kb_ref, vb_ref, bias_ref, sink_ref, y_ref, ko_ref, vo_ref, *, nstep):
    n0 = pl.program_id(0) * nstep
    scale = SWA_HD ** -0.5
    w = WINDOW
    ri = lax.broadcasted_iota(jnp.int32, (SWA_HD, SWA_HD), 0)
    ci = lax.broadcasted_iota(jnp.int32, (SWA_HD, SWA_HD), 1)
    eye = (ri == ci).astype(F32)
    last = lax.broadcasted_iota(jnp.int32, (SWA_HD, w), 1) == w - 1

    def push(cache_t, new_row):
        col = jnp.sum(eye * new_row, axis=1, keepdims=True)
        return jnp.where(last, col, pltpu.roll(cache_t, w - 1, axis=1))

    for j in range(nstep):
        qrow = q_ref[pl.ds(n0 + j, 1), :]
        kvrow = kv_ref[pl.ds(n0 + j, 1), :]
        for kv in range(SWA_KVH):
            hs = slice(kv * SWA_HD, (kv + 1) * SWA_HD)
            kt = push(kb_ref[j, kv], kvrow[:, hs])
            vt = push(vb_ref[j, kv], kvrow[:, SWA_KV + kv * SWA_HD:SWA_KV + (kv + 1) * SWA_HD])
            ko_ref[j, kv] = kt
            vo_ref[j, kv] = vt
            q4 = jnp.concatenate(
                [qrow[:, (kv * SWA_GRP + g) * SWA_HD:(kv * SWA_GRP + g + 1) * SWA_HD]
                 for g in range(SWA_GRP)], axis=0).astype(BF16)
            s = jnp.dot(q4, kt.astype(BF16), preferred_element_type=F32) * scale + bias_ref[kv]
            sink = sink_ref[kv * SWA_GRP:(kv + 1) * SWA_GRP, :]
            m = jnp.maximum(jnp.max(s, axis=-1, keepdims=True), sink)
            p = jnp.exp(s - m)
            den = jnp.sum(p, axis=-1, keepdims=True) + jnp.exp(sink - m)
            o = lax.dot_general((p / den).astype(BF16), vt.astype(BF16), (((1,), (1,)), ((), ())),
                                preferred_element_type=F32)
            for g in range(SWA_GRP):
                h = kv * SWA_GRP + g
                y_ref[j, :, h * SWA_HD:(h + 1) * SWA_HD] = o[g:g + 1, :].astype(y_ref.dtype)


def _swa_decode(zt, kbuf_t, vbuf_t, bias_dec, sinks, l, nstep=8):
    nb = zt.shape[0]
    w = WINDOW
    assert nb % nstep == 0
    kern = functools.partial(_swa_decode_kernel, nstep=nstep)
    cspec = lambda: _lspec(l, (nstep, SWA_KVH, SWA_HD, w), lambda n: (n, 0, 0, 0))
    ospec = lambda: pl.BlockSpec((nstep, SWA_KVH, SWA_HD, w), lambda n: (n, 0, 0, 0))
    return pl.pallas_call(
        kern,
        grid=(nb // nstep,),
        in_specs=[pl.BlockSpec((nb, SWA_Q), lambda n: (0, TCOL_SQ // SWA_Q)),
                  pl.BlockSpec((nb, 2 * SWA_KV), lambda n: (0, TCOL_SK // (2 * SWA_KV))),
                  cspec(), cspec(),
                  pl.BlockSpec((SWA_KVH, SWA_GRP, w), lambda n: (0, 0, 0)),
                  _lspec(l, (SWA_QH, 1), lambda n: (0, 0))],
        out_specs=[pl.BlockSpec((nstep, 1, SWA_Q), lambda n: (n, 0, 0)), ospec(), ospec()],
        out_shape=[jax.ShapeDtypeStruct((nb, 1, SWA_Q), BF16),
                   jax.ShapeDtypeStruct(kbuf_t.shape[1:], F32), jax.ShapeDtypeStruct(vbuf_t.shape[1:], F32)],
        compiler_params=_cparams(("arbitrary",)),
        name="swa_decode",
    )(zt, zt, kbuf_t, vbuf_t, bias_dec, sinks)


def _seq_chunk(t):
    for c in (688, 512, 256, 128, 64, 32, 16):
        if t % c == 0:
            return c
    raise ValueError(f"sequence length {t} must be a multiple of {GLA_CHUNK}")


def kernel(x_prompt, x_sample, state_s5_re, state_s5_im, state_gla, cache_swa_k, cache_swa_v, meta_tokens, rel_bias, norm_pre_mix, norm_post_mix, norm_pre_ffn, norm_post_ffn, w_in, s5_lam_re, s5_lam_im, s5_log_step, s5_b_re, s5_b_im, s5_c_re, s5_c_im, s5_d, s5_w_glu, s5_b_glu, gla_w_gate2, gla_b_gate2, gla_g_out, swa_sinks, w_up_s5, w_up_gla, w_up_swa, w_out, w_ff1, w_ff2):
    depth = w_in.shape[0]
    nb, seq, d = x_prompt.shape
    t = seq + N_META
    nd = x_sample.shape[0]
    assert x_sample.shape[1] == 1 and cache_swa_k.shape[2] == WINDOW and t % GLA_CHUNK == 0
    tc = _seq_chunk(t)

    w_in_t = jnp.swapaxes(w_in.astype(BF16), 1, 2)
    w_glu = s5_w_glu.astype(BF16)
    wg2 = gla_w_gate2.astype(BF16)
    wu_a, wu_b, wu_c = w_up_s5.astype(BF16), w_up_gla.astype(BF16), w_up_swa.astype(BF16)
    wo, w1, w2 = w_out.astype(BF16), w_ff1, w_ff2
    row3 = lambda a: a.reshape(depth, 1, a.shape[-1])
    g_pre, g_post, g_pre_f, g_post_f = (row3(a) for a in (norm_pre_mix, norm_post_mix,
                                                           norm_pre_ffn, norm_post_ffn))
    b_glu, d5, bg2, go = row3(s5_b_glu), row3(s5_d), row3(gla_b_gate2), row3(gla_g_out)
    sinks3 = swa_sinks.reshape(depth, SWA_QH, 1)
    s5r = state_s5_re.reshape(depth, nd, -1)
    s5i = state_s5_im.reshape(depth, nd, -1)
    ckb = jnp.transpose(cache_swa_k, (0, 1, 3, 4, 2))
    cvb = jnp.transpose(cache_swa_v, (0, 1, 3, 4, 2))

    ar, ai, bbr_t, bbi_t = _s5_discretize(s5_lam_re, s5_lam_im, s5_log_step,
                                          jnp.swapaxes(s5_b_re, -1, -2), jnp.swapaxes(s5_b_im, -1, -2))
    bmat, cmat = _s5_matrices(bbr_t, bbi_t, s5_c_re, s5_c_im)
    a_half = jnp.concatenate([ar.reshape(depth, S5_NGB, 2, S5_HW), ai.reshape(depth, S5_NGB, 2, S5_HW)],
                             axis=-1)
    arow = jnp.tile(a_half, (1, 1, SUBLANES // 2, 1))
    adec = jnp.stack([ar.reshape(depth, -1), ai.reshape(depth, -1)], axis=1)

    bias = _bias_table(rel_bias)
    bias_dec = bias[1, :, WINDOW - 1, WINDOW:].reshape(SWA_KVH, SWA_GRP, WINDOW)

    meta = jnp.broadcast_to(meta_tokens.astype(x_prompt.dtype)[None], (nb, N_META, d))
    xp = jnp.concatenate([meta, x_prompt], axis=1).reshape(nb * t, d)
    xs = x_sample.reshape(nd, d)

    outs = {k: [] for k in ('p_re', 'p_im', 'p_gla', 'p_k', 'p_v', 's_re', 's_im', 's_gla', 's_k', 's_v')}
    for l in range(depth):
        zh, zhd, za, zad = _norm_mm(xp, xs, g_pre, w_in_t, l, 0, HEAD_COLS, 1536, extra_col0=COL_GA)
        zt, ztd = _norm_mm(xp, xs, g_pre, w_in_t, l, TAIL_START, MIX_COLS, 1280)
        zg, zgd = _norm_mm(xp, xs, g_pre, w_in_t, l, GATE_START, GATE_COLS, 2048, gate=True)
        zh3 = zh.reshape(nb, t, HEAD_COLS)
        za3 = za.reshape(nb, t, LANES)
        zt3 = zt.reshape(nb, t, MIX_COLS)
        z5, st5 = _s5_prompt(zh3, bmat, cmat, arow, d5, l, tc)
        z5d, nr, ni = _s5_decode(zhd, bmat, cmat, adec, d5, s5r, s5i, l)
        yb, pg = _gla_prompt(zh3, za3, zt3, wg2, bg2, go, l, tc)
        ybd, sg = _gla_decode(zhd, zad, ztd, wg2, bg2, go, state_gla, l)
        yc = _swa_prompt(zt3, bias, swa_sinks, l)
        ycd, sk, sv = _swa_decode(ztd, ckb, cvb, bias_dec, sinks3, l)
        xp, xs = _merge_out((z5.reshape(nb * t, S5_WIDTH), yb.reshape(nb * t, GLA_V),
                             yc.reshape(nb * t, SWA_Q)),
                            (z5d, ybd.reshape(nd, GLA_V), ycd.reshape(nd, SWA_Q)),
                            zg, zgd, w_glu, b_glu, wu_a, wu_b, wu_c, wo, xp, xs, g_post, l)
        xp, xs = _ffn(xp, xs, g_pre_f, w1, w2, g_post_f, l)
        st5 = st5.reshape(S5_NGB, nb, 2, 2, S5_HALF, S5_STATE).transpose(3, 1, 0, 2, 4, 5)
        st5 = st5.reshape(2, nb, S5_GROUPS, S5_STATE)
        outs['p_re'].append(st5[0])
        outs['p_im'].append(st5[1])
        outs['p_gla'].append(pg)
        outs['p_k'].append(zt3[:, t - WINDOW:, TCOL_SK:TCOL_SK + SWA_KV].reshape(nb, WINDOW, SWA_KVH, SWA_HD))
        outs['p_v'].append(zt3[:, t - WINDOW:, TCOL_SV:TCOL_SV + SWA_KV].reshape(nb, WINDOW, SWA_KVH, SWA_HD))
        outs['s_re'].append(nr.reshape(nd, S5_GROUPS, S5_STATE))
        outs['s_im'].append(ni.reshape(nd, S5_GROUPS, S5_STATE))
        outs['s_gla'].append(sg)
        outs['s_k'].append(jnp.transpose(sk, (0, 3, 1, 2)))
        outs['s_v'].append(jnp.transpose(sv, (0, 3, 1, 2)))

    y_prompt = xp.reshape(nb, t, d)[:, N_META:]
    y_sample = xs.reshape(nd, 1, d)
    st = {k: jnp.stack(v) for k, v in outs.items()}
    return (y_prompt, y_sample, st['p_re'], st['p_im'], st['p_gla'], st['p_k'], st['p_v'],
            st['s_re'], st['s_im'], st['s_gla'], st['s_k'], st['s_v'])
```

```python
import functools
import math

import numpy as np
import jax
import jax.numpy as jnp
from jax import lax
from jax.experimental import pallas as pl
from jax.experimental.pallas import tpu as pltpu

F32 = jnp.float32
BF16 = jnp.bfloat16

D_MODEL = 2048
N_META = 16
S5_GROUP = 16
S5_WIDTH = D_MODEL // 2
S5_GROUPS = S5_WIDTH // S5_GROUP
S5_STATE = 64
GLA_HEADS = 4
GLA_DK = D_MODEL // 16
GLA_DV = D_MODEL // 8
GLA_RANK = 16
GLA_TAU = 16.0
GLA_CHUNK = 16
SWA_QH = 16
SWA_KVH = 4
SWA_GRP = SWA_QH // SWA_KVH
SWA_HD = 64
WINDOW = 128
N_BUCKETS = 32
MAX_DISTANCE = WINDOW
D_FF = 4 * D_MODEL
N_BRANCH = 3
EPS = 1e-6

GLA_QK = GLA_HEADS * GLA_DK
GLA_V = GLA_HEADS * GLA_DV
SWA_Q = SWA_QH * SWA_HD
SWA_KV = SWA_KVH * SWA_HD

COL_U = 0
COL_GQ = COL_U + S5_WIDTH
COL_GK = COL_GQ + GLA_QK
COL_GV = COL_GK + GLA_QK
COL_GA = COL_GV + GLA_V
LANES = 128
SUBLANES = 8
HEAD_COLS = COL_GA
TAIL_START = COL_GA + GLA_RANK
TCOL_R = 0
TCOL_SQ = TCOL_R + GLA_V
TCOL_SK = TCOL_SQ + SWA_Q
TCOL_SV = TCOL_SK + SWA_KV
TCOL_GATE = TCOL_SV + SWA_KV
MIX_COLS = TCOL_GATE
GATE_START = TAIL_START + TCOL_GATE
GATE_COLS = N_BRANCH * D_MODEL

S5_GB = 8
S5_NGB = S5_GROUPS // S5_GB
S5_HALF = S5_GB // 2
S5_HW = S5_HALF * S5_STATE

VMEM_LIMIT = 60 * 1024 * 1024


def _cparams(sem):
    return pltpu.CompilerParams(dimension_semantics=sem, vmem_limit_bytes=VMEM_LIMIT)


def _rms(x, g):
    ms = jnp.mean(x * x, axis=-1, keepdims=True)
    return (x * lax.rsqrt(ms + EPS)) * g


def _row_tile(m):
    for t in (688, 512, 256, 128, 64, 32, 16, 8):
        if m % t == 0:
            return t
    return m


def _lspec(l, shape, imap):
    return pl.BlockSpec((None,) + tuple(shape), lambda *idx: (l,) + tuple(imap(*idx)))


class _Tiles:
    def __init__(self, m, n_inner):
        self.tm = _row_tile(m)
        self.npt = m // self.tm
        self.grid = (self.npt, n_inner)

    def dstep(self, i, j):
        return jnp.where(i == self.npt - 1, j, 0)


def _with_decode(npt, prompt_fn, decode_fn):
    prompt_fn()
    pl.when(pl.program_id(0) == npt - 1)(decode_fn)


def _nt_dot(h, wt):
    return lax.dot_general(h, wt, (((1,), (1,)), ((), ())), preferred_element_type=F32)


def _norm_mm_kernel(*refs, npt, gate, extra):
    if extra:
        xp_ref, xd_ref, g_ref, w_ref, we_ref, op_ref, od_ref, ep_ref, ed_ref, hp_ref, hd_ref = refs
    else:
        xp_ref, xd_ref, g_ref, w_ref, op_ref, od_ref, hp_ref, hd_ref = refs
        we_ref = ep_ref = ed_ref = None

    def run(x_ref, o_ref, e_ref, h_ref):
        @pl.when(pl.program_id(1) == 0)
        def _():
            h_ref[...] = _rms(x_ref[...], g_ref[...]).astype(BF16)
            if extra:
                e_ref[...] = _nt_dot(h_ref[...], we_ref[0])

        z = _nt_dot(h_ref[...], w_ref[0])
        o_ref[...] = (jax.nn.sigmoid(z) if gate else z).astype(o_ref.dtype)

    _with_decode(npt, lambda: run(xp_ref, op_ref, ep_ref, hp_ref),
                 lambda: run(xd_ref, od_ref, ed_ref, hd_ref))


def _norm_mm(xp, xd, g, wt, l, col0, n_cols, tn, gate=False, extra_col0=None):
    m, k = xp.shape
    nd = xd.shape[0]
    tl = _Tiles(m, n_cols // tn)
    extra = extra_col0 is not None
    odt = BF16 if gate else F32
    kern = functools.partial(_norm_mm_kernel, npt=tl.npt, gate=gate, extra=extra)
    wspec = lambda width, off: pl.BlockSpec(
        (pl.Element(1), pl.Element(width), pl.Element(k)),
        lambda i, j: (l, pl.multiple_of(off(j), 16), 0))
    in_specs = [pl.BlockSpec((tl.tm, k), lambda i, j: (i, 0)),
                pl.BlockSpec((nd, k), lambda i, j: (0, 0)),
                _lspec(l, (1, k), lambda i, j: (0, 0)),
                wspec(tn, lambda j: col0 + j * tn)]
    out_specs = [pl.BlockSpec((tl.tm, tn), lambda i, j: (i, j)),
                 pl.BlockSpec((nd, tn), lambda i, j: (0, tl.dstep(i, j)))]
    out_shape = [jax.ShapeDtypeStruct((m, n_cols), odt), jax.ShapeDtypeStruct((nd, n_cols), odt)]
    args = [xp, xd, g, wt]
    if extra:
        in_specs.append(wspec(LANES, lambda j: extra_col0 + 0 * j))
        out_specs += [pl.BlockSpec((tl.tm, LANES), lambda i, j: (i, 0)),
                      pl.BlockSpec((nd, LANES), lambda i, j: (0, 0))]
        out_shape += [jax.ShapeDtypeStruct((m, LANES), F32), jax.ShapeDtypeStruct((nd, LANES), F32)]
        args.append(wt)
    return pl.pallas_call(
        kern,
        grid=tl.grid,
        in_specs=in_specs,
        out_specs=out_specs,
        out_shape=out_shape,
        scratch_shapes=[pltpu.VMEM((tl.tm, k), BF16), pltpu.VMEM((nd, k), BF16)],
        compiler_params=_cparams(("arbitrary", "arbitrary")),
        name="norm_mm",
    )(*args)


def _merge_out_kernel(zp_ref, ybp_ref, ycp_ref, zd_ref, ybd_ref, ycd_ref,
                      gap_ref, gbp_ref, gcp_ref, gad_ref, gbd_ref, gcd_ref,
                      wglu_ref, bglu_ref, wa_ref, wb_ref, wc_ref, wo_ref, xp_ref, xd_ref, g_ref,
                      op_ref, od_ref, mp_ref, md_ref, yap_ref, yad_ref, *, npt, nk, tk, tn):
    s = pl.program_id(1)

    def run(z_ref, ya_ref, yb_ref, yc_ref, ga_ref, gb_ref, gc_ref, x_ref, o_ref, m_ref):
        @pl.when(s == 0)
        def _():
            z = z_ref[...]
            a = jnp.dot(z.astype(BF16), wglu_ref[...], preferred_element_type=F32) + bglu_ref[...]
            ya_ref[...] = (z * jax.nn.sigmoid(a)).astype(BF16)

        @pl.when(s < nk)
        def _():
            m = (ga_ref[...].astype(F32) * jnp.dot(ya_ref[...], wa_ref[...], preferred_element_type=F32)
                 + gb_ref[...].astype(F32) * jnp.dot(yb_ref[...], wb_ref[...], preferred_element_type=F32)
                 + gc_ref[...].astype(F32) * jnp.dot(yc_ref[...], wc_ref[...], preferred_element_type=F32))
            m_ref[:, pl.ds(pl.multiple_of(s * tk, tk), tk)] = m.astype(BF16)

        @pl.when(s >= nk)
        def _():
            c0 = pl.multiple_of((s - nk) * tn, tn)
            o_ref[:, pl.ds(c0, tn)] = jnp.dot(m_ref[...], wo_ref[...], preferred_element_type=F32)

        @pl.when(s == pl.num_programs(1) - 1)
        def _():
            o_ref[...] = x_ref[...] + _rms(o_ref[...], g_ref[...])

    _with_decode(
        npt,
        lambda: run(zp_ref, yap_ref, ybp_ref, ycp_ref, gap_ref, gbp_ref, gcp_ref, xp_ref, op_ref, mp_ref),
        lambda: run(zd_ref, yad_ref, ybd_ref, ycd_ref, gad_ref, gbd_ref, gcd_ref, xd_ref, od_ref, md_ref))


def _merge_out(yp, yd, ztp, ztd, wglu, bglu, wa, wb, wc, wo, xp, xd, g, l, tk=512, tn=512):
    m, d = xp.shape
    nd = xd.shape[0]
    kb = yp[0].shape[1]
    gate0 = 0
    nk = d // tk
    tl = _Tiles(m, nk + d // tn)
    kidx = lambda s: jnp.minimum(s, nk - 1)
    nidx = lambda s: jnp.maximum(s - nk, 0)
    gp_spec = lambda br: pl.BlockSpec((tl.tm, tk), lambda i, s: (i, gate0 + br * nk + kidx(s)))
    gd_spec = lambda br: pl.BlockSpec(
        (nd, tk), lambda i, s: (0, gate0 + br * nk + kidx(tl.dstep(i, s))))
    yp_spec = pl.BlockSpec((tl.tm, kb), lambda i, s: (i, 0))
    yd_spec = pl.BlockSpec((nd, kb), lambda i, s: (0, 0))
    w_spec = _lspec(l, (kb, tk), lambda i, s: (0, kidx(s)))
    kern = functools.partial(_merge_out_kernel, npt=tl.npt, nk=nk, tk=tk, tn=tn)
    return pl.pallas_call(
        kern,
        grid=tl.grid,
        in_specs=[yp_spec, yp_spec, yp_spec, yd_spec, yd_spec, yd_spec,
                  gp_spec(0), gp_spec(1), gp_spec(2), gd_spec(0), gd_spec(1), gd_spec(2),
                  _lspec(l, (kb, kb), lambda i, s: (0, 0)),
                  _lspec(l, (1, kb), lambda i, s: (0, 0)),
                  w_spec, w_spec, w_spec,
                  _lspec(l, (d, tn), lambda i, s: (0, nidx(s))),
                  pl.BlockSpec((tl.tm, d), lambda i, s: (i, 0)),
                  pl.BlockSpec((nd, d), lambda i, s: (0, 0)),
                  _lspec(l, (1, d), lambda i, s: (0, 0))],
        out_specs=[pl.BlockSpec((tl.tm, d), lambda i, s: (i, 0)),
                   pl.BlockSpec((nd, d), lambda i, s: (0, 0))],
        out_shape=[jax.ShapeDtypeStruct((m, d), F32), jax.ShapeDtypeStruct((nd, d), F32)],
        scratch_shapes=[pltpu.VMEM((tl.tm, d), BF16), pltpu.VMEM((nd, d), BF16),
                        pltpu.VMEM((tl.tm, kb), BF16), pltpu.VMEM((nd, kb), BF16)],
        compiler_params=_cparams(("arbitrary", "arbitrary")),
        name="merge_out",
    )(*yp, *yd, ztp, ztp, ztp, ztd, ztd, ztd, wglu, bglu, wa, wb, wc, wo, xp, xd, g)


def _ffn_kernel(xp_ref, xd_ref, g1_ref, w1_ref, w2_ref, g2_ref, op_ref, od_ref,
                hp_ref, accp_ref, hd_ref, accd_ref, *, npt):
    f = pl.program_id(1)

    def run(x_ref, o_ref, h_ref, acc_ref):
        @pl.when(f == 0)
        def _():
            h_ref[...] = _rms(x_ref[...], g1_ref[...]).astype(BF16)
            acc_ref[...] = jnp.zeros_like(acc_ref)

        a = jnp.dot(h_ref[...], w1_ref[...], preferred_element_type=F32)
        a = jnp.square(jnp.maximum(a, 0.0)).astype(BF16)
        acc_ref[...] += jnp.dot(a, w2_ref[...], preferred_element_type=F32)

        @pl.when(f == pl.num_programs(1) - 1)
        def _():
            o_ref[...] = x_ref[...] + _rms(acc_ref[...], g2_ref[...])

    _with_decode(npt, lambda: run(xp_ref, op_ref, hp_ref, accp_ref),
                 lambda: run(xd_ref, od_ref, hd_ref, accd_ref))


def _ffn(xp, xd, g1, w1, w2, g2, l, tf=1024):
    m, d = xp.shape
    nd = xd.shape[0]
    ff = w1.shape[2]
    tl = _Tiles(m, ff // tf)
    kern = functools.partial(_ffn_kernel, npt=tl.npt)
    return pl.pallas_call(
        kern,
        grid=tl.grid,
        in_specs=[pl.BlockSpec((tl.tm, d), lambda i, f: (i, 0)),
                  pl.BlockSpec((nd, d), lambda i, f: (0, 0)),
                  _lspec(l, (1, d), lambda i, f: (0, 0)),
                  _lspec(l, (d, tf), lambda i, f: (0, f)),
                  _lspec(l, (tf, d), lambda i, f: (f, 0)),
                  _lspec(l, (1, d), lambda i, f: (0, 0))],
        out_specs=[pl.BlockSpec((tl.tm, d), lambda i, f: (i, 0)),
                   pl.BlockSpec((nd, d), lambda i, f: (0, 0))],
        out_shape=[jax.ShapeDtypeStruct((m, d), F32), jax.ShapeDtypeStruct((nd, d), F32)],
        scratch_shapes=[pltpu.VMEM((tl.tm, d), BF16), pltpu.VMEM((tl.tm, d), F32),
                        pltpu.VMEM((nd, d), BF16), pltpu.VMEM((nd, d), F32)],
        compiler_params=_cparams(("arbitrary", "arbitrary")),
        name="ffn",
    )(xp, xd, g1, w1, w2, g2)


def _s5_disc_kernel(lr_ref, li_ref, ls_ref, br_ref, bi_ref, ar_ref, ai_ref, bbr_ref, bbi_ref):
    lr = lr_ref[0]
    li = li_ref[0]
    dt = jnp.exp(ls_ref[0])
    mag = jnp.exp(lr * dt)
    ar = mag * jnp.cos(li * dt)
    ai = mag * jnp.sin(li * dt)
    den = lr * lr + li * li
    fr = ((ar - 1.0) * lr + ai * li) / den
    fi = (ai * lr - (ar - 1.0) * li) / den
    ar_ref[0] = ar
    ai_ref[0] = ai
    br = br_ref[0]
    bi = bi_ref[0]
    bbr_ref[0] = fr[:, None, :] * br - fi[:, None, :] * bi
    bbi_ref[0] = fr[:, None, :] * bi + fi[:, None, :] * br


def _s5_discretize(lam_re, lam_im, log_step, b_re_t, b_im_t):
    depth = lam_re.shape[0]
    gp = pl.BlockSpec((1, S5_GROUPS, S5_STATE), lambda l: (l, 0, 0))
    bspec = pl.BlockSpec((1, S5_GROUPS, S5_GROUP, S5_STATE), lambda l: (l, 0, 0, 0))
    return pl.pallas_call(
        _s5_disc_kernel,
        grid=(depth,),
        in_specs=[gp, gp, pl.BlockSpec((1, S5_GROUPS, 1), lambda l: (l, 0, 0)), bspec, bspec],
        out_specs=[gp, gp, bspec, bspec],
        out_shape=[jax.ShapeDtypeStruct(lam_re.shape, F32), jax.ShapeDtypeStruct(lam_re.shape, F32),
                   jax.ShapeDtypeStruct(b_re_t.shape, F32), jax.ShapeDtypeStruct(b_re_t.shape, F32)],
        compiler_params=_cparams(("parallel",)),
        name="s5_discretize",
    )(lam_re, lam_im, log_step.reshape(depth, S5_GROUPS, 1), b_re_t, b_im_t)


def _blockdiag(r):
    eye = jnp.eye(S5_HALF, dtype=r.dtype)
    z = jnp.einsum('ab,lgqbhp->lgqahbp', eye, r)
    s = r.shape
    return z.reshape(s[0], s[1], s[2], S5_HALF * s[4], S5_HALF * s[5])


def _s5_matrices(bbr_t, bbi_t, c_re, c_im):
    depth = bbr_t.shape[0]
    shp = (depth, S5_NGB, 2, S5_HALF, S5_GROUP, S5_STATE)
    bre = _blockdiag(bbr_t.reshape(shp))
    bim = _blockdiag(bbi_t.reshape(shp))
    bcat = jnp.concatenate([bre, bim], axis=-1)
    zb = jnp.zeros_like(bcat[:, :, 0])
    bmat = jnp.stack([jnp.concatenate([bcat[:, :, 0], zb], axis=-2),
                      jnp.concatenate([zb, bcat[:, :, 1]], axis=-2)], axis=2)
    cre = jnp.swapaxes(_blockdiag(c_re.reshape(shp)), -1, -2)
    cim = jnp.swapaxes(_blockdiag(c_im.reshape(shp)), -1, -2)
    ccat = jnp.concatenate([cre, -cim], axis=-2)
    zc = jnp.zeros_like(ccat[:, :, 0])
    cmat = jnp.stack([jnp.concatenate([ccat[:, :, 0], zc], axis=-1),
                      jnp.concatenate([zc, ccat[:, :, 1]], axis=-1)], axis=2)
    return bmat.astype(BF16), cmat.astype(BF16)


def _s5_prompt_kernel(u_ref, b_ref, c_ref, a_ref, d_ref, z_ref, st_ref, e_ref, x_ref, *, nb, tc, ngb):
    @pl.when(pl.program_id(1) == 0)
    def _():
        x_ref[...] = jnp.zeros_like(x_ref)

    nlb = 2 * S5_HW // LANES
    hlb = nlb // 2
    for b in range(ngb):
        for n in range(nb):
            ub = u_ref[n, :, b * LANES:(b + 1) * LANES].astype(BF16)
            for q in range(2):
                e = jnp.dot(ub, b_ref[b, q], preferred_element_type=F32)
                for j in range(nlb):
                    e_ref[b * nlb + j, pl.ds(n * 2 + q, tc, stride=SUBLANES), :] = (
                        e[:, j * LANES:(j + 1) * LANES])

    ar, ai = [], []
    for b in range(ngb):
        a = a_ref[b]
        ar += [a[:, j * LANES:(j + 1) * LANES] for j in range(hlb)]
        ai += [a[:, (hlb + j) * LANES:(hlb + j + 1) * LANES] for j in range(hlb)]
    slabs = [(b * nlb + j, b * nlb + hlb + j) for b in range(ngb) for j in range(hlb)]

    def step(t, carry):
        row = pl.multiple_of(t * SUBLANES, SUBLANES)
        new = []
        for c, (sr, si) in enumerate(slabs):
            xr, xi = carry[2 * c], carry[2 * c + 1]
            nxr = ar[c] * xr - ai[c] * xi + e_ref[sr, pl.ds(row, SUBLANES), :]
            nxi = ar[c] * xi + ai[c] * xr + e_ref[si, pl.ds(row, SUBLANES), :]
            e_ref[sr, pl.ds(row, SUBLANES), :] = nxr
            e_ref[si, pl.ds(row, SUBLANES), :] = nxi
            new += [nxr, nxi]
        return tuple(new)

    init = []
    for b in range(ngb):
        x0 = x_ref[b]
        for j in range(hlb):
            init += [x0[:, j * LANES:(j + 1) * LANES], x0[:, (hlb + j) * LANES:(hlb + j + 1) * LANES]]
    fin = lax.fori_loop(0, tc, step, tuple(init), unroll=8)
    for b in range(ngb):
        for j in range(hlb):
            c = b * hlb + j
            x_ref[b, :, j * LANES:(j + 1) * LANES] = fin[2 * c]
            x_ref[b, :, (hlb + j) * LANES:(hlb + j + 1) * LANES] = fin[2 * c + 1]
    st_ref[...] = x_ref[...]

    for b in range(ngb):
        for n in range(nb):
            y = d_ref[:, b * LANES:(b + 1) * LANES] * u_ref[n, :, b * LANES:(b + 1) * LANES]
            for q in range(2):
                xs = jnp.concatenate(
                    [e_ref[b * nlb + j, pl.ds(n * 2 + q, tc, stride=SUBLANES), :] for j in range(nlb)],
                    axis=1)
                y = y + jnp.dot(xs.astype(BF16), c_ref[b, q], preferred_element_type=F32)
            z_ref[n, :, b * LANES:(b + 1) * LANES] = jax.nn.gelu(y)


def _s5_prompt(zh3, bmat, cmat, arow, d, l, tc):
    nb, t, _ = zh3.shape
    assert 2 * nb == SUBLANES, "state rows are (batch, half) pairs filling one sublane tile"
    ngb = 2
    kern = functools.partial(_s5_prompt_kernel, nb=nb, tc=tc, ngb=ngb)
    return pl.pallas_call(
        kern,
        grid=(S5_NGB // ngb, t // tc),
        in_specs=[pl.BlockSpec((nb, tc, ngb * LANES), lambda g, c: (0, c, g)),
                  _lspec(l, (ngb, 2, LANES, 2 * S5_HW), lambda g, c: (g, 0, 0, 0)),
                  _lspec(l, (ngb, 2, 2 * S5_HW, LANES), lambda g, c: (g, 0, 0, 0)),
                  _lspec(l, (ngb, SUBLANES, 2 * S5_HW), lambda g, c: (g, 0, 0)),
                  _lspec(l, (1, ngb * LANES), lambda g, c: (0, g))],
        out_specs=[pl.BlockSpec((nb, tc, ngb * LANES), lambda g, c: (0, c, g)),
                   pl.BlockSpec((ngb, SUBLANES, 2 * S5_HW), lambda g, c: (g, 0, 0))],
        out_shape=[jax.ShapeDtypeStruct((nb, t, S5_WIDTH), F32),
                   jax.ShapeDtypeStruct((S5_NGB, SUBLANES, 2 * S5_HW), F32)],
        scratch_shapes=[pltpu.VMEM((ngb * 2 * S5_HW // LANES, tc * SUBLANES, LANES), F32),
                        pltpu.VMEM((ngb, SUBLANES, 2 * S5_HW), F32)],
        compiler_params=_cparams(("parallel", "arbitrary")),
        name="s5_prompt",
    )(zh3, bmat, cmat, arow, d)


def _s5_decode_kernel(u_ref, b_ref, c_ref, a_ref, d_ref, sr_ref, si_ref, z_ref, nr_ref, ni_ref):
    for g in range(S5_NGB):
        u = u_ref[:, g * LANES:(g + 1) * LANES]
        ub = u.astype(BF16)
        y = d_ref[:, g * LANES:(g + 1) * LANES] * u
        for q in range(2):
            c0 = (g * 2 + q) * S5_HW
            e = jnp.dot(ub, b_ref[g, q], preferred_element_type=F32)
            ar = a_ref[0:1, c0:c0 + S5_HW]
            ai = a_ref[1:2, c0:c0 + S5_HW]
            x0r = sr_ref[:, c0:c0 + S5_HW]
            x0i = si_ref[:, c0:c0 + S5_HW]
            nr = ar * x0r - ai * x0i + e[:, :S5_HW]
            ni = ar * x0i + ai * x0r + e[:, S5_HW:]
            nr_ref[:, c0:c0 + S5_HW] = nr
            ni_ref[:, c0:c0 + S5_HW] = ni
            xs = jnp.concatenate([nr, ni], axis=1).astype(BF16)
            y = y + jnp.dot(xs, c_ref[g, q], preferred_element_type=F32)
        z_ref[:, g * LANES:(g + 1) * LANES] = jax.nn.gelu(y)


def _s5_decode(zh, bmat, cmat, adec, d, sr, si, l):
    nb = zh.shape[0]
    ns = S5_GROUPS * S5_STATE
    full = lambda shape: pl.BlockSpec(shape, lambda i: (0,) * len(shape))
    lfull = lambda shape: _lspec(l, shape, lambda i: (0,) * len(shape))
    return pl.pallas_call(
        _s5_decode_kernel,
        grid=(1,),
        in_specs=[full((nb, S5_WIDTH)), lfull(bmat.shape[1:]), lfull(cmat.shape[1:]), lfull((2, ns)),
                  lfull((1, S5_WIDTH)), lfull((nb, ns)), lfull((nb, ns))],
        out_specs=[full((nb, S5_WIDTH)), full((nb, ns)), full((nb, ns))],
        out_shape=[jax.ShapeDtypeStruct((nb, S5_WIDTH), F32),
                   jax.ShapeDtypeStruct((nb, ns), F32), jax.ShapeDtypeStruct((nb, ns), F32)],
        compiler_params=_cparams(("arbitrary",)),
        name="s5_decode",
    )(zh, bmat, cmat, adec, d, sr, si)


def _gla_finish(o, r, g):
    return _rms(o, g) * (r * jax.nn.sigmoid(r))


def _gla_prompt_kernel(q_ref, k_ref, v_ref, a_ref, r_ref, wg_ref, bg_ref, go_ref, y_ref, st_ref,
                       s_ref, lg_ref, o_ref, qi_ref, ku_ref, cd_ref, *, tc):
    c = pl.program_id(1)

    @pl.when(c == 0)
    def _():
        s_ref[...] = jnp.zeros_like(s_ref)

    a16 = a_ref[0][:, :GLA_RANK].astype(BF16)
    lg_ref[...] = jax.nn.log_sigmoid(
        jnp.dot(a16, wg_ref[...], preferred_element_type=F32) + bg_ref[...]) / GLA_TAU

    ch = GLA_CHUNK
    hs = SUBLANES
    rowi = lax.broadcasted_iota(jnp.int32, (ch, GLA_DK), 0)
    rowh = lax.broadcasted_iota(jnp.int32, (hs, GLA_DK), 0)
    coli = lax.broadcasted_iota(jnp.int32, (hs, ch), 1)
    scale = GLA_DK ** -0.5

    def prep(idx, carry):
        r0 = pl.multiple_of(idx * ch, ch)
        for h in range(GLA_HEADS):
            ks = slice(h * GLA_DK, (h + 1) * GLA_DK)
            vs = slice(h * GLA_DV, (h + 1) * GLA_DV)
            b = lg_ref[pl.ds(r0, ch), ks]
            for sh in (1, 2, 4, 8):
                b = b + jnp.where(rowi >= sh, pltpu.roll(b, sh, axis=0), 0.0)
            qh = q_ref[0, pl.ds(r0, ch), ks] * scale
            kh = k_ref[0, pl.ds(r0, ch), ks]
            halves = []
            for half in range(ch // hs):
                bq = b[half * hs:(half + 1) * hs]
                qq = qh[half * hs:(half + 1) * hs]
                att = jnp.zeros((hs, ch), F32)
                for s in range((half + 1) * hs):
                    diff = bq - b[s:s + 1, :]
                    if s >= half * hs:
                        diff = jnp.where(rowh + half * hs >= s, diff, -jnp.inf)
                    col = jnp.sum(qq * kh[s:s + 1, :] * jnp.exp(diff), axis=1, keepdims=True)
                    att = jnp.where(coli == s, col, att)
                halves.append(att)
            att = jnp.concatenate(halves, axis=0).astype(BF16)
            vh = v_ref[0, pl.ds(r0, ch), vs].astype(BF16)
            o_ref[pl.ds(r0, ch), vs] = jnp.dot(att, vh, preferred_element_type=F32)
            blast = b[ch - 1:ch, :]
            qi_ref[pl.ds(r0, ch), ks] = (qh * jnp.exp(b)).astype(BF16)
            ku_ref[pl.ds(r0, ch), ks] = (kh * jnp.exp(blast - b)).astype(BF16)
            cd_ref[pl.ds(pl.multiple_of(idx * hs, hs), hs), ks] = jnp.broadcast_to(
                jnp.exp(blast), (hs, GLA_DK))
        return carry

    lax.fori_loop(0, tc // ch, prep, 0, unroll=2)

    def scan(idx, carry):
        r0 = pl.multiple_of(idx * ch, ch)
        for h in range(GLA_HEADS):
            ks = slice(h * GLA_DK, (h + 1) * GLA_DK)
            vs = slice(h * GLA_DV, (h + 1) * GLA_DV)
            st = s_ref[h]
            inter = lax.dot_general(qi_ref[pl.ds(r0, ch), ks], st.astype(BF16),
                                    (((1,), (1,)), ((), ())), preferred_element_type=F32)
            o_ref[pl.ds(r0, ch), vs] += inter
            vh = v_ref[0, pl.ds(r0, ch), vs].astype(BF16)
            kv = lax.dot_general(vh, ku_ref[pl.ds(r0, ch), ks], (((0,), (0,)), ((), ())),
                                 preferred_element_type=F32)
            cd = cd_ref[pl.ds(pl.multiple_of(idx * hs, hs), 1), ks]
            s_ref[h] = cd * st + kv
        return carry

    lax.fori_loop(0, tc // ch, scan, 0, unroll=4)

    for h in range(GLA_HEADS):
        vs = slice(h * GLA_DV, (h + 1) * GLA_DV)
        y_ref[0, :, vs] = _gla_finish(o_ref[:, vs], r_ref[0, :, vs], go_ref[...]).astype(y_ref.dtype)
        st_ref[0, h] = s_ref[h].T


def _gla_prompt(zh3, za3, zt3, wg, bg, go, l, tc):
    nb, t, _ = zh3.shape
    kern = functools.partial(_gla_prompt_kernel, tc=tc)
    return pl.pallas_call(
        kern,
        grid=(nb, t // tc),
        in_specs=[pl.BlockSpec((1, tc, GLA_QK), lambda n, c: (n, c, COL_GQ // GLA_QK)),
                  pl.BlockSpec((1, tc, GLA_QK), lambda n, c: (n, c, COL_GK // GLA_QK)),
                  pl.BlockSpec((1, tc, GLA_V), lambda n, c: (n, c, COL_GV // GLA_V)),
                  pl.BlockSpec((1, tc, LANES), lambda n, c: (n, c, 0)),
                  pl.BlockSpec((1, tc, GLA_V), lambda n, c: (n, c, TCOL_R // GLA_V)),
                  _lspec(l, (GLA_RANK, GLA_QK), lambda n, c: (0, 0)),
                  _lspec(l, (1, GLA_QK), lambda n, c: (0, 0)),
                  _lspec(l, (1, GLA_DV), lambda n, c: (0, 0))],
        out_specs=[pl.BlockSpec((1, tc, GLA_V), lambda n, c: (n, c, 0)),
                   pl.BlockSpec((1, GLA_HEADS, GLA_DK, GLA_DV), lambda n, c: (n, 0, 0, 0))],
        out_shape=[jax.ShapeDtypeStruct((nb, t, GLA_V), BF16),
                   jax.ShapeDtypeStruct((nb, GLA_HEADS, GLA_DK, GLA_DV), F32)],
        scratch_shapes=[pltpu.VMEM((GLA_HEADS, GLA_DV, GLA_DK), F32),
                        pltpu.VMEM((tc, GLA_QK), F32),
                        pltpu.VMEM((tc, GLA_V), F32),
                        pltpu.VMEM((tc, GLA_QK), BF16),
                        pltpu.VMEM((tc, GLA_QK), BF16),
                        pltpu.VMEM((tc // GLA_CHUNK * SUBLANES, GLA_QK), F32)],
        compiler_params=_cparams(("parallel", "arbitrary")),
        name="gla_prompt",
    )(zh3, zh3, zh3, za3, zt3, wg, bg, go)


def _gla_decode_kernel(zh_ref, za_ref, r_ref, wg_ref, bg_ref, go_ref, s_ref, y_ref, so_ref, *, nstep):
    n0 = pl.multiple_of(pl.program_id(0) * nstep, nstep)
    rows = zh_ref[pl.ds(n0, nstep), :]
    a16 = za_ref[pl.ds(n0, nstep), :GLA_RANK].astype(BF16)
    lg = jax.nn.log_sigmoid(
        jnp.dot(a16, wg_ref[...], preferred_element_type=F32) + bg_ref[...]) / GLA_TAU
    eg = jnp.exp(lg)
    rr = r_ref[pl.ds(n0, nstep), :]
    ri = lax.broadcasted_iota(jnp.int32, (GLA_DK, GLA_DK), 0)
    ci = lax.broadcasted_iota(jnp.int32, (GLA_DK, GLA_DK), 1)
    eye = (ri == ci).astype(F32)

    def col(v):
        return jnp.sum(eye * v, axis=1, keepdims=True)

    for j in range(nstep):
        row = rows[j:j + 1]
        for h in range(GLA_HEADS):
            ks = slice(h * GLA_DK, (h + 1) * GLA_DK)
            qh = row[:, COL_GQ + h * GLA_DK:COL_GQ + (h + 1) * GLA_DK] * (GLA_DK ** -0.5)
            kh = row[:, COL_GK + h * GLA_DK:COL_GK + (h + 1) * GLA_DK]
            vh = row[:, COL_GV + h * GLA_DV:COL_GV + (h + 1) * GLA_DV]
            s = col(eg[j:j + 1, ks]) * s_ref[j, h] + col(kh) * vh
            so_ref[j, h] = s
            o = jnp.sum(col(qh) * s, axis=0, keepdims=True)
            vs = slice(h * GLA_DV, (h + 1) * GLA_DV)
            y_ref[j, :, vs] = _gla_finish(o, rr[j:j + 1, vs], go_ref[...]).astype(y_ref.dtype)


def _gla_decode(zh, za, zt, wg, bg, go, s0, l, nstep=SUBLANES):
    nb = zh.shape[0]
    assert nb % nstep == 0
    kern = functools.partial(_gla_decode_kernel, nstep=nstep)
    sblock = (nstep, GLA_HEADS, GLA_DK, GLA_DV)
    return pl.pallas_call(
        kern,
        grid=(nb // nstep,),
        in_specs=[pl.BlockSpec((nb, HEAD_COLS), lambda n: (0, 0)),
                  pl.BlockSpec((nb, LANES), lambda n: (0, 0)),
                  pl.BlockSpec((nb, GLA_V), lambda n: (0, TCOL_R // GLA_V)),
                  _lspec(l, (GLA_RANK, GLA_QK), lambda n: (0, 0)),
                  _lspec(l, (1, GLA_QK), lambda n: (0, 0)),
                  _lspec(l, (1, GLA_DV), lambda n: (0, 0)),
                  _lspec(l, sblock, lambda n: (n, 0, 0, 0))],
        out_specs=[pl.BlockSpec((nstep, 1, GLA_V), lambda n: (n, 0, 0)),
                   pl.BlockSpec(sblock, lambda n: (n, 0, 0, 0))],
        out_shape=[jax.ShapeDtypeStruct((nb, 1, GLA_V), BF16),
                   jax.ShapeDtypeStruct(s0.shape[1:], F32)],
        compiler_params=_cparams(("arbitrary",)),
        name="gla_decode",
    )(zh, za, zt, wg, bg, go, s0)


def _bucket_table():
    dist = (np.arange(WINDOW)[:, None] + WINDOW) - np.arange(2 * WINDOW)[None, :]
    max_exact = N_BUCKETS // 2
    d = np.maximum(dist, 0)
    ratio = np.log(np.maximum(d, 1).astype(np.float32) / np.float32(max_exact)) / np.float32(
        math.log(MAX_DISTANCE / max_exact))
    large = max_exact + (ratio.astype(np.float32) * np.float32(N_BUCKETS - max_exact)).astype(np.int32)
    large = np.minimum(large, N_BUCKETS - 1)
    return np.where(d < max_exact, d, large).astype(np.int32)


def _bias_kernel(bk_ref, rb_ref, o_ref):
    h = pl.program_id(1)
    bk = bk_ref[...]
    acc = jnp.zeros(bk.shape, F32)
    for b in range(N_BUCKETS):
        acc = jnp.where(bk == b, rb_ref[b, h], acc)
    cols = lax.broadcasted_iota(jnp.int32, bk.shape, 1)
    dist = WINDOW + lax.broadcasted_iota(jnp.int32, bk.shape, 0) - cols
    band = (dist >= 0) & (dist < WINDOW)
    keep = band & ((cols >= WINDOW) | (pl.program_id(0) == 1))
    o_ref[0, 0] = jnp.where(keep, acc, -jnp.inf)


def _bias_table(rel_bias):
    bk = jnp.asarray(_bucket_table())
    return pl.pallas_call(
        _bias_kernel,
        grid=(2, SWA_QH),
        in_specs=[pl.BlockSpec((WINDOW, 2 * WINDOW), lambda v, h: (0, 0)),
                  pl.BlockSpec(memory_space=pltpu.SMEM)],
        out_specs=pl.BlockSpec((1, 1, WINDOW, 2 * WINDOW), lambda v, h: (v, h, 0, 0)),
        out_shape=jax.ShapeDtypeStruct((2, SWA_QH, WINDOW, 2 * WINDOW), F32),
        compiler_params=_cparams(("parallel", "parallel")),
        name="swa_bias",
    )(bk, rel_bias)


def _swa_prompt_kernel(q_ref, kp_ref, kc_ref, vp_ref, vc_ref, bias_ref, sink_ref, y_ref, *, t, l):
    i = pl.program_id(1)
    w = WINDOW
    inseq = i * w + lax.broadcasted_iota(jnp.int32, (w, 1), 0) < t
    scale = SWA_HD ** -0.5
    for kv in range(SWA_KVH):
        hs = slice(kv * SWA_HD, (kv + 1) * SWA_HD)
        kh = jnp.concatenate([kp_ref[0, :, hs], jnp.where(inseq, kc_ref[0, :, hs], 0.0)],
                             axis=0).astype(BF16)
        vh = jnp.concatenate([vp_ref[0, :, hs], jnp.where(inseq, vc_ref[0, :, hs], 0.0)],
                             axis=0).astype(BF16)
        for g in range(SWA_GRP):
            h = kv * SWA_GRP + g
            qh = (q_ref[0, :, h * SWA_HD:(h + 1) * SWA_HD] * scale).astype(BF16)
            s = lax.dot_general(qh, kh, (((1,), (1,)), ((), ())), preferred_element_type=F32)
            s = s + bias_ref[h]
            sink = sink_ref[l, h]
            m = jnp.maximum(jnp.max(s, axis=-1, keepdims=True), sink)
            p = jnp.exp(s - m)
            den = jnp.sum(p, axis=-1, keepdims=True) + jnp.exp(sink - m)
            o = jnp.dot(p.astype(BF16), vh, preferred_element_type=F32) * (1.0 / den)
            y_ref[0, :, h * SWA_HD:(h + 1) * SWA_HD] = o.astype(y_ref.dtype)


def _swa_prompt(zt3, bias, sinks, l):
    nb, t, _ = zt3.shape
    w = WINDOW
    nblk = pl.cdiv(t, w)
    kern = functools.partial(_swa_prompt_kernel, t=t, l=l)
    kcol = TCOL_SK // SWA_KV
    vcol = TCOL_SV // SWA_KV
    prev = lambda i: jnp.maximum(i - 1, 0)
    return pl.pallas_call(
        kern,
        grid=(nb, nblk),
        in_specs=[pl.BlockSpec((1, w, SWA_Q), lambda n, i: (n, i, TCOL_SQ // SWA_Q)),
                  pl.BlockSpec((1, w, SWA_KV), lambda n, i: (n, prev(i), kcol)),
                  pl.BlockSpec((1, w, SWA_KV), lambda n, i: (n, i, kcol)),
                  pl.BlockSpec((1, w, SWA_KV), lambda n, i: (n, prev(i), vcol)),
                  pl.BlockSpec((1, w, SWA_KV), lambda n, i: (n, i, vcol)),
                  pl.BlockSpec((None, SWA_QH, w, 2 * w), lambda n, i: (jnp.minimum(i, 1), 0, 0, 0)),
                  pl.BlockSpec(memory_space=pltpu.SMEM)],
        out_specs=pl.BlockSpec((1, w, SWA_Q), lambda n, i: (n, i, 0)),
        out_shape=jax.ShapeDtypeStruct((nb, t, SWA_Q), BF16),
        compiler_params=_cparams(("parallel", "arbitrary")),
        name="swa_prompt",
    )(zt3, zt3, zt3, zt3, zt3, bias, sinks)


def _swa_decode_kernel(q_ref, kv_ref, kb_ref, vb_ref, bias_ref, sink_ref, y_ref, ko_ref, vo_ref, *, nstep):
    n0 = pl.program_id(0) * nstep
    scale = SWA_HD ** -0.5
    w = WINDOW
    ri = lax.broadcasted_iota(jnp.int32, (SWA_HD, SWA_HD), 0)
    ci = lax.broadcasted_iota(jnp.int32, (SWA_HD, SWA_HD), 1)
    eye = (ri == ci).astype(F32)
    last = lax.broadcasted_iota(jnp.int32, (SWA_HD, w), 1) == w - 1

    def push(cache_t, new_row):
        col = jnp.sum(eye * new_row, axis=1, keepdims=True)
        return jnp.where(last, col, pltpu.roll(cache_t, w - 1, axis=1))

    scores = []
    for j in range(nstep):
        qrow = q_ref[pl.ds(n0 + j, 1), :]
        kvrow = kv_ref[pl.ds(n0 + j, 1), :]
        for kv in range(SWA_KVH):
            hs = slice(kv * SWA_HD, (kv + 1) * SWA_HD)
            kt = push(kb_ref[j, kv], kvrow[:, hs])
            ko_ref[j, kv] = kt
            vo_ref[j, kv] = push(vb_ref[j, kv],
                                 kvrow[:, SWA_KV + kv * SWA_HD:SWA_KV + (kv + 1) * SWA_HD])
            q4 = jnp.concatenate(
                [qrow[:, (kv * SWA_GRP + g) * SWA_HD:(kv * SWA_GRP + g + 1) * SWA_HD]
                 for g in range(SWA_GRP)], axis=0).astype(BF16)
            scores.append(jnp.dot(q4, kt.astype(BF16), preferred_element_type=F32))
    s = jnp.concatenate(scores, axis=0) * scale
    s = s + jnp.concatenate([bias_ref[...].reshape(SWA_QH, w)] * nstep, axis=0)
    sink = jnp.concatenate([sink_ref[...]] * nstep, axis=0)
    m = jnp.maximum(jnp.max(s, axis=-1, keepdims=True), sink)
    p = jnp.exp(s - m)
    den = jnp.sum(p, axis=-1, keepdims=True) + jnp.exp(sink - m)
    pn = (p / den).astype(BF16)
    for j in range(nstep):
        for kv in range(SWA_KVH):
            r0 = (j * SWA_KVH + kv) * SWA_GRP
            o = lax.dot_general(pn[r0:r0 + SWA_GRP], vo_ref[j, kv].astype(BF16),
                                (((1,), (1,)), ((), ())), preferred_element_type=F32)
            for g in range(SWA_GRP):
                h = kv * SWA_GRP + g
                y_ref[j, :, h * SWA_HD:(h + 1) * SWA_HD] = o[g:g + 1, :].astype(y_ref.dtype)


def _swa_decode(zt, kbuf_t, vbuf_t, bias_dec, sinks, l, nstep=8):
    nb = zt.shape[0]
    w = WINDOW
    assert nb % nstep == 0
    kern = functools.partial(_swa_decode_kernel, nstep=nstep)
    cspec = lambda: _lspec(l, (nstep, SWA_KVH, SWA_HD, w), lambda n: (n, 0, 0, 0))
    ospec = lambda: pl.BlockSpec((nstep, SWA_KVH, SWA_HD, w), lambda n: (n, 0, 0, 0))
    return pl.pallas_call(
        kern,
        grid=(nb // nstep,),
        in_specs=[pl.BlockSpec((nb, SWA_Q), lambda n: (0, TCOL_SQ // SWA_Q)),
                  pl.BlockSpec((nb, 2 * SWA_KV), lambda n: (0, TCOL_SK // (2 * SWA_KV))),
                  cspec(), cspec(),
                  pl.BlockSpec((SWA_KVH, SWA_GRP, w), lambda n: (0, 0, 0)),
                  _lspec(l, (SWA_QH, 1), lambda n: (0, 0))],
        out_specs=[pl.BlockSpec((nstep, 1, SWA_Q), lambda n: (n, 0, 0)), ospec(), ospec()],
        out_shape=[jax.ShapeDtypeStruct((nb, 1, SWA_Q), BF16),
                   jax.ShapeDtypeStruct(kbuf_t.shape[1:], F32), jax.ShapeDtypeStruct(vbuf_t.shape[1:], F32)],
        compiler_params=_cparams(("arbitrary",)),
        name="swa_decode",
    )(zt, zt, kbuf_t, vbuf_t, bias_dec, sinks)


def _seq_chunk(t):
    for c in (688, 512, 256, 128, 64, 32, 16):
        if t % c == 0:
            return c
    raise ValueError(f"sequence length {t} must be a multiple of {GLA_CHUNK}")


def kernel(x_prompt, x_sample, state_s5_re, state_s5_im, state_gla, cache_swa_k, cache_swa_v, meta_tokens, rel_bias, norm_pre_mix, norm_post_mix, norm_pre_ffn, norm_post_ffn, w_in, s5_lam_re, s5_lam_im, s5_log_step, s5_b_re, s5_b_im, s5_c_re, s5_c_im, s5_d, s5_w_glu, s5_b_glu, gla_w_gate2, gla_b_gate2, gla_g_out, swa_sinks, w_up_s5, w_up_gla, w_up_swa, w_out, w_ff1, w_ff2):
    depth = w_in.shape[0]
    nb, seq, d = x_prompt.shape
    t = seq + N_META
    nd = x_sample.shape[0]
    assert x_sample.shape[1] == 1 and cache_swa_k.shape[2] == WINDOW and t % GLA_CHUNK == 0
    tc = _seq_chunk(t)

    w_in_t = jnp.swapaxes(w_in.astype(BF16), 1, 2)
    w_glu = s5_w_glu.astype(BF16)
    wg2 = gla_w_gate2.astype(BF16)
    wu_a, wu_b, wu_c = w_up_s5.astype(BF16), w_up_gla.astype(BF16), w_up_swa.astype(BF16)
    wo, w1, w2 = w_out.astype(BF16), w_ff1.astype(BF16), w_ff2.astype(BF16)
    row3 = lambda a: a.reshape(depth, 1, a.shape[-1])
    g_pre, g_post, g_pre_f, g_post_f = (row3(a) for a in (norm_pre_mix, norm_post_mix,
                                                           norm_pre_ffn, norm_post_ffn))
    b_glu, d5, bg2, go = row3(s5_b_glu), row3(s5_d), row3(gla_b_gate2), row3(gla_g_out)
    sinks3 = swa_sinks.reshape(depth, SWA_QH, 1)
    s5r = state_s5_re.reshape(depth, nd, -1)
    s5i = state_s5_im.reshape(depth, nd, -1)
    ckb = jnp.transpose(cache_swa_k, (0, 1, 3, 4, 2))
    cvb = jnp.transpose(cache_swa_v, (0, 1, 3, 4, 2))

    ar, ai, bbr_t, bbi_t = _s5_discretize(s5_lam_re, s5_lam_im, s5_log_step,
                                          jnp.swapaxes(s5_b_re, -1, -2), jnp.swapaxes(s5_b_im, -1, -2))
    bmat, cmat = _s5_matrices(bbr_t, bbi_t, s5_c_re, s5_c_im)
    a_half = jnp.concatenate([ar.reshape(depth, S5_NGB, 2, S5_HW), ai.reshape(depth, S5_NGB, 2, S5_HW)],
                             axis=-1)
    arow = jnp.tile(a_half, (1, 1, SUBLANES // 2, 1))
    adec = jnp.stack([ar.reshape(depth, -1), ai.reshape(depth, -1)], axis=1)

    bias = _bias_table(rel_bias)
    bias_dec = bias[1, :, WINDOW - 1, WINDOW:].reshape(SWA_KVH, SWA_GRP, WINDOW)

    meta = jnp.broadcast_to(meta_tokens.astype(x_prompt.dtype)[None], (nb, N_META, d))
    xp = jnp.concatenate([meta, x_prompt], axis=1).reshape(nb * t, d)
    xs = x_sample.reshape(nd, d)

    outs = {k: [] for k in ('p_re', 'p_im', 'p_gla', 'p_k', 'p_v', 's_re', 's_im', 's_gla', 's_k', 's_v')}
    for l in range(depth):
        zh, zhd, za, zad = _norm_mm(xp, xs, g_pre, w_in_t, l, 0, HEAD_COLS, 1536, extra_col0=COL_GA)
        zt, ztd = _norm_mm(xp, xs, g_pre, w_in_t, l, TAIL_START, MIX_COLS, 1280)
        zg, zgd = _norm_mm(xp, xs, g_pre, w_in_t, l, GATE_START, GATE_COLS, 2048, gate=True)
        zh3 = zh.reshape(nb, t, HEAD_COLS)
        za3 = za.reshape(nb, t, LANES)
        zt3 = zt.reshape(nb, t, MIX_COLS)
        z5, st5 = _s5_prompt(zh3, bmat, cmat, arow, d5, l, tc)
        z5d, nr, ni = _s5_decode(zhd, bmat, cmat, adec, d5, s5r, s5i, l)
        yb, pg = _gla_prompt(zh3, za3, zt3, wg2, bg2, go, l, tc)
        ybd, sg = _gla_decode(zhd, zad, ztd, wg2, bg2, go, state_gla, l)
        yc = _swa_prompt(zt3, bias, swa_sinks, l)
        ycd, sk, sv = _swa_decode(ztd, ckb, cvb, bias_dec, sinks3, l)
        xp, xs = _merge_out((z5.reshape(nb * t, S5_WIDTH), yb.reshape(nb * t, GLA_V),
                             yc.reshape(nb * t, SWA_Q)),
                            (z5d, ybd.reshape(nd, GLA_V), ycd.reshape(nd, SWA_Q)),
                            zg, zgd, w_glu, b_glu, wu_a, wu_b, wu_c, wo, xp, xs, g_post, l)
        xp, xs = _ffn(xp, xs, g_pre_f, w1, w2, g_post_f, l)
        st5 = st5.reshape(S5_NGB, nb, 2, 2, S5_HALF, S5_STATE).transpose(3, 1, 0, 2, 4, 5)
        st5 = st5.reshape(2, nb, S5_GROUPS, S5_STATE)
        outs['p_re'].append(st5[0])
        outs['p_im'].append(st5[1])
        outs['p_gla'].append(pg)
        outs['p_k'].append(zt3[:, t - WINDOW:, TCOL_SK:TCOL_SK + SWA_KV].reshape(nb, WINDOW, SWA_KVH, SWA_HD))
        outs['p_v'].append(zt3[:, t - WINDOW:, TCOL_SV:TCOL_SV + SWA_KV].reshape(nb, WINDOW, SWA_KVH, SWA_HD))
        outs['s_re'].append(nr.reshape(nd, S5_GROUPS, S5_STATE))
        outs['s_im'].append(ni.reshape(nd, S5_GROUPS, S5_STATE))
        outs['s_gla'].append(sg)
        outs['s_k'].append(jnp.transpose(sk, (0, 3, 1, 2)))
        outs['s_v'].append(jnp.transpose(sv, (0, 3, 1, 2)))

    y_prompt = xp.reshape(nb, t, d)[:, N_META:]
    y_sample = xs.reshape(nd, 1, d)
    st = {k: jnp.stack(v) for k, v in outs.items()}
    return (y_prompt, y_sample, st['p_re'], st['p_im'], st['p_gla'], st['p_k'], st['p_v'],
            st['s_re'], st['s_im'], st['s_gla'], st['s_k'], st['s_v'])
```

```python
import functools
import math

import numpy as np
import jax
import jax.numpy as jnp
from jax import lax
from jax.experimental import pallas as pl
from jax.experimental.pallas import tpu as pltpu

F32 = jnp.float32
BF16 = jnp.bfloat16

D_MODEL = 2048
N_META = 16
S5_GROUP = 16
S5_WIDTH = D_MODEL // 2
S5_GROUPS = S5_WIDTH // S5_GROUP
S5_STATE = 64
GLA_HEADS = 4
GLA_DK = D_MODEL // 16
GLA_DV = D_MODEL // 8
GLA_RANK = 16
GLA_TAU = 16.0
GLA_CHUNK = 16
SWA_QH = 16
SWA_KVH = 4
SWA_GRP = SWA_QH // SWA_KVH
SWA_HD = 64
WINDOW = 128
N_BUCKETS = 32
MAX_DISTANCE = WINDOW
D_FF = 4 * D_MODEL
N_BRANCH = 3
EPS = 1e-6

GLA_QK = GLA_HEADS * GLA_DK
GLA_V = GLA_HEADS * GLA_DV
SWA_Q = SWA_QH * SWA_HD
SWA_KV = SWA_KVH * SWA_HD

COL_U = 0
COL_GQ = COL_U + S5_WIDTH
COL_GK = COL_GQ + GLA_QK
COL_GV = COL_GK + GLA_QK
COL_GA = COL_GV + GLA_V
LANES = 128
SUBLANES = 8
HEAD_COLS = COL_GA
TAIL_START = COL_GA + GLA_RANK
TCOL_R = 0
TCOL_SQ = TCOL_R + GLA_V
TCOL_SK = TCOL_SQ + SWA_Q
TCOL_SV = TCOL_SK + SWA_KV
TCOL_GATE = TCOL_SV + SWA_KV
MIX_COLS = TCOL_GATE
GATE_START = TAIL_START + TCOL_GATE
GATE_COLS = N_BRANCH * D_MODEL

S5_GB = 8
S5_NGB = S5_GROUPS // S5_GB
S5_HALF = S5_GB // 2
S5_HW = S5_HALF * S5_STATE

VMEM_LIMIT = 60 * 1024 * 1024


def _cparams(sem):
    return pltpu.CompilerParams(dimension_semantics=sem, vmem_limit_bytes=VMEM_LIMIT)


def _rms(x, g):
    ms = jnp.mean(x * x, axis=-1, keepdims=True)
    return (x * lax.rsqrt(ms + EPS)) * g


def _row_tile(m):
    for t in (688, 512, 256, 128, 64, 32, 16, 8):
        if m % t == 0:
            return t
    return m


def _lspec(l, shape, imap):
    return pl.BlockSpec((None,) + tuple(shape), lambda *idx: (l,) + tuple(imap(*idx)))


class _Tiles:
    def __init__(self, m, n_inner):
        self.tm = _row_tile(m)
        self.npt = m // self.tm
        self.grid = (self.npt, n_inner)

    def dstep(self, i, j):
        return jnp.where(i == self.npt - 1, j, 0)


def _with_decode(npt, prompt_fn, decode_fn):
    prompt_fn()
    pl.when(pl.program_id(0) == npt - 1)(decode_fn)


def _nt_dot(h, wt):
    return lax.dot_general(h, wt, (((1,), (1,)), ((), ())), preferred_element_type=F32)


def _norm_mm_kernel(*refs, npt, gate, extra):
    if extra:
        xp_ref, xd_ref, g_ref, w_ref, we_ref, op_ref, od_ref, ep_ref, ed_ref, hp_ref, hd_ref = refs
    else:
        xp_ref, xd_ref, g_ref, w_ref, op_ref, od_ref, hp_ref, hd_ref = refs
        we_ref = ep_ref = ed_ref = None

    def run(x_ref, o_ref, e_ref, h_ref):
        @pl.when(pl.program_id(1) == 0)
        def _():
            h_ref[...] = _rms(x_ref[...], g_ref[...]).astype(BF16)
            if extra:
                e_ref[...] = _nt_dot(h_ref[...], we_ref[0])

        z = _nt_dot(h_ref[...], w_ref[0])
        o_ref[...] = (jax.nn.sigmoid(z) if gate else z).astype(o_ref.dtype)

    _with_decode(npt, lambda: run(xp_ref, op_ref, ep_ref, hp_ref),
                 lambda: run(xd_ref, od_ref, ed_ref, hd_ref))


def _norm_mm(xp, xd, g, wt, l, col0, n_cols, tn, gate=False, extra_col0=None):
    m, k = xp.shape
    nd = xd.shape[0]
    tl = _Tiles(m, n_cols // tn)
    extra = extra_col0 is not None
    odt = BF16 if gate else F32
    kern = functools.partial(_norm_mm_kernel, npt=tl.npt, gate=gate, extra=extra)
    wspec = lambda width, off: pl.BlockSpec(
        (pl.Element(1), pl.Element(width), pl.Element(k)),
        lambda i, j: (l, pl.multiple_of(off(j), 16), 0))
    in_specs = [pl.BlockSpec((tl.tm, k), lambda i, j: (i, 0)),
                pl.BlockSpec((nd, k), lambda i, j: (0, 0)),
                _lspec(l, (1, k), lambda i, j: (0, 0)),
                wspec(tn, lambda j: col0 + j * tn)]
    out_specs = [pl.BlockSpec((tl.tm, tn), lambda i, j: (i, j)),
                 pl.BlockSpec((nd, tn), lambda i, j: (0, tl.dstep(i, j)))]
    out_shape = [jax.ShapeDtypeStruct((m, n_cols), odt), jax.ShapeDtypeStruct((nd, n_cols), odt)]
    args = [xp, xd, g, wt]
    if extra:
        in_specs.append(wspec(LANES, lambda j: extra_col0 + 0 * j))
        out_specs += [pl.BlockSpec((tl.tm, LANES), lambda i, j: (i, 0)),
                      pl.BlockSpec((nd, LANES), lambda i, j: (0, 0))]
        out_shape += [jax.ShapeDtypeStruct((m, LANES), F32), jax.ShapeDtypeStruct((nd, LANES), F32)]
        args.append(wt)
    return pl.pallas_call(
        kern,
        grid=tl.grid,
        in_specs=in_specs,
        out_specs=out_specs,
        out_shape=out_shape,
        scratch_shapes=[pltpu.VMEM((tl.tm, k), BF16), pltpu.VMEM((nd, k), BF16)],
        compiler_params=_cparams(("arbitrary", "arbitrary")),
        name="norm_mm",
    )(*args)


def _merge_out_kernel(zp_ref, ybp_ref, ycp_ref, zd_ref, ybd_ref, ycd_ref,
                      gap_ref, gbp_ref, gcp_ref, gad_ref, gbd_ref, gcd_ref,
                      wglu_ref, bglu_ref, wa_ref, wb_ref, wc_ref, wo_ref, xp_ref, xd_ref, g_ref,
                      op_ref, od_ref, mp_ref, md_ref, yap_ref, yad_ref, *, npt, nk, tk, tn):
    s = pl.program_id(1)

    def run(z_ref, ya_ref, yb_ref, yc_ref, ga_ref, gb_ref, gc_ref, x_ref, o_ref, m_ref):
        @pl.when(s == 0)
        def _():
            z = z_ref[...]
            a = jnp.dot(z.astype(BF16), wglu_ref[...], preferred_element_type=F32) + bglu_ref[...]
            ya_ref[...] = (z * jax.nn.sigmoid(a)).astype(BF16)

        @pl.when(s < nk)
        def _():
            m = (ga_ref[...].astype(F32) * jnp.dot(ya_ref[...], wa_ref[...], preferred_element_type=F32)
                 + gb_ref[...].astype(F32) * jnp.dot(yb_ref[...], wb_ref[...], preferred_element_type=F32)
                 + gc_ref[...].astype(F32) * jnp.dot(yc_ref[...], wc_ref[...], preferred_element_type=F32))
            m_ref[:, pl.ds(pl.multiple_of(s * tk, tk), tk)] = m.astype(BF16)

        @pl.when(s >= nk)
        def _():
            c0 = pl.multiple_of((s - nk) * tn, tn)
            o_ref[:, pl.ds(c0, tn)] = jnp.dot(m_ref[...], wo_ref[...], preferred_element_type=F32)

        @pl.when(s == pl.num_programs(1) - 1)
        def _():
            o_ref[...] = x_ref[...] + _rms(o_ref[...], g_ref[...])

    _with_decode(
        npt,
        lambda: run(zp_ref, yap_ref, ybp_ref, ycp_ref, gap_ref, gbp_ref, gcp_ref, xp_ref, op_ref, mp_ref),
        lambda: run(zd_ref, yad_ref, ybd_ref, ycd_ref, gad_ref, gbd_ref, gcd_ref, xd_ref, od_ref, md_ref))


def _merge_out(yp, yd, ztp, ztd, wglu, bglu, wa, wb, wc, wo, xp, xd, g, l, tk=512, tn=512):
    m, d = xp.shape
    nd = xd.shape[0]
    kb = yp[0].shape[1]
    gate0 = 0
    nk = d // tk
    tl = _Tiles(m, nk + d // tn)
    kidx = lambda s: jnp.minimum(s, nk - 1)
    nidx = lambda s: jnp.maximum(s - nk, 0)
    gp_spec = lambda br: pl.BlockSpec((tl.tm, tk), lambda i, s: (i, gate0 + br * nk + kidx(s)))
    gd_spec = lambda br: pl.BlockSpec(
        (nd, tk), lambda i, s: (0, gate0 + br * nk + kidx(tl.dstep(i, s))))
    yp_spec = pl.BlockSpec((tl.tm, kb), lambda i, s: (i, 0))
    yd_spec = pl.BlockSpec((nd, kb), lambda i, s: (0, 0))
    w_spec = _lspec(l, (kb, tk), lambda i, s: (0, kidx(s)))
    kern = functools.partial(_merge_out_kernel, npt=tl.npt, nk=nk, tk=tk, tn=tn)
    return pl.pallas_call(
        kern,
        grid=tl.grid,
        in_specs=[yp_spec, yp_spec, yp_spec, yd_spec, yd_spec, yd_spec,
                  gp_spec(0), gp_spec(1), gp_spec(2), gd_spec(0), gd_spec(1), gd_spec(2),
                  _lspec(l, (kb, kb), lambda i, s: (0, 0)),
                  _lspec(l, (1, kb), lambda i, s: (0, 0)),
                  w_spec, w_spec, w_spec,
                  _lspec(l, (d, tn), lambda i, s: (0, nidx(s))),
                  pl.BlockSpec((tl.tm, d), lambda i, s: (i, 0)),
                  pl.BlockSpec((nd, d), lambda i, s: (0, 0)),
                  _lspec(l, (1, d), lambda i, s: (0, 0))],
        out_specs=[pl.BlockSpec((tl.tm, d), lambda i, s: (i, 0)),
                   pl.BlockSpec((nd, d), lambda i, s: (0, 0))],
        out_shape=[jax.ShapeDtypeStruct((m, d), F32), jax.ShapeDtypeStruct((nd, d), F32)],
        scratch_shapes=[pltpu.VMEM((tl.tm, d), BF16), pltpu.VMEM((nd, d), BF16),
                        pltpu.VMEM((tl.tm, kb), BF16), pltpu.VMEM((nd, kb), BF16)],
        compiler_params=_cparams(("arbitrary", "arbitrary")),
        name="merge_out",
    )(*yp, *yd, ztp, ztp, ztp, ztd, ztd, ztd, wglu, bglu, wa, wb, wc, wo, xp, xd, g)


def _ffn_kernel(xp_ref, xd_ref, g1_ref, w1_ref, w2_ref, g2_ref, op_ref, od_ref,
                hp_ref, accp_ref, hd_ref, accd_ref, *, npt):
    f = pl.program_id(1)

    def run(x_ref, o_ref, h_ref, acc_ref):
        @pl.when(f == 0)
        def _():
            h_ref[...] = _rms(x_ref[...], g1_ref[...]).astype(BF16)
            acc_ref[...] = jnp.zeros_like(acc_ref)

        a = jnp.dot(h_ref[...], w1_ref[...], preferred_element_type=F32)
        a = jnp.square(jnp.maximum(a, 0.0)).astype(BF16)
        acc_ref[...] += jnp.dot(a, w2_ref[...], preferred_element_type=F32)

        @pl.when(f == pl.num_programs(1) - 1)
        def _():
            o_ref[...] = x_ref[...] + _rms(acc_ref[...], g2_ref[...])

    _with_decode(npt, lambda: run(xp_ref, op_ref, hp_ref, accp_ref),
                 lambda: run(xd_ref, od_ref, hd_ref, accd_ref))


def _ffn(xp, xd, g1, w1, w2, g2, l, tf=1024):
    m, d = xp.shape
    nd = xd.shape[0]
    ff = w1.shape[2]
    tl = _Tiles(m, ff // tf)
    kern = functools.partial(_ffn_kernel, npt=tl.npt)
    return pl.pallas_call(
        kern,
        grid=tl.grid,
        in_specs=[pl.BlockSpec((tl.tm, d), lambda i, f: (i, 0)),
                  pl.BlockSpec((nd, d), lambda i, f: (0, 0)),
                  _lspec(l, (1, d), lambda i, f: (0, 0)),
                  _lspec(l, (d, tf), lambda i, f: (0, f)),
                  _lspec(l, (tf, d), lambda i, f: (f, 0)),
                  _lspec(l, (1, d), lambda i, f: (0, 0))],
        out_specs=[pl.BlockSpec((tl.tm, d), lambda i, f: (i, 0)),
                   pl.BlockSpec((nd, d), lambda i, f: (0, 0))],
        out_shape=[jax.ShapeDtypeStruct((m, d), F32), jax.ShapeDtypeStruct((nd, d), F32)],
        scratch_shapes=[pltpu.VMEM((tl.tm, d), BF16), pltpu.VMEM((tl.tm, d), F32),
                        pltpu.VMEM((nd, d), BF16), pltpu.VMEM((nd, d), F32)],
        compiler_params=_cparams(("arbitrary", "arbitrary")),
        name="ffn",
    )(xp, xd, g1, w1, w2, g2)


def _s5_disc_kernel(lr_ref, li_ref, ls_ref, br_ref, bi_ref, ar_ref, ai_ref, bbr_ref, bbi_ref):
    lr = lr_ref[0]
    li = li_ref[0]
    dt = jnp.exp(ls_ref[0])
    mag = jnp.exp(lr * dt)
    ar = mag * jnp.cos(li * dt)
    ai = mag * jnp.sin(li * dt)
    den = lr * lr + li * li
    fr = ((ar - 1.0) * lr + ai * li) / den
    fi = (ai * lr - (ar - 1.0) * li) / den
    ar_ref[0] = ar
    ai_ref[0] = ai
    br = br_ref[0]
    bi = bi_ref[0]
    bbr_ref[0] = fr[:, None, :] * br - fi[:, None, :] * bi
    bbi_ref[0] = fr[:, None, :] * bi + fi[:, None, :] * br


def _s5_discretize(lam_re, lam_im, log_step, b_re_t, b_im_t):
    depth = lam_re.shape[0]
    gp = pl.BlockSpec((1, S5_GROUPS, S5_STATE), lambda l: (l, 0, 0))
    bspec = pl.BlockSpec((1, S5_GROUPS, S5_GROUP, S5_STATE), lambda l: (l, 0, 0, 0))
    return pl.pallas_call(
        _s5_disc_kernel,
        grid=(depth,),
        in_specs=[gp, gp, pl.BlockSpec((1, S5_GROUPS, 1), lambda l: (l, 0, 0)), bspec, bspec],
        out_specs=[gp, gp, bspec, bspec],
        out_shape=[jax.ShapeDtypeStruct(lam_re.shape, F32), jax.ShapeDtypeStruct(lam_re.shape, F32),
                   jax.ShapeDtypeStruct(b_re_t.shape, F32), jax.ShapeDtypeStruct(b_re_t.shape, F32)],
        compiler_params=_cparams(("parallel",)),
        name="s5_discretize",
    )(lam_re, lam_im, log_step.reshape(depth, S5_GROUPS, 1), b_re_t, b_im_t)


def _blockdiag(r):
    eye = jnp.eye(S5_HALF, dtype=r.dtype)
    z = jnp.einsum('ab,lgqbhp->lgqahbp', eye, r)
    s = r.shape
    return z.reshape(s[0], s[1], s[2], S5_HALF * s[4], S5_HALF * s[5])


def _s5_matrices(bbr_t, bbi_t, c_re, c_im):
    depth = bbr_t.shape[0]
    shp = (depth, S5_NGB, 2, S5_HALF, S5_GROUP, S5_STATE)
    bre = _blockdiag(bbr_t.reshape(shp))
    bim = _blockdiag(bbi_t.reshape(shp))
    bcat = jnp.concatenate([bre, bim], axis=-1)
    zb = jnp.zeros_like(bcat[:, :, 0])
    bmat = jnp.stack([jnp.concatenate([bcat[:, :, 0], zb], axis=-2),
                      jnp.concatenate([zb, bcat[:, :, 1]], axis=-2)], axis=2)
    cre = jnp.swapaxes(_blockdiag(c_re.reshape(shp)), -1, -2)
    cim = jnp.swapaxes(_blockdiag(c_im.reshape(shp)), -1, -2)
    ccat = jnp.concatenate([cre, -cim], axis=-2)
    zc = jnp.zeros_like(ccat[:, :, 0])
    cmat = jnp.stack([jnp.concatenate([ccat[:, :, 0], zc], axis=-1),
                      jnp.concatenate([zc, ccat[:, :, 1]], axis=-1)], axis=2)
    return bmat.astype(BF16), cmat.astype(BF16)


def _s5_prompt_kernel(u_ref, b_ref, c_ref, a_ref, d_ref, z_ref, st_ref, e_ref, x_ref, *, nb, tc, ngb):
    @pl.when(pl.program_id(1) == 0)
    def _():
        x_ref[...] = jnp.zeros_like(x_ref)

    nlb = 2 * S5_HW // LANES
    hlb = nlb // 2
    for b in range(ngb):
        for n in range(nb):
            ub = u_ref[n, :, b * LANES:(b + 1) * LANES].astype(BF16)
            for q in range(2):
                e = jnp.dot(ub, b_ref[b, q], preferred_element_type=F32)
                for j in range(nlb):
                    e_ref[b * nlb + j, pl.ds(n * 2 + q, tc, stride=SUBLANES), :] = (
                        e[:, j * LANES:(j + 1) * LANES])

    ar, ai = [], []
    for b in range(ngb):
        a = a_ref[b]
        ar += [a[:, j * LANES:(j + 1) * LANES] for j in range(hlb)]
        ai += [a[:, (hlb + j) * LANES:(hlb + j + 1) * LANES] for j in range(hlb)]
    slabs = [(b * nlb + j, b * nlb + hlb + j) for b in range(ngb) for j in range(hlb)]

    def step(t, carry):
        row = pl.multiple_of(t * SUBLANES, SUBLANES)
        new = []
        for c, (sr, si) in enumerate(slabs):
            xr, xi = carry[2 * c], carry[2 * c + 1]
            nxr = ar[c] * xr - ai[c] * xi + e_ref[sr, pl.ds(row, SUBLANES), :]
            nxi = ar[c] * xi + ai[c] * xr + e_ref[si, pl.ds(row, SUBLANES), :]
            e_ref[sr, pl.ds(row, SUBLANES), :] = nxr
            e_ref[si, pl.ds(row, SUBLANES), :] = nxi
            new += [nxr, nxi]
        return tuple(new)

    init = []
    for b in range(ngb):
        x0 = x_ref[b]
        for j in range(hlb):
            init += [x0[:, j * LANES:(j + 1) * LANES], x0[:, (hlb + j) * LANES:(hlb + j + 1) * LANES]]
    fin = lax.fori_loop(0, tc, step, tuple(init), unroll=8)
    for b in range(ngb):
        for j in range(hlb):
            c = b * hlb + j
            x_ref[b, :, j * LANES:(j + 1) * LANES] = fin[2 * c]
            x_ref[b, :, (hlb + j) * LANES:(hlb + j + 1) * LANES] = fin[2 * c + 1]
    st_ref[...] = x_ref[...]

    for b in range(ngb):
        for n in range(nb):
            y = d_ref[:, b * LANES:(b + 1) * LANES] * u_ref[n, :, b * LANES:(b + 1) * LANES]
            for q in range(2):
                xs = jnp.concatenate(
                    [e_ref[b * nlb + j, pl.ds(n * 2 + q, tc, stride=SUBLANES), :] for j in range(nlb)],
                    axis=1)
                y = y + jnp.dot(xs.astype(BF16), c_ref[b, q], preferred_element_type=F32)
            z_ref[n, :, b * LANES:(b + 1) * LANES] = jax.nn.gelu(y)


def _s5_prompt(zh3, bmat, cmat, arow, d, l, tc):
    nb, t, _ = zh3.shape
    assert 2 * nb == SUBLANES, "state rows are (batch, half) pairs filling one sublane tile"
    ngb = 2
    kern = functools.partial(_s5_prompt_kernel, nb=nb, tc=tc, ngb=ngb)
    return pl.pallas_call(
        kern,
        grid=(S5_NGB // ngb, t // tc),
        in_specs=[pl.BlockSpec((nb, tc, ngb * LANES), lambda g, c: (0, c, g)),
                  _lspec(l, (ngb, 2, LANES, 2 * S5_HW), lambda g, c: (g, 0, 0, 0)),
                  _lspec(l, (ngb, 2, 2 * S5_HW, LANES), lambda g, c: (g, 0, 0, 0)),
                  _lspec(l, (ngb, SUBLANES, 2 * S5_HW), lambda g, c: (g, 0, 0)),
                  _lspec(l, (1, ngb * LANES), lambda g, c: (0, g))],
        out_specs=[pl.BlockSpec((nb, tc, ngb * LANES), lambda g, c: (0, c, g)),
                   pl.BlockSpec((ngb, SUBLANES, 2 * S5_HW), lambda g, c: (g, 0, 0))],
        out_shape=[jax.ShapeDtypeStruct((nb, t, S5_WIDTH), F32),
                   jax.ShapeDtypeStruct((S5_NGB, SUBLANES, 2 * S5_HW), F32)],
        scratch_shapes=[pltpu.VMEM((ngb * 2 * S5_HW // LANES, tc * SUBLANES, LANES), F32),
                        pltpu.VMEM((ngb, SUBLANES, 2 * S5_HW), F32)],
        compiler_params=_cparams(("parallel", "arbitrary")),
        name="s5_prompt",
    )(zh3, bmat, cmat, arow, d)


def _s5_decode_kernel(u_ref, b_ref, c_ref, a_ref, d_ref, sr_ref, si_ref, z_ref, nr_ref, ni_ref):
    for g in range(S5_NGB):
        u = u_ref[:, g * LANES:(g + 1) * LANES]
        ub = u.astype(BF16)
        y = d_ref[:, g * LANES:(g + 1) * LANES] * u
        for q in range(2):
            c0 = (g * 2 + q) * S5_HW
            e = jnp.dot(ub, b_ref[g, q], preferred_element_type=F32)
            ar = a_ref[0:1, c0:c0 + S5_HW]
            ai = a_ref[1:2, c0:c0 + S5_HW]
            x0r = sr_ref[:, c0:c0 + S5_HW]
            x0i = si_ref[:, c0:c0 + S5_HW]
            nr = ar * x0r - ai * x0i + e[:, :S5_HW]
            ni = ar * x0i + ai * x0r + e[:, S5_HW:]
            nr_ref[:, c0:c0 + S5_HW] = nr
            ni_ref[:, c0:c0 + S5_HW] = ni
            xs = jnp.concatenate([nr, ni], axis=1).astype(BF16)
            y = y + jnp.dot(xs, c_ref[g, q], preferred_element_type=F32)
        z_ref[:, g * LANES:(g + 1) * LANES] = jax.nn.gelu(y)


def _s5_decode(zh, bmat, cmat, adec, d, sr, si, l):
    nb = zh.shape[0]
    ns = S5_GROUPS * S5_STATE
    full = lambda shape: pl.BlockSpec(shape, lambda i: (0,) * len(shape))
    lfull = lambda shape: _lspec(l, shape, lambda i: (0,) * len(shape))
    return pl.pallas_call(
        _s5_decode_kernel,
        grid=(1,),
        in_specs=[full((nb, S5_WIDTH)), lfull(bmat.shape[1:]), lfull(cmat.shape[1:]), lfull((2, ns)),
                  lfull((1, S5_WIDTH)), lfull((nb, ns)), lfull((nb, ns))],
        out_specs=[full((nb, S5_WIDTH)), full((nb, ns)), full((nb, ns))],
        out_shape=[jax.ShapeDtypeStruct((nb, S5_WIDTH), F32),
                   jax.ShapeDtypeStruct((nb, ns), F32), jax.ShapeDtypeStruct((nb, ns), F32)],
        compiler_params=_cparams(("arbitrary",)),
        name="s5_decode",
    )(zh, bmat, cmat, adec, d, sr, si)


def _gla_finish(o, r, g):
    return _rms(o, g) * (r * jax.nn.sigmoid(r))


def _gla_prompt_kernel(q_ref, k_ref, v_ref, a_ref, r_ref, wg_ref, bg_ref, go_ref, y_ref, st_ref,
                       s_ref, lg_ref, o_ref, qi_ref, ku_ref, cd_ref, *, tc):
    c = pl.program_id(1)

    @pl.when(c == 0)
    def _():
        s_ref[...] = jnp.zeros_like(s_ref)

    a16 = a_ref[0][:, :GLA_RANK].astype(BF16)
    lg_ref[...] = jax.nn.log_sigmoid(
        jnp.dot(a16, wg_ref[...], preferred_element_type=F32) + bg_ref[...]) / GLA_TAU

    ch = GLA_CHUNK
    hs = SUBLANES
    rowi = lax.broadcasted_iota(jnp.int32, (ch, GLA_DK), 0)
    rowh = lax.broadcasted_iota(jnp.int32, (hs, GLA_DK), 0)
    coli = lax.broadcasted_iota(jnp.int32, (hs, ch), 1)
    scale = GLA_DK ** -0.5

    def prep(idx, carry):
        r0 = pl.multiple_of(idx * ch, ch)
        for h in range(GLA_HEADS):
            ks = slice(h * GLA_DK, (h + 1) * GLA_DK)
            vs = slice(h * GLA_DV, (h + 1) * GLA_DV)
            b = lg_ref[pl.ds(r0, ch), ks]
            for sh in (1, 2, 4, 8):
                b = b + jnp.where(rowi >= sh, pltpu.roll(b, sh, axis=0), 0.0)
            qh = q_ref[0, pl.ds(r0, ch), ks] * scale
            kh = k_ref[0, pl.ds(r0, ch), ks]
            halves = []
            for half in range(ch // hs):
                bq = b[half * hs:(half + 1) * hs]
                qq = qh[half * hs:(half + 1) * hs]
                att = jnp.zeros((hs, ch), F32)
                for s in range((half + 1) * hs):
                    diff = bq - b[s:s + 1, :]
                    if s >= half * hs:
                        diff = jnp.where(rowh + half * hs >= s, diff, -jnp.inf)
                    col = jnp.sum(qq * kh[s:s + 1, :] * jnp.exp(diff), axis=1, keepdims=True)
                    att = jnp.where(coli == s, col, att)
                halves.append(att)
            att = jnp.concatenate(halves, axis=0).astype(BF16)
            vh = v_ref[0, pl.ds(r0, ch), vs].astype(BF16)
            o_ref[pl.ds(r0, ch), vs] = jnp.dot(att, vh, preferred_element_type=F32)
            blast = b[ch - 1:ch, :]
            qi_ref[pl.ds(r0, ch), ks] = (qh * jnp.exp(b)).astype(BF16)
            ku_ref[pl.ds(r0, ch), ks] = (kh * jnp.exp(blast - b)).astype(BF16)
            cd_ref[pl.ds(pl.multiple_of(idx * hs, hs), hs), ks] = jnp.broadcast_to(
                jnp.exp(blast), (hs, GLA_DK))
        return carry

    lax.fori_loop(0, tc // ch, prep, 0, unroll=4)

    def scan(idx, carry):
        r0 = pl.multiple_of(idx * ch, ch)
        for h in range(GLA_HEADS):
            ks = slice(h * GLA_DK, (h + 1) * GLA_DK)
            vs = slice(h * GLA_DV, (h + 1) * GLA_DV)
            st = s_ref[h]
            inter = lax.dot_general(qi_ref[pl.ds(r0, ch), ks], st.astype(BF16),
                                    (((1,), (1,)), ((), ())), preferred_element_type=F32)
            o_ref[pl.ds(r0, ch), vs] += inter
            vh = v_ref[0, pl.ds(r0, ch), vs].astype(BF16)
            kv = lax.dot_general(vh, ku_ref[pl.ds(r0, ch), ks], (((0,), (0,)), ((), ())),
                                 preferred_element_type=F32)
            cd = cd_ref[pl.ds(pl.multiple_of(idx * hs, hs), 1), ks]
            s_ref[h] = cd * st + kv
        return carry

    lax.fori_loop(0, tc // ch, scan, 0, unroll=4)

    for h in range(GLA_HEADS):
        vs = slice(h * GLA_DV, (h + 1) * GLA_DV)
        y_ref[0, :, vs] = _gla_finish(o_ref[:, vs], r_ref[0, :, vs], go_ref[...]).astype(y_ref.dtype)
        st_ref[0, h] = s_ref[h].T


def _gla_prompt(zh3, za3, zt3, wg, bg, go, l, tc):
    nb, t, _ = zh3.shape
    kern = functools.partial(_gla_prompt_kernel, tc=tc)
    return pl.pallas_call(
        kern,
        grid=(nb, t // tc),
        in_specs=[pl.BlockSpec((1, tc, GLA_QK), lambda n, c: (n, c, COL_GQ // GLA_QK)),
                  pl.BlockSpec((1, tc, GLA_QK), lambda n, c: (n, c, COL_GK // GLA_QK)),
                  pl.BlockSpec((1, tc, GLA_V), lambda n, c: (n, c, COL_GV // GLA_V)),
                  pl.BlockSpec((1, tc, LANES), lambda n, c: (n, c, 0)),
                  pl.BlockSpec((1, tc, GLA_V), lambda n, c: (n, c, TCOL_R // GLA_V)),
                  _lspec(l, (GLA_RANK, GLA_QK), lambda n, c: (0, 0)),
                  _lspec(l, (1, GLA_QK), lambda n, c: (0, 0)),
                  _lspec(l, (1, GLA_DV), lambda n, c: (0, 0))],
        out_specs=[pl.BlockSpec((1, tc, GLA_V), lambda n, c: (n, c, 0)),
                   pl.BlockSpec((1, GLA_HEADS, GLA_DK, GLA_DV), lambda n, c: (n, 0, 0, 0))],
        out_shape=[jax.ShapeDtypeStruct((nb, t, GLA_V), BF16),
                   jax.ShapeDtypeStruct((nb, GLA_HEADS, GLA_DK, GLA_DV), F32)],
        scratch_shapes=[pltpu.VMEM((GLA_HEADS, GLA_DV, GLA_DK), F32),
                        pltpu.VMEM((tc, GLA_QK), F32),
                        pltpu.VMEM((tc, GLA_V), F32),
                        pltpu.VMEM((tc, GLA_QK), BF16),
                        pltpu.VMEM((tc, GLA_QK), BF16),
                        pltpu.VMEM((tc // GLA_CHUNK * SUBLANES, GLA_QK), F32)],
        compiler_params=_cparams(("parallel", "arbitrary")),
        name="gla_prompt",
    )(zh3, zh3, zh3, za3, zt3, wg, bg, go)


def _gla_decode_kernel(zh_ref, za_ref, r_ref, wg_ref, bg_ref, go_ref, s_ref, y_ref, so_ref, *, nstep):
    n0 = pl.multiple_of(pl.program_id(0) * nstep, nstep)
    rows = zh_ref[pl.ds(n0, nstep), :]
    a16 = za_ref[pl.ds(n0, nstep), :GLA_RANK].astype(BF16)
    lg = jax.nn.log_sigmoid(
        jnp.dot(a16, wg_ref[...], preferred_element_type=F32) + bg_ref[...]) / GLA_TAU
    eg = jnp.exp(lg)
    rr = r_ref[pl.ds(n0, nstep), :]
    ri = lax.broadcasted_iota(jnp.int32, (GLA_DK, GLA_DK), 0)
    ci = lax.broadcasted_iota(jnp.int32, (GLA_DK, GLA_DK), 1)
    eye = (ri == ci).astype(F32)

    def col(v):
        return jnp.sum(eye * v, axis=1, keepdims=True)

    for j in range(nstep):
        row = rows[j:j + 1]
        for h in range(GLA_HEADS):
            ks = slice(h * GLA_DK, (h + 1) * GLA_DK)
            qh = row[:, COL_GQ + h * GLA_DK:COL_GQ + (h + 1) * GLA_DK] * (GLA_DK ** -0.5)
            kh = row[:, COL_GK + h * GLA_DK:COL_GK + (h + 1) * GLA_DK]
            vh = row[:, COL_GV + h * GLA_DV:COL_GV + (h + 1) * GLA_DV]
            s = col(eg[j:j + 1, ks]) * s_ref[j, h] + col(kh) * vh
            so_ref[j, h] = s
            o = jnp.sum(col(qh) * s, axis=0, keepdims=True)
            vs = slice(h * GLA_DV, (h + 1) * GLA_DV)
            y_ref[j, :, vs] = _gla_finish(o, rr[j:j + 1, vs], go_ref[...]).astype(y_ref.dtype)


def _gla_decode(zh, za, zt, wg, bg, go, s0, l, nstep=SUBLANES):
    nb = zh.shape[0]
    assert nb % nstep == 0
    kern = functools.partial(_gla_decode_kernel, nstep=nstep)
    sblock = (nstep, GLA_HEADS, GLA_DK, GLA_DV)
    return pl.pallas_call(
        kern,
        grid=(nb // nstep,),
        in_specs=[pl.BlockSpec((nb, HEAD_COLS), lambda n: (0, 0)),
                  pl.BlockSpec((nb, LANES), lambda n: (0, 0)),
                  pl.BlockSpec((nb, GLA_V), lambda n: (0, TCOL_R // GLA_V)),
                  _lspec(l, (GLA_RANK, GLA_QK), lambda n: (0, 0)),
                  _lspec(l, (1, GLA_QK), lambda n: (0, 0)),
                  _lspec(l, (1, GLA_DV), lambda n: (0, 0)),
                  _lspec(l, sblock, lambda n: (n, 0, 0, 0))],
        out_specs=[pl.BlockSpec((nstep, 1, GLA_V), lambda n: (n, 0, 0)),
                   pl.BlockSpec(sblock, lambda n: (n, 0, 0, 0))],
        out_shape=[jax.ShapeDtypeStruct((nb, 1, GLA_V), BF16),
                   jax.ShapeDtypeStruct(s0.shape[1:], F32)],
        compiler_params=_cparams(("arbitrary",)),
        name="gla_decode",
    )(zh, za, zt, wg, bg, go, s0)


def _bucket_table():
    dist = (np.arange(WINDOW)[:, None] + WINDOW) - np.arange(2 * WINDOW)[None, :]
    max_exact = N_BUCKETS // 2
    d = np.maximum(dist, 0)
    ratio = np.log(np.maximum(d, 1).astype(np.float32) / np.float32(max_exact)) / np.float32(
        math.log(MAX_DISTANCE / max_exact))
    large = max_exact + (ratio.astype(np.float32) * np.float32(N_BUCKETS - max_exact)).astype(np.int32)
    large = np.minimum(large, N_BUCKETS - 1)
    return np.where(d < max_exact, d, large).astype(np.int32)


def _bias_kernel(bk_ref, rb_ref, o_ref):
    h = pl.program_id(1)
    bk = bk_ref[...]
    acc = jnp.zeros(bk.shape, F32)
    for b in range(N_BUCKETS):
        acc = jnp.where(bk == b, rb_ref[b, h], acc)
    cols = lax.broadcasted_iota(jnp.int32, bk.shape, 1)
    dist = WINDOW + lax.broadcasted_iota(jnp.int32, bk.shape, 0) - cols
    band = (dist >= 0) & (dist < WINDOW)
    keep = band & ((cols >= WINDOW) | (pl.program_id(0) == 1))
    o_ref[0, 0] = jnp.where(keep, acc, -jnp.inf)


def _bias_table(rel_bias):
    bk = jnp.asarray(_bucket_table())
    return pl.pallas_call(
        _bias_kernel,
        grid=(2, SWA_QH),
        in_specs=[pl.BlockSpec((WINDOW, 2 * WINDOW), lambda v, h: (0, 0)),
                  pl.BlockSpec(memory_space=pltpu.SMEM)],
        out_specs=pl.BlockSpec((1, 1, WINDOW, 2 * WINDOW), lambda v, h: (v, h, 0, 0)),
        out_shape=jax.ShapeDtypeStruct((2, SWA_QH, WINDOW, 2 * WINDOW), F32),
        compiler_params=_cparams(("parallel", "parallel")),
        name="swa_bias",
    )(bk, rel_bias)


def _swa_prompt_kernel(q_ref, kp_ref, kc_ref, vp_ref, vc_ref, bias_ref, sink_ref, y_ref, *, t, l):
    i = pl.program_id(1)
    w = WINDOW
    inseq = i * w + lax.broadcasted_iota(jnp.int32, (w, 1), 0) < t
    scale = SWA_HD ** -0.5
    for kv in range(SWA_KVH):
        hs = slice(kv * SWA_HD, (kv + 1) * SWA_HD)
        kh = jnp.concatenate([kp_ref[0, :, hs], jnp.where(inseq, kc_ref[0, :, hs], 0.0)],
                             axis=0).astype(BF16)
        vh = jnp.concatenate([vp_ref[0, :, hs], jnp.where(inseq, vc_ref[0, :, hs], 0.0)],
                             axis=0).astype(BF16)
        for g in range(SWA_GRP):
            h = kv * SWA_GRP + g
            qh = (q_ref[0, :, h * SWA_HD:(h + 1) * SWA_HD] * scale).astype(BF16)
            s = lax.dot_general(qh, kh, (((1,), (1,)), ((), ())), preferred_element_type=F32)
            s = s + bias_ref[h]
            sink = sink_ref[l, h]
            m = jnp.maximum(jnp.max(s, axis=-1, keepdims=True), sink)
            p = jnp.exp(s - m)
            den = jnp.sum(p, axis=-1, keepdims=True) + jnp.exp(sink - m)
            o = jnp.dot(p.astype(BF16), vh, preferred_element_type=F32) * (1.0 / den)
            y_ref[0, :, h * SWA_HD:(h + 1) * SWA_HD] = o.astype(y_ref.dtype)


def _swa_prompt(zt3, bias, sinks, l):
    nb, t, _ = zt3.shape
    w = WINDOW
    nblk = pl.cdiv(t, w)
    kern = functools.partial(_swa_prompt_kernel, t=t, l=l)
    kcol = TCOL_SK // SWA_KV
    vcol = TCOL_SV // SWA_KV
    prev = lambda i: jnp.maximum(i - 1, 0)
    return pl.pallas_call(
        kern,
        grid=(nb, nblk),
        in_specs=[pl.BlockSpec((1, w, SWA_Q), lambda n, i: (n, i, TCOL_SQ // SWA_Q)),
                  pl.BlockSpec((1, w, SWA_KV), lambda n, i: (n, prev(i), kcol)),
                  pl.BlockSpec((1, w, SWA_KV), lambda n, i: (n, i, kcol)),
                  pl.BlockSpec((1, w, SWA_KV), lambda n, i: (n, prev(i), vcol)),
                  pl.BlockSpec((1, w, SWA_KV), lambda n, i: (n, i, vcol)),
                  pl.BlockSpec((None, SWA_QH, w, 2 * w), lambda n, i: (jnp.minimum(i, 1), 0, 0, 0)),
                  pl.BlockSpec(memory_space=pltpu.SMEM)],
        out_specs=pl.BlockSpec((1, w, SWA_Q), lambda n, i: (n, i, 0)),
        out_shape=jax.ShapeDtypeStruct((nb, t, SWA_Q), BF16),
        compiler_params=_cparams(("parallel", "arbitrary")),
        name="swa_prompt",
    )(zt3, zt3, zt3, zt3, zt3, bias, sinks)


def _swa_decode_kernel(q_ref, kv_ref, kb_ref, vb_ref, bias_ref, sink_ref, y_ref, ko_ref, vo_ref, *, nstep):
    n0 = pl.program_id(0) * nstep
    scale = SWA_HD ** -0.5
    w = WINDOW
    ri = lax.broadcasted_iota(jnp.int32, (SWA_HD, SWA_HD), 0)
    ci = lax.broadcasted_iota(jnp.int32, (SWA_HD, SWA_HD), 1)
    eye = (ri == ci).astype(F32)
    last = lax.broadcasted_iota(jnp.int32, (SWA_HD, w), 1) == w - 1

    def push(cache_t, new_row):
        col = jnp.sum(eye * new_row, axis=1, keepdims=True)
        return jnp.where(last, col, pltpu.roll(cache_t, w - 1, axis=1))

    scores = []
    for j in range(nstep):
        qrow = q_ref[pl.ds(n0 + j, 1), :]
        kvrow = kv_ref[pl.ds(n0 + j, 1), :]
        for kv in range(SWA_KVH):
            hs = slice(kv * SWA_HD, (kv + 1) * SWA_HD)
            kt = push(kb_ref[j, kv], kvrow[:, hs])
            ko_ref[j, kv] = kt
            vo_ref[j, kv] = push(vb_ref[j, kv],
                                 kvrow[:, SWA_KV + kv * SWA_HD:SWA_KV + (kv + 1) * SWA_HD])
            q4 = jnp.concatenate(
                [qrow[:, (kv * SWA_GRP + g) * SWA_HD:(kv * SWA_GRP + g + 1) * SWA_HD]
                 for g in range(SWA_GRP)], axis=0).astype(BF16)
            scores.append(jnp.dot(q4, kt.astype(BF16), preferred_element_type=F32))
    s = jnp.concatenate(scores, axis=0) * scale
    s = s + jnp.concatenate([bias_ref[...].reshape(SWA_QH, w)] * nstep, axis=0)
    sink = jnp.concatenate([sink_ref[...]] * nstep, axis=0)
    m = jnp.maximum(jnp.max(s, axis=-1, keepdims=True), sink)
    p = jnp.exp(s - m)
    den = jnp.sum(p, axis=-1, keepdims=True) + jnp.exp(sink - m)
    pn = (p / den).astype(BF16)
    for j in range(nstep):
        for kv in range(SWA_KVH):
            r0 = (j * SWA_KVH + kv) * SWA_GRP
            o = lax.dot_general(pn[r0:r0 + SWA_GRP], vo_ref[j, kv].astype(BF16),
                                (((1,), (1,)), ((), ())), preferred_element_type=F32)
            for g in range(SWA_GRP):
                h = kv * SWA_GRP + g
                y_ref[j, :, h * SWA_HD:(h + 1) * SWA_HD] = o[g:g + 1, :].astype(y_ref.dtype)


def _swa_decode(zt, kbuf_t, vbuf_t, bias_dec, sinks, l, nstep=8):
    nb = zt.shape[0]
    w = WINDOW
    assert nb % nstep == 0
    kern = functools.partial(_swa_decode_kernel, nstep=nstep)
    cspec = lambda: _lspec(l, (nstep, SWA_KVH, SWA_HD, w), lambda n: (n, 0, 0, 0))
    ospec = lambda: pl.BlockSpec((nstep, SWA_KVH, SWA_HD, w), lambda n: (n, 0, 0, 0))
    return pl.pallas_call(
        kern,
        grid=(nb // nstep,),
        in_specs=[pl.BlockSpec((nb, SWA_Q), lambda n: (0, TCOL_SQ // SWA_Q)),
                  pl.BlockSpec((nb, 2 * SWA_KV), lambda n: (0, TCOL_SK // (2 * SWA_KV))),
                  cspec(), cspec(),
                  pl.BlockSpec((SWA_KVH, SWA_GRP, w), lambda n: (0, 0, 0)),
                  _lspec(l, (SWA_QH, 1), lambda n: (0, 0))],
        out_specs=[pl.BlockSpec((nstep, 1, SWA_Q), lambda n: (n, 0, 0)), ospec(), ospec()],
        out_shape=[jax.ShapeDtypeStruct((nb, 1, SWA_Q), BF16),
                   jax.ShapeDtypeStruct(kbuf_t.shape[1:], F32), jax.ShapeDtypeStruct(vbuf_t.shape[1:], F32)],
        compiler_params=_cparams(("arbitrary",)),
        name="swa_decode",
    )(zt, zt, kbuf_t, vbuf_t, bias_dec, sinks)


def _seq_chunk(t):
    for c in (688, 512, 256, 128, 64, 32, 16):
        if t % c == 0:
            return c
    raise ValueError(f"sequence length {t} must be a multiple of {GLA_CHUNK}")


def kernel(x_prompt, x_sample, state_s5_re, state_s5_im, state_gla, cache_swa_k, cache_swa_v, meta_tokens, rel_bias, norm_pre_mix, norm_post_mix, norm_pre_ffn, norm_post_ffn, w_in, s5_lam_re, s5_lam_im, s5_log_step, s5_b_re, s5_b_im, s5_c_re, s5_c_im, s5_d, s5_w_glu, s5_b_glu, gla_w_gate2, gla_b_gate2, gla_g_out, swa_sinks, w_up_s5, w_up_gla, w_up_swa, w_out, w_ff1, w_ff2):
    depth = w_in.shape[0]
    nb, seq, d = x_prompt.shape
    t = seq + N_META
    nd = x_sample.shape[0]
    assert x_sample.shape[1] == 1 and cache_swa_k.shape[2] == WINDOW and t % GLA_CHUNK == 0
    tc = _seq_chunk(t)

    w_in_t = jnp.swapaxes(w_in.astype(BF16), 1, 2)
    w_glu = s5_w_glu.astype(BF16)
    wg2 = gla_w_gate2.astype(BF16)
    wu_a, wu_b, wu_c = w_up_s5.astype(BF16), w_up_gla.astype(BF16), w_up_swa.astype(BF16)
    wo, w1, w2 = w_out.astype(BF16), w_ff1.astype(BF16), w_ff2.astype(BF16)
    row3 = lambda a: a.reshape(depth, 1, a.shape[-1])
    g_pre, g_post, g_pre_f, g_post_f = (row3(a) for a in (norm_pre_mix, norm_post_mix,
                                                           norm_pre_ffn, norm_post_ffn))
    b_glu, d5, bg2, go = row3(s5_b_glu), row3(s5_d), row3(gla_b_gate2), row3(gla_g_out)
    sinks3 = swa_sinks.reshape(depth, SWA_QH, 1)
    s5r = state_s5_re.reshape(depth, nd, -1)
    s5i = state_s5_im.reshape(depth, nd, -1)
    ckb = jnp.transpose(cache_swa_k, (0, 1, 3, 4, 2))
    cvb = jnp.transpose(cache_swa_v, (0, 1, 3, 4, 2))

    ar, ai, bbr_t, bbi_t = _s5_discretize(s5_lam_re, s5_lam_im, s5_log_step,
                                          jnp.swapaxes(s5_b_re, -1, -2), jnp.swapaxes(s5_b_im, -1, -2))
    bmat, cmat = _s5_matrices(bbr_t, bbi_t, s5_c_re, s5_c_im)
    a_half = jnp.concatenate([ar.reshape(depth, S5_NGB, 2, S5_HW), ai.reshape(depth, S5_NGB, 2, S5_HW)],
                             axis=-1)
    arow = jnp.tile(a_half, (1, 1, SUBLANES // 2, 1))
    adec = jnp.stack([ar.reshape(depth, -1), ai.reshape(depth, -1)], axis=1)

    bias = _bias_table(rel_bias)
    bias_dec = bias[1, :, WINDOW - 1, WINDOW:].reshape(SWA_KVH, SWA_GRP, WINDOW)

    meta = jnp.broadcast_to(meta_tokens.astype(x_prompt.dtype)[None], (nb, N_META, d))
    xp = jnp.concatenate([meta, x_prompt], axis=1).reshape(nb * t, d)
    xs = x_sample.reshape(nd, d)

    outs = {k: [] for k in ('p_re', 'p_im', 'p_gla', 'p_k', 'p_v', 's_re', 's_im', 's_gla', 's_k', 's_v')}
    for l in range(depth):
        zh, zhd, za, zad = _norm_mm(xp, xs, g_pre, w_in_t, l, 0, HEAD_COLS, 1536, extra_col0=COL_GA)
        zt, ztd = _norm_mm(xp, xs, g_pre, w_in_t, l, TAIL_START, MIX_COLS, 2560)
        zg, zgd = _norm_mm(xp, xs, g_pre, w_in_t, l, GATE_START, GATE_COLS, 3072, gate=True)
        zh3 = zh.reshape(nb, t, HEAD_COLS)
        za3 = za.reshape(nb, t, LANES)
        zt3 = zt.reshape(nb, t, MIX_COLS)
        z5, st5 = _s5_prompt(zh3, bmat, cmat, arow, d5, l, tc)
        z5d, nr, ni = _s5_decode(zhd, bmat, cmat, adec, d5, s5r, s5i, l)
        yb, pg = _gla_prompt(zh3, za3, zt3, wg2, bg2, go, l, tc)
        ybd, sg = _gla_decode(zhd, zad, ztd, wg2, bg2, go, state_gla, l)
        yc = _swa_prompt(zt3, bias, swa_sinks, l)
        ycd, sk, sv = _swa_decode(ztd, ckb, cvb, bias_dec, sinks3, l)
        xp, xs = _merge_out((z5.reshape(nb * t, S5_WIDTH), yb.reshape(nb * t, GLA_V),
                             yc.reshape(nb * t, SWA_Q)),
                            (z5d, ybd.reshape(nd, GLA_V), ycd.reshape(nd, SWA_Q)),
                            zg, zgd, w_glu, b_glu, wu_a, wu_b, wu_c, wo, xp, xs, g_post, l)
        xp, xs = _ffn(xp, xs, g_pre_f, w1, w2, g_post_f, l)
        st5 = st5.reshape(S5_NGB, nb, 2, 2, S5_HALF, S5_STATE).transpose(3, 1, 0, 2, 4, 5)
        st5 = st5.reshape(2, nb, S5_GROUPS, S5_STATE)
        outs['p_re'].append(st5[0])
        outs['p_im'].append(st5[1])
        outs['p_gla'].append(pg)
        outs['p_k'].append(zt3[:, t - WINDOW:, TCOL_SK:TCOL_SK + SWA_KV].reshape(nb, WINDOW, SWA_KVH, SWA_HD))
        outs['p_v'].append(zt3[:, t - WINDOW:, TCOL_SV:TCOL_SV + SWA_KV].reshape(nb, WINDOW, SWA_KVH, SWA_HD))
        outs['s_re'].append(nr.reshape(nd, S5_GROUPS, S5_STATE))
        outs['s_im'].append(ni.reshape(nd, S5_GROUPS, S5_STATE))
        outs['s_gla'].append(sg)
        outs['s_k'].append(jnp.transpose(sk, (0, 3, 1, 2)))
        outs['s_v'].append(jnp.transpose(sv, (0, 3, 1, 2)))

    y_prompt = xp.reshape(nb, t, d)[:, N_META:]
    y_sample = xs.reshape(nd, 1, d)
    st = {k: jnp.stack(v) for k, v in outs.items()}
    return (y_prompt, y_sample, st['p_re'], st['p_im'], st['p_gla'], st['p_k'], st['p_v'],
            st['s_re'], st['s_im'], st['s_gla'], st['s_k'], st['s_v'])
```

```python
import functools
import math

import numpy as np
import jax
import jax.numpy as jnp
from jax import lax
from jax.experimental import pallas as pl
from jax.experimental.pallas import tpu as pltpu

F32 = jnp.float32
BF16 = jnp.bfloat16

D_MODEL = 2048
N_META = 16
S5_GROUP = 16
S5_WIDTH = D_MODEL // 2
S5_GROUPS = S5_WIDTH // S5_GROUP
S5_STATE = 64
GLA_HEADS = 4
GLA_DK = D_MODEL // 16
GLA_DV = D_MODEL // 8
GLA_RANK = 16
GLA_TAU = 16.0
GLA_CHUNK = 16
SWA_QH = 16
SWA_KVH = 4
SWA_GRP = SWA_QH // SWA_KVH
SWA_HD = 64
WINDOW = 128
N_BUCKETS = 32
MAX_DISTANCE = WINDOW
D_FF = 4 * D_MODEL
N_BRANCH = 3
EPS = 1e-6

GLA_QK = GLA_HEADS * GLA_DK
GLA_V = GLA_HEADS * GLA_DV
SWA_Q = SWA_QH * SWA_HD
SWA_KV = SWA_KVH * SWA_HD

COL_U = 0
COL_GQ = COL_U + S5_WIDTH
COL_GK = COL_GQ + GLA_QK
COL_GV = COL_GK + GLA_QK
COL_GA = COL_GV + GLA_V
LANES = 128
SUBLANES = 8
HEAD_COLS = COL_GA
TAIL_START = COL_GA + GLA_RANK
TCOL_R = 0
TCOL_SQ = TCOL_R + GLA_V
TCOL_SK = TCOL_SQ + SWA_Q
TCOL_SV = TCOL_SK + SWA_KV
TCOL_GATE = TCOL_SV + SWA_KV
MIX_COLS = TCOL_GATE
GATE_START = TAIL_START + TCOL_GATE
GATE_COLS = N_BRANCH * D_MODEL

S5_GB = 8
S5_NGB = S5_GROUPS // S5_GB
S5_HALF = S5_GB // 2
S5_HW = S5_HALF * S5_STATE

VMEM_LIMIT = 60 * 1024 * 1024


def _cparams(sem):
    return pltpu.CompilerParams(dimension_semantics=sem, vmem_limit_bytes=VMEM_LIMIT)


def _rms(x, g):
    ms = jnp.mean(x * x, axis=-1, keepdims=True)
    return (x * lax.rsqrt(ms + EPS)) * g


def _row_tile(m):
    for t in (688, 512, 256, 128, 64, 32, 16, 8):
        if m % t == 0:
            return t
    return m


def _lspec(l, shape, imap):
    return pl.BlockSpec((None,) + tuple(shape), lambda *idx: (l,) + tuple(imap(*idx)))


class _Tiles:
    def __init__(self, m, n_inner, tm=None):
        self.tm = tm or _row_tile(m)
        assert m % self.tm == 0
        self.npt = m // self.tm
        self.grid = (self.npt, n_inner)

    def dstep(self, i, j):
        return jnp.where(i == self.npt - 1, j, 0)


def _with_decode(npt, prompt_fn, decode_fn):
    prompt_fn()
    pl.when(pl.program_id(0) == npt - 1)(decode_fn)


def _nt_dot(h, wt):
    return lax.dot_general(h, wt, (((1,), (1,)), ((), ())), preferred_element_type=F32)


def _norm_mm_kernel(*refs, npt, gate, extra):
    if extra:
        xp_ref, xd_ref, g_ref, w_ref, we_ref, op_ref, od_ref, ep_ref, ed_ref, hp_ref, hd_ref = refs
    else:
        xp_ref, xd_ref, g_ref, w_ref, op_ref, od_ref, hp_ref, hd_ref = refs
        we_ref = ep_ref = ed_ref = None

    def run(x_ref, o_ref, e_ref, h_ref):
        @pl.when(pl.program_id(1) == 0)
        def _():
            h_ref[...] = _rms(x_ref[...], g_ref[...]).astype(BF16)
            if extra:
                e_ref[...] = _nt_dot(h_ref[...], we_ref[0])

        z = _nt_dot(h_ref[...], w_ref[0])
        o_ref[...] = (jax.nn.sigmoid(z) if gate else z).astype(o_ref.dtype)

    _with_decode(npt, lambda: run(xp_ref, op_ref, ep_ref, hp_ref),
                 lambda: run(xd_ref, od_ref, ed_ref, hd_ref))


def _norm_mm(xp, xd, g, wt, l, col0, n_cols, tn, gate=False, extra_col0=None, tm=None):
    m, k = xp.shape
    nd = xd.shape[0]
    tl = _Tiles(m, n_cols // tn, tm)
    extra = extra_col0 is not None
    odt = BF16 if gate else F32
    kern = functools.partial(_norm_mm_kernel, npt=tl.npt, gate=gate, extra=extra)
    wspec = lambda width, off: pl.BlockSpec(
        (pl.Element(1), pl.Element(width), pl.Element(k)),
        lambda i, j: (l, pl.multiple_of(off(j), 16), 0))
    in_specs = [pl.BlockSpec((tl.tm, k), lambda i, j: (i, 0)),
                pl.BlockSpec((nd, k), lambda i, j: (0, 0)),
                _lspec(l, (1, k), lambda i, j: (0, 0)),
                wspec(tn, lambda j: col0 + j * tn)]
    out_specs = [pl.BlockSpec((tl.tm, tn), lambda i, j: (i, j)),
                 pl.BlockSpec((nd, tn), lambda i, j: (0, tl.dstep(i, j)))]
    out_shape = [jax.ShapeDtypeStruct((m, n_cols), odt), jax.ShapeDtypeStruct((nd, n_cols), odt)]
    args = [xp, xd, g, wt]
    if extra:
        in_specs.append(wspec(LANES, lambda j: extra_col0 + 0 * j))
        out_specs += [pl.BlockSpec((tl.tm, LANES), lambda i, j: (i, 0)),
                      pl.BlockSpec((nd, LANES), lambda i, j: (0, 0))]
        out_shape += [jax.ShapeDtypeStruct((m, LANES), F32), jax.ShapeDtypeStruct((nd, LANES), F32)]
        args.append(wt)
    return pl.pallas_call(
        kern,
        grid=tl.grid,
        in_specs=in_specs,
        out_specs=out_specs,
        out_shape=out_shape,
        scratch_shapes=[pltpu.VMEM((tl.tm, k), BF16), pltpu.VMEM((nd, k), BF16)],
        compiler_params=_cparams(("arbitrary", "arbitrary")),
        name="norm_mm",
    )(*args)


def _merge_out_kernel(zp_ref, ybp_ref, ycp_ref, zd_ref, ybd_ref, ycd_ref,
                      gap_ref, gbp_ref, gcp_ref, gad_ref, gbd_ref, gcd_ref,
                      wglu_ref, bglu_ref, wa_ref, wb_ref, wc_ref, wo_ref, xp_ref, xd_ref, g_ref,
                      op_ref, od_ref, mp_ref, md_ref, yap_ref, yad_ref, *, npt, nk, tk, tn):
    s = pl.program_id(1)

    def run(z_ref, ya_ref, yb_ref, yc_ref, ga_ref, gb_ref, gc_ref, x_ref, o_ref, m_ref):
        @pl.when(s == 0)
        def _():
            z = z_ref[...]
            a = jnp.dot(z.astype(BF16), wglu_ref[...], preferred_element_type=F32) + bglu_ref[...]
            ya_ref[...] = (z * jax.nn.sigmoid(a)).astype(BF16)

        @pl.when(s < nk)
        def _():
            m = (ga_ref[...].astype(F32) * jnp.dot(ya_ref[...], wa_ref[...], preferred_element_type=F32)
                 + gb_ref[...].astype(F32) * jnp.dot(yb_ref[...], wb_ref[...], preferred_element_type=F32)
                 + gc_ref[...].astype(F32) * jnp.dot(yc_ref[...], wc_ref[...], preferred_element_type=F32))
            m_ref[:, pl.ds(pl.multiple_of(s * tk, tk), tk)] = m.astype(BF16)

        @pl.when(s >= nk)
        def _():
            c0 = pl.multiple_of((s - nk) * tn, tn)
            o_ref[:, pl.ds(c0, tn)] = jnp.dot(m_ref[...], wo_ref[...], preferred_element_type=F32)

        @pl.when(s == pl.num_programs(1) - 1)
        def _():
            o_ref[...] = x_ref[...] + _rms(o_ref[...], g_ref[...])

    _with_decode(
        npt,
        lambda: run(zp_ref, yap_ref, ybp_ref, ycp_ref, gap_ref, gbp_ref, gcp_ref, xp_ref, op_ref, mp_ref),
        lambda: run(zd_ref, yad_ref, ybd_ref, ycd_ref, gad_ref, gbd_ref, gcd_ref, xd_ref, od_ref, md_ref))


def _merge_out(yp, yd, ztp, ztd, wglu, bglu, wa, wb, wc, wo, xp, xd, g, l, tk=512, tn=512):
    m, d = xp.shape
    nd = xd.shape[0]
    kb = yp[0].shape[1]
    gate0 = 0
    nk = d // tk
    tl = _Tiles(m, nk + d // tn)
    kidx = lambda s: jnp.minimum(s, nk - 1)
    nidx = lambda s: jnp.maximum(s - nk, 0)
    gp_spec = lambda br: pl.BlockSpec((tl.tm, tk), lambda i, s: (i, gate0 + br * nk + kidx(s)))
    gd_spec = lambda br: pl.BlockSpec(
        (nd, tk), lambda i, s: (0, gate0 + br * nk + kidx(tl.dstep(i, s))))
    yp_spec = pl.BlockSpec((tl.tm, kb), lambda i, s: (i, 0))
    yd_spec = pl.BlockSpec((nd, kb), lambda i, s: (0, 0))
    w_spec = _lspec(l, (kb, tk), lambda i, s: (0, kidx(s)))
    kern = functools.partial(_merge_out_kernel, npt=tl.npt, nk=nk, tk=tk, tn=tn)
    return pl.pallas_call(
        kern,
        grid=tl.grid,
        in_specs=[yp_spec, yp_spec, yp_spec, yd_spec, yd_spec, yd_spec,
                  gp_spec(0), gp_spec(1), gp_spec(2), gd_spec(0), gd_spec(1), gd_spec(2),
                  _lspec(l, (kb, kb), lambda i, s: (0, 0)),
                  _lspec(l, (1, kb), lambda i, s: (0, 0)),
                  w_spec, w_spec, w_spec,
                  _lspec(l, (d, tn), lambda i, s: (0, nidx(s))),
                  pl.BlockSpec((tl.tm, d), lambda i, s: (i, 0)),
                  pl.BlockSpec((nd, d), lambda i, s: (0, 0)),
                  _lspec(l, (1, d), lambda i, s: (0, 0))],
        out_specs=[pl.BlockSpec((tl.tm, d), lambda i, s: (i, 0)),
                   pl.BlockSpec((nd, d), lambda i, s: (0, 0))],
        out_shape=[jax.ShapeDtypeStruct((m, d), F32), jax.ShapeDtypeStruct((nd, d), F32)],
        scratch_shapes=[pltpu.VMEM((tl.tm, d), BF16), pltpu.VMEM((nd, d), BF16),
                        pltpu.VMEM((tl.tm, kb), BF16), pltpu.VMEM((nd, kb), BF16)],
        compiler_params=_cparams(("arbitrary", "arbitrary")),
        name="merge_out",
    )(*yp, *yd, ztp, ztp, ztp, ztd, ztd, ztd, wglu, bglu, wa, wb, wc, wo, xp, xd, g)


def _ffn_kernel(xp_ref, xd_ref, g1_ref, w1_ref, w2_ref, g2_ref, op_ref, od_ref,
                hp_ref, accp_ref, hd_ref, accd_ref, *, npt):
    f = pl.program_id(1)

    def run(x_ref, o_ref, h_ref, acc_ref):
        @pl.when(f == 0)
        def _():
            h_ref[...] = _rms(x_ref[...], g1_ref[...]).astype(BF16)
            acc_ref[...] = jnp.zeros_like(acc_ref)

        a = jnp.dot(h_ref[...], w1_ref[...], preferred_element_type=F32)
        a = jnp.square(jnp.maximum(a, 0.0)).astype(BF16)
        acc_ref[...] += jnp.dot(a, w2_ref[...], preferred_element_type=F32)

        @pl.when(f == pl.num_programs(1) - 1)
        def _():
            o_ref[...] = x_ref[...] + _rms(acc_ref[...], g2_ref[...])

    _with_decode(npt, lambda: run(xp_ref, op_ref, hp_ref, accp_ref),
                 lambda: run(xd_ref, od_ref, hd_ref, accd_ref))


def _ffn(xp, xd, g1, w1, w2, g2, l, tf=1024):
    m, d = xp.shape
    nd = xd.shape[0]
    ff = w1.shape[2]
    tl = _Tiles(m, ff // tf)
    kern = functools.partial(_ffn_kernel, npt=tl.npt)
    return pl.pallas_call(
        kern,
        grid=tl.grid,
        in_specs=[pl.BlockSpec((tl.tm, d), lambda i, f: (i, 0)),
                  pl.BlockSpec((nd, d), lambda i, f: (0, 0)),
                  _lspec(l, (1, d), lambda i, f: (0, 0)),
                  _lspec(l, (d, tf), lambda i, f: (0, f)),
                  _lspec(l, (tf, d), lambda i, f: (f, 0)),
                  _lspec(l, (1, d), lambda i, f: (0, 0))],
        out_specs=[pl.BlockSpec((tl.tm, d), lambda i, f: (i, 0)),
                   pl.BlockSpec((nd, d), lambda i, f: (0, 0))],
        out_shape=[jax.ShapeDtypeStruct((m, d), F32), jax.ShapeDtypeStruct((nd, d), F32)],
        scratch_shapes=[pltpu.VMEM((tl.tm, d), BF16), pltpu.VMEM((tl.tm, d), F32),
                        pltpu.VMEM((nd, d), BF16), pltpu.VMEM((nd, d), F32)],
        compiler_params=_cparams(("arbitrary", "arbitrary")),
        name="ffn",
    )(xp, xd, g1, w1, w2, g2)


def _s5_disc_kernel(lr_ref, li_ref, ls_ref, br_ref, bi_ref, ar_ref, ai_ref, bbr_ref, bbi_ref):
    lr = lr_ref[0]
    li = li_ref[0]
    dt = jnp.exp(ls_ref[0])
    mag = jnp.exp(lr * dt)
    ar = mag * jnp.cos(li * dt)
    ai = mag * jnp.sin(li * dt)
    den = lr * lr + li * li
    fr = ((ar - 1.0) * lr + ai * li) / den
    fi = (ai * lr - (ar - 1.0) * li) / den
    ar_ref[0] = ar
    ai_ref[0] = ai
    br = br_ref[0]
    bi = bi_ref[0]
    bbr_ref[0] = fr[:, None, :] * br - fi[:, None, :] * bi
    bbi_ref[0] = fr[:, None, :] * bi + fi[:, None, :] * br


def _s5_discretize(lam_re, lam_im, log_step, b_re_t, b_im_t):
    depth = lam_re.shape[0]
    gp = pl.BlockSpec((1, S5_GROUPS, S5_STATE), lambda l: (l, 0, 0))
    bspec = pl.BlockSpec((1, S5_GROUPS, S5_GROUP, S5_STATE), lambda l: (l, 0, 0, 0))
    return pl.pallas_call(
        _s5_disc_kernel,
        grid=(depth,),
        in_specs=[gp, gp, pl.BlockSpec((1, S5_GROUPS, 1), lambda l: (l, 0, 0)), bspec, bspec],
        out_specs=[gp, gp, bspec, bspec],
        out_shape=[jax.ShapeDtypeStruct(lam_re.shape, F32), jax.ShapeDtypeStruct(lam_re.shape, F32),
                   jax.ShapeDtypeStruct(b_re_t.shape, F32), jax.ShapeDtypeStruct(b_re_t.shape, F32)],
        compiler_params=_cparams(("parallel",)),
        name="s5_discretize",
    )(lam_re, lam_im, log_step.reshape(depth, S5_GROUPS, 1), b_re_t, b_im_t)


def _blockdiag(r):
    eye = jnp.eye(S5_HALF, dtype=r.dtype)
    z = jnp.einsum('ab,lgqbhp->lgqahbp', eye, r)
    s = r.shape
    return z.reshape(s[0], s[1], s[2], S5_HALF * s[4], S5_HALF * s[5])


def _s5_matrices(bbr_t, bbi_t, c_re, c_im):
    depth = bbr_t.shape[0]
    shp = (depth, S5_NGB, 2, S5_HALF, S5_GROUP, S5_STATE)
    bre = _blockdiag(bbr_t.reshape(shp))
    bim = _blockdiag(bbi_t.reshape(shp))
    bcat = jnp.concatenate([bre, bim], axis=-1)
    zb = jnp.zeros_like(bcat[:, :, 0])
    bmat = jnp.stack([jnp.concatenate([bcat[:, :, 0], zb], axis=-2),
                      jnp.concatenate([zb, bcat[:, :, 1]], axis=-2)], axis=2)
    cre = jnp.swapaxes(_blockdiag(c_re.reshape(shp)), -1, -2)
    cim = jnp.swapaxes(_blockdiag(c_im.reshape(shp)), -1, -2)
    ccat = jnp.concatenate([cre, -cim], axis=-2)
    zc = jnp.zeros_like(ccat[:, :, 0])
    cmat = jnp.stack([jnp.concatenate([ccat[:, :, 0], zc], axis=-1),
                      jnp.concatenate([zc, ccat[:, :, 1]], axis=-1)], axis=2)
    return bmat.astype(BF16), cmat.astype(BF16)


def _s5_prompt_kernel(u_ref, b_ref, c_ref, a_ref, d_ref, z_ref, st_ref, e_ref, x_ref, *, nb, tc, ngb):
    @pl.when(pl.program_id(1) == 0)
    def _():
        x_ref[...] = jnp.zeros_like(x_ref)

    nlb = 2 * S5_HW // LANES
    hlb = nlb // 2
    for b in range(ngb):
        for n in range(nb):
            ub = u_ref[n, :, b * LANES:(b + 1) * LANES].astype(BF16)
            for q in range(2):
                e = jnp.dot(ub, b_ref[b, q], preferred_element_type=F32)
                for j in range(nlb):
                    e_ref[b * nlb + j, pl.ds(n * 2 + q, tc, stride=SUBLANES), :] = (
                        e[:, j * LANES:(j + 1) * LANES])

    ar, ai = [], []
    for b in range(ngb):
        a = a_ref[b]
        ar += [a[:, j * LANES:(j + 1) * LANES] for j in range(hlb)]
        ai += [a[:, (hlb + j) * LANES:(hlb + j + 1) * LANES] for j in range(hlb)]
    slabs = [(b * nlb + j, b * nlb + hlb + j) for b in range(ngb) for j in range(hlb)]

    def step(t, carry):
        row = pl.multiple_of(t * SUBLANES, SUBLANES)
        new = []
        for c, (sr, si) in enumerate(slabs):
            xr, xi = carry[2 * c], carry[2 * c + 1]
            nxr = ar[c] * xr - ai[c] * xi + e_ref[sr, pl.ds(row, SUBLANES), :]
            nxi = ar[c] * xi + ai[c] * xr + e_ref[si, pl.ds(row, SUBLANES), :]
            e_ref[sr, pl.ds(row, SUBLANES), :] = nxr
            e_ref[si, pl.ds(row, SUBLANES), :] = nxi
            new += [nxr, nxi]
        return tuple(new)

    init = []
    for b in range(ngb):
        x0 = x_ref[b]
        for j in range(hlb):
            init += [x0[:, j * LANES:(j + 1) * LANES], x0[:, (hlb + j) * LANES:(hlb + j + 1) * LANES]]
    fin = lax.fori_loop(0, tc, step, tuple(init), unroll=8)
    for b in range(ngb):
        for j in range(hlb):
            c = b * hlb + j
            x_ref[b, :, j * LANES:(j + 1) * LANES] = fin[2 * c]
            x_ref[b, :, (hlb + j) * LANES:(hlb + j + 1) * LANES] = fin[2 * c + 1]
    st_ref[...] = x_ref[...]

    for b in range(ngb):
        for n in range(nb):
            y = d_ref[:, b * LANES:(b + 1) * LANES] * u_ref[n, :, b * LANES:(b + 1) * LANES]
            for q in range(2):
                xs = jnp.concatenate(
                    [e_ref[b * nlb + j, pl.ds(n * 2 + q, tc, stride=SUBLANES), :] for j in range(nlb)],
                    axis=1)
                y = y + jnp.dot(xs.astype(BF16), c_ref[b, q], preferred_element_type=F32)
            z_ref[n, :, b * LANES:(b + 1) * LANES] = jax.nn.gelu(y)


def _s5_prompt(zh3, bmat, cmat, arow, d, l, tc):
    nb, t, _ = zh3.shape
    assert 2 * nb == SUBLANES, "state rows are (batch, half) pairs filling one sublane tile"
    ngb = 2
    kern = functools.partial(_s5_prompt_kernel, nb=nb, tc=tc, ngb=ngb)
    return pl.pallas_call(
        kern,
        grid=(S5_NGB // ngb, t // tc),
        in_specs=[pl.BlockSpec((nb, tc, ngb * LANES), lambda g, c: (0, c, g)),
                  _lspec(l, (ngb, 2, LANES, 2 * S5_HW), lambda g, c: (g, 0, 0, 0)),
                  _lspec(l, (ngb, 2, 2 * S5_HW, LANES), lambda g, c: (g, 0, 0, 0)),
                  _lspec(l, (ngb, SUBLANES, 2 * S5_HW), lambda g, c: (g, 0, 0)),
                  _lspec(l, (1, ngb * LANES), lambda g, c: (0, g))],
        out_specs=[pl.BlockSpec((nb, tc, ngb * LANES), lambda g, c: (0, c, g)),
                   pl.BlockSpec((ngb, SUBLANES, 2 * S5_HW), lambda g, c: (g, 0, 0))],
        out_shape=[jax.ShapeDtypeStruct((nb, t, S5_WIDTH), F32),
                   jax.ShapeDtypeStruct((S5_NGB, SUBLANES, 2 * S5_HW), F32)],
        scratch_shapes=[pltpu.VMEM((ngb * 2 * S5_HW // LANES, tc * SUBLANES, LANES), F32),
                        pltpu.VMEM((ngb, SUBLANES, 2 * S5_HW), F32)],
        compiler_params=_cparams(("parallel", "arbitrary")),
        name="s5_prompt",
    )(zh3, bmat, cmat, arow, d)


def _s5_decode_kernel(u_ref, b_ref, c_ref, a_ref, d_ref, sr_ref, si_ref, z_ref, nr_ref, ni_ref):
    for g in range(S5_NGB):
        u = u_ref[:, g * LANES:(g + 1) * LANES]
        ub = u.astype(BF16)
        y = d_ref[:, g * LANES:(g + 1) * LANES] * u
        for q in range(2):
            c0 = (g * 2 + q) * S5_HW
            e = jnp.dot(ub, b_ref[g, q], preferred_element_type=F32)
            ar = a_ref[0:1, c0:c0 + S5_HW]
            ai = a_ref[1:2, c0:c0 + S5_HW]
            x0r = sr_ref[:, c0:c0 + S5_HW]
            x0i = si_ref[:, c0:c0 + S5_HW]
            nr = ar * x0r - ai * x0i + e[:, :S5_HW]
            ni = ar * x0i + ai * x0r + e[:, S5_HW:]
            nr_ref[:, c0:c0 + S5_HW] = nr
            ni_ref[:, c0:c0 + S5_HW] = ni
            xs = jnp.concatenate([nr, ni], axis=1).astype(BF16)
            y = y + jnp.dot(xs, c_ref[g, q], preferred_element_type=F32)
        z_ref[:, g * LANES:(g + 1) * LANES] = jax.nn.gelu(y)


def _s5_decode(zh, bmat, cmat, adec, d, sr, si, l):
    nb = zh.shape[0]
    ns = S5_GROUPS * S5_STATE
    full = lambda shape: pl.BlockSpec(shape, lambda i: (0,) * len(shape))
    lfull = lambda shape: _lspec(l, shape, lambda i: (0,) * len(shape))
    return pl.pallas_call(
        _s5_decode_kernel,
        grid=(1,),
        in_specs=[full((nb, S5_WIDTH)), lfull(bmat.shape[1:]), lfull(cmat.shape[1:]), lfull((2, ns)),
                  lfull((1, S5_WIDTH)), lfull((nb, ns)), lfull((nb, ns))],
        out_specs=[full((nb, S5_WIDTH)), full((nb, ns)), full((nb, ns))],
        out_shape=[jax.ShapeDtypeStruct((nb, S5_WIDTH), F32),
                   jax.ShapeDtypeStruct((nb, ns), F32), jax.ShapeDtypeStruct((nb, ns), F32)],
        compiler_params=_cparams(("arbitrary",)),
        name="s5_decode",
    )(zh, bmat, cmat, adec, d, sr, si)


def _gla_finish(o, r, g):
    return _rms(o, g) * (r * jax.nn.sigmoid(r))


def _gla_prompt_kernel(q_ref, k_ref, v_ref, a_ref, r_ref, wg_ref, bg_ref, go_ref, y_ref, st_ref,
                       s_ref, lg_ref, o_ref, qi_ref, ku_ref, cd_ref, *, tc):
    c = pl.program_id(1)

    @pl.when(c == 0)
    def _():
        s_ref[...] = jnp.zeros_like(s_ref)

    a16 = a_ref[0][:, :GLA_RANK].astype(BF16)
    lg_ref[...] = jax.nn.log_sigmoid(
        jnp.dot(a16, wg_ref[...], preferred_element_type=F32) + bg_ref[...]) / GLA_TAU

    ch = GLA_CHUNK
    hs = SUBLANES
    rowi = lax.broadcasted_iota(jnp.int32, (ch, GLA_DK), 0)
    rowh = lax.broadcasted_iota(jnp.int32, (hs, GLA_DK), 0)
    coli = lax.broadcasted_iota(jnp.int32, (hs, ch), 1)
    scale = GLA_DK ** -0.5

    def prep(idx, carry):
        r0 = pl.multiple_of(idx * ch, ch)
        for h in range(GLA_HEADS):
            ks = slice(h * GLA_DK, (h + 1) * GLA_DK)
            vs = slice(h * GLA_DV, (h + 1) * GLA_DV)
            b = lg_ref[pl.ds(r0, ch), ks]
            for sh in (1, 2, 4, 8):
                b = b + jnp.where(rowi >= sh, pltpu.roll(b, sh, axis=0), 0.0)
            qh = q_ref[0, pl.ds(r0, ch), ks] * scale
            kh = k_ref[0, pl.ds(r0, ch), ks]
            halves = []
            for half in range(ch // hs):
                bq = b[half * hs:(half + 1) * hs]
                qq = qh[half * hs:(half + 1) * hs]
                att = jnp.zeros((hs, ch), F32)
                for s in range((half + 1) * hs):
                    diff = bq - b[s:s + 1, :]
                    if s >= half * hs:
                        diff = jnp.where(rowh + half * hs >= s, diff, -jnp.inf)
                    col = jnp.sum(qq * kh[s:s + 1, :] * jnp.exp(diff), axis=1, keepdims=True)
                    att = jnp.where(coli == s, col, att)
                halves.append(att)
            att = jnp.concatenate(halves, axis=0).astype(BF16)
            vh = v_ref[0, pl.ds(r0, ch), vs].astype(BF16)
            o_ref[pl.ds(r0, ch), vs] = jnp.dot(att, vh, preferred_element_type=F32)
            blast = b[ch - 1:ch, :]
            qi_ref[pl.ds(r0, ch), ks] = (qh * jnp.exp(b)).astype(BF16)
            ku_ref[pl.ds(r0, ch), ks] = (kh * jnp.exp(blast - b)).astype(BF16)
            cd_ref[pl.ds(pl.multiple_of(idx * hs, hs), hs), ks] = jnp.broadcast_to(
                jnp.exp(blast), (hs, GLA_DK))
        return carry

    lax.fori_loop(0, tc // ch, prep, 0, unroll=4)

    def scan(idx, carry):
        r0 = pl.multiple_of(idx * ch, ch)
        for h in range(GLA_HEADS):
            ks = slice(h * GLA_DK, (h + 1) * GLA_DK)
            vs = slice(h * GLA_DV, (h + 1) * GLA_DV)
            st = s_ref[h]
            inter = lax.dot_general(qi_ref[pl.ds(r0, ch), ks], st.astype(BF16),
                                    (((1,), (1,)), ((), ())), preferred_element_type=F32)
            o_ref[pl.ds(r0, ch), vs] += inter
            vh = v_ref[0, pl.ds(r0, ch), vs].astype(BF16)
            kv = lax.dot_general(vh, ku_ref[pl.ds(r0, ch), ks], (((0,), (0,)), ((), ())),
                                 preferred_element_type=F32)
            cd = cd_ref[pl.ds(pl.multiple_of(idx * hs, hs), 1), ks]
            s_ref[h] = cd * st + kv
        return carry

    lax.fori_loop(0, tc // ch, scan, 0, unroll=4)

    for h in range(GLA_HEADS):
        vs = slice(h * GLA_DV, (h + 1) * GLA_DV)
        y_ref[0, :, vs] = _gla_finish(o_ref[:, vs], r_ref[0, :, vs], go_ref[...]).astype(y_ref.dtype)
        st_ref[0, h] = s_ref[h].T


def _gla_prompt(zh3, za3, zt3, wg, bg, go, l, tc):
    nb, t, _ = zh3.shape
    kern = functools.partial(_gla_prompt_kernel, tc=tc)
    return pl.pallas_call(
        kern,
        grid=(nb, t // tc),
        in_specs=[pl.BlockSpec((1, tc, GLA_QK), lambda n, c: (n, c, COL_GQ // GLA_QK)),
                  pl.BlockSpec((1, tc, GLA_QK), lambda n, c: (n, c, COL_GK // GLA_QK)),
                  pl.BlockSpec((1, tc, GLA_V), lambda n, c: (n, c, COL_GV // GLA_V)),
                  pl.BlockSpec((1, tc, LANES), lambda n, c: (n, c, 0)),
                  pl.BlockSpec((1, tc, GLA_V), lambda n, c: (n, c, TCOL_R // GLA_V)),
                  _lspec(l, (GLA_RANK, GLA_QK), lambda n, c: (0, 0)),
                  _lspec(l, (1, GLA_QK), lambda n, c: (0, 0)),
                  _lspec(l, (1, GLA_DV), lambda n, c: (0, 0))],
        out_specs=[pl.BlockSpec((1, tc, GLA_V), lambda n, c: (n, c, 0)),
                   pl.BlockSpec((1, GLA_HEADS, GLA_DK, GLA_DV), lambda n, c: (n, 0, 0, 0))],
        out_shape=[jax.ShapeDtypeStruct((nb, t, GLA_V), BF16),
                   jax.ShapeDtypeStruct((nb, GLA_HEADS, GLA_DK, GLA_DV), F32)],
        scratch_shapes=[pltpu.VMEM((GLA_HEADS, GLA_DV, GLA_DK), F32),
                        pltpu.VMEM((tc, GLA_QK), F32),
                        pltpu.VMEM((tc, GLA_V), F32),
                        pltpu.VMEM((tc, GLA_QK), BF16),
                        pltpu.VMEM((tc, GLA_QK), BF16),
                        pltpu.VMEM((tc // GLA_CHUNK * SUBLANES, GLA_QK), F32)],
        compiler_params=_cparams(("parallel", "arbitrary")),
        name="gla_prompt",
    )(zh3, zh3, zh3, za3, zt3, wg, bg, go)


def _gla_decode_kernel(zh_ref, za_ref, r_ref, wg_ref, bg_ref, go_ref, s_ref, y_ref, so_ref, *, nstep):
    n0 = pl.multiple_of(pl.program_id(0) * nstep, nstep)
    rows = zh_ref[pl.ds(n0, nstep), :]
    a16 = za_ref[pl.ds(n0, nstep), :GLA_RANK].astype(BF16)
    lg = jax.nn.log_sigmoid(
        jnp.dot(a16, wg_ref[...], preferred_element_type=F32) + bg_ref[...]) / GLA_TAU
    eg = jnp.exp(lg)
    rr = r_ref[pl.ds(n0, nstep), :]
    ri = lax.broadcasted_iota(jnp.int32, (GLA_DK, GLA_DK), 0)
    ci = lax.broadcasted_iota(jnp.int32, (GLA_DK, GLA_DK), 1)
    eye = (ri == ci).astype(F32)

    def col(v):
        return jnp.sum(eye * v, axis=1, keepdims=True)

    for j in range(nstep):
        row = rows[j:j + 1]
        for h in range(GLA_HEADS):
            ks = slice(h * GLA_DK, (h + 1) * GLA_DK)
            qh = row[:, COL_GQ + h * GLA_DK:COL_GQ + (h + 1) * GLA_DK] * (GLA_DK ** -0.5)
            kh = row[:, COL_GK + h * GLA_DK:COL_GK + (h + 1) * GLA_DK]
            vh = row[:, COL_GV + h * GLA_DV:COL_GV + (h + 1) * GLA_DV]
            s = col(eg[j:j + 1, ks]) * s_ref[j, h] + col(kh) * vh
            so_ref[j, h] = s
            o = jnp.sum(col(qh) * s, axis=0, keepdims=True)
            vs = slice(h * GLA_DV, (h + 1) * GLA_DV)
            y_ref[j, :, vs] = _gla_finish(o, rr[j:j + 1, vs], go_ref[...]).astype(y_ref.dtype)


def _gla_decode(zh, za, zt, wg, bg, go, s0, l, nstep=SUBLANES):
    nb = zh.shape[0]
    assert nb % nstep == 0
    kern = functools.partial(_gla_decode_kernel, nstep=nstep)
    sblock = (nstep, GLA_HEADS, GLA_DK, GLA_DV)
    return pl.pallas_call(
        kern,
        grid=(nb // nstep,),
        in_specs=[pl.BlockSpec((nb, HEAD_COLS), lambda n: (0, 0)),
                  pl.BlockSpec((nb, LANES), lambda n: (0, 0)),
                  pl.BlockSpec((nb, GLA_V), lambda n: (0, TCOL_R // GLA_V)),
                  _lspec(l, (GLA_RANK, GLA_QK), lambda n: (0, 0)),
                  _lspec(l, (1, GLA_QK), lambda n: (0, 0)),
                  _lspec(l, (1, GLA_DV), lambda n: (0, 0)),
                  _lspec(l, sblock, lambda n: (n, 0, 0, 0))],
        out_specs=[pl.BlockSpec((nstep, 1, GLA_V), lambda n: (n, 0, 0)),
                   pl.BlockSpec(sblock, lambda n: (n, 0, 0, 0))],
        out_shape=[jax.ShapeDtypeStruct((nb, 1, GLA_V), BF16),
                   jax.ShapeDtypeStruct(s0.shape[1:], F32)],
        compiler_params=_cparams(("arbitrary",)),
        name="gla_decode",
    )(zh, za, zt, wg, bg, go, s0)


def _bucket_table():
    dist = (np.arange(WINDOW)[:, None] + WINDOW) - np.arange(2 * WINDOW)[None, :]
    max_exact = N_BUCKETS // 2
    d = np.maximum(dist, 0)
    ratio = np.log(np.maximum(d, 1).astype(np.float32) / np.float32(max_exact)) / np.float32(
        math.log(MAX_DISTANCE / max_exact))
    large = max_exact + (ratio.astype(np.float32) * np.float32(N_BUCKETS - max_exact)).astype(np.int32)
    large = np.minimum(large, N_BUCKETS - 1)
    return np.where(d < max_exact, d, large).astype(np.int32)


def _bias_kernel(bk_ref, rb_ref, o_ref):
    h = pl.program_id(1)
    bk = bk_ref[...]
    acc = jnp.zeros(bk.shape, F32)
    for b in range(N_BUCKETS):
        acc = jnp.where(bk == b, rb_ref[b, h], acc)
    cols = lax.broadcasted_iota(jnp.int32, bk.shape, 1)
    dist = WINDOW + lax.broadcasted_iota(jnp.int32, bk.shape, 0) - cols
    band = (dist >= 0) & (dist < WINDOW)
    keep = band & ((cols >= WINDOW) | (pl.program_id(0) == 1))
    o_ref[0, 0] = jnp.where(keep, acc, -jnp.inf)


def _bias_table(rel_bias):
    bk = jnp.asarray(_bucket_table())
    return pl.pallas_call(
        _bias_kernel,
        grid=(2, SWA_QH),
        in_specs=[pl.BlockSpec((WINDOW, 2 * WINDOW), lambda v, h: (0, 0)),
                  pl.BlockSpec(memory_space=pltpu.SMEM)],
        out_specs=pl.BlockSpec((1, 1, WINDOW, 2 * WINDOW), lambda v, h: (v, h, 0, 0)),
        out_shape=jax.ShapeDtypeStruct((2, SWA_QH, WINDOW, 2 * WINDOW), F32),
        compiler_params=_cparams(("parallel", "parallel")),
        name="swa_bias",
    )(bk, rel_bias)


def _swa_prompt_kernel(q_ref, kp_ref, kc_ref, vp_ref, vc_ref, bias_ref, sink_ref, y_ref, *, t, l):
    i = pl.program_id(1)
    w = WINDOW
    inseq = i * w + lax.broadcasted_iota(jnp.int32, (w, 1), 0) < t
    scale = SWA_HD ** -0.5
    for kv in range(SWA_KVH):
        hs = slice(kv * SWA_HD, (kv + 1) * SWA_HD)
        kh = jnp.concatenate([kp_ref[0, :, hs], jnp.where(inseq, kc_ref[0, :, hs], 0.0)],
                             axis=0).astype(BF16)
        vh = jnp.concatenate([vp_ref[0, :, hs], jnp.where(inseq, vc_ref[0, :, hs], 0.0)],
                             axis=0).astype(BF16)
        for g in range(SWA_GRP):
            h = kv * SWA_GRP + g
            qh = (q_ref[0, :, h * SWA_HD:(h + 1) * SWA_HD] * scale).astype(BF16)
            s = lax.dot_general(qh, kh, (((1,), (1,)), ((), ())), preferred_element_type=F32)
            s = s + bias_ref[h]
            sink = sink_ref[l, h]
            m = jnp.maximum(jnp.max(s, axis=-1, keepdims=True), sink)
            p = jnp.exp(s - m)
            den = jnp.sum(p, axis=-1, keepdims=True) + jnp.exp(sink - m)
            o = jnp.dot(p.astype(BF16), vh, preferred_element_type=F32) * (1.0 / den)
            y_ref[0, :, h * SWA_HD:(h + 1) * SWA_HD] = o.astype(y_ref.dtype)


def _swa_prompt(zt3, bias, sinks, l):
    nb, t, _ = zt3.shape
    w = WINDOW
    nblk = pl.cdiv(t, w)
    kern = functools.partial(_swa_prompt_kernel, t=t, l=l)
    kcol = TCOL_SK // SWA_KV
    vcol = TCOL_SV // SWA_KV
    prev = lambda i: jnp.maximum(i - 1, 0)
    return pl.pallas_call(
        kern,
        grid=(nb, nblk),
        in_specs=[pl.BlockSpec((1, w, SWA_Q), lambda n, i: (n, i, TCOL_SQ // SWA_Q)),
                  pl.BlockSpec((1, w, SWA_KV), lambda n, i: (n, prev(i), kcol)),
                  pl.BlockSpec((1, w, SWA_KV), lambda n, i: (n, i, kcol)),
                  pl.BlockSpec((1, w, SWA_KV), lambda n, i: (n, prev(i), vcol)),
                  pl.BlockSpec((1, w, SWA_KV), lambda n, i: (n, i, vcol)),
                  pl.BlockSpec((None, SWA_QH, w, 2 * w), lambda n, i: (jnp.minimum(i, 1), 0, 0, 0)),
                  pl.BlockSpec(memory_space=pltpu.SMEM)],
        out_specs=pl.BlockSpec((1, w, SWA_Q), lambda n, i: (n, i, 0)),
        out_shape=jax.ShapeDtypeStruct((nb, t, SWA_Q), BF16),
        compiler_params=_cparams(("parallel", "arbitrary")),
        name="swa_prompt",
    )(zt3, zt3, zt3, zt3, zt3, bias, sinks)


def _swa_decode_kernel(q_ref, kv_ref, kb_ref, vb_ref, bias_ref, sink_ref, y_ref, ko_ref, vo_ref, *, nstep):
    n0 = pl.program_id(0) * nstep
    scale = SWA_HD ** -0.5
    w = WINDOW
    ri = lax.broadcasted_iota(jnp.int32, (SWA_HD, SWA_HD), 0)
    ci = lax.broadcasted_iota(jnp.int32, (SWA_HD, SWA_HD), 1)
    eye = (ri == ci).astype(F32)
    last = lax.broadcasted_iota(jnp.int32, (SWA_HD, w), 1) == w - 1

    def push(cache_t, new_row):
        col = jnp.sum(eye * new_row, axis=1, keepdims=True)
        return jnp.where(last, col, pltpu.roll(cache_t, w - 1, axis=1))

    scores = []
    for j in range(nstep):
        qrow = q_ref[pl.ds(n0 + j, 1), :]
        kvrow = kv_ref[pl.ds(n0 + j, 1), :]
        for kv in range(SWA_KVH):
            hs = slice(kv * SWA_HD, (kv + 1) * SWA_HD)
            kt = push(kb_ref[j, kv], kvrow[:, hs])
            ko_ref[j, kv] = kt
            vo_ref[j, kv] = push(vb_ref[j, kv],
                                 kvrow[:, SWA_KV + kv * SWA_HD:SWA_KV + (kv + 1) * SWA_HD])
            q4 = jnp.concatenate(
                [qrow[:, (kv * SWA_GRP + g) * SWA_HD:(kv * SWA_GRP + g + 1) * SWA_HD]
                 for g in range(SWA_GRP)], axis=0).astype(BF16)
            scores.append(jnp.dot(q4, kt.astype(BF16), preferred_element_type=F32))
    s = jnp.concatenate(scores, axis=0) * scale
    s = s + jnp.concatenate([bias_ref[...].reshape(SWA_QH, w)] * nstep, axis=0)
    sink = jnp.concatenate([sink_ref[...]] * nstep, axis=0)
    m = jnp.maximum(jnp.max(s, axis=-1, keepdims=True), sink)
    p = jnp.exp(s - m)
    den = jnp.sum(p, axis=-1, keepdims=True) + jnp.exp(sink - m)
    pn = (p / den).astype(BF16)
    for j in range(nstep):
        for kv in range(SWA_KVH):
            r0 = (j * SWA_KVH + kv) * SWA_GRP
            o = lax.dot_general(pn[r0:r0 + SWA_GRP], vo_ref[j, kv].astype(BF16),
                                (((1,), (1,)), ((), ())), preferred_element_type=F32)
            for g in range(SWA_GRP):
                h = kv * SWA_GRP + g
                y_ref[j, :, h * SWA_HD:(h + 1) * SWA_HD] = o[g:g + 1, :].astype(y_ref.dtype)


def _swa_decode(zt, kbuf_t, vbuf_t, bias_dec, sinks, l, nstep=8):
    nb = zt.shape[0]
    w = WINDOW
    assert nb % nstep == 0
    kern = functools.partial(_swa_decode_kernel, nstep=nstep)
    cspec = lambda: _lspec(l, (nstep, SWA_KVH, SWA_HD, w), lambda n: (n, 0, 0, 0))
    ospec = lambda: pl.BlockSpec((nstep, SWA_KVH, SWA_HD, w), lambda n: (n, 0, 0, 0))
    return pl.pallas_call(
        kern,
        grid=(nb // nstep,),
        in_specs=[pl.BlockSpec((nb, SWA_Q), lambda n: (0, TCOL_SQ // SWA_Q)),
                  pl.BlockSpec((nb, 2 * SWA_KV), lambda n: (0, TCOL_SK // (2 * SWA_KV))),
                  cspec(), cspec(),
                  pl.BlockSpec((SWA_KVH, SWA_GRP, w), lambda n: (0, 0, 0)),
                  _lspec(l, (SWA_QH, 1), lambda n: (0, 0))],
        out_specs=[pl.BlockSpec((nstep, 1, SWA_Q), lambda n: (n, 0, 0)), ospec(), ospec()],
        out_shape=[jax.ShapeDtypeStruct((nb, 1, SWA_Q), BF16),
                   jax.ShapeDtypeStruct(kbuf_t.shape[1:], F32), jax.ShapeDtypeStruct(vbuf_t.shape[1:], F32)],
        compiler_params=_cparams(("arbitrary",)),
        name="swa_decode",
    )(zt, zt, kbuf_t, vbuf_t, bias_dec, sinks)


def _seq_chunk(t):
    for c in (688, 512, 256, 128, 64, 32, 16):
        if t % c == 0:
            return c
    raise ValueError(f"sequence length {t} must be a multiple of {GLA_CHUNK}")


def kernel(x_prompt, x_sample, state_s5_re, state_s5_im, state_gla, cache_swa_k, cache_swa_v, meta_tokens, rel_bias, norm_pre_mix, norm_post_mix, norm_pre_ffn, norm_post_ffn, w_in, s5_lam_re, s5_lam_im, s5_log_step, s5_b_re, s5_b_im, s5_c_re, s5_c_im, s5_d, s5_w_glu, s5_b_glu, gla_w_gate2, gla_b_gate2, gla_g_out, swa_sinks, w_up_s5, w_up_gla, w_up_swa, w_out, w_ff1, w_ff2):
    depth = w_in.shape[0]
    nb, seq, d = x_prompt.shape
    t = seq + N_META
    nd = x_sample.shape[0]
    assert x_sample.shape[1] == 1 and cache_swa_k.shape[2] == WINDOW and t % GLA_CHUNK == 0
    tc = _seq_chunk(t)

    w_in_t = jnp.swapaxes(w_in.astype(BF16), 1, 2)
    w_glu = s5_w_glu.astype(BF16)
    wg2 = gla_w_gate2.astype(BF16)
    wu_a, wu_b, wu_c = w_up_s5.astype(BF16), w_up_gla.astype(BF16), w_up_swa.astype(BF16)
    wo, w1, w2 = w_out.astype(BF16), w_ff1.astype(BF16), w_ff2.astype(BF16)
    row3 = lambda a: a.reshape(depth, 1, a.shape[-1])
    g_pre, g_post, g_pre_f, g_post_f = (row3(a) for a in (norm_pre_mix, norm_post_mix,
                                                           norm_pre_ffn, norm_post_ffn))
    b_glu, d5, bg2, go = row3(s5_b_glu), row3(s5_d), row3(gla_b_gate2), row3(gla_g_out)
    sinks3 = swa_sinks.reshape(depth, SWA_QH, 1)
    s5r = state_s5_re.reshape(depth, nd, -1)
    s5i = state_s5_im.reshape(depth, nd, -1)
    ckb = jnp.transpose(cache_swa_k, (0, 1, 3, 4, 2))
    cvb = jnp.transpose(cache_swa_v, (0, 1, 3, 4, 2))

    ar, ai, bbr_t, bbi_t = _s5_discretize(s5_lam_re, s5_lam_im, s5_log_step,
                                          jnp.swapaxes(s5_b_re, -1, -2), jnp.swapaxes(s5_b_im, -1, -2))
    bmat, cmat = _s5_matrices(bbr_t, bbi_t, s5_c_re, s5_c_im)
    a_half = jnp.concatenate([ar.reshape(depth, S5_NGB, 2, S5_HW), ai.reshape(depth, S5_NGB, 2, S5_HW)],
                             axis=-1)
    arow = jnp.tile(a_half, (1, 1, SUBLANES // 2, 1))
    adec = jnp.stack([ar.reshape(depth, -1), ai.reshape(depth, -1)], axis=1)

    bias = _bias_table(rel_bias)
    bias_dec = bias[1, :, WINDOW - 1, WINDOW:].reshape(SWA_KVH, SWA_GRP, WINDOW)

    meta = jnp.broadcast_to(meta_tokens.astype(x_prompt.dtype)[None], (nb, N_META, d))
    xp = jnp.concatenate([meta, x_prompt], axis=1).reshape(nb * t, d)
    xs = x_sample.reshape(nd, d)

    outs = {k: [] for k in ('p_re', 'p_im', 'p_gla', 'p_k', 'p_v', 's_re', 's_im', 's_gla', 's_k', 's_v')}
    for l in range(depth):
        tm2 = 2 * _row_tile(nb * t)
        zh, zhd, za, zad = _norm_mm(xp, xs, g_pre, w_in_t, l, 0, HEAD_COLS, 1024, extra_col0=COL_GA, tm=tm2)
        zt, ztd = _norm_mm(xp, xs, g_pre, w_in_t, l, TAIL_START, MIX_COLS, 1280, tm=tm2)
        zg, zgd = _norm_mm(xp, xs, g_pre, w_in_t, l, GATE_START, GATE_COLS, 1536, gate=True, tm=tm2)
        zh3 = zh.reshape(nb, t, HEAD_COLS)
        za3 = za.reshape(nb, t, LANES)
        zt3 = zt.reshape(nb, t, MIX_COLS)
        z5, st5 = _s5_prompt(zh3, bmat, cmat, arow, d5, l, tc)
        z5d, nr, ni = _s5_decode(zhd, bmat, cmat, adec, d5, s5r, s5i, l)
        yb, pg = _gla_prompt(zh3, za3, zt3, wg2, bg2, go, l, tc)
        ybd, sg = _gla_decode(zhd, zad, ztd, wg2, bg2, go, state_gla, l)
        yc = _swa_prompt(zt3, bias, swa_sinks, l)
        ycd, sk, sv = _swa_decode(ztd, ckb, cvb, bias_dec, sinks3, l)
        xp, xs = _merge_out((z5.reshape(nb * t, S5_WIDTH), yb.reshape(nb * t, GLA_V),
                             yc.reshape(nb * t, SWA_Q)),
                            (z5d, ybd.reshape(nd, GLA_V), ycd.reshape(nd, SWA_Q)),
                            zg, zgd, w_glu, b_glu, wu_a, wu_b, wu_c, wo, xp, xs, g_post, l)
        xp, xs = _ffn(xp, xs, g_pre_f, w1, w2, g_post_f, l)
        st5 = st5.reshape(S5_NGB, nb, 2, 2, S5_HALF, S5_STATE).transpose(3, 1, 0, 2, 4, 5)
        st5 = st5.reshape(2, nb, S5_GROUPS, S5_STATE)
        outs['p_re'].append(st5[0])
        outs['p_im'].append(st5[1])
        outs['p_gla'].append(pg)
        outs['p_k'].append(zt3[:, t - WINDOW:, TCOL_SK:TCOL_SK + SWA_KV].reshape(nb, WINDOW, SWA_KVH, SWA_HD))
        outs['p_v'].append(zt3[:, t - WINDOW:, TCOL_SV:TCOL_SV + SWA_KV].reshape(nb, WINDOW, SWA_KVH, SWA_HD))
        outs['s_re'].append(nr.reshape(nd, S5_GROUPS, S5_STATE))
        outs['s_im'].append(ni.reshape(nd, S5_GROUPS, S5_STATE))
        outs['s_gla'].append(sg)
        outs['s_k'].append(jnp.transpose(sk, (0, 3, 1, 2)))
        outs['s_v'].append(jnp.transpose(sv, (0, 3, 1, 2)))

    y_prompt = xp.reshape(nb, t, d)[:, N_META:]
    y_sample = xs.reshape(nd, 1, d)
    st = {k: jnp.stack(v) for k, v in outs.items()}
    return (y_prompt, y_sample, st['p_re'], st['p_im'], st['p_gla'], st['p_k'], st['p_v'],
            st['s_re'], st['s_im'], st['s_gla'], st['s_k'], st['s_v'])
```

```python
import functools
import math

import numpy as np
import jax
import jax.numpy as jnp
from jax import lax
from jax.experimental import pallas as pl
from jax.experimental.pallas import tpu as pltpu

F32 = jnp.float32
BF16 = jnp.bfloat16

D_MODEL = 2048
N_META = 16
S5_GROUP = 16
S5_WIDTH = D_MODEL // 2
S5_GROUPS = S5_WIDTH // S5_GROUP
S5_STATE = 64
GLA_HEADS = 4
GLA_DK = D_MODEL // 16
GLA_DV = D_MODEL // 8
GLA_RANK = 16
GLA_TAU = 16.0
GLA_CHUNK = 16
SWA_QH = 16
SWA_KVH = 4
SWA_GRP = SWA_QH // SWA_KVH
SWA_HD = 64
WINDOW = 128
N_BUCKETS = 32
MAX_DISTANCE = WINDOW
D_FF = 4 * D_MODEL
N_BRANCH = 3
EPS = 1e-6

GLA_QK = GLA_HEADS * GLA_DK
GLA_V = GLA_HEADS * GLA_DV
SWA_Q = SWA_QH * SWA_HD
SWA_KV = SWA_KVH * SWA_HD

COL_U = 0
COL_GQ = COL_U + S5_WIDTH
COL_GK = COL_GQ + GLA_QK
COL_GV = COL_GK + GLA_QK
COL_GA = COL_GV + GLA_V
LANES = 128
SUBLANES = 8
HEAD_COLS = COL_GA
TAIL_START = COL_GA + GLA_RANK
TCOL_R = 0
TCOL_SQ = TCOL_R + GLA_V
TCOL_SK = TCOL_SQ + SWA_Q
TCOL_SV = TCOL_SK + SWA_KV
TCOL_GATE = TCOL_SV + SWA_KV
MIX_COLS = TCOL_GATE
GATE_START = TAIL_START + TCOL_GATE
GATE_COLS = N_BRANCH * D_MODEL

S5_GB = 8
S5_NGB = S5_GROUPS // S5_GB
S5_HALF = S5_GB // 2
S5_HW = S5_HALF * S5_STATE

VMEM_LIMIT = 60 * 1024 * 1024


def _cparams(sem):
    return pltpu.CompilerParams(dimension_semantics=sem, vmem_limit_bytes=VMEM_LIMIT)


def _rms(x, g):
    ms = jnp.mean(x * x, axis=-1, keepdims=True)
    return (x * lax.rsqrt(ms + EPS)) * g


def _row_tile(m):
    for t in (688, 512, 256, 128, 64, 32, 16, 8):
        if m % t == 0:
            return t
    return m


def _lspec(l, shape, imap):
    return pl.BlockSpec((None,) + tuple(shape), lambda *idx: (l,) + tuple(imap(*idx)))


class _Tiles:
    def __init__(self, m, n_inner):
        self.tm = _row_tile(m)
        self.npt = m // self.tm
        self.grid = (self.npt, n_inner)

    def dstep(self, i, j):
        return jnp.where(i == self.npt - 1, j, 0)


def _with_decode(npt, prompt_fn, decode_fn):
    prompt_fn()
    pl.when(pl.program_id(0) == npt - 1)(decode_fn)


def _nt_dot(h, wt):
    return lax.dot_general(h, wt, (((1,), (1,)), ((), ())), preferred_element_type=F32)


def _norm_mm_kernel(*refs, npt, gate, extra):
    if extra:
        xp_ref, xd_ref, g_ref, w_ref, we_ref, op_ref, od_ref, ep_ref, ed_ref, hp_ref, hd_ref = refs
    else:
        xp_ref, xd_ref, g_ref, w_ref, op_ref, od_ref, hp_ref, hd_ref = refs
        we_ref = ep_ref = ed_ref = None

    def run(x_ref, o_ref, e_ref, h_ref):
        @pl.when(pl.program_id(1) == 0)
        def _():
            h_ref[...] = _rms(x_ref[...], g_ref[...]).astype(BF16)
            if extra:
                e_ref[...] = _nt_dot(h_ref[...], we_ref[0])

        z = _nt_dot(h_ref[...], w_ref[0])
        o_ref[...] = (jax.nn.sigmoid(z) if gate else z).astype(o_ref.dtype)

    _with_decode(npt, lambda: run(xp_ref, op_ref, ep_ref, hp_ref),
                 lambda: run(xd_ref, od_ref, ed_ref, hd_ref))


def _norm_mm(xp, xd, g, wt, l, col0, n_cols, tn, gate=False, extra_col0=None):
    m, k = xp.shape
    nd = xd.shape[0]
    tl = _Tiles(m, n_cols // tn)
    extra = extra_col0 is not None
    odt = BF16 if gate else F32
    kern = functools.partial(_norm_mm_kernel, npt=tl.npt, gate=gate, extra=extra)
    wspec = lambda width, off: pl.BlockSpec(
        (pl.Element(1), pl.Element(width), pl.Element(k)),
        lambda i, j: (l, pl.multiple_of(off(j), 16), 0))
    in_specs = [pl.BlockSpec((tl.tm, k), lambda i, j: (i, 0)),
                pl.BlockSpec((nd, k), lambda i, j: (0, 0)),
                _lspec(l, (1, k), lambda i, j: (0, 0)),
                wspec(tn, lambda j: col0 + j * tn)]
    out_specs = [pl.BlockSpec((tl.tm, tn), lambda i, j: (i, j)),
                 pl.BlockSpec((nd, tn), lambda i, j: (0, tl.dstep(i, j)))]
    out_shape = [jax.ShapeDtypeStruct((m, n_cols), odt), jax.ShapeDtypeStruct((nd, n_cols), odt)]
    args = [xp, xd, g, wt]
    if extra:
        in_specs.append(wspec(LANES, lambda j: extra_col0 + 0 * j))
        out_specs += [pl.BlockSpec((tl.tm, LANES), lambda i, j: (i, 0)),
                      pl.BlockSpec((nd, LANES), lambda i, j: (0, 0))]
        out_shape += [jax.ShapeDtypeStruct((m, LANES), F32), jax.ShapeDtypeStruct((nd, LANES), F32)]
        args.append(wt)
    return pl.pallas_call(
        kern,
        grid=tl.grid,
        in_specs=in_specs,
        out_specs=out_specs,
        out_shape=out_shape,
        scratch_shapes=[pltpu.VMEM((tl.tm, k), BF16), pltpu.VMEM((nd, k), BF16)],
        compiler_params=_cparams(("arbitrary", "arbitrary")),
        name="norm_mm",
    )(*args)


def _merge_out_kernel(zp_ref, ybp_ref, ycp_ref, zd_ref, ybd_ref, ycd_ref,
                      gap_ref, gbp_ref, gcp_ref, gad_ref, gbd_ref, gcd_ref,
                      wglu_ref, bglu_ref, wa_ref, wb_ref, wc_ref, wo_ref, xp_ref, xd_ref, g_ref,
                      op_ref, od_ref, mp_ref, md_ref, yap_ref, yad_ref, *, npt, nk, tk, tn):
    s = pl.program_id(1)

    def run(z_ref, ya_ref, yb_ref, yc_ref, ga_ref, gb_ref, gc_ref, x_ref, o_ref, m_ref):
        @pl.when(s == 0)
        def _():
            z = z_ref[...]
            a = jnp.dot(z.astype(BF16), wglu_ref[...], preferred_element_type=F32) + bglu_ref[...]
            ya_ref[...] = (z * jax.nn.sigmoid(a)).astype(BF16)

        @pl.when(s < nk)
        def _():
            m = (ga_ref[...].astype(F32) * jnp.dot(ya_ref[...], wa_ref[...], preferred_element_type=F32)
                 + gb_ref[...].astype(F32) * jnp.dot(yb_ref[...], wb_ref[...], preferred_element_type=F32)
                 + gc_ref[...].astype(F32) * jnp.dot(yc_ref[...], wc_ref[...], preferred_element_type=F32))
            m_ref[:, pl.ds(pl.multiple_of(s * tk, tk), tk)] = m.astype(BF16)

        @pl.when(s >= nk)
        def _():
            c0 = pl.multiple_of((s - nk) * tn, tn)
            o_ref[:, pl.ds(c0, tn)] = jnp.dot(m_ref[...], wo_ref[...], preferred_element_type=F32)

        @pl.when(s == pl.num_programs(1) - 1)
        def _():
            o_ref[...] = x_ref[...] + _rms(o_ref[...], g_ref[...])

    _with_decode(
        npt,
        lambda: run(zp_ref, yap_ref, ybp_ref, ycp_ref, gap_ref, gbp_ref, gcp_ref, xp_ref, op_ref, mp_ref),
        lambda: run(zd_ref, yad_ref, ybd_ref, ycd_ref, gad_ref, gbd_ref, gcd_ref, xd_ref, od_ref, md_ref))


def _merge_out(yp, yd, ztp, ztd, wglu, bglu, wa, wb, wc, wo, xp, xd, g, l, tk=512, tn=512):
    m, d = xp.shape
    nd = xd.shape[0]
    kb = yp[0].shape[1]
    gate0 = 0
    nk = d // tk
    tl = _Tiles(m, nk + d // tn)
    kidx = lambda s: jnp.minimum(s, nk - 1)
    nidx = lambda s: jnp.maximum(s - nk, 0)
    gp_spec = lambda br: pl.BlockSpec((tl.tm, tk), lambda i, s: (i, gate0 + br * nk + kidx(s)))
    gd_spec = lambda br: pl.BlockSpec(
        (nd, tk), lambda i, s: (0, gate0 + br * nk + kidx(tl.dstep(i, s))))
    yp_spec = pl.BlockSpec((tl.tm, kb), lambda i, s: (i, 0))
    yd_spec = pl.BlockSpec((nd, kb), lambda i, s: (0, 0))
    w_spec = _lspec(l, (kb, tk), lambda i, s: (0, kidx(s)))
    kern = functools.partial(_merge_out_kernel, npt=tl.npt, nk=nk, tk=tk, tn=tn)
    return pl.pallas_call(
        kern,
        grid=tl.grid,
        in_specs=[yp_spec, yp_spec, yp_spec, yd_spec, yd_spec, yd_spec,
                  gp_spec(0), gp_spec(1), gp_spec(2), gd_spec(0), gd_spec(1), gd_spec(2),
                  _lspec(l, (kb, kb), lambda i, s: (0, 0)),
                  _lspec(l, (1, kb), lambda i, s: (0, 0)),
                  w_spec, w_spec, w_spec,
                  _lspec(l, (d, tn), lambda i, s: (0, nidx(s))),
                  pl.BlockSpec((tl.tm, d), lambda i, s: (i, 0)),
                  pl.BlockSpec((nd, d), lambda i, s: (0, 0)),
                  _lspec(l, (1, d), lambda i, s: (0, 0))],
        out_specs=[pl.BlockSpec((tl.tm, d), lambda i, s: (i, 0)),
                   pl.BlockSpec((nd, d), lambda i, s: (0, 0))],
        out_shape=[jax.ShapeDtypeStruct((m, d), F32), jax.ShapeDtypeStruct((nd, d), F32)],
        scratch_shapes=[pltpu.VMEM((tl.tm, d), BF16), pltpu.VMEM((nd, d), BF16),
                        pltpu.VMEM((tl.tm, kb), BF16), pltpu.VMEM((nd, kb), BF16)],
        compiler_params=_cparams(("arbitrary", "arbitrary")),
        name="merge_out",
    )(*yp, *yd, ztp, ztp, ztp, ztd, ztd, ztd, wglu, bglu, wa, wb, wc, wo, xp, xd, g)


def _ffn_kernel(xp_ref, xd_ref, g1_ref, w1_ref, w2_ref, g2_ref, op_ref, od_ref,
                hp_ref, accp_ref, hd_ref, accd_ref, *, npt):
    f = pl.program_id(1)

    def run(x_ref, o_ref, h_ref, acc_ref):
        @pl.when(f == 0)
        def _():
            h_ref[...] = _rms(x_ref[...], g1_ref[...]).astype(BF16)
            acc_ref[...] = jnp.zeros_like(acc_ref)

        a = jnp.dot(h_ref[...], w1_ref[...], preferred_element_type=F32)
        a = jnp.square(jnp.maximum(a, 0.0)).astype(BF16)
        acc_ref[...] += jnp.dot(a, w2_ref[...], preferred_element_type=F32)

        @pl.when(f == pl.num_programs(1) - 1)
        def _():
            o_ref[...] = x_ref[...] + _rms(acc_ref[...], g2_ref[...])

    _with_decode(npt, lambda: run(xp_ref, op_ref, hp_ref, accp_ref),
                 lambda: run(xd_ref, od_ref, hd_ref, accd_ref))


def _ffn(xp, xd, g1, w1, w2, g2, l, tf=1024):
    m, d = xp.shape
    nd = xd.shape[0]
    ff = w1.shape[2]
    tl = _Tiles(m, ff // tf)
    kern = functools.partial(_ffn_kernel, npt=tl.npt)
    return pl.pallas_call(
        kern,
        grid=tl.grid,
        in_specs=[pl.BlockSpec((tl.tm, d), lambda i, f: (i, 0)),
                  pl.BlockSpec((nd, d), lambda i, f: (0, 0)),
                  _lspec(l, (1, d), lambda i, f: (0, 0)),
                  _lspec(l, (d, tf), lambda i, f: (0, f)),
                  _lspec(l, (tf, d), lambda i, f: (f, 0)),
                  _lspec(l, (1, d), lambda i, f: (0, 0))],
        out_specs=[pl.BlockSpec((tl.tm, d), lambda i, f: (i, 0)),
                   pl.BlockSpec((nd, d), lambda i, f: (0, 0))],
        out_shape=[jax.ShapeDtypeStruct((m, d), F32), jax.ShapeDtypeStruct((nd, d), F32)],
        scratch_shapes=[pltpu.VMEM((tl.tm, d), BF16), pltpu.VMEM((tl.tm, d), F32),
                        pltpu.VMEM((nd, d), BF16), pltpu.VMEM((nd, d), F32)],
        compiler_params=_cparams(("arbitrary", "arbitrary")),
        name="ffn",
    )(xp, xd, g1, w1, w2, g2)


def _s5_disc_kernel(lr_ref, li_ref, ls_ref, br_ref, bi_ref, ar_ref, ai_ref, bbr_ref, bbi_ref):
    lr = lr_ref[0]
    li = li_ref[0]
    dt = jnp.exp(ls_ref[0])
    mag = jnp.exp(lr * dt)
    ar = mag * jnp.cos(li * dt)
    ai = mag * jnp.sin(li * dt)
    den = lr * lr + li * li
    fr = ((ar - 1.0) * lr + ai * li) / den
    fi = (ai * lr - (ar - 1.0) * li) / den
    ar_ref[0] = ar
    ai_ref[0] = ai
    br = br_ref[0]
    bi = bi_ref[0]
    bbr_ref[0] = fr[:, None, :] * br - fi[:, None, :] * bi
    bbi_ref[0] = fr[:, None, :] * bi + fi[:, None, :] * br


def _s5_discretize(lam_re, lam_im, log_step, b_re_t, b_im_t):
    depth = lam_re.shape[0]
    gp = pl.BlockSpec((1, S5_GROUPS, S5_STATE), lambda l: (l, 0, 0))
    bspec = pl.BlockSpec((1, S5_GROUPS, S5_GROUP, S5_STATE), lambda l: (l, 0, 0, 0))
    return pl.pallas_call(
        _s5_disc_kernel,
        grid=(depth,),
        in_specs=[gp, gp, pl.BlockSpec((1, S5_GROUPS, 1), lambda l: (l, 0, 0)), bspec, bspec],
        out_specs=[gp, gp, bspec, bspec],
        out_shape=[jax.ShapeDtypeStruct(lam_re.shape, F32), jax.ShapeDtypeStruct(lam_re.shape, F32),
                   jax.ShapeDtypeStruct(b_re_t.shape, F32), jax.ShapeDtypeStruct(b_re_t.shape, F32)],
        compiler_params=_cparams(("parallel",)),
        name="s5_discretize",
    )(lam_re, lam_im, log_step.reshape(depth, S5_GROUPS, 1), b_re_t, b_im_t)


def _blockdiag(r):
    eye = jnp.eye(S5_HALF, dtype=r.dtype)
    z = jnp.einsum('ab,lgqbhp->lgqahbp', eye, r)
    s = r.shape
    return z.reshape(s[0], s[1], s[2], S5_HALF * s[4], S5_HALF * s[5])


def _s5_matrices(bbr_t, bbi_t, c_re, c_im):
    depth = bbr_t.shape[0]
    shp = (depth, S5_NGB, 2, S5_HALF, S5_GROUP, S5_STATE)
    bre = _blockdiag(bbr_t.reshape(shp))
    bim = _blockdiag(bbi_t.reshape(shp))
    bcat = jnp.concatenate([bre, bim], axis=-1)
    zb = jnp.zeros_like(bcat[:, :, 0])
    bmat = jnp.stack([jnp.concatenate([bcat[:, :, 0], zb], axis=-2),
                      jnp.concatenate([zb, bcat[:, :, 1]], axis=-2)], axis=2)
    cre = jnp.swapaxes(_blockdiag(c_re.reshape(shp)), -1, -2)
    cim = jnp.swapaxes(_blockdiag(c_im.reshape(shp)), -1, -2)
    ccat = jnp.concatenate([cre, -cim], axis=-2)
    zc = jnp.zeros_like(ccat[:, :, 0])
    cmat = jnp.stack([jnp.concatenate([ccat[:, :, 0], zc], axis=-1),
                      jnp.concatenate([zc, ccat[:, :, 1]], axis=-1)], axis=2)
    return bmat.astype(BF16), cmat.astype(BF16)


def _s5_prompt_kernel(u_ref, b_ref, c_ref, a_ref, d_ref, z_ref, st_ref, e_ref, x_ref, *, nb, tc, ngb):
    @pl.when(pl.program_id(1) == 0)
    def _():
        x_ref[...] = jnp.zeros_like(x_ref)

    nlb = 2 * S5_HW // LANES
    hlb = nlb // 2
    for b in range(ngb):
        for n in range(nb):
            ub = u_ref[n, :, b * LANES:(b + 1) * LANES].astype(BF16)
            for q in range(2):
                e = jnp.dot(ub, b_ref[b, q], preferred_element_type=F32)
                for j in range(nlb):
                    e_ref[b * nlb + j, pl.ds(n * 2 + q, tc, stride=SUBLANES), :] = (
                        e[:, j * LANES:(j + 1) * LANES])

    ar, ai = [], []
    for b in range(ngb):
        a = a_ref[b]
        ar += [a[:, j * LANES:(j + 1) * LANES] for j in range(hlb)]
        ai += [a[:, (hlb + j) * LANES:(hlb + j + 1) * LANES] for j in range(hlb)]
    slabs = [(b * nlb + j, b * nlb + hlb + j) for b in range(ngb) for j in range(hlb)]

    def step(t, carry):
        row = pl.multiple_of(t * SUBLANES, SUBLANES)
        new = []
        for c, (sr, si) in enumerate(slabs):
            xr, xi = carry[2 * c], carry[2 * c + 1]
            nxr = ar[c] * xr - ai[c] * xi + e_ref[sr, pl.ds(row, SUBLANES), :]
            nxi = ar[c] * xi + ai[c] * xr + e_ref[si, pl.ds(row, SUBLANES), :]
            e_ref[sr, pl.ds(row, SUBLANES), :] = nxr
            e_ref[si, pl.ds(row, SUBLANES), :] = nxi
            new += [nxr, nxi]
        return tuple(new)

    init = []
    for b in range(ngb):
        x0 = x_ref[b]
        for j in range(hlb):
            init += [x0[:, j * LANES:(j + 1) * LANES], x0[:, (hlb + j) * LANES:(hlb + j + 1) * LANES]]
    fin = lax.fori_loop(0, tc, step, tuple(init), unroll=8)
    for b in range(ngb):
        for j in range(hlb):
            c = b * hlb + j
            x_ref[b, :, j * LANES:(j + 1) * LANES] = fin[2 * c]
            x_ref[b, :, (hlb + j) * LANES:(hlb + j + 1) * LANES] = fin[2 * c + 1]
    st_ref[...] = x_ref[...]

    for b in range(ngb):
        for n in range(nb):
            y = d_ref[:, b * LANES:(b + 1) * LANES] * u_ref[n, :, b * LANES:(b + 1) * LANES]
            for q in range(2):
                xs = jnp.concatenate(
                    [e_ref[b * nlb + j, pl.ds(n * 2 + q, tc, stride=SUBLANES), :] for j in range(nlb)],
                    axis=1)
                y = y + jnp.dot(xs.astype(BF16), c_ref[b, q], preferred_element_type=F32)
            z_ref[n, :, b * LANES:(b + 1) * LANES] = jax.nn.gelu(y)


def _s5_prompt(zh3, bmat, cmat, arow, d, l, tc):
    nb, t, _ = zh3.shape
    assert 2 * nb == SUBLANES, "state rows are (batch, half) pairs filling one sublane tile"
    ngb = 2
    kern = functools.partial(_s5_prompt_kernel, nb=nb, tc=tc, ngb=ngb)
    return pl.pallas_call(
        kern,
        grid=(S5_NGB // ngb, t // tc),
        in_specs=[pl.BlockSpec((nb, tc, ngb * LANES), lambda g, c: (0, c, g)),
                  _lspec(l, (ngb, 2, LANES, 2 * S5_HW), lambda g, c: (g, 0, 0, 0)),
                  _lspec(l, (ngb, 2, 2 * S5_HW, LANES), lambda g, c: (g, 0, 0, 0)),
                  _lspec(l, (ngb, SUBLANES, 2 * S5_HW), lambda g, c: (g, 0, 0)),
                  _lspec(l, (1, ngb * LANES), lambda g, c: (0, g))],
        out_specs=[pl.BlockSpec((nb, tc, ngb * LANES), lambda g, c: (0, c, g)),
                   pl.BlockSpec((ngb, SUBLANES, 2 * S5_HW), lambda g, c: (g, 0, 0))],
        out_shape=[jax.ShapeDtypeStruct((nb, t, S5_WIDTH), F32),
                   jax.ShapeDtypeStruct((S5_NGB, SUBLANES, 2 * S5_HW), F32)],
        scratch_shapes=[pltpu.VMEM((ngb * 2 * S5_HW // LANES, tc * SUBLANES, LANES), F32),
                        pltpu.VMEM((ngb, SUBLANES, 2 * S5_HW), F32)],
        compiler_params=_cparams(("parallel", "arbitrary")),
        name="s5_prompt",
    )(zh3, bmat, cmat, arow, d)


def _s5_decode_kernel(u_ref, b_ref, c_ref, a_ref, d_ref, sr_ref, si_ref, z_ref, nr_ref, ni_ref):
    for g in range(S5_NGB):
        u = u_ref[:, g * LANES:(g + 1) * LANES]
        ub = u.astype(BF16)
        y = d_ref[:, g * LANES:(g + 1) * LANES] * u
        for q in range(2):
            c0 = (g * 2 + q) * S5_HW
            e = jnp.dot(ub, b_ref[g, q], preferred_element_type=F32)
            ar = a_ref[0:1, c0:c0 + S5_HW]
            ai = a_ref[1:2, c0:c0 + S5_HW]
            x0r = sr_ref[:, c0:c0 + S5_HW]
            x0i = si_ref[:, c0:c0 + S5_HW]
            nr = ar * x0r - ai * x0i + e[:, :S5_HW]
            ni = ar * x0i + ai * x0r + e[:, S5_HW:]
            nr_ref[:, c0:c0 + S5_HW] = nr
            ni_ref[:, c0:c0 + S5_HW] = ni
            xs = jnp.concatenate([nr, ni], axis=1).astype(BF16)
            y = y + jnp.dot(xs, c_ref[g, q], preferred_element_type=F32)
        z_ref[:, g * LANES:(g + 1) * LANES] = jax.nn.gelu(y)


def _s5_decode(zh, bmat, cmat, adec, d, sr, si, l):
    nb = zh.shape[0]
    ns = S5_GROUPS * S5_STATE
    full = lambda shape: pl.BlockSpec(shape, lambda i: (0,) * len(shape))
    lfull = lambda shape: _lspec(l, shape, lambda i: (0,) * len(shape))
    return pl.pallas_call(
        _s5_decode_kernel,
        grid=(1,),
        in_specs=[full((nb, S5_WIDTH)), lfull(bmat.shape[1:]), lfull(cmat.shape[1:]), lfull((2, ns)),
                  lfull((1, S5_WIDTH)), lfull((nb, ns)), lfull((nb, ns))],
        out_specs=[full((nb, S5_WIDTH)), full((nb, ns)), full((nb, ns))],
        out_shape=[jax.ShapeDtypeStruct((nb, S5_WIDTH), F32),
                   jax.ShapeDtypeStruct((nb, ns), F32), jax.ShapeDtypeStruct((nb, ns), F32)],
        compiler_params=_cparams(("arbitrary",)),
        name="s5_decode",
    )(zh, bmat, cmat, adec, d, sr, si)


def _gla_finish(o, r, g):
    return _rms(o, g) * (r * jax.nn.sigmoid(r))


def _gla_prompt_kernel(q_ref, k_ref, v_ref, a_ref, r_ref, wg_ref, bg_ref, go_ref, y_ref, st_ref,
                       s_ref, lg_ref, o_ref, qi_ref, ku_ref, cd_ref, *, tc):
    c = pl.program_id(1)

    @pl.when(c == 0)
    def _():
        s_ref[...] = jnp.zeros_like(s_ref)

    a16 = a_ref[0][:, :GLA_RANK].astype(BF16)
    lg_ref[...] = jax.nn.log_sigmoid(
        jnp.dot(a16, wg_ref[...], preferred_element_type=F32) + bg_ref[...]) / GLA_TAU

    ch = GLA_CHUNK
    hs = SUBLANES
    rowi = lax.broadcasted_iota(jnp.int32, (ch, GLA_DK), 0)
    rowh = lax.broadcasted_iota(jnp.int32, (hs, GLA_DK), 0)
    coli = lax.broadcasted_iota(jnp.int32, (hs, ch), 1)
    scale = GLA_DK ** -0.5

    def prep(idx, carry):
        r0 = pl.multiple_of(idx * ch, ch)
        for h in range(GLA_HEADS):
            ks = slice(h * GLA_DK, (h + 1) * GLA_DK)
            vs = slice(h * GLA_DV, (h + 1) * GLA_DV)
            b = lg_ref[pl.ds(r0, ch), ks]
            for sh in (1, 2, 4, 8):
                b = b + jnp.where(rowi >= sh, pltpu.roll(b, sh, axis=0), 0.0)
            qh = q_ref[0, pl.ds(r0, ch), ks] * scale
            kh = k_ref[0, pl.ds(r0, ch), ks]
            halves = []
            for half in range(ch // hs):
                bq = b[half * hs:(half + 1) * hs]
                qq = qh[half * hs:(half + 1) * hs]
                att = jnp.zeros((hs, ch), F32)
                for s in range((half + 1) * hs):
                    diff = bq - b[s:s + 1, :]
                    if s >= half * hs:
                        diff = jnp.where(rowh + half * hs >= s, diff, -jnp.inf)
                    col = jnp.sum(qq * kh[s:s + 1, :] * jnp.exp(diff), axis=1, keepdims=True)
                    att = jnp.where(coli == s, col, att)
                halves.append(att)
            att = jnp.concatenate(halves, axis=0).astype(BF16)
            vh = v_ref[0, pl.ds(r0, ch), vs].astype(BF16)
            o_ref[pl.ds(r0, ch), vs] = jnp.dot(att, vh, preferred_element_type=F32)
            blast = b[ch - 1:ch, :]
            qi_ref[pl.ds(r0, ch), ks] = (qh * jnp.exp(b)).astype(BF16)
            ku_ref[pl.ds(r0, ch), ks] = (kh * jnp.exp(blast - b)).astype(BF16)
            cd_ref[pl.ds(pl.multiple_of(idx * hs, hs), hs), ks] = jnp.broadcast_to(
                jnp.exp(blast), (hs, GLA_DK))
        return carry

    lax.fori_loop(0, tc // ch, prep, 0, unroll=4)

    def scan(idx, carry):
        r0 = pl.multiple_of(idx * ch, ch)
        for h in range(GLA_HEADS):
            ks = slice(h * GLA_DK, (h + 1) * GLA_DK)
            vs = slice(h * GLA_DV, (h + 1) * GLA_DV)
            st = s_ref[h]
            inter = lax.dot_general(qi_ref[pl.ds(r0, ch), ks], st.astype(BF16),
                                    (((1,), (1,)), ((), ())), preferred_element_type=F32)
            o_ref[pl.ds(r0, ch), vs] += inter
            vh = v_ref[0, pl.ds(r0, ch), vs].astype(BF16)
            kv = lax.dot_general(vh, ku_ref[pl.ds(r0, ch), ks], (((0,), (0,)), ((), ())),
                                 preferred_element_type=F32)
            cd = cd_ref[pl.ds(pl.multiple_of(idx * hs, hs), 1), ks]
            s_ref[h] = cd * st + kv
        return carry

    lax.fori_loop(0, tc // ch, scan, 0, unroll=4)

    for h in range(GLA_HEADS):
        vs = slice(h * GLA_DV, (h + 1) * GLA_DV)
        y_ref[0, :, vs] = _gla_finish(o_ref[:, vs], r_ref[0, :, vs], go_ref[...]).astype(y_ref.dtype)
        st_ref[0, h] = s_ref[h].T


def _gla_prompt(zh3, za3, zt3, wg, bg, go, l, tc):
    nb, t, _ = zh3.shape
    kern = functools.partial(_gla_prompt_kernel, tc=tc)
    return pl.pallas_call(
        kern,
        grid=(nb, t // tc),
        in_specs=[pl.BlockSpec((1, tc, GLA_QK), lambda n, c: (n, c, COL_GQ // GLA_QK)),
                  pl.BlockSpec((1, tc, GLA_QK), lambda n, c: (n, c, COL_GK // GLA_QK)),
                  pl.BlockSpec((1, tc, GLA_V), lambda n, c: (n, c, COL_GV // GLA_V)),
                  pl.BlockSpec((1, tc, LANES), lambda n, c: (n, c, 0)),
                  pl.BlockSpec((1, tc, GLA_V), lambda n, c: (n, c, TCOL_R // GLA_V)),
                  _lspec(l, (GLA_RANK, GLA_QK), lambda n, c: (0, 0)),
                  _lspec(l, (1, GLA_QK), lambda n, c: (0, 0)),
                  _lspec(l, (1, GLA_DV), lambda n, c: (0, 0))],
        out_specs=[pl.BlockSpec((1, tc, GLA_V), lambda n, c: (n, c, 0)),
                   pl.BlockSpec((1, GLA_HEADS, GLA_DK, GLA_DV), lambda n, c: (n, 0, 0, 0))],
        out_shape=[jax.ShapeDtypeStruct((nb, t, GLA_V), BF16),
                   jax.ShapeDtypeStruct((nb, GLA_HEADS, GLA_DK, GLA_DV), F32)],
        scratch_shapes=[pltpu.VMEM((GLA_HEADS, GLA_DV, GLA_DK), F32),
                        pltpu.VMEM((tc, GLA_QK), F32),
                        pltpu.VMEM((tc, GLA_V), F32),
                        pltpu.VMEM((tc, GLA_QK), BF16),
                        pltpu.VMEM((tc, GLA_QK), BF16),
                        pltpu.VMEM((tc // GLA_CHUNK * SUBLANES, GLA_QK), F32)],
        compiler_params=_cparams(("parallel", "arbitrary")),
        name="gla_prompt",
    )(zh3, zh3, zh3, za3, zt3, wg, bg, go)


def _gla_decode_kernel(zh_ref, za_ref, r_ref, wg_ref, bg_ref, go_ref, s_ref, *rest, nstep):
    y_ref, so_ref = rest[-2:]
    n0 = pl.multiple_of(pl.program_id(0) * nstep, nstep)
    rows = zh_ref[pl.ds(n0, nstep), :]
    a16 = za_ref[pl.ds(n0, nstep), :GLA_RANK].astype(BF16)
    lg = jax.nn.log_sigmoid(
        jnp.dot(a16, wg_ref[...], preferred_element_type=F32) + bg_ref[...]) / GLA_TAU
    eg = jnp.exp(lg)
    rr = r_ref[pl.ds(n0, nstep), :]
    ri = lax.broadcasted_iota(jnp.int32, (GLA_DK, GLA_DK), 0)
    ci = lax.broadcasted_iota(jnp.int32, (GLA_DK, GLA_DK), 1)
    eye = (ri == ci).astype(F32)

    def col(v):
        return jnp.sum(eye * v, axis=1, keepdims=True)

    for j in range(nstep):
        row = rows[j:j + 1]
        for h in range(GLA_HEADS):
            ks = slice(h * GLA_DK, (h + 1) * GLA_DK)
            qh = row[:, COL_GQ + h * GLA_DK:COL_GQ + (h + 1) * GLA_DK] * (GLA_DK ** -0.5)
            kh = row[:, COL_GK + h * GLA_DK:COL_GK + (h + 1) * GLA_DK]
            vh = row[:, COL_GV + h * GLA_DV:COL_GV + (h + 1) * GLA_DV]
            s = col(eg[j:j + 1, ks]) * s_ref[j, h] + col(kh) * vh
            so_ref[j, h] = s
            o = jnp.sum(col(qh) * s, axis=0, keepdims=True)
            vs = slice(h * GLA_DV, (h + 1) * GLA_DV)
            y_ref[j, :, vs] = _gla_finish(o, rr[j:j + 1, vs], go_ref[...]).astype(y_ref.dtype)


def _gla_decode(zh, za, zt, wg, bg, go, s0, l, s_all=None, nstep=SUBLANES):
    nb = zh.shape[0]
    assert nb % nstep == 0
    kern = functools.partial(_gla_decode_kernel, nstep=nstep)
    sblock = (nstep, GLA_HEADS, GLA_DK, GLA_DV)
    in_specs = [pl.BlockSpec((nb, HEAD_COLS), lambda n: (0, 0)),
                pl.BlockSpec((nb, LANES), lambda n: (0, 0)),
                pl.BlockSpec((nb, GLA_V), lambda n: (0, TCOL_R // GLA_V)),
                _lspec(l, (GLA_RANK, GLA_QK), lambda n: (0, 0)),
                _lspec(l, (1, GLA_QK), lambda n: (0, 0)),
                _lspec(l, (1, GLA_DV), lambda n: (0, 0)),
                _lspec(l, sblock, lambda n: (n, 0, 0, 0))]
    args = [zh, za, zt, wg, bg, go, s0]
    aliases = {}
    if s_all is not None:
        in_specs.append(pl.BlockSpec(memory_space=pl.ANY))
        args.append(s_all)
        aliases = {len(args) - 1: 1}
    return pl.pallas_call(
        kern,
        grid=(nb // nstep,),
        in_specs=in_specs,
        out_specs=[pl.BlockSpec((nstep, 1, GLA_V), lambda n: (n, 0, 0)),
                   _lspec(l, sblock, lambda n: (n, 0, 0, 0))],
        out_shape=[jax.ShapeDtypeStruct((nb, 1, GLA_V), BF16),
                   jax.ShapeDtypeStruct(s0.shape, F32)],
        input_output_aliases=aliases,
        compiler_params=_cparams(("arbitrary",)),
        name="gla_decode",
    )(*args)


def _bucket_table():
    dist = (np.arange(WINDOW)[:, None] + WINDOW) - np.arange(2 * WINDOW)[None, :]
    max_exact = N_BUCKETS // 2
    d = np.maximum(dist, 0)
    ratio = np.log(np.maximum(d, 1).astype(np.float32) / np.float32(max_exact)) / np.float32(
        math.log(MAX_DISTANCE / max_exact))
    large = max_exact + (ratio.astype(np.float32) * np.float32(N_BUCKETS - max_exact)).astype(np.int32)
    large = np.minimum(large, N_BUCKETS - 1)
    return np.where(d < max_exact, d, large).astype(np.int32)


def _bias_kernel(bk_ref, rb_ref, o_ref):
    h = pl.program_id(1)
    bk = bk_ref[...]
    acc = jnp.zeros(bk.shape, F32)
    for b in range(N_BUCKETS):
        acc = jnp.where(bk == b, rb_ref[b, h], acc)
    cols = lax.broadcasted_iota(jnp.int32, bk.shape, 1)
    dist = WINDOW + lax.broadcasted_iota(jnp.int32, bk.shape, 0) - cols
    band = (dist >= 0) & (dist < WINDOW)
    keep = band & ((cols >= WINDOW) | (pl.program_id(0) == 1))
    o_ref[0, 0] = jnp.where(keep, acc, -jnp.inf)


def _bias_table(rel_bias):
    bk = jnp.asarray(_bucket_table())
    return pl.pallas_call(
        _bias_kernel,
        grid=(2, SWA_QH),
        in_specs=[pl.BlockSpec((WINDOW, 2 * WINDOW), lambda v, h: (0, 0)),
                  pl.BlockSpec(memory_space=pltpu.SMEM)],
        out_specs=pl.BlockSpec((1, 1, WINDOW, 2 * WINDOW), lambda v, h: (v, h, 0, 0)),
        out_shape=jax.ShapeDtypeStruct((2, SWA_QH, WINDOW, 2 * WINDOW), F32),
        compiler_params=_cparams(("parallel", "parallel")),
        name="swa_bias",
    )(bk, rel_bias)


def _swa_prompt_kernel(q_ref, kp_ref, kc_ref, vp_ref, vc_ref, bias_ref, sink_ref, y_ref, *, t, l):
    i = pl.program_id(1)
    w = WINDOW
    inseq = i * w + lax.broadcasted_iota(jnp.int32, (w, 1), 0) < t
    scale = SWA_HD ** -0.5
    for kv in range(SWA_KVH):
        hs = slice(kv * SWA_HD, (kv + 1) * SWA_HD)
        kh = jnp.concatenate([kp_ref[0, :, hs], jnp.where(inseq, kc_ref[0, :, hs], 0.0)],
                             axis=0).astype(BF16)
        vh = jnp.concatenate([vp_ref[0, :, hs], jnp.where(inseq, vc_ref[0, :, hs], 0.0)],
                             axis=0).astype(BF16)
        for g in range(SWA_GRP):
            h = kv * SWA_GRP + g
            qh = (q_ref[0, :, h * SWA_HD:(h + 1) * SWA_HD] * scale).astype(BF16)
            s = lax.dot_general(qh, kh, (((1,), (1,)), ((), ())), preferred_element_type=F32)
            s = s + bias_ref[h]
            sink = sink_ref[l, h]
            m = jnp.maximum(jnp.max(s, axis=-1, keepdims=True), sink)
            p = jnp.exp(s - m)
            den = jnp.sum(p, axis=-1, keepdims=True) + jnp.exp(sink - m)
            o = jnp.dot(p.astype(BF16), vh, preferred_element_type=F32) * (1.0 / den)
            y_ref[0, :, h * SWA_HD:(h + 1) * SWA_HD] = o.astype(y_ref.dtype)


def _swa_prompt(zt3, bias, sinks, l):
    nb, t, _ = zt3.shape
    w = WINDOW
    nblk = pl.cdiv(t, w)
    kern = functools.partial(_swa_prompt_kernel, t=t, l=l)
    kcol = TCOL_SK // SWA_KV
    vcol = TCOL_SV // SWA_KV
    prev = lambda i: jnp.maximum(i - 1, 0)
    return pl.pallas_call(
        kern,
        grid=(nb, nblk),
        in_specs=[pl.BlockSpec((1, w, SWA_Q), lambda n, i: (n, i, TCOL_SQ // SWA_Q)),
                  pl.BlockSpec((1, w, SWA_KV), lambda n, i: (n, prev(i), kcol)),
                  pl.BlockSpec((1, w, SWA_KV), lambda n, i: (n, i, kcol)),
                  pl.BlockSpec((1, w, SWA_KV), lambda n, i: (n, prev(i), vcol)),
                  pl.BlockSpec((1, w, SWA_KV), lambda n, i: (n, i, vcol)),
                  pl.BlockSpec((None, SWA_QH, w, 2 * w), lambda n, i: (jnp.minimum(i, 1), 0, 0, 0)),
                  pl.BlockSpec(memory_space=pltpu.SMEM)],
        out_specs=pl.BlockSpec((1, w, SWA_Q), lambda n, i: (n, i, 0)),
        out_shape=jax.ShapeDtypeStruct((nb, t, SWA_Q), BF16),
        compiler_params=_cparams(("parallel", "arbitrary")),
        name="swa_prompt",
    )(zt3, zt3, zt3, zt3, zt3, bias, sinks)


def _swa_decode_kernel(q_ref, kv_ref, kb_ref, vb_ref, bias_ref, sink_ref, y_ref, ko_ref, vo_ref, *, nstep):
    n0 = pl.program_id(0) * nstep
    scale = SWA_HD ** -0.5
    w = WINDOW
    ri = lax.broadcasted_iota(jnp.int32, (SWA_HD, SWA_HD), 0)
    ci = lax.broadcasted_iota(jnp.int32, (SWA_HD, SWA_HD), 1)
    eye = (ri == ci).astype(F32)
    last = lax.broadcasted_iota(jnp.int32, (SWA_HD, w), 1) == w - 1

    def push(cache_t, new_row):
        col = jnp.sum(eye * new_row, axis=1, keepdims=True)
        return jnp.where(last, col, pltpu.roll(cache_t, w - 1, axis=1))

    scores = []
    for j in range(nstep):
        qrow = q_ref[pl.ds(n0 + j, 1), :]
        kvrow = kv_ref[pl.ds(n0 + j, 1), :]
        for kv in range(SWA_KVH):
            hs = slice(kv * SWA_HD, (kv + 1) * SWA_HD)
            kt = push(kb_ref[j, kv], kvrow[:, hs])
            ko_ref[j, kv] = kt
            vo_ref[j, kv] = push(vb_ref[j, kv],
                                 kvrow[:, SWA_KV + kv * SWA_HD:SWA_KV + (kv + 1) * SWA_HD])
            q4 = jnp.concatenate(
                [qrow[:, (kv * SWA_GRP + g) * SWA_HD:(kv * SWA_GRP + g + 1) * SWA_HD]
                 for g in range(SWA_GRP)], axis=0).astype(BF16)
            scores.append(jnp.dot(q4, kt.astype(BF16), preferred_element_type=F32))
    s = jnp.concatenate(scores, axis=0) * scale
    s = s + jnp.concatenate([bias_ref[...].reshape(SWA_QH, w)] * nstep, axis=0)
    sink = jnp.concatenate([sink_ref[...]] * nstep, axis=0)
    m = jnp.maximum(jnp.max(s, axis=-1, keepdims=True), sink)
    p = jnp.exp(s - m)
    den = jnp.sum(p, axis=-1, keepdims=True) + jnp.exp(sink - m)
    pn = (p / den).astype(BF16)
    for j in range(nstep):
        for kv in range(SWA_KVH):
            r0 = (j * SWA_KVH + kv) * SWA_GRP
            o = lax.dot_general(pn[r0:r0 + SWA_GRP], vo_ref[j, kv].astype(BF16),
                                (((1,), (1,)), ((), ())), preferred_element_type=F32)
            for g in range(SWA_GRP):
                h = kv * SWA_GRP + g
                y_ref[j, :, h * SWA_HD:(h + 1) * SWA_HD] = o[g:g + 1, :].astype(y_ref.dtype)


def _swa_decode(zt, kbuf_t, vbuf_t, bias_dec, sinks, l, nstep=8):
    nb = zt.shape[0]
    w = WINDOW
    assert nb % nstep == 0
    kern = functools.partial(_swa_decode_kernel, nstep=nstep)
    cspec = lambda: _lspec(l, (nstep, SWA_KVH, SWA_HD, w), lambda n: (n, 0, 0, 0))
    ospec = lambda: pl.BlockSpec((nstep, SWA_KVH, SWA_HD, w), lambda n: (n, 0, 0, 0))
    return pl.pallas_call(
        kern,
        grid=(nb // nstep,),
        in_specs=[pl.BlockSpec((nb, SWA_Q), lambda n: (0, TCOL_SQ // SWA_Q)),
                  pl.BlockSpec((nb, 2 * SWA_KV), lambda n: (0, TCOL_SK // (2 * SWA_KV))),
                  cspec(), cspec(),
                  pl.BlockSpec((SWA_KVH, SWA_GRP, w), lambda n: (0, 0, 0)),
                  _lspec(l, (SWA_QH, 1), lambda n: (0, 0))],
        out_specs=[pl.BlockSpec((nstep, 1, SWA_Q), lambda n: (n, 0, 0)), ospec(), ospec()],
        out_shape=[jax.ShapeDtypeStruct((nb, 1, SWA_Q), BF16),
                   jax.ShapeDtypeStruct(kbuf_t.shape[1:], F32), jax.ShapeDtypeStruct(vbuf_t.shape[1:], F32)],
        compiler_params=_cparams(("arbitrary",)),
        name="swa_decode",
    )(zt, zt, kbuf_t, vbuf_t, bias_dec, sinks)


def _seq_chunk(t):
    for c in (688, 512, 256, 128, 64, 32, 16):
        if t % c == 0:
            return c
    raise ValueError(f"sequence length {t} must be a multiple of {GLA_CHUNK}")


def kernel(x_prompt, x_sample, state_s5_re, state_s5_im, state_gla, cache_swa_k, cache_swa_v, meta_tokens, rel_bias, norm_pre_mix, norm_post_mix, norm_pre_ffn, norm_post_ffn, w_in, s5_lam_re, s5_lam_im, s5_log_step, s5_b_re, s5_b_im, s5_c_re, s5_c_im, s5_d, s5_w_glu, s5_b_glu, gla_w_gate2, gla_b_gate2, gla_g_out, swa_sinks, w_up_s5, w_up_gla, w_up_swa, w_out, w_ff1, w_ff2):
    depth = w_in.shape[0]
    nb, seq, d = x_prompt.shape
    t = seq + N_META
    nd = x_sample.shape[0]
    assert x_sample.shape[1] == 1 and cache_swa_k.shape[2] == WINDOW and t % GLA_CHUNK == 0
    tc = _seq_chunk(t)

    w_in_t = jnp.swapaxes(w_in.astype(BF16), 1, 2)
    w_glu = s5_w_glu.astype(BF16)
    wg2 = gla_w_gate2.astype(BF16)
    wu_a, wu_b, wu_c = w_up_s5.astype(BF16), w_up_gla.astype(BF16), w_up_swa.astype(BF16)
    wo, w1, w2 = w_out.astype(BF16), w_ff1.astype(BF16), w_ff2.astype(BF16)
    row3 = lambda a: a.reshape(depth, 1, a.shape[-1])
    g_pre, g_post, g_pre_f, g_post_f = (row3(a) for a in (norm_pre_mix, norm_post_mix,
                                                           norm_pre_ffn, norm_post_ffn))
    b_glu, d5, bg2, go = row3(s5_b_glu), row3(s5_d), row3(gla_b_gate2), row3(gla_g_out)
    sinks3 = swa_sinks.reshape(depth, SWA_QH, 1)
    s5r = state_s5_re.reshape(depth, nd, -1)
    s5i = state_s5_im.reshape(depth, nd, -1)
    ckb = jnp.transpose(cache_swa_k, (0, 1, 3, 4, 2))
    cvb = jnp.transpose(cache_swa_v, (0, 1, 3, 4, 2))

    ar, ai, bbr_t, bbi_t = _s5_discretize(s5_lam_re, s5_lam_im, s5_log_step,
                                          jnp.swapaxes(s5_b_re, -1, -2), jnp.swapaxes(s5_b_im, -1, -2))
    bmat, cmat = _s5_matrices(bbr_t, bbi_t, s5_c_re, s5_c_im)
    a_half = jnp.concatenate([ar.reshape(depth, S5_NGB, 2, S5_HW), ai.reshape(depth, S5_NGB, 2, S5_HW)],
                             axis=-1)
    arow = jnp.tile(a_half, (1, 1, SUBLANES // 2, 1))
    adec = jnp.stack([ar.reshape(depth, -1), ai.reshape(depth, -1)], axis=1)

    bias = _bias_table(rel_bias)
    bias_dec = bias[1, :, WINDOW - 1, WINDOW:].reshape(SWA_KVH, SWA_GRP, WINDOW)

    meta = jnp.broadcast_to(meta_tokens.astype(x_prompt.dtype)[None], (nb, N_META, d))
    xp = jnp.concatenate([meta, x_prompt], axis=1).reshape(nb * t, d)
    xs = x_sample.reshape(nd, d)

    outs = {k: [] for k in ('p_re', 'p_im', 'p_gla', 'p_k', 'p_v', 's_re', 's_im', 's_k', 's_v')}
    sg_all = None
    for l in range(depth):
        zh, zhd, za, zad = _norm_mm(xp, xs, g_pre, w_in_t, l, 0, HEAD_COLS, 1536, extra_col0=COL_GA)
        zt, ztd = _norm_mm(xp, xs, g_pre, w_in_t, l, TAIL_START, MIX_COLS, 2560)
        zg, zgd = _norm_mm(xp, xs, g_pre, w_in_t, l, GATE_START, GATE_COLS, 3072, gate=True)
        zh3 = zh.reshape(nb, t, HEAD_COLS)
        za3 = za.reshape(nb, t, LANES)
        zt3 = zt.reshape(nb, t, MIX_COLS)
        z5, st5 = _s5_prompt(zh3, bmat, cmat, arow, d5, l, tc)
        z5d, nr, ni = _s5_decode(zhd, bmat, cmat, adec, d5, s5r, s5i, l)
        yb, pg = _gla_prompt(zh3, za3, zt3, wg2, bg2, go, l, tc)
        ybd, sg_all = _gla_decode(zhd, zad, ztd, wg2, bg2, go, state_gla, l, sg_all)
        yc = _swa_prompt(zt3, bias, swa_sinks, l)
        ycd, sk, sv = _swa_decode(ztd, ckb, cvb, bias_dec, sinks3, l)
        xp, xs = _merge_out((z5.reshape(nb * t, S5_WIDTH), yb.reshape(nb * t, GLA_V),
                             yc.reshape(nb * t, SWA_Q)),
                            (z5d, ybd.reshape(nd, GLA_V), ycd.reshape(nd, SWA_Q)),
                            zg, zgd, w_glu, b_glu, wu_a, wu_b, wu_c, wo, xp, xs, g_post, l)
        xp, xs = _ffn(xp, xs, g_pre_f, w1, w2, g_post_f, l)
        st5 = st5.reshape(S5_NGB, nb, 2, 2, S5_HALF, S5_STATE).transpose(3, 1, 0, 2, 4, 5)
        st5 = st5.reshape(2, nb, S5_GROUPS, S5_STATE)
        outs['p_re'].append(st5[0])
        outs['p_im'].append(st5[1])
        outs['p_gla'].append(pg)
        outs['p_k'].append(zt3[:, t - WINDOW:, TCOL_SK:TCOL_SK + SWA_KV].reshape(nb, WINDOW, SWA_KVH, SWA_HD))
        outs['p_v'].append(zt3[:, t - WINDOW:, TCOL_SV:TCOL_SV + SWA_KV].reshape(nb, WINDOW, SWA_KVH, SWA_HD))
        outs['s_re'].append(nr.reshape(nd, S5_GROUPS, S5_STATE))
        outs['s_im'].append(ni.reshape(nd, S5_GROUPS, S5_STATE))
        outs['s_k'].append(jnp.transpose(sk, (0, 3, 1, 2)))
        outs['s_v'].append(jnp.transpose(sv, (0, 3, 1, 2)))

    y_prompt = xp.reshape(nb, t, d)[:, N_META:]
    y_sample = xs.reshape(nd, 1, d)
    st = {k: jnp.stack(v) for k, v in outs.items()}
    return (y_prompt, y_sample, st['p_re'], st['p_im'], st['p_gla'], st['p_k'], st['p_v'],
            st['s_re'], st['s_im'], sg_all, st['s_k'], st['s_v'])
```
